```python
import jax, jax.numpy as jnp
from jax import lax
import numpy as np

D_MODEL = 1024
BATCH = 8
SEQ = 16384
DEPTH = 4

N_MIXERS = 2
N_HEADS = 16
HEAD_DIM = D_MODEL // N_HEADS
Q_BLOCK = 128
CONV_WIDTH = 3
D_FF = 4 * D_MODEL
PLE_DIM = 256
N_ATTN = (DEPTH + 1) // 2
N_CONV = DEPTH // 2
RMS_EPS = 1e-6
NEG_INF = -1e30

kernel_name = "fox_shortconv_hybrid_trunk"


def rmsnorm(x, g):
    xf = x.astype(jnp.float32)
    y = xf * lax.rsqrt(jnp.mean(xf * xf, axis=-1, keepdims=True) + RMS_EPS)
    return (y * g.astype(jnp.float32)).astype(x.dtype)


def fox_attention(h, w_in, b_f, w_out):
    B, S, D = h.shape
    proj = h @ w_in
    q = proj[..., :D].reshape(B, S, N_HEADS, HEAD_DIM).transpose(0, 2, 1, 3)
    k = proj[..., D:2 * D].reshape(B, S, N_HEADS, HEAD_DIM).transpose(0, 2, 1, 3)
    v = proj[..., 2 * D:3 * D].reshape(B, S, N_HEADS, HEAD_DIM).transpose(0, 2, 1, 3)
    q = q * jnp.asarray(HEAD_DIM ** -0.5, q.dtype)
    f_logit = (proj[..., 3 * D:] + b_f).astype(jnp.float32)
    log_f = jax.nn.log_sigmoid(f_logit)
    c = lax.cumsum(log_f, axis=1).transpose(0, 2, 1)
    k_pos = jnp.arange(S)

    def q_block(i):
        start = i * Q_BLOCK
        qb = lax.dynamic_slice_in_dim(q, start, Q_BLOCK, axis=2)
        cb = lax.dynamic_slice_in_dim(c, start, Q_BLOCK, axis=2)
        s = jnp.einsum('bhqd,bhkd->bhqk', qb, k, preferred_element_type=jnp.float32)
        s = s + cb[..., :, None] - c[..., None, :]
        q_pos = start + jnp.arange(Q_BLOCK)
        s = jnp.where(k_pos[None, :] <= q_pos[:, None], s, NEG_INF)
        pr = jax.nn.softmax(s, axis=-1)
        return jnp.einsum('bhqk,bhkd->bhqd', pr.astype(v.dtype), v)

    o = lax.map(q_block, jnp.arange(S // Q_BLOCK))
    o = o.transpose(1, 0, 3, 2, 4).reshape(B, S, D)
    return o @ w_out


def short_conv(h, w_in, conv_w, w_out):
    D = h.shape[-1]
    proj = h @ w_in
    b_gate = proj[..., :D]
    c_gate = proj[..., D:2 * D]
    u = proj[..., 2 * D:]
    z = c_gate * u
    zc = lax.conv_general_dilated(
        z, conv_w[:, None, :].astype(z.dtype), window_strides=(1,),
        padding=[(CONV_WIDTH - 1, 0)],
        dimension_numbers=('NWC', 'WIO', 'NWC'),
        feature_group_count=D)
    return (b_gate * zc) @ w_out


def sq_relu_mlp(h, w_up, w_down):
    return jnp.square(jax.nn.relu(h @ w_up)) @ w_down


def _fwd_setup_inputs(seed: int = 0) -> dict:
    key = jax.random.key(seed)
    ks = jax.random.split(key, 14)
    f32 = jnp.float32
    nrm = lambda k, shape, scale: jax.random.normal(k, shape, f32) * scale
    x = jax.random.normal(ks[0], (BATCH, SEQ, D_MODEL), f32)
    p = jax.random.normal(ks[1], (DEPTH, BATCH, SEQ, PLE_DIM), f32)
    norm_g = 1.0 + nrm(ks[2], (DEPTH, 6, D_MODEL), 0.05)
    w_attn_in = nrm(ks[3], (N_ATTN, D_MODEL, 3 * D_MODEL + N_HEADS), D_MODEL ** -0.5)
    b_forget = 2.0 + nrm(ks[4], (N_ATTN, N_HEADS), 0.5)
    w_attn_out = nrm(ks[5], (N_ATTN, D_MODEL, D_MODEL), D_MODEL ** -0.5)
    w_conv_in = nrm(ks[6], (N_CONV, D_MODEL, 3 * D_MODEL), D_MODEL ** -0.5)
    conv_w = nrm(ks[7], (N_CONV, CONV_WIDTH, D_MODEL), CONV_WIDTH ** -0.5)
    w_conv_out = nrm(ks[8], (N_CONV, D_MODEL, D_MODEL), D_MODEL ** -0.5)
    w_mlp_up = nrm(ks[9], (DEPTH, D_MODEL, D_FF), D_MODEL ** -0.5)
    w_mlp_down = nrm(ks[10], (DEPTH, D_FF, D_MODEL), D_FF ** -0.5)
    w_ple_proj = nrm(ks[11], (DEPTH, PLE_DIM, D_MODEL), PLE_DIM ** -0.5)
    w_ple_gate = nrm(ks[12], (DEPTH, D_MODEL, D_MODEL), D_MODEL ** -0.5)
    return {"x": x, "p": p, "norm_g": norm_g, "w_attn_in": w_attn_in,
            "b_forget": b_forget, "w_attn_out": w_attn_out, "w_conv_in": w_conv_in,
            "conv_w": conv_w, "w_conv_out": w_conv_out, "w_mlp_up": w_mlp_up,
            "w_mlp_down": w_mlp_down, "w_ple_proj": w_ple_proj, "w_ple_gate": w_ple_gate}


def _fwd_reference(x, p, norm_g, w_attn_in, b_forget, w_attn_out, w_conv_in, conv_w,
              w_conv_out, w_mlp_up, w_mlp_down, w_ple_proj, w_ple_gate):
    for i in range(DEPTH):
        g = norm_g[i]
        hn = rmsnorm(x, g[0])
        j = i // N_MIXERS
        if i % N_MIXERS == 0:
            m = fox_attention(hn, w_attn_in[j], b_forget[j], w_attn_out[j])
        else:
            m = short_conv(hn, w_conv_in[j], conv_w[j], w_conv_out[j])
        x = x + rmsnorm(m, g[1])
        f = sq_relu_mlp(rmsnorm(x, g[2]), w_mlp_up[i], w_mlp_down[i])
        x = x + rmsnorm(f, g[3])
        gate = jax.nn.sigmoid(rmsnorm(x, g[4]) @ w_ple_gate[i])
        e = (p[i] @ w_ple_proj[i]) * gate
        x = x + rmsnorm(e, g[5])
    return x


import jax as _jax
import jax.numpy as _jnp

TWIN_FORMAT = 'train_step'
FWD_PARAMS = ['x', 'p', 'norm_g', 'w_attn_in', 'b_forget', 'w_attn_out', 'w_conv_in', 'conv_w', 'w_conv_out', 'w_mlp_up', 'w_mlp_down', 'w_ple_proj', 'w_ple_gate']
TWIN_WEIGHTS = ['norm_g', 'w_attn_in', 'b_forget', 'w_attn_out', 'w_conv_in', 'conv_w', 'w_conv_out', 'w_mlp_up', 'w_mlp_down', 'w_ple_proj', 'w_ple_gate']
TWIN_DIFF_INPUT = 'x'
TWIN_INPUTS = ['x', 'p', 'norm_g', 'w_attn_in', 'b_forget', 'w_attn_out', 'w_conv_in', 'conv_w', 'w_conv_out', 'w_mlp_up', 'w_mlp_down', 'w_ple_proj', 'w_ple_gate', 'loss_target', 'm_norm_g', 'm_w_attn_in', 'm_b_forget', 'm_w_attn_out', 'm_w_conv_in', 'm_conv_w', 'm_w_conv_out', 'm_w_mlp_up', 'm_w_mlp_down', 'm_w_ple_proj', 'm_w_ple_gate', 'v_norm_g', 'v_w_attn_in', 'v_b_forget', 'v_w_attn_out', 'v_w_conv_in', 'v_conv_w', 'v_w_conv_out', 'v_w_mlp_up', 'v_w_mlp_down', 'v_w_ple_proj', 'v_w_ple_gate']
TWIN_OUTPUTS = ['loss', 'grad_x', 'grad_norm_g', 'grad_w_attn_in', 'grad_b_forget', 'grad_w_attn_out', 'grad_w_conv_in', 'grad_conv_w', 'grad_w_conv_out', 'grad_w_mlp_up', 'grad_w_mlp_down', 'grad_w_ple_proj', 'grad_w_ple_gate', 'delta_norm_g', 'delta_w_attn_in', 'delta_b_forget', 'delta_w_attn_out', 'delta_w_conv_in', 'delta_conv_w', 'delta_w_conv_out', 'delta_w_mlp_up', 'delta_w_mlp_down', 'delta_w_ple_proj', 'delta_w_ple_gate', 'new_m_norm_g', 'new_m_w_attn_in', 'new_m_b_forget', 'new_m_w_attn_out', 'new_m_w_conv_in', 'new_m_conv_w', 'new_m_w_conv_out', 'new_m_w_mlp_up', 'new_m_w_mlp_down', 'new_m_w_ple_proj', 'new_m_w_ple_gate', 'new_v_norm_g', 'new_v_w_attn_in', 'new_v_b_forget', 'new_v_w_attn_out', 'new_v_w_conv_in', 'new_v_conv_w', 'new_v_w_conv_out', 'new_v_w_mlp_up', 'new_v_w_mlp_down', 'new_v_w_ple_proj', 'new_v_w_ple_gate']
TWIN_LEAF_KINDS = {'loss': 'loss', 'grad_x': 'grad_x', 'grad_norm_g': 'grad_w', 'grad_w_attn_in': 'grad_w', 'grad_b_forget': 'grad_w', 'grad_w_attn_out': 'grad_w', 'grad_w_conv_in': 'grad_w', 'grad_conv_w': 'grad_w', 'grad_w_conv_out': 'grad_w', 'grad_w_mlp_up': 'grad_w', 'grad_w_mlp_down': 'grad_w', 'grad_w_ple_proj': 'grad_w', 'grad_w_ple_gate': 'grad_w', 'delta_norm_g': 'delta_w', 'delta_w_attn_in': 'delta_w', 'delta_b_forget': 'delta_w', 'delta_w_attn_out': 'delta_w', 'delta_w_conv_in': 'delta_w', 'delta_conv_w': 'delta_w', 'delta_w_conv_out': 'delta_w', 'delta_w_mlp_up': 'delta_w', 'delta_w_mlp_down': 'delta_w', 'delta_w_ple_proj': 'delta_w', 'delta_w_ple_gate': 'delta_w', 'new_m_norm_g': 'new_m', 'new_m_w_attn_in': 'new_m', 'new_m_b_forget': 'new_m', 'new_m_w_attn_out': 'new_m', 'new_m_w_conv_in': 'new_m', 'new_m_conv_w': 'new_m', 'new_m_w_conv_out': 'new_m', 'new_m_w_mlp_up': 'new_m', 'new_m_w_mlp_down': 'new_m', 'new_m_w_ple_proj': 'new_m', 'new_m_w_ple_gate': 'new_m', 'new_v_norm_g': 'new_v', 'new_v_w_attn_in': 'new_v', 'new_v_b_forget': 'new_v', 'new_v_w_attn_out': 'new_v', 'new_v_w_conv_in': 'new_v', 'new_v_conv_w': 'new_v', 'new_v_w_conv_out': 'new_v', 'new_v_w_mlp_up': 'new_v', 'new_v_w_mlp_down': 'new_v', 'new_v_w_ple_proj': 'new_v', 'new_v_w_ple_gate': 'new_v'}


def _forward(args):
    return _fwd_reference(*[args[k] for k in FWD_PARAMS])


def _output_shape():
    def fwd():
        inp = _fwd_setup_inputs(0)
        return _fwd_reference(*[inp[k] for k in FWD_PARAMS])
    out = _jax.eval_shape(fwd)
    return out.shape, out.dtype

N_MICROBATCH = 1
ADAM_LR = 0.001
ADAM_B1 = 0.9
ADAM_B2 = 0.999
ADAM_EPS = 1e-08
ADAM_WD = 0.01
ADAM_STEP = 10
PER_EXAMPLE_BATCH_AXIS = {'x': 0, 'p': 1, 'loss_target': 0}
SHARED_INPUTS = []
_WEIGHT_DTYPES = {'norm_g': _jnp.float32, 'w_attn_in': _jnp.float32, 'b_forget': _jnp.float32, 'w_attn_out': _jnp.float32, 'w_conv_in': _jnp.float32, 'conv_w': _jnp.float32, 'w_conv_out': _jnp.float32, 'w_mlp_up': _jnp.float32, 'w_mlp_down': _jnp.float32, 'w_ple_proj': _jnp.float32, 'w_ple_gate': _jnp.float32}
MOMENT_SCALE = {'norm_g': 9.670063e+01, 'w_attn_in': 1.690099e+01, 'b_forget': 4.493799e+01, 'w_attn_out': 3.300941e+01, 'w_conv_in': 2.334884e+00, 'conv_w': 2.592977e+00, 'w_conv_out': 2.622318e+00, 'w_mlp_up': 7.449323e+00, 'w_mlp_down': 6.074366e+01, 'w_ple_proj': 2.469473e+00, 'w_ple_gate': 9.559665e-01}


def _to_microbatches(a, axis):
    t = _jnp.moveaxis(a, axis, 0)
    t = t.reshape((N_MICROBATCH, t.shape[0] // N_MICROBATCH) + t.shape[1:])
    return _jnp.moveaxis(t, 1, axis + 1)


def setup_inputs(seed: int = 0) -> dict:
    inp = _fwd_setup_inputs(seed)
    key = _jax.random.fold_in(_jax.random.key(seed), 7919)
    shape, _ = _output_shape()
    out = dict(inp)
    out["loss_target"] = _jax.random.normal(_jax.random.fold_in(key, 0), shape, _jnp.float32)
    for i, name in enumerate(TWIN_WEIGHTS):
        w = inp[name].astype(_jnp.float32)
        if MOMENT_SCALE is None:
            s = _jnp.sqrt(_jnp.mean(_jnp.square(w)) + 1e-30)
        else:
            s = MOMENT_SCALE[name]
        km, kv = _jax.random.split(_jax.random.fold_in(key, i + 1))
        out[name] = w
        out["m_" + name] = s * _jax.random.normal(km, w.shape, _jnp.float32)
        out["v_" + name] = (s * s) * _jax.random.uniform(kv, w.shape, _jnp.float32, 0.5, 1.5)
    if N_MICROBATCH > 1:
        for name, axis in PER_EXAMPLE_BATCH_AXIS.items():
            out[name] = _to_microbatches(out[name], axis)
    return {'x': out['x'], 'p': out['p'], 'norm_g': out['norm_g'], 'w_attn_in': out['w_attn_in'], 'b_forget': out['b_forget'], 'w_attn_out': out['w_attn_out'], 'w_conv_in': out['w_conv_in'], 'conv_w': out['conv_w'], 'w_conv_out': out['w_conv_out'], 'w_mlp_up': out['w_mlp_up'], 'w_mlp_down': out['w_mlp_down'], 'w_ple_proj': out['w_ple_proj'], 'w_ple_gate': out['w_ple_gate'], 'loss_target': out['loss_target'], 'm_norm_g': out['m_norm_g'], 'm_w_attn_in': out['m_w_attn_in'], 'm_b_forget': out['m_b_forget'], 'm_w_attn_out': out['m_w_attn_out'], 'm_w_conv_in': out['m_w_conv_in'], 'm_conv_w': out['m_conv_w'], 'm_w_conv_out': out['m_w_conv_out'], 'm_w_mlp_up': out['m_w_mlp_up'], 'm_w_mlp_down': out['m_w_mlp_down'], 'm_w_ple_proj': out['m_w_ple_proj'], 'm_w_ple_gate': out['m_w_ple_gate'], 'v_norm_g': out['v_norm_g'], 'v_w_attn_in': out['v_w_attn_in'], 'v_b_forget': out['v_b_forget'], 'v_w_attn_out': out['v_w_attn_out'], 'v_w_conv_in': out['v_w_conv_in'], 'v_conv_w': out['v_conv_w'], 'v_w_conv_out': out['v_w_conv_out'], 'v_w_mlp_up': out['v_w_mlp_up'], 'v_w_mlp_down': out['v_w_mlp_down'], 'v_w_ple_proj': out['v_w_ple_proj'], 'v_w_ple_gate': out['v_w_ple_gate']}


def _loss(weights, diff, rest, loss_target):
    with _jax.named_scope("forward"):
        args = {**rest, TWIN_DIFF_INPUT: diff, **{k: w.astype(_WEIGHT_DTYPES[k]) for k, w in weights.items()}}
        y = _forward(args)
    with _jax.named_scope("loss_head"):
        err = _jnp.square(y.astype(_jnp.float32) - loss_target)
        return 0.5 * _jnp.sum(_jnp.mean(err, axis=-1)) if err.ndim else 0.5 * err


def _adamw(w, g, m, v):
    m = ADAM_B1 * m + (1.0 - ADAM_B1) * g
    v = ADAM_B2 * v + (1.0 - ADAM_B2) * _jnp.square(g)
    m_hat = m / (1.0 - ADAM_B1 ** ADAM_STEP)
    v_hat = v / (1.0 - ADAM_B2 ** ADAM_STEP)
    delta = -ADAM_LR * (m_hat / (_jnp.sqrt(v_hat) + ADAM_EPS) + ADAM_WD * w)
    return delta, m, v


def reference(x, p, norm_g, w_attn_in, b_forget, w_attn_out, w_conv_in, conv_w, w_conv_out, w_mlp_up, w_mlp_down, w_ple_proj, w_ple_gate, loss_target, m_norm_g, m_w_attn_in, m_b_forget, m_w_attn_out, m_w_conv_in, m_conv_w, m_w_conv_out, m_w_mlp_up, m_w_mlp_down, m_w_ple_proj, m_w_ple_gate, v_norm_g, v_w_attn_in, v_b_forget, v_w_attn_out, v_w_conv_in, v_conv_w, v_w_conv_out, v_w_mlp_up, v_w_mlp_down, v_w_ple_proj, v_w_ple_gate):
    given = dict(x=x, p=p, norm_g=norm_g, w_attn_in=w_attn_in, b_forget=b_forget, w_attn_out=w_attn_out, w_conv_in=w_conv_in, conv_w=conv_w, w_conv_out=w_conv_out, w_mlp_up=w_mlp_up, w_mlp_down=w_mlp_down, w_ple_proj=w_ple_proj, w_ple_gate=w_ple_gate, loss_target=loss_target, m_norm_g=m_norm_g, m_w_attn_in=m_w_attn_in, m_b_forget=m_b_forget, m_w_attn_out=m_w_attn_out, m_w_conv_in=m_w_conv_in, m_conv_w=m_conv_w, m_w_conv_out=m_w_conv_out, m_w_mlp_up=m_w_mlp_up, m_w_mlp_down=m_w_mlp_down, m_w_ple_proj=m_w_ple_proj, m_w_ple_gate=m_w_ple_gate, v_norm_g=v_norm_g, v_w_attn_in=v_w_attn_in, v_b_forget=v_b_forget, v_w_attn_out=v_w_attn_out, v_w_conv_in=v_w_conv_in, v_conv_w=v_conv_w, v_w_conv_out=v_w_conv_out, v_w_mlp_up=v_w_mlp_up, v_w_mlp_down=v_w_mlp_down, v_w_ple_proj=v_w_ple_proj, v_w_ple_gate=v_w_ple_gate)
    weights = {n: given[n] for n in TWIN_WEIGHTS}
    shared = {n: given[n] for n in SHARED_INPUTS}
    per_example = {n: given[n] for n in ['x', 'p']}
    grad_fn = _jax.value_and_grad(_loss, argnums=(0, 1))

    def one_microbatch(ex, loss_target):
        ex = dict(ex)
        diff = ex.pop(TWIN_DIFF_INPUT)
        return grad_fn(weights, diff, {**shared, **ex}, loss_target)

    if N_MICROBATCH == 1:
        loss, (grad_w, grad_x) = one_microbatch(per_example, given["loss_target"])
    else:
        def body(carry, xs):
            loss_sum, grad_sum = carry
            l_k, (gw_k, gx_k) = one_microbatch(xs[0], xs[1])
            with _jax.named_scope("update"):
                return (loss_sum + l_k, _jax.tree.map(_jnp.add, grad_sum, gw_k)), gx_k

        init = (_jnp.zeros((), _jnp.float32), _jax.tree.map(_jnp.zeros_like, weights))
        (loss, grad_w), grad_x = _jax.lax.scan(body, init, (per_example, given["loss_target"]))
    with _jax.named_scope("update"):
        delta_w, new_m, new_v = {}, {}, {}
        for n in TWIN_WEIGHTS:
            delta_w[n], new_m[n], new_v[n] = _adamw(weights[n], grad_w[n], given["m_" + n], given["v_" + n])
    return (loss, grad_x, *[grad_w[n] for n in TWIN_WEIGHTS], *[delta_w[n] for n in TWIN_WEIGHTS],
            *[new_m[n] for n in TWIN_WEIGHTS], *[new_v[n] for n in TWIN_WEIGHTS])
```

```python
import functools

import jax
import jax.numpy as jnp
from jax import lax
from jax.experimental import pallas as pl
from jax.experimental.pallas import tpu as pltpu

F32 = jnp.float32
BF16 = jnp.bfloat16

N_DEV = 8
LANES = 128
HEAD_DIM = 64
VMEM_LIMIT_BYTES = 56 * 1024 * 1024
RMS_EPS = 1e-6
NEG_INF = -1e30
ADAM_LR = 0.001
ADAM_B1 = 0.9
ADAM_B2 = 0.999
ADAM_EPS = 1e-08
ADAM_WD = 0.01
ADAM_STEP = 10
WEIGHT_NAMES = ('norm_g', 'w_attn_in', 'b_forget', 'w_attn_out', 'w_conv_in', 'conv_w', 'w_conv_out',
                'w_mlp_up', 'w_mlp_down', 'w_ple_proj', 'w_ple_gate')
SHARD_AXIS = {'norm_g': 2, 'w_attn_in': 2, 'b_forget': None, 'w_attn_out': 1, 'w_conv_in': 2, 'conv_w': 2,
              'w_conv_out': 1, 'w_mlp_up': 2, 'w_mlp_down': 1, 'w_ple_proj': 2, 'w_ple_gate': 1}
FLAT_COLS = 1024
FLAT_ROW_TILE = 512


def _params(**kw):
    return pltpu.CompilerParams(vmem_limit_bytes=VMEM_LIMIT_BYTES, **kw)


def _tile(n, cap):
    if n <= cap:
        return n
    t = (cap // LANES) * LANES
    while n % t:
        t -= LANES
    return t


def _mm(a, b, *, ta=False, tb=False, extras=(), epi=None, out_dtypes=(F32,), name):
    m_dim, k_dim = (a.shape[1], a.shape[0]) if ta else a.shape
    n_dim = b.shape[0] if tb else b.shape[1]
    assert k_dim == (b.shape[1] if tb else b.shape[0])
    tk = _tile(k_dim, 1024)
    nk = k_dim // tk
    simple = not extras and len(out_dtypes) == 1
    tm = _tile(m_dim, 1024 if (nk > 1 and simple) else 512)
    tn = _tile(n_dim, 1024)
    grid = (n_dim // tn, m_dim // tm, nk)
    a_spec = (pl.BlockSpec((tk, tm), lambda j, i, k: (k, i)) if ta
              else pl.BlockSpec((tm, tk), lambda j, i, k: (i, k)))
    b_spec = (pl.BlockSpec((tn, tk), lambda j, i, k: (j, k)) if tb
              else pl.BlockSpec((tk, tn), lambda j, i, k: (k, j)))
    mn_spec = pl.BlockSpec((tm, tn), lambda j, i, k: (i, j))
    dims = (((0 if ta else 1,), (1 if tb else 0,)), ((), ()))
    n_extra, n_out = len(extras), len(out_dtypes)
    if epi is None:
        epi = lambda acc: (acc,)

    def body(a_ref, b_ref, *rest):
        e_refs, o_refs = rest[:n_extra], rest[n_extra:n_extra + n_out]
        part = lax.dot_general(a_ref[...].astype(BF16), b_ref[...].astype(BF16), dims,
                               preferred_element_type=F32)

        def finish(acc):
            for o_ref, val in zip(o_refs, epi(acc, *[e[...] for e in e_refs])):
                o_ref[...] = val.astype(o_ref.dtype)

        if nk == 1:
            finish(part)
        else:
            acc_ref = rest[-1]
            k = pl.program_id(2)

            @pl.when(k == 0)
            def _():
                acc_ref[...] = part

            @pl.when(k > 0)
            def _():
                acc_ref[...] += part

            @pl.when(k == nk - 1)
            def _():
                finish(acc_ref[...])

    outs = pl.pallas_call(
        body, name=name, grid=grid,
        in_specs=[a_spec, b_spec] + [mn_spec] * n_extra,
        out_specs=[mn_spec] * n_out,
        out_shape=[jax.ShapeDtypeStruct((m_dim, n_dim), dt) for dt in out_dtypes],
        scratch_shapes=[pltpu.VMEM((tm, tn), F32)] if nk > 1 else [],
        compiler_params=_params(dimension_semantics=("parallel", "parallel", "arbitrary")),
    )(a, b, *extras)
    return outs[0] if n_out == 1 else outs


def _rows(fn, row_ins, vec_ins, row_outs, vec_outs, *, name, tt=256, reverse=False):
    t_dim = row_ins[0].shape[0]
    tt = min(tt, t_dim)
    n = t_dim // tt
    n_ri, n_vi, n_ro, n_vo = len(row_ins), len(vec_ins), len(row_outs), len(vec_outs)
    pos = (lambda i: (n - 1 - i, 0)) if reverse else (lambda i: (i, 0))
    fixed = lambda i: (0, 0)

    def body(*refs):
        ri = refs[:n_ri]
        vi = refs[n_ri:n_ri + n_vi]
        ro = refs[n_ri + n_vi:n_ri + n_vi + n_ro]
        vo = refs[n_ri + n_vi + n_ro:n_ri + n_vi + n_ro + n_vo]
        scratch = refs[n_ri + n_vi + n_ro + n_vo:]
        r_out, v_out = fn([r[...] for r in ri], [v[...] for v in vi], *scratch)
        for o_ref, val in zip(ro, r_out):
            o_ref[...] = val.astype(o_ref.dtype)
        i = pl.program_id(0)
        for o_ref, val in zip(vo, v_out):
            @pl.when(i == 0)
            def _(o_ref=o_ref, val=val):
                o_ref[...] = val

            @pl.when(i > 0)
            def _(o_ref=o_ref, val=val):
                o_ref[...] += val

    return body, dict(
        grid=(n,),
        in_specs=[pl.BlockSpec((tt, r.shape[1]), pos) for r in row_ins]
        + [pl.BlockSpec(v.shape, fixed) for v in vec_ins],
        out_specs=[pl.BlockSpec((tt, w), pos) for w, _ in row_outs]
        + [pl.BlockSpec(s, fixed) for s in vec_outs],
        out_shape=[jax.ShapeDtypeStruct((t_dim, w), dt) for w, dt in row_outs]
        + [jax.ShapeDtypeStruct(s, F32) for s in vec_outs],
        name=name,
        compiler_params=_params(dimension_semantics=("arbitrary",)),
    )


def _rows_call(fn, row_ins, vec_ins, row_outs, vec_outs, *, name, tt=256, reverse=False, scratch=()):
    body, kw = _rows(fn, row_ins, vec_ins, row_outs, vec_outs, name=name, tt=tt, reverse=reverse)
    return pl.pallas_call(body, scratch_shapes=list(scratch), **kw)(*row_ins, *vec_ins)


def _rstd(x):
    return lax.rsqrt(jnp.mean(x * x, axis=-1, keepdims=True) + RMS_EPS)


def _norm(x, g):
    return x * _rstd(x) * g


def _norm_bwd(x, g, dy):
    xh = x * _rstd(x)
    gy = dy * g
    dx = _rstd(x) * (gy - xh * jnp.mean(gy * xh, axis=-1, keepdims=True))
    return dx, jnp.sum(dy * xh, axis=0, keepdims=True)


def _sigmoid(x):
    return 1.0 / (1.0 + jnp.exp(-x))


def _log_sigmoid(x):
    return jnp.minimum(x, 0.0) - jnp.log(1.0 + jnp.exp(-jnp.abs(x)))


def _split3(x):
    hi = x.astype(BF16)
    r1 = x - hi.astype(F32)
    mid = r1.astype(BF16)
    lo = (r1 - mid.astype(F32)).astype(BF16)
    return hi, mid, lo


def _cumsum_fwd(fl, bias, *, name):
    w = fl.shape[1]
    tt = min(512, fl.shape[0])

    def fn(rows, vecs, carry_ref):
        i = pl.program_id(0)

        @pl.when(i == 0)
        def _():
            carry_ref[...] = jnp.zeros_like(carry_ref)

        lf = _log_sigmoid(rows[0] + vecs[0])
        r = lax.broadcasted_iota(jnp.int32, (tt, tt), 0)
        c = lax.broadcasted_iota(jnp.int32, (tt, tt), 1)
        tri = (c <= r).astype(BF16)
        acc = carry_ref[0:1, :]
        for part in _split3(lf):
            acc = acc + jnp.dot(tri, part, preferred_element_type=F32)
        carry_ref[0:1, :] = acc[tt - 1:tt, :]
        return [acc], []

    return _rows_call(fn, [fl], [bias], [(w, F32)], [], name=name, tt=tt,
                      scratch=[pltpu.VMEM((8, w), F32)])[0]


def _cumsum_bwd(dc, fl, bias, *, name):
    w = fl.shape[1]
    tt = min(512, fl.shape[0])

    def fn(rows, vecs, carry_ref):
        i = pl.program_id(0)

        @pl.when(i == 0)
        def _():
            carry_ref[...] = jnp.zeros_like(carry_ref)

        r = lax.broadcasted_iota(jnp.int32, (tt, tt), 0)
        c = lax.broadcasted_iota(jnp.int32, (tt, tt), 1)
        tri = (c >= r).astype(BF16)
        acc = carry_ref[0:1, :]
        for part in _split3(rows[0]):
            acc = acc + jnp.dot(tri, part, preferred_element_type=F32)
        carry_ref[0:1, :] = acc[0:1, :]
        dfl = acc * _sigmoid(-(rows[1] + vecs[0]))
        return [dfl], [jnp.sum(dfl, axis=0, keepdims=True)]

    return _rows_call(fn, [dc, fl], [bias], [(w, BF16)], [(1, w)], name=name, tt=tt, reverse=True,
                      scratch=[pltpu.VMEM((8, w), F32)])


def _head_masks(tb):
    lane = lax.broadcasted_iota(jnp.int32, (tb, LANES), 1)
    return [lane < HEAD_DIM, lane >= HEAD_DIM]


def _flash_fwd(qkv, c_hb, c_rows, *, tb, name):
    t_dim = qkv.shape[0]
    d = qkv.shape[1] // 3
    heads = d // HEAD_DIM
    cb = d // LANES
    scale = HEAD_DIM ** -0.5

    def body(q_ref, k_ref, v_ref, cc_ref, cr_ref, o_ref, lse_ref):
        i = pl.program_id(1)
        q = q_ref[...]
        masks = _head_masks(tb)
        row = lax.broadcasted_iota(jnp.int32, (tb, tb), 0)
        col = lax.broadcasted_iota(jnp.int32, (tb, tb), 1)
        outs = []
        for e in range(2):
            qe = jnp.where(masks[e], q, jnp.zeros_like(q))
            ccol = cc_ref[0, e][:, 0:1]

            def step(j, carry, diagonal, e=e, qe=qe, ccol=ccol):
                m, l, acc = carry
                off = pl.multiple_of(j * tb, tb)
                kj = k_ref[pl.ds(off, tb), :]
                vj = v_ref[pl.ds(off, tb), :]
                crow = cr_ref[0, e, :, pl.ds(off, tb)]
                s = lax.dot_general(qe, kj, (((1,), (1,)), ((), ())), preferred_element_type=F32) * scale
                s = s + (ccol - crow)
                if diagonal:
                    s = jnp.where(col <= row, s, NEG_INF)
                m_new = jnp.maximum(m, jnp.max(s, axis=1, keepdims=True))
                p = jnp.exp(s - m_new)
                alpha = jnp.exp(m - m_new)
                l = alpha * l + jnp.sum(p, axis=1, keepdims=True)
                acc = alpha * acc + jnp.dot(p.astype(BF16), vj, preferred_element_type=F32)
                return m_new, l, acc

            init = (jnp.full((tb, 1), NEG_INF, F32), jnp.zeros((tb, 1), F32), jnp.zeros((tb, LANES), F32))
            carry = lax.fori_loop(0, i, functools.partial(step, diagonal=False), init)
            m, l, acc = step(i, carry, True)
            outs.append(acc / l)
            lse_ref[0, e] = jnp.broadcast_to(m + jnp.log(l), (tb, LANES))
        o_ref[...] = jnp.where(masks[0], outs[0], outs[1]).astype(o_ref.dtype)

    hb_spec = pl.BlockSpec((1, 2, tb, LANES), lambda h, i: (h, 0, i, 0))
    row_spec = pl.BlockSpec((1, 2, 1, t_dim), lambda h, i: (h, 0, 0, 0))
    return pl.pallas_call(
        body, name=name, grid=(heads // 2, t_dim // tb),
        in_specs=[pl.BlockSpec((tb, LANES), lambda h, i: (i, h)),
                  pl.BlockSpec((t_dim, LANES), lambda h, i: (0, cb + h)),
                  pl.BlockSpec((t_dim, LANES), lambda h, i: (0, 2 * cb + h)),
                  hb_spec, row_spec],
        out_specs=[pl.BlockSpec((tb, LANES), lambda h, i: (i, h)), hb_spec],
        out_shape=[jax.ShapeDtypeStruct((t_dim, d), BF16),
                   jax.ShapeDtypeStruct((heads // 2, 2, t_dim, LANES), F32)],
        compiler_params=_params(dimension_semantics=("parallel", "arbitrary")),
    )(qkv, qkv, qkv, c_hb, c_rows)


def _flash_dq(qkv, o, do, c_hb, c_rows, lse_hb, *, tb, name):
    t_dim = qkv.shape[0]
    d = qkv.shape[1] // 3
    heads = d // HEAD_DIM
    cb = d // LANES
    scale = HEAD_DIM ** -0.5

    def body(q_ref, k_ref, v_ref, o_ref, do_ref, cc_ref, cr_ref, lse_ref, dq_ref, dl_ref, rs_ref):
        i = pl.program_id(1)
        q = q_ref[...]
        do_blk = do_ref[...]
        prod = do_blk.astype(F32) * o_ref[...].astype(F32)
        masks = _head_masks(tb)
        row = lax.broadcasted_iota(jnp.int32, (tb, tb), 0)
        col = lax.broadcasted_iota(jnp.int32, (tb, tb), 1)
        outs = []
        for e in range(2):
            qe = jnp.where(masks[e], q, jnp.zeros_like(q))
            doe = jnp.where(masks[e], do_blk, jnp.zeros_like(do_blk))
            delta = jnp.sum(jnp.where(masks[e], prod, 0.0), axis=1, keepdims=True)
            dl_ref[0, e] = jnp.broadcast_to(delta, (tb, LANES))
            ccol = cc_ref[0, e][:, 0:1]
            lse = lse_ref[0, e][:, 0:1]

            def step(j, carry, diagonal, e=e, qe=qe, doe=doe, delta=delta, ccol=ccol, lse=lse):
                acc, rsum = carry
                off = pl.multiple_of(j * tb, tb)
                kj = k_ref[pl.ds(off, tb), :]
                vj = v_ref[pl.ds(off, tb), :]
                crow = cr_ref[0, e, :, pl.ds(off, tb)]
                s = lax.dot_general(qe, kj, (((1,), (1,)), ((), ())), preferred_element_type=F32) * scale
                s = s + (ccol - crow)
                if diagonal:
                    s = jnp.where(col <= row, s, NEG_INF)
                p = jnp.exp(s - lse)
                dp = lax.dot_general(doe, vj, (((1,), (1,)), ((), ())), preferred_element_type=F32)
                ds = p * (dp - delta)
                return (acc + jnp.dot(ds.astype(BF16), kj, preferred_element_type=F32),
                        rsum + jnp.sum(ds, axis=1, keepdims=True))

            init = (jnp.zeros((tb, LANES), F32), jnp.zeros((tb, 1), F32))
            acc, rsum = step(i, lax.fori_loop(0, i, functools.partial(step, diagonal=False), init), True)
            outs.append(acc)
            rs_ref[0, e] = jnp.broadcast_to(rsum, (tb, LANES))
        dq_ref[...] = (jnp.where(masks[0], outs[0], outs[1]) * scale).astype(dq_ref.dtype)

    blk = pl.BlockSpec((tb, LANES), lambda h, i: (i, h))
    hb_spec = pl.BlockSpec((1, 2, tb, LANES), lambda h, i: (h, 0, i, 0))
    row_spec = pl.BlockSpec((1, 2, 1, t_dim), lambda h, i: (h, 0, 0, 0))
    hb_shape = jax.ShapeDtypeStruct((heads // 2, 2, t_dim, LANES), F32)
    return pl.pallas_call(
        body, name=name, grid=(heads // 2, t_dim // tb),
        in_specs=[blk,
                  pl.BlockSpec((t_dim, LANES), lambda h, i: (0, cb + h)),
                  pl.BlockSpec((t_dim, LANES), lambda h, i: (0, 2 * cb + h)),
                  blk, blk, hb_spec, row_spec, hb_spec],
        out_specs=[blk, hb_spec, hb_spec],
        out_shape=[jax.ShapeDtypeStruct((t_dim, d), BF16), hb_shape, hb_shape],
        compiler_params=_params(dimension_semantics=("parallel", "arbitrary")),
    )(qkv, qkv, qkv, o, do, c_hb, c_rows, lse_hb)


def _flash_dkv(qkv, do, c_hb, c_rows, lse_rows, delta_rows, *, tb, name):
    t_dim = qkv.shape[0]
    d = qkv.shape[1] // 3
    heads = d // HEAD_DIM
    cb = d // LANES
    scale = HEAD_DIM ** -0.5
    nq = t_dim // tb

    def body(q_ref, k_ref, v_ref, do_ref, cc_ref, cr_ref, lr_ref, dr_ref, dk_ref, dv_ref, dsum_ref):
        j = pl.program_id(1)
        k_blk = k_ref[...]
        v_blk = v_ref[...]
        masks = _head_masks(tb)
        row = lax.broadcasted_iota(jnp.int32, (tb, tb), 0)
        col = lax.broadcasted_iota(jnp.int32, (tb, tb), 1)
        dks, dvs = [], []
        for e in range(2):
            ke = jnp.where(masks[e], k_blk, jnp.zeros_like(k_blk))
            ve = jnp.where(masks[e], v_blk, jnp.zeros_like(v_blk))
            ccol = cc_ref[0, e][:, 0:1]

            def step(i, carry, diagonal, e=e, ke=ke, ve=ve, ccol=ccol):
                dk, dv, dsum = carry
                off = pl.multiple_of(i * tb, tb)
                qi = q_ref[pl.ds(off, tb), :]
                doi = do_ref[pl.ds(off, tb), :]
                crow = cr_ref[0, e, :, pl.ds(off, tb)]
                lse = lr_ref[0, e, :, pl.ds(off, tb)]
                delta = dr_ref[0, e, :, pl.ds(off, tb)]
                st = lax.dot_general(ke, qi, (((1,), (1,)), ((), ())), preferred_element_type=F32) * scale
                st = st + (crow - ccol)
                if diagonal:
                    st = jnp.where(col >= row, st, NEG_INF)
                pt = jnp.exp(st - lse)
                dpt = lax.dot_general(ve, doi, (((1,), (1,)), ((), ())), preferred_element_type=F32)
                dst = pt * (dpt - delta)
                dv = dv + jnp.dot(pt.astype(BF16), doi, preferred_element_type=F32)
                dk = dk + jnp.dot(dst.astype(BF16), qi, preferred_element_type=F32)
                return dk, dv, dsum + jnp.sum(dst, axis=1, keepdims=True)

            zero = jnp.zeros((tb, LANES), F32)
            carry = step(j, (zero, zero, jnp.zeros((tb, 1), F32)), True)
            dk, dv, dsum = lax.fori_loop(j + 1, nq, functools.partial(step, diagonal=False), carry)
            dks.append(dk)
            dvs.append(dv)
            dsum_ref[0, e] = jnp.broadcast_to(dsum, (tb, LANES))
        dk_ref[...] = (jnp.where(masks[0], dks[0], dks[1]) * scale).astype(dk_ref.dtype)
        dv_ref[...] = jnp.where(masks[0], dvs[0], dvs[1]).astype(dv_ref.dtype)

    blk = pl.BlockSpec((tb, LANES), lambda h, j: (j, h))
    hb_spec = pl.BlockSpec((1, 2, tb, LANES), lambda h, j: (h, 0, j, 0))
    row_spec = pl.BlockSpec((1, 2, 1, t_dim), lambda h, j: (h, 0, 0, 0))
    return pl.pallas_call(
        body, name=name, grid=(heads // 2, nq),
        in_specs=[pl.BlockSpec((t_dim, LANES), lambda h, j: (0, h)),
                  pl.BlockSpec((tb, LANES), lambda h, j: (j, cb + h)),
                  pl.BlockSpec((tb, LANES), lambda h, j: (j, 2 * cb + h)),
                  pl.BlockSpec((t_dim, LANES), lambda h, j: (0, h)),
                  hb_spec, row_spec, row_spec, row_spec],
        out_specs=[blk, blk, hb_spec],
        out_shape=[jax.ShapeDtypeStruct((t_dim, d), BF16), jax.ShapeDtypeStruct((t_dim, d), BF16),
                   jax.ShapeDtypeStruct((heads // 2, 2, t_dim, LANES), F32)],
        compiler_params=_params(dimension_semantics=("parallel", "arbitrary")),
    )(qkv, qkv, qkv, do, c_hb, c_rows, lse_rows, delta_rows)


def _rows_of(hb):
    pairs, _, t_dim, _ = hb.shape
    return hb[:, :, :, 0].reshape(pairs, 2, 1, t_dim)


def _shift_down(z, prev, n, tt):
    out = pltpu.roll(z, n, axis=0)
    row = lax.broadcasted_iota(jnp.int32, z.shape, 0)
    for r in range(n):
        out = jnp.where(row == r, prev[8 - n + r:8 - n + r + 1, :], out)
    return out


def _shift_up(z, nxt, n, tt):
    out = pltpu.roll(z, tt - n, axis=0)
    row = lax.broadcasted_iota(jnp.int32, z.shape, 0)
    for r in range(n):
        out = jnp.where(row == tt - n + r, nxt[r:r + 1, :], out)
    return out


def _conv_fwd(proj, conv_w, *, name, tt=256):
    t_dim, d3 = proj.shape
    d = d3 // 3
    tt = min(tt, t_dim)

    def body(p_ref, prev_ref, w_ref, y_ref):
        i = pl.program_id(0)
        p = p_ref[...]
        pp = prev_ref[...]
        z = p[:, d:2 * d] * p[:, 2 * d:]
        zp = jnp.where(i > 0, pp[:, d:2 * d] * pp[:, 2 * d:], 0.0)
        w = w_ref[...]
        zc = w[2:3, :] * z + w[1:2, :] * _shift_down(z, zp, 1, tt) + w[0:1, :] * _shift_down(z, zp, 2, tt)
        y_ref[...] = (p[:, :d] * zc).astype(y_ref.dtype)

    return pl.pallas_call(
        body, name=name, grid=(t_dim // tt,),
        in_specs=[pl.BlockSpec((tt, d3), lambda i: (i, 0)),
                  pl.BlockSpec((8, d3), lambda i: (jnp.maximum(i * (tt // 8) - 1, 0), 0)),
                  pl.BlockSpec(conv_w.shape, lambda i: (0, 0))],
        out_specs=pl.BlockSpec((tt, d), lambda i: (i, 0)),
        out_shape=jax.ShapeDtypeStruct((t_dim, d), BF16),
        compiler_params=_params(dimension_semantics=("arbitrary",)),
    )(proj, proj, conv_w)


def _conv_bwd(proj, dy, conv_w, *, name, tt=256):
    t_dim, d3 = proj.shape
    d = d3 // 3
    tt = min(tt, t_dim)
    n = t_dim // tt

    def body(p_ref, prev_ref, next_ref, dy_ref, dyn_ref, w_ref, dp_ref, dw_ref):
        i = pl.program_id(0)
        p = p_ref[...]
        pp = prev_ref[...]
        pn = next_ref[...]
        bg, cg, u = p[:, :d], p[:, d:2 * d], p[:, 2 * d:]
        z = cg * u
        zp = jnp.where(i > 0, pp[:, d:2 * d] * pp[:, 2 * d:], 0.0)
        w = w_ref[...]
        z1 = _shift_down(z, zp, 1, tt)
        z2 = _shift_down(z, zp, 2, tt)
        zc = w[2:3, :] * z + w[1:2, :] * z1 + w[0:1, :] * z2
        dy_blk = dy_ref[...]
        dzc = dy_blk * bg
        dzn = jnp.where(i < n - 1, dyn_ref[...] * pn[:, :d], 0.0)
        dz = w[2:3, :] * dzc + w[1:2, :] * _shift_up(dzc, dzn, 1, tt) + w[0:1, :] * _shift_up(dzc, dzn, 2, tt)
        dp_ref[:, :d] = (dy_blk * zc).astype(dp_ref.dtype)
        dp_ref[:, d:2 * d] = (dz * u).astype(dp_ref.dtype)
        dp_ref[:, 2 * d:] = (dz * cg).astype(dp_ref.dtype)
        part = jnp.concatenate([jnp.sum(dzc * z2, axis=0, keepdims=True),
                                jnp.sum(dzc * z1, axis=0, keepdims=True),
                                jnp.sum(dzc * z, axis=0, keepdims=True),
                                jnp.zeros((5, d), F32)], axis=0)

        @pl.when(i == 0)
        def _():
            dw_ref[...] = part

        @pl.when(i > 0)
        def _():
            dw_ref[...] += part

    last8 = t_dim // 8 - 1
    return pl.pallas_call(
        body, name=name, grid=(n,),
        in_specs=[pl.BlockSpec((tt, d3), lambda i: (i, 0)),
                  pl.BlockSpec((8, d3), lambda i: (jnp.maximum(i * (tt // 8) - 1, 0), 0)),
                  pl.BlockSpec((8, d3), lambda i: (jnp.minimum((i + 1) * (tt // 8), last8), 0)),
                  pl.BlockSpec((tt, d), lambda i: (i, 0)),
                  pl.BlockSpec((8, d), lambda i: (jnp.minimum((i + 1) * (tt // 8), last8), 0)),
                  pl.BlockSpec(conv_w.shape, lambda i: (0, 0))],
        out_specs=[pl.BlockSpec((tt, d3), lambda i: (i, 0)), pl.BlockSpec((8, d), lambda i: (0, 0))],
        out_shape=[jax.ShapeDtypeStruct((t_dim, d3), BF16), jax.ShapeDtypeStruct((8, d), F32)],
        compiler_params=_params(dimension_semantics=("arbitrary",)),
    )(proj, proj, proj, dy, dy, conv_w)


def _exchange(src, *, gather, name):
    rows, cols = src.shape[-2:]

    def body(src_ref, dst_ref, send_sems, recv_sems, local_sem):
        x, y, c = lax.axis_index("x"), lax.axis_index("y"), lax.axis_index("c")
        me = 4 * x + 2 * y + c
        mine = src_ref if gather else src_ref.at[me]
        local = pltpu.make_async_copy(mine, dst_ref.at[me], local_sem)
        local.start()
        copies = []
        for k in range(1, N_DEV):
            px = 1 - x if k & 4 else x
            py = 1 - y if k & 2 else y
            pc = 1 - c if k & 1 else c
            cp = pltpu.make_async_remote_copy(
                src_ref=src_ref if gather else src_ref.at[4 * px + 2 * py + pc],
                dst_ref=dst_ref.at[me],
                send_sem=send_sems.at[k - 1], recv_sem=recv_sems.at[k - 1],
                device_id=(px, py, pc), device_id_type=pl.DeviceIdType.MESH)
            cp.start()
            copies.append(cp)
        for cp in copies:
            cp.wait_recv()
        for cp in copies:
            cp.wait_send()
        local.wait()

    return pl.pallas_call(
        body, name=name,
        in_specs=[pl.BlockSpec(memory_space=pl.ANY)],
        out_specs=pl.BlockSpec(memory_space=pl.ANY),
        out_shape=jax.ShapeDtypeStruct((N_DEV, rows, cols), src.dtype),
        scratch_shapes=[pltpu.SemaphoreType.DMA((N_DEV - 1,)), pltpu.SemaphoreType.DMA((N_DEV - 1,)),
                        pltpu.SemaphoreType.DMA],
    )(src)


def _adamw(recv, w, m, v, *, name):
    rows, cols = w.shape
    tt = min(FLAT_ROW_TILE, rows)

    def body(r_ref, w_ref, m_ref, v_ref, g_ref, d_ref, nm_ref, nv_ref):
        g = r_ref[0].astype(F32)
        for s in range(1, N_DEV):
            g = g + r_ref[s].astype(F32)
        m_new = ADAM_B1 * m_ref[...] + (1.0 - ADAM_B1) * g
        v_new = ADAM_B2 * v_ref[...] + (1.0 - ADAM_B2) * (g * g)
        m_hat = m_new / (1.0 - ADAM_B1 ** ADAM_STEP)
        v_hat = v_new / (1.0 - ADAM_B2 ** ADAM_STEP)
        g_ref[...] = g
        d_ref[...] = -ADAM_LR * (m_hat / (jnp.sqrt(v_hat) + ADAM_EPS) + ADAM_WD * w_ref[...])
        nm_ref[...] = m_new
        nv_ref[...] = v_new

    spec = pl.BlockSpec((tt, cols), lambda i: (i, 0))
    return pl.pallas_call(
        body, name=name, grid=(rows // tt,),
        in_specs=[pl.BlockSpec((N_DEV, tt, cols), lambda i: (0, i, 0)), spec, spec, spec],
        out_specs=[spec] * 4,
        out_shape=[jax.ShapeDtypeStruct((rows, cols), F32)] * 4,
        compiler_params=_params(dimension_semantics=("parallel",)),
    )(recv, w, m, v)


def _flat_rows(n_elems, cols, row_tile):
    rows = -(-n_elems // cols)
    return -(-rows // row_tile) * row_tile


def _pack(parts, cols, row_tile, dtype):
    flat = jnp.concatenate([p.reshape(-1).astype(dtype) for p in parts])
    rows = _flat_rows(flat.shape[0], cols, row_tile)
    return jnp.pad(flat, (0, rows * cols - flat.shape[0])).reshape(rows, cols)


def _unpack(flat2d, shapes):
    lead = flat2d.shape[:-2]
    flat = flat2d.reshape(*lead, -1)
    out, off = [], 0
    for shp in shapes:
        n = 1
        for s in shp:
            n *= s
        out.append(flat[..., off:off + n].reshape(*lead, *shp))
        off += n
    return out


def _full_from_blocks(blocks, axis):
    if axis is None:
        return blocks[0]
    moved = jnp.moveaxis(blocks, 0, axis)
    shp = list(moved.shape)
    return moved.reshape(shp[:axis] + [shp[axis] * shp[axis + 1]] + shp[axis + 2:])


def _blocks_from_full(full, axis):
    if axis is None:
        return jnp.broadcast_to(full, (N_DEV,) + full.shape)
    shp = list(full.shape)
    split = full.reshape(shp[:axis] + [N_DEV, shp[axis] // N_DEV] + shp[axis + 1:])
    return jnp.moveaxis(split, axis, 0)


def kernel(x, p, norm_g, w_attn_in, b_forget, w_attn_out, w_conv_in, conv_w, w_conv_out, w_mlp_up, w_mlp_down, w_ple_proj, w_ple_gate, loss_target, m_norm_g, m_w_attn_in, m_b_forget, m_w_attn_out, m_w_conv_in, m_conv_w, m_w_conv_out, m_w_mlp_up, m_w_mlp_down, m_w_ple_proj, m_w_ple_gate, v_norm_g, v_w_attn_in, v_b_forget, v_w_attn_out, v_w_conv_in, v_conv_w, v_w_conv_out, v_w_mlp_up, v_w_mlp_down, v_w_ple_proj, v_w_ple_gate):
    shards = dict(norm_g=norm_g, w_attn_in=w_attn_in, b_forget=b_forget, w_attn_out=w_attn_out,
                  w_conv_in=w_conv_in, conv_w=conv_w, w_conv_out=w_conv_out, w_mlp_up=w_mlp_up,
                  w_mlp_down=w_mlp_down, w_ple_proj=w_ple_proj, w_ple_gate=w_ple_gate)
    m_shards = dict(norm_g=m_norm_g, w_attn_in=m_w_attn_in, b_forget=m_b_forget, w_attn_out=m_w_attn_out,
                    w_conv_in=m_w_conv_in, conv_w=m_conv_w, w_conv_out=m_w_conv_out, w_mlp_up=m_w_mlp_up,
                    w_mlp_down=m_w_mlp_down, w_ple_proj=m_w_ple_proj, w_ple_gate=m_w_ple_gate)
    v_shards = dict(norm_g=v_norm_g, w_attn_in=v_w_attn_in, b_forget=v_b_forget, w_attn_out=v_w_attn_out,
                    w_conv_in=v_w_conv_in, conv_w=v_conv_w, w_conv_out=v_w_conv_out, w_mlp_up=v_w_mlp_up,
                    w_mlp_down=v_w_mlp_down, w_ple_proj=v_w_ple_proj, w_ple_gate=v_w_ple_gate)
    shard_shapes = [shards[n].shape for n in WEIGHT_NAMES]
    t_dim, d = x.shape[-2:]
    depth = p.shape[0]
    heads = b_forget.shape[1]
    assert d == heads * HEAD_DIM and x.shape[0] == 1
    tb = min(512, t_dim // 2)
    x0 = x.reshape(t_dim, d)
    target = loss_target.reshape(t_dim, d)

    small_names = ('norm_g', 'conv_w')
    gathered = _exchange(_pack([shards[n] for n in WEIGHT_NAMES], FLAT_COLS, FLAT_ROW_TILE, BF16),
                         gather=True, name="gather_weights")
    gathered_small = _exchange(_pack([shards[n] for n in small_names], LANES, 8, F32),
                               gather=True, name="gather_gains")
    full = {n: _full_from_blocks(blk, SHARD_AXIS[n])
            for n, blk in zip(WEIGHT_NAMES, _unpack(gathered, shard_shapes))}
    for n, blk in zip(small_names, _unpack(gathered_small, [shards[n].shape for n in small_names])):
        full[n] = _full_from_blocks(blk, SHARD_AXIS[n])
    gains = full['norm_g']
    taps = full['conv_w']
    w_in_pad = jnp.pad(full['w_attn_in'], ((0, 0), (0, 0), (0, LANES - heads)))
    bias_pad = jnp.pad(b_forget, ((0, 0), (0, LANES - heads)))

    def gain(i, k):
        return gains[i, k].reshape(1, d)

    def add_norm(x_prev, branch, g_branch, g_next, name):
        def fn(rows, vecs):
            x_new = rows[0] + _norm(rows[1], vecs[0])
            return [x_new, _norm(x_new, vecs[1])], []
        return _rows_call(fn, [x_prev, branch], [g_branch, g_next], [(d, F32), (d, BF16)], [], name=name)

    saved = []
    x_cur = x0
    hn = _rows_call(lambda rows, vecs: ([_norm(rows[0], vecs[0])], []), [x0], [gain(0, 0)], [(d, BF16)], [],
                    name="norm_in")[0]
    loss_rows = dy = None
    for i in range(depth):
        j = i // 2
        s = dict(x0=x_cur, hn=hn)
        if i % 2 == 0:
            w_in = w_in_pad[j]
            s['qkv'] = _mm(hn, w_in[:, :3 * d], out_dtypes=(BF16,), name=f"attn_in_{i}")
            s['fl'] = _mm(hn, w_in[:, 3 * d:], name=f"attn_gate_{i}")
            c = _cumsum_fwd(s['fl'], bias_pad[j:j + 1], name=f"gate_cumsum_{i}")
            c_t = c[:, :heads].T.reshape(heads // 2, 2, t_dim)
            s['c_hb'] = jnp.broadcast_to(c_t[:, :, :, None], (heads // 2, 2, t_dim, LANES))
            s['c_rows'] = c_t.reshape(heads // 2, 2, 1, t_dim)
            s['o'], s['lse_hb'] = _flash_fwd(s['qkv'], s['c_hb'], s['c_rows'], tb=tb, name=f"attn_fwd_{i}")
            s['m'] = _mm(s['o'], full['w_attn_out'][j], name=f"attn_out_{i}")
        else:
            s['proj'] = _mm(hn, full['w_conv_in'][j], name=f"conv_in_{i}")
            s['y'] = _conv_fwd(s['proj'], taps[j], name=f"conv_fwd_{i}")
            s['m'] = _mm(s['y'], full['w_conv_out'][j], name=f"conv_out_{i}")
        s['x1'], s['h2'] = add_norm(x_cur, s['m'], gain(i, 1), gain(i, 2), f"mix_norm_{i}")
        s['u'], s['a'] = _mm(s['h2'], full['w_mlp_up'][i], out_dtypes=(BF16, BF16), name=f"mlp_up_{i}",
                             epi=lambda acc: (acc, jnp.square(jnp.maximum(acc, 0.0))))
        s['f'] = _mm(s['a'], full['w_mlp_down'][i], name=f"mlp_down_{i}")
        s['x2'], s['h4'] = add_norm(s['x1'], s['f'], gain(i, 3), gain(i, 4), f"mlp_norm_{i}")
        s['pi'] = p[i].reshape(t_dim, -1)
        s['pp'] = _mm(s['pi'], full['w_ple_proj'][i], name=f"ple_proj_{i}")
        s['gl'], s['e'] = _mm(s['h4'], full['w_ple_gate'][i], extras=(s['pp'],), out_dtypes=(F32, F32),
                              name=f"ple_gate_{i}", epi=lambda acc, pp: (acc, pp * _sigmoid(acc)))
        if i + 1 < depth:
            x_cur, hn = add_norm(s['x2'], s['e'], gain(i, 5), gain(i + 1, 0), f"ple_norm_{i}")
        else:
            def loss_fn(rows, vecs):
                err = rows[0] + _norm(rows[1], vecs[0]) - rows[2]
                part = 0.5 * jnp.sum(jnp.sum(err * err, axis=1, keepdims=True), axis=0, keepdims=True) / d
                return [err / d], [jnp.broadcast_to(part, (1, LANES))]
            dy, loss_rows = _rows_call(loss_fn, [s['x2'], s['e'], target], [gain(i, 5)], [(d, F32)],
                                       [(1, LANES)], name="loss")
        saved.append(s)
    loss = lax.psum(loss_rows[0, 0], ("x", "y", "c"))

    grads = {n: [None] * full[n].shape[0] for n in WEIGHT_NAMES}
    d_gains = [[None] * 6 for _ in range(depth)]
    dx = dy
    for i in reversed(range(depth)):
        j = i // 2
        s = saved[i]

        def ple_fn(rows, vecs):
            de, dg = _norm_bwd(rows[0], vecs[0], rows[1])
            sg = _sigmoid(rows[2])
            return [de * sg, de * rows[3] * sg * (1.0 - sg)], [dg]
        dpp, dgl, d_gains[i][5] = _rows_call(ple_fn, [s['e'], dx, s['gl'], s['pp']], [gain(i, 5)],
                                             [(d, BF16), (d, BF16)], [(1, d)], name=f"ple_bwd_{i}")
        grads['w_ple_proj'][i] = _mm(s['pi'], dpp, ta=True, name=f"ple_proj_dw_{i}")
        grads['w_ple_gate'][i] = _mm(s['h4'], dgl, ta=True, name=f"ple_gate_dw_{i}")
        dh4 = _mm(dgl, full['w_ple_gate'][i], tb=True, name=f"ple_gate_dx_{i}")

        def two_norm_bwd(x_res, dh, dx_in, branch, g_res, g_branch, name):
            def fn(rows, vecs):
                d_res, dg_res = _norm_bwd(rows[0], vecs[0], rows[1])
                dx_out = rows[2] + d_res
                d_branch, dg_branch = _norm_bwd(rows[3], vecs[1], dx_out)
                return [dx_out, d_branch], [dg_res, dg_branch]
            return _rows_call(fn, [x_res, dh, dx_in, branch], [g_res, g_branch], [(d, F32), (d, BF16)],
                              [(1, d), (1, d)], name=name)

        dx2, df, d_gains[i][4], d_gains[i][3] = two_norm_bwd(s['x2'], dh4, dx, s['f'], gain(i, 4), gain(i, 3),
                                                            f"mlp_norm_bwd_{i}")
        grads['w_mlp_down'][i] = _mm(s['a'], df, ta=True, name=f"mlp_down_dw_{i}")
        du = _mm(df, full['w_mlp_down'][i], tb=True, extras=(s['u'],), out_dtypes=(BF16,), name=f"mlp_down_dx_{i}",
                 epi=lambda acc, u: (acc * (2.0 * jnp.maximum(u.astype(F32), 0.0)),))
        grads['w_mlp_up'][i] = _mm(s['h2'], du, ta=True, name=f"mlp_up_dw_{i}")
        dh2 = _mm(du, full['w_mlp_up'][i], tb=True, name=f"mlp_up_dx_{i}")
        dx1, dm, d_gains[i][2], d_gains[i][1] = two_norm_bwd(s['x1'], dh2, dx2, s['m'], gain(i, 2), gain(i, 1),
                                                            f"mix_norm_bwd_{i}")
        if i % 2 == 0:
            grads['w_attn_out'][j] = _mm(s['o'], dm, ta=True, name=f"attn_out_dw_{i}")
            do = _mm(dm, full['w_attn_out'][j], tb=True, out_dtypes=(BF16,), name=f"attn_out_dx_{i}")
            dq, delta_hb, rsum_hb = _flash_dq(s['qkv'], s['o'], do, s['c_hb'], s['c_rows'], s['lse_hb'], tb=tb,
                                              name=f"attn_dq_{i}")
            dk, dv, csum_hb = _flash_dkv(s['qkv'], do, s['c_hb'], s['c_rows'], _rows_of(s['lse_hb']),
                                         _rows_of(delta_hb), tb=tb, name=f"attn_dkv_{i}")
            dc = (rsum_hb[:, :, :, 0] - csum_hb[:, :, :, 0]).reshape(heads, t_dim).T
            dfl, db = _cumsum_bwd(jnp.pad(dc, ((0, 0), (0, LANES - heads))), s['fl'], bias_pad[j:j + 1],
                                  name=f"gate_cumsum_bwd_{i}")
            grads['b_forget'][j] = db[0, :heads]
            dproj = jnp.concatenate([dq, dk, dv, dfl], axis=1)
            grads['w_attn_in'][j] = _mm(s['hn'], dproj, ta=True, name=f"attn_in_dw_{i}")[:, :3 * d + heads]
            dhn = _mm(dproj, w_in_pad[j], tb=True, name=f"attn_in_dx_{i}")
        else:
            grads['w_conv_out'][j] = _mm(s['y'], dm, ta=True, name=f"conv_out_dw_{i}")
            dyc = _mm(dm, full['w_conv_out'][j], tb=True, name=f"conv_out_dx_{i}")
            dproj, dtaps = _conv_bwd(s['proj'], dyc, taps[j], name=f"conv_bwd_{i}")
            grads['conv_w'][j] = dtaps[:3]
            grads['w_conv_in'][j] = _mm(s['hn'], dproj, ta=True, name=f"conv_in_dw_{i}")
            dhn = _mm(dproj, full['w_conv_in'][j], tb=True, name=f"conv_in_dx_{i}")

        def in_fn(rows, vecs):
            d_res, dg = _norm_bwd(rows[0], vecs[0], rows[1])
            return [rows[2] + d_res], [dg]
        dx, d_gains[i][0] = _rows_call(in_fn, [s['x0'], dhn, dx1], [gain(i, 0)], [(d, F32)], [(1, d)],
                                       name=f"in_norm_bwd_{i}")
    grad_x = dx.reshape(x.shape)
    grads['norm_g'] = [jnp.concatenate(row, axis=0) for row in d_gains]
    grad_full = {n: jnp.stack(grads[n]) for n in WEIGHT_NAMES}

    send = jnp.concatenate([_blocks_from_full(grad_full[n], SHARD_AXIS[n]).reshape(N_DEV, -1).astype(BF16)
                            for n in WEIGHT_NAMES], axis=1)
    flat_rows = _flat_rows(send.shape[1], FLAT_COLS, FLAT_ROW_TILE)
    send = jnp.pad(send, ((0, 0), (0, flat_rows * FLAT_COLS - send.shape[1]))).reshape(N_DEV, flat_rows, FLAT_COLS)
    recv = _exchange(send, gather=False, name="exchange_grads")
    flat = lambda group: _pack([group[n] for n in WEIGHT_NAMES], FLAT_COLS, FLAT_ROW_TILE, F32)
    g_flat, delta_flat, m_flat, v_flat = _adamw(recv, flat(shards), flat(m_shards), flat(v_shards), name="adamw")
    outs = [_unpack(a, shard_shapes) for a in (g_flat, delta_flat, m_flat, v_flat)]
    return (loss, grad_x, *outs[0], *outs[1], *outs[2], *outs[3])
```

```python
import functools

import jax
import jax.numpy as jnp
from jax import lax
from jax.experimental import pallas as pl
from jax.experimental.pallas import tpu as pltpu

F32 = jnp.float32
BF16 = jnp.bfloat16

N_DEV = 8
LANES = 128
HEAD_DIM = 64
VMEM_LIMIT_BYTES = 56 * 1024 * 1024
RMS_EPS = 1e-6
NEG_INF = -1e30
ADAM_LR = 0.001
ADAM_B1 = 0.9
ADAM_B2 = 0.999
ADAM_EPS = 1e-08
ADAM_WD = 0.01
ADAM_STEP = 10
WEIGHT_NAMES = ('norm_g', 'w_attn_in', 'b_forget', 'w_attn_out', 'w_conv_in', 'conv_w', 'w_conv_out',
                'w_mlp_up', 'w_mlp_down', 'w_ple_proj', 'w_ple_gate')
SHARD_AXIS = {'norm_g': 2, 'w_attn_in': 2, 'b_forget': None, 'w_attn_out': 1, 'w_conv_in': 2, 'conv_w': 2,
              'w_conv_out': 1, 'w_mlp_up': 2, 'w_mlp_down': 1, 'w_ple_proj': 2, 'w_ple_gate': 1}
FLAT_COLS = 1024
FLAT_ROW_TILE = 512


def _params(**kw):
    return pltpu.CompilerParams(vmem_limit_bytes=VMEM_LIMIT_BYTES, **kw)


def _tile(n, cap):
    if n <= cap:
        return n
    t = (cap // LANES) * LANES
    while n % t:
        t -= LANES
    return t


def _mm(a, b, *, ta=False, tb=False, extras=(), epi=None, out_dtypes=(F32,), name):
    m_dim, k_dim = (a.shape[1], a.shape[0]) if ta else a.shape
    n_dim = b.shape[0] if tb else b.shape[1]
    assert k_dim == (b.shape[1] if tb else b.shape[0])
    tk = _tile(k_dim, 1024)
    nk = k_dim // tk
    simple = not extras and len(out_dtypes) == 1
    tm = _tile(m_dim, 1024 if (nk > 1 and simple) else 512)
    tn = _tile(n_dim, 1024)
    grid = (n_dim // tn, m_dim // tm, nk)
    a_spec = (pl.BlockSpec((tk, tm), lambda j, i, k: (k, i)) if ta
              else pl.BlockSpec((tm, tk), lambda j, i, k: (i, k)))
    b_spec = (pl.BlockSpec((tn, tk), lambda j, i, k: (j, k)) if tb
              else pl.BlockSpec((tk, tn), lambda j, i, k: (k, j)))
    mn_spec = pl.BlockSpec((tm, tn), lambda j, i, k: (i, j))
    dims = (((0 if ta else 1,), (1 if tb else 0,)), ((), ()))
    n_extra, n_out = len(extras), len(out_dtypes)
    if epi is None:
        epi = lambda acc: (acc,)

    def body(a_ref, b_ref, *rest):
        e_refs, o_refs = rest[:n_extra], rest[n_extra:n_extra + n_out]
        part = lax.dot_general(a_ref[...].astype(BF16), b_ref[...].astype(BF16), dims,
                               preferred_element_type=F32)

        def finish(acc):
            for o_ref, val in zip(o_refs, epi(acc, *[e[...] for e in e_refs])):
                o_ref[...] = val.astype(o_ref.dtype)

        if nk == 1:
            finish(part)
        else:
            acc_ref = rest[-1]
            k = pl.program_id(2)

            @pl.when(k == 0)
            def _():
                acc_ref[...] = part

            @pl.when(k > 0)
            def _():
                acc_ref[...] += part

            @pl.when(k == nk - 1)
            def _():
                finish(acc_ref[...])

    outs = pl.pallas_call(
        body, name=name, grid=grid,
        in_specs=[a_spec, b_spec] + [mn_spec] * n_extra,
        out_specs=[mn_spec] * n_out,
        out_shape=[jax.ShapeDtypeStruct((m_dim, n_dim), dt) for dt in out_dtypes],
        scratch_shapes=[pltpu.VMEM((tm, tn), F32)] if nk > 1 else [],
        compiler_params=_params(dimension_semantics=("parallel", "parallel", "arbitrary")),
    )(a, b, *extras)
    return outs[0] if n_out == 1 else outs


def _rows(fn, row_ins, vec_ins, row_outs, vec_outs, *, name, tt=256, reverse=False):
    t_dim = row_ins[0].shape[0]
    tt = min(tt, t_dim)
    n = t_dim // tt
    n_ri, n_vi, n_ro, n_vo = len(row_ins), len(vec_ins), len(row_outs), len(vec_outs)
    pos = (lambda i: (n - 1 - i, 0)) if reverse else (lambda i: (i, 0))
    fixed = lambda i: (0, 0)

    def body(*refs):
        ri = refs[:n_ri]
        vi = refs[n_ri:n_ri + n_vi]
        ro = refs[n_ri + n_vi:n_ri + n_vi + n_ro]
        vo = refs[n_ri + n_vi + n_ro:n_ri + n_vi + n_ro + n_vo]
        scratch = refs[n_ri + n_vi + n_ro + n_vo:]
        r_out, v_out = fn([r[...] for r in ri], [v[...] for v in vi], *scratch)
        for o_ref, val in zip(ro, r_out):
            o_ref[...] = val.astype(o_ref.dtype)
        i = pl.program_id(0)
        for o_ref, val in zip(vo, v_out):
            @pl.when(i == 0)
            def _(o_ref=o_ref, val=val):
                o_ref[...] = val

            @pl.when(i > 0)
            def _(o_ref=o_ref, val=val):
                o_ref[...] += val

    return body, dict(
        grid=(n,),
        in_specs=[pl.BlockSpec((tt, r.shape[1]), pos) for r in row_ins]
        + [pl.BlockSpec(v.shape, fixed) for v in vec_ins],
        out_specs=[pl.BlockSpec((tt, w), pos) for w, _ in row_outs]
        + [pl.BlockSpec(s, fixed) for s in vec_outs],
        out_shape=[jax.ShapeDtypeStruct((t_dim, w), dt) for w, dt in row_outs]
        + [jax.ShapeDtypeStruct(s, F32) for s in vec_outs],
        name=name,
        compiler_params=_params(dimension_semantics=("arbitrary",)),
    )


def _rows_call(fn, row_ins, vec_ins, row_outs, vec_outs, *, name, tt=256, reverse=False, scratch=()):
    body, kw = _rows(fn, row_ins, vec_ins, row_outs, vec_outs, name=name, tt=tt, reverse=reverse)
    return pl.pallas_call(body, scratch_shapes=list(scratch), **kw)(*row_ins, *vec_ins)


def _rstd(x):
    return lax.rsqrt(jnp.mean(x * x, axis=-1, keepdims=True) + RMS_EPS)


def _norm(x, g):
    return x * _rstd(x) * g


def _norm_bwd(x, g, dy):
    xh = x * _rstd(x)
    gy = dy * g
    dx = _rstd(x) * (gy - xh * jnp.mean(gy * xh, axis=-1, keepdims=True))
    return dx, jnp.sum(dy * xh, axis=0, keepdims=True)


def _sigmoid(x):
    return 1.0 / (1.0 + jnp.exp(-x))


def _log_sigmoid(x):
    return jnp.minimum(x, 0.0) - jnp.log(1.0 + jnp.exp(-jnp.abs(x)))


def _split3(x):
    hi = x.astype(BF16)
    r1 = x - hi.astype(F32)
    mid = r1.astype(BF16)
    lo = (r1 - mid.astype(F32)).astype(BF16)
    return hi, mid, lo


def _cumsum_fwd(fl, bias, *, name):
    w = fl.shape[1]
    tt = min(512, fl.shape[0])

    def fn(rows, vecs, carry_ref):
        i = pl.program_id(0)

        @pl.when(i == 0)
        def _():
            carry_ref[...] = jnp.zeros_like(carry_ref)

        lf = _log_sigmoid(rows[0] + vecs[0])
        r = lax.broadcasted_iota(jnp.int32, (tt, tt), 0)
        c = lax.broadcasted_iota(jnp.int32, (tt, tt), 1)
        tri = (c <= r).astype(BF16)
        acc = carry_ref[0:1, :]
        for part in _split3(lf):
            acc = acc + jnp.dot(tri, part, preferred_element_type=F32)
        carry_ref[0:1, :] = acc[tt - 1:tt, :]
        return [acc], []

    return _rows_call(fn, [fl], [bias], [(w, F32)], [], name=name, tt=tt,
                      scratch=[pltpu.VMEM((8, w), F32)])[0]


def _cumsum_bwd(dc, fl, bias, *, name):
    w = fl.shape[1]
    tt = min(512, fl.shape[0])

    def fn(rows, vecs, carry_ref):
        i = pl.program_id(0)

        @pl.when(i == 0)
        def _():
            carry_ref[...] = jnp.zeros_like(carry_ref)

        r = lax.broadcasted_iota(jnp.int32, (tt, tt), 0)
        c = lax.broadcasted_iota(jnp.int32, (tt, tt), 1)
        tri = (c >= r).astype(BF16)
        acc = carry_ref[0:1, :]
        for part in _split3(rows[0]):
            acc = acc + jnp.dot(tri, part, preferred_element_type=F32)
        carry_ref[0:1, :] = acc[0:1, :]
        dfl = acc * _sigmoid(-(rows[1] + vecs[0]))
        return [dfl], [jnp.sum(dfl, axis=0, keepdims=True)]

    return _rows_call(fn, [dc, fl], [bias], [(w, BF16)], [(1, w)], name=name, tt=tt, reverse=True,
                      scratch=[pltpu.VMEM((8, w), F32)])


def _head_masks(tb):
    lane = lax.broadcasted_iota(jnp.int32, (tb, LANES), 1)
    return [lane < HEAD_DIM, lane >= HEAD_DIM]


PRUNE_MARGIN = 40.0


def _head_norms(qkv, *, name):
    t_dim = qkv.shape[0]
    d = qkv.shape[1] // 3
    heads = d // HEAD_DIM
    tt = min(512, t_dim)

    def body(q_ref, k_ref, o_ref):
        col = lax.broadcasted_iota(jnp.int32, (d, LANES), 0) // HEAD_DIM
        lane = lax.broadcasted_iota(jnp.int32, (d, LANES), 1)
        tile_max = None
        for ref, first in ((q_ref, 0), (k_ref, heads)):
            x = ref[...].astype(F32)
            sums = jnp.dot((x * x).astype(BF16), (col + first == lane).astype(BF16), preferred_element_type=F32)
            part = jnp.max(sums, axis=0, keepdims=True)
            tile_max = part if tile_max is None else jnp.maximum(tile_max, part)
        i = pl.program_id(0)

        @pl.when(i == 0)
        def _():
            o_ref[...] = tile_max

        @pl.when(i > 0)
        def _():
            o_ref[...] = jnp.maximum(o_ref[...], tile_max)

    return pl.pallas_call(
        body, name=name, grid=(t_dim // tt,),
        in_specs=[pl.BlockSpec((tt, d), lambda i: (i, 0)), pl.BlockSpec((tt, d), lambda i: (i, 1))],
        out_specs=pl.BlockSpec((1, LANES), lambda i: (0, 0)),
        out_shape=jax.ShapeDtypeStruct((1, LANES), F32),
        compiler_params=_params(dimension_semantics=("arbitrary",)),
    )(qkv, qkv)


def _prune_table(c_t, norms, tb):
    heads = c_t.shape[0]
    bound = 1.02 * HEAD_DIM ** -0.5 * jnp.sqrt(norms[0, :heads] * norms[0, heads:2 * heads])
    return jnp.concatenate([c_t[:, ::tb], c_t[:, tb - 1::tb], -(PRUNE_MARGIN + 2.0 * bound)[:, None]], axis=1)


def _kept_before(prune_ref, h, i, nq):
    first, thr = prune_ref[h, i], prune_ref[h, 2 * nq]
    return lax.fori_loop(0, i, lambda j, n: n + (first - prune_ref[h, nq + j] >= thr).astype(jnp.int32),
                         jnp.int32(0))


def _kept_after(prune_ref, h, j, nq):
    last, thr = prune_ref[h, nq + j], prune_ref[h, 2 * nq]
    return lax.fori_loop(j + 1, nq, lambda i, n: n + (prune_ref[h, i] - last >= thr).astype(jnp.int32),
                         jnp.int32(0))


def _flash_fwd(qkv, c_hb, c_rows, prune, *, tb, name):
    t_dim = qkv.shape[0]
    d = qkv.shape[1] // 3
    heads = d // HEAD_DIM
    cb = d // LANES
    scale = HEAD_DIM ** -0.5
    nq = t_dim // tb

    def body(prune_ref, q_ref, k_ref, v_ref, cc_ref, cr_ref, o_ref, lse_ref):
        i = pl.program_id(1)
        q = q_ref[...]
        masks = _head_masks(tb)
        row = lax.broadcasted_iota(jnp.int32, (tb, tb), 0)
        col = lax.broadcasted_iota(jnp.int32, (tb, tb), 1)
        outs = []
        for e in range(2):
            qe = jnp.where(masks[e], q, jnp.zeros_like(q))
            ccol = cc_ref[0, e][:, 0:1]

            def step(j, carry, diagonal, e=e, qe=qe, ccol=ccol):
                m, l, acc = carry
                off = pl.multiple_of(j * tb, tb)
                kj = k_ref[pl.ds(off, tb), :]
                vj = v_ref[pl.ds(off, tb), :]
                crow = cr_ref[0, e, :, pl.ds(off, tb)]
                s = lax.dot_general(qe, kj, (((1,), (1,)), ((), ())), preferred_element_type=F32) * scale
                s = s + (ccol - crow)
                if diagonal:
                    s = jnp.where(col <= row, s, NEG_INF)
                m_new = jnp.maximum(m, jnp.max(s, axis=1, keepdims=True))
                p = jnp.exp(s - m_new)
                alpha = jnp.exp(m - m_new)
                l = alpha * l + jnp.sum(p, axis=1, keepdims=True)
                acc = alpha * acc + jnp.dot(p.astype(BF16), vj, preferred_element_type=F32)
                return m_new, l, acc

            init = (jnp.full((tb, 1), NEG_INF, F32), jnp.zeros((tb, 1), F32), jnp.zeros((tb, LANES), F32))
            kept = _kept_before(prune_ref, 2 * pl.program_id(0) + e, i, nq)
            carry = lax.fori_loop(i - kept, i, functools.partial(step, diagonal=False), init)
            m, l, acc = step(i, carry, True)
            outs.append(acc / l)
            lse_ref[0, e] = jnp.broadcast_to(m + jnp.log(l), (tb, LANES))
        o_ref[...] = jnp.where(masks[0], outs[0], outs[1]).astype(o_ref.dtype)

    hb_spec = pl.BlockSpec((1, 2, tb, LANES), lambda h, i: (h, 0, i, 0))
    row_spec = pl.BlockSpec((1, 2, 1, t_dim), lambda h, i: (h, 0, 0, 0))
    return pl.pallas_call(
        body, name=name, grid=(heads // 2, t_dim // tb),
        in_specs=[pl.BlockSpec(memory_space=pltpu.SMEM),
                  pl.BlockSpec((tb, LANES), lambda h, i: (i, h)),
                  pl.BlockSpec((t_dim, LANES), lambda h, i: (0, cb + h)),
                  pl.BlockSpec((t_dim, LANES), lambda h, i: (0, 2 * cb + h)),
                  hb_spec, row_spec],
        out_specs=[pl.BlockSpec((tb, LANES), lambda h, i: (i, h)), hb_spec],
        out_shape=[jax.ShapeDtypeStruct((t_dim, d), BF16),
                   jax.ShapeDtypeStruct((heads // 2, 2, t_dim, LANES), F32)],
        compiler_params=_params(dimension_semantics=("parallel", "arbitrary")),
    )(prune, qkv, qkv, qkv, c_hb, c_rows)


def _flash_dq(qkv, o, do, c_hb, c_rows, lse_hb, prune, *, tb, name):
    t_dim = qkv.shape[0]
    d = qkv.shape[1] // 3
    heads = d // HEAD_DIM
    cb = d // LANES
    scale = HEAD_DIM ** -0.5
    nq = t_dim // tb

    def body(prune_ref, q_ref, k_ref, v_ref, o_ref, do_ref, cc_ref, cr_ref, lse_ref, dq_ref, dl_ref, rs_ref):
        i = pl.program_id(1)
        q = q_ref[...]
        do_blk = do_ref[...]
        prod = do_blk.astype(F32) * o_ref[...].astype(F32)
        masks = _head_masks(tb)
        row = lax.broadcasted_iota(jnp.int32, (tb, tb), 0)
        col = lax.broadcasted_iota(jnp.int32, (tb, tb), 1)
        outs = []
        for e in range(2):
            qe = jnp.where(masks[e], q, jnp.zeros_like(q))
            doe = jnp.where(masks[e], do_blk, jnp.zeros_like(do_blk))
            delta = jnp.sum(jnp.where(masks[e], prod, 0.0), axis=1, keepdims=True)
            dl_ref[0, e] = jnp.broadcast_to(delta, (tb, LANES))
            ccol = cc_ref[0, e][:, 0:1]
            lse = lse_ref[0, e][:, 0:1]

            def step(j, carry, diagonal, e=e, qe=qe, doe=doe, delta=delta, ccol=ccol, lse=lse):
                acc, rsum = carry
                off = pl.multiple_of(j * tb, tb)
                kj = k_ref[pl.ds(off, tb), :]
                vj = v_ref[pl.ds(off, tb), :]
                crow = cr_ref[0, e, :, pl.ds(off, tb)]
                s = lax.dot_general(qe, kj, (((1,), (1,)), ((), ())), preferred_element_type=F32) * scale
                s = s + (ccol - crow)
                if diagonal:
                    s = jnp.where(col <= row, s, NEG_INF)
                p = jnp.exp(s - lse)
                dp = lax.dot_general(doe, vj, (((1,), (1,)), ((), ())), preferred_element_type=F32)
                ds = p * (dp - delta)
                return (acc + jnp.dot(ds.astype(BF16), kj, preferred_element_type=F32),
                        rsum + jnp.sum(ds, axis=1, keepdims=True))

            init = (jnp.zeros((tb, LANES), F32), jnp.zeros((tb, 1), F32))
            kept = _kept_before(prune_ref, 2 * pl.program_id(0) + e, i, nq)
            acc, rsum = step(i, lax.fori_loop(i - kept, i, functools.partial(step, diagonal=False), init), True)
            outs.append(acc)
            rs_ref[0, e] = jnp.broadcast_to(rsum, (tb, LANES))
        dq_ref[...] = (jnp.where(masks[0], outs[0], outs[1]) * scale).astype(dq_ref.dtype)

    blk = pl.BlockSpec((tb, LANES), lambda h, i: (i, h))
    hb_spec = pl.BlockSpec((1, 2, tb, LANES), lambda h, i: (h, 0, i, 0))
    row_spec = pl.BlockSpec((1, 2, 1, t_dim), lambda h, i: (h, 0, 0, 0))
    hb_shape = jax.ShapeDtypeStruct((heads // 2, 2, t_dim, LANES), F32)
    return pl.pallas_call(
        body, name=name, grid=(heads // 2, t_dim // tb),
        in_specs=[pl.BlockSpec(memory_space=pltpu.SMEM), blk,
                  pl.BlockSpec((t_dim, LANES), lambda h, i: (0, cb + h)),
                  pl.BlockSpec((t_dim, LANES), lambda h, i: (0, 2 * cb + h)),
                  blk, blk, hb_spec, row_spec, hb_spec],
        out_specs=[blk, hb_spec, hb_spec],
        out_shape=[jax.ShapeDtypeStruct((t_dim, d), BF16), hb_shape, hb_shape],
        compiler_params=_params(dimension_semantics=("parallel", "arbitrary")),
    )(prune, qkv, qkv, qkv, o, do, c_hb, c_rows, lse_hb)


def _flash_dkv(qkv, do, c_hb, c_rows, lse_rows, delta_rows, prune, *, tb, name):
    t_dim = qkv.shape[0]
    d = qkv.shape[1] // 3
    heads = d // HEAD_DIM
    cb = d // LANES
    scale = HEAD_DIM ** -0.5
    nq = t_dim // tb

    def body(prune_ref, q_ref, k_ref, v_ref, do_ref, cc_ref, cr_ref, lr_ref, dr_ref, dk_ref, dv_ref, dsum_ref):
        j = pl.program_id(1)
        k_blk = k_ref[...]
        v_blk = v_ref[...]
        masks = _head_masks(tb)
        row = lax.broadcasted_iota(jnp.int32, (tb, tb), 0)
        col = lax.broadcasted_iota(jnp.int32, (tb, tb), 1)
        dks, dvs = [], []
        for e in range(2):
            ke = jnp.where(masks[e], k_blk, jnp.zeros_like(k_blk))
            ve = jnp.where(masks[e], v_blk, jnp.zeros_like(v_blk))
            ccol = cc_ref[0, e][:, 0:1]

            def step(i, carry, diagonal, e=e, ke=ke, ve=ve, ccol=ccol):
                dk, dv, dsum = carry
                off = pl.multiple_of(i * tb, tb)
                qi = q_ref[pl.ds(off, tb), :]
                doi = do_ref[pl.ds(off, tb), :]
                crow = cr_ref[0, e, :, pl.ds(off, tb)]
                lse = lr_ref[0, e, :, pl.ds(off, tb)]
                delta = dr_ref[0, e, :, pl.ds(off, tb)]
                st = lax.dot_general(ke, qi, (((1,), (1,)), ((), ())), preferred_element_type=F32) * scale
                st = st + (crow - ccol)
                if diagonal:
                    st = jnp.where(col >= row, st, NEG_INF)
                pt = jnp.exp(st - lse)
                dpt = lax.dot_general(ve, doi, (((1,), (1,)), ((), ())), preferred_element_type=F32)
                dst = pt * (dpt - delta)
                dv = dv + jnp.dot(pt.astype(BF16), doi, preferred_element_type=F32)
                dk = dk + jnp.dot(dst.astype(BF16), qi, preferred_element_type=F32)
                return dk, dv, dsum + jnp.sum(dst, axis=1, keepdims=True)

            zero = jnp.zeros((tb, LANES), F32)
            carry = step(j, (zero, zero, jnp.zeros((tb, 1), F32)), True)
            kept = _kept_after(prune_ref, 2 * pl.program_id(0) + e, j, nq)
            dk, dv, dsum = lax.fori_loop(j + 1, j + 1 + kept, functools.partial(step, diagonal=False), carry)
            dks.append(dk)
            dvs.append(dv)
            dsum_ref[0, e] = jnp.broadcast_to(dsum, (tb, LANES))
        dk_ref[...] = (jnp.where(masks[0], dks[0], dks[1]) * scale).astype(dk_ref.dtype)
        dv_ref[...] = jnp.where(masks[0], dvs[0], dvs[1]).astype(dv_ref.dtype)

    blk = pl.BlockSpec((tb, LANES), lambda h, j: (j, h))
    hb_spec = pl.BlockSpec((1, 2, tb, LANES), lambda h, j: (h, 0, j, 0))
    row_spec = pl.BlockSpec((1, 2, 1, t_dim), lambda h, j: (h, 0, 0, 0))
    return pl.pallas_call(
        body, name=name, grid=(heads // 2, nq),
        in_specs=[pl.BlockSpec(memory_space=pltpu.SMEM),
                  pl.BlockSpec((t_dim, LANES), lambda h, j: (0, h)),
                  pl.BlockSpec((tb, LANES), lambda h, j: (j, cb + h)),
                  pl.BlockSpec((tb, LANES), lambda h, j: (j, 2 * cb + h)),
                  pl.BlockSpec((t_dim, LANES), lambda h, j: (0, h)),
                  hb_spec, row_spec, row_spec, row_spec],
        out_specs=[blk, blk, hb_spec],
        out_shape=[jax.ShapeDtypeStruct((t_dim, d), BF16), jax.ShapeDtypeStruct((t_dim, d), BF16),
                   jax.ShapeDtypeStruct((heads // 2, 2, t_dim, LANES), F32)],
        compiler_params=_params(dimension_semantics=("parallel", "arbitrary")),
    )(prune, qkv, qkv, qkv, do, c_hb, c_rows, lse_rows, delta_rows)


def _rows_of(hb):
    pairs, _, t_dim, _ = hb.shape
    return hb[:, :, :, 0].reshape(pairs, 2, 1, t_dim)


def _shift_down(z, prev, n, tt):
    out = pltpu.roll(z, n, axis=0)
    row = lax.broadcasted_iota(jnp.int32, z.shape, 0)
    for r in range(n):
        out = jnp.where(row == r, prev[8 - n + r:8 - n + r + 1, :], out)
    return out


def _shift_up(z, nxt, n, tt):
    out = pltpu.roll(z, tt - n, axis=0)
    row = lax.broadcasted_iota(jnp.int32, z.shape, 0)
    for r in range(n):
        out = jnp.where(row == tt - n + r, nxt[r:r + 1, :], out)
    return out


def _conv_fwd(proj, conv_w, *, name, tt=256):
    t_dim, d3 = proj.shape
    d = d3 // 3
    tt = min(tt, t_dim)

    def body(p_ref, prev_ref, w_ref, y_ref):
        i = pl.program_id(0)
        p = p_ref[...]
        pp = prev_ref[...]
        z = p[:, d:2 * d] * p[:, 2 * d:]
        zp = jnp.where(i > 0, pp[:, d:2 * d] * pp[:, 2 * d:], 0.0)
        w = w_ref[...]
        zc = w[2:3, :] * z + w[1:2, :] * _shift_down(z, zp, 1, tt) + w[0:1, :] * _shift_down(z, zp, 2, tt)
        y_ref[...] = (p[:, :d] * zc).astype(y_ref.dtype)

    return pl.pallas_call(
        body, name=name, grid=(t_dim // tt,),
        in_specs=[pl.BlockSpec((tt, d3), lambda i: (i, 0)),
                  pl.BlockSpec((8, d3), lambda i: (jnp.maximum(i * (tt // 8) - 1, 0), 0)),
                  pl.BlockSpec(conv_w.shape, lambda i: (0, 0))],
        out_specs=pl.BlockSpec((tt, d), lambda i: (i, 0)),
        out_shape=jax.ShapeDtypeStruct((t_dim, d), BF16),
        compiler_params=_params(dimension_semantics=("arbitrary",)),
    )(proj, proj, conv_w)


def _conv_bwd(proj, dy, conv_w, *, name, tt=256):
    t_dim, d3 = proj.shape
    d = d3 // 3
    tt = min(tt, t_dim)
    n = t_dim // tt

    def body(p_ref, prev_ref, next_ref, dy_ref, dyn_ref, w_ref, dp_ref, dw_ref):
        i = pl.program_id(0)
        p = p_ref[...]
        pp = prev_ref[...]
        pn = next_ref[...]
        bg, cg, u = p[:, :d], p[:, d:2 * d], p[:, 2 * d:]
        z = cg * u
        zp = jnp.where(i > 0, pp[:, d:2 * d] * pp[:, 2 * d:], 0.0)
        w = w_ref[...]
        z1 = _shift_down(z, zp, 1, tt)
        z2 = _shift_down(z, zp, 2, tt)
        zc = w[2:3, :] * z + w[1:2, :] * z1 + w[0:1, :] * z2
        dy_blk = dy_ref[...]
        dzc = dy_blk * bg
        dzn = jnp.where(i < n - 1, dyn_ref[...] * pn[:, :d], 0.0)
        dz = w[2:3, :] * dzc + w[1:2, :] * _shift_up(dzc, dzn, 1, tt) + w[0:1, :] * _shift_up(dzc, dzn, 2, tt)
        dp_ref[:, :d] = (dy_blk * zc).astype(dp_ref.dtype)
        dp_ref[:, d:2 * d] = (dz * u).astype(dp_ref.dtype)
        dp_ref[:, 2 * d:] = (dz * cg).astype(dp_ref.dtype)
        part = jnp.concatenate([jnp.sum(dzc * z2, axis=0, keepdims=True),
                                jnp.sum(dzc * z1, axis=0, keepdims=True),
                                jnp.sum(dzc * z, axis=0, keepdims=True),
                                jnp.zeros((5, d), F32)], axis=0)

        @pl.when(i == 0)
        def _():
            dw_ref[...] = part

        @pl.when(i > 0)
        def _():
            dw_ref[...] += part

    last8 = t_dim // 8 - 1
    return pl.pallas_call(
        body, name=name, grid=(n,),
        in_specs=[pl.BlockSpec((tt, d3), lambda i: (i, 0)),
                  pl.BlockSpec((8, d3), lambda i: (jnp.maximum(i * (tt // 8) - 1, 0), 0)),
                  pl.BlockSpec((8, d3), lambda i: (jnp.minimum((i + 1) * (tt // 8), last8), 0)),
                  pl.BlockSpec((tt, d), lambda i: (i, 0)),
                  pl.BlockSpec((8, d), lambda i: (jnp.minimum((i + 1) * (tt // 8), last8), 0)),
                  pl.BlockSpec(conv_w.shape, lambda i: (0, 0))],
        out_specs=[pl.BlockSpec((tt, d3), lambda i: (i, 0)), pl.BlockSpec((8, d), lambda i: (0, 0))],
        out_shape=[jax.ShapeDtypeStruct((t_dim, d3), BF16), jax.ShapeDtypeStruct((8, d), F32)],
        compiler_params=_params(dimension_semantics=("arbitrary",)),
    )(proj, proj, proj, dy, dy, conv_w)


def _exchange(src, *, gather, name):
    rows, cols = src.shape[-2:]

    def body(src_ref, dst_ref, send_sems, recv_sems, local_sem):
        x, y, c = lax.axis_index("x"), lax.axis_index("y"), lax.axis_index("c")
        me = 4 * x + 2 * y + c
        mine = src_ref if gather else src_ref.at[me]
        local = pltpu.make_async_copy(mine, dst_ref.at[me], local_sem)
        local.start()
        copies = []
        for k in range(1, N_DEV):
            px = 1 - x if k & 4 else x
            py = 1 - y if k & 2 else y
            pc = 1 - c if k & 1 else c
            cp = pltpu.make_async_remote_copy(
                src_ref=src_ref if gather else src_ref.at[4 * px + 2 * py + pc],
                dst_ref=dst_ref.at[me],
                send_sem=send_sems.at[k - 1], recv_sem=recv_sems.at[k - 1],
                device_id=(px, py, pc), device_id_type=pl.DeviceIdType.MESH)
            cp.start()
            copies.append(cp)
        for cp in copies:
            cp.wait_recv()
        for cp in copies:
            cp.wait_send()
        local.wait()

    return pl.pallas_call(
        body, name=name,
        in_specs=[pl.BlockSpec(memory_space=pl.ANY)],
        out_specs=pl.BlockSpec(memory_space=pl.ANY),
        out_shape=jax.ShapeDtypeStruct((N_DEV, rows, cols), src.dtype),
        scratch_shapes=[pltpu.SemaphoreType.DMA((N_DEV - 1,)), pltpu.SemaphoreType.DMA((N_DEV - 1,)),
                        pltpu.SemaphoreType.DMA],
    )(src)


def _adamw(recv, w, m, v, *, name):
    rows, cols = w.shape
    tt = min(FLAT_ROW_TILE, rows)

    def body(r_ref, w_ref, m_ref, v_ref, g_ref, d_ref, nm_ref, nv_ref):
        g = r_ref[0].astype(F32)
        for s in range(1, N_DEV):
            g = g + r_ref[s].astype(F32)
        m_new = ADAM_B1 * m_ref[...] + (1.0 - ADAM_B1) * g
        v_new = ADAM_B2 * v_ref[...] + (1.0 - ADAM_B2) * (g * g)
        m_hat = m_new / (1.0 - ADAM_B1 ** ADAM_STEP)
        v_hat = v_new / (1.0 - ADAM_B2 ** ADAM_STEP)
        g_ref[...] = g
        d_ref[...] = -ADAM_LR * (m_hat / (jnp.sqrt(v_hat) + ADAM_EPS) + ADAM_WD * w_ref[...])
        nm_ref[...] = m_new
        nv_ref[...] = v_new

    spec = pl.BlockSpec((tt, cols), lambda i: (i, 0))
    return pl.pallas_call(
        body, name=name, grid=(rows // tt,),
        in_specs=[pl.BlockSpec((N_DEV, tt, cols), lambda i: (0, i, 0)), spec, spec, spec],
        out_specs=[spec] * 4,
        out_shape=[jax.ShapeDtypeStruct((rows, cols), F32)] * 4,
        compiler_params=_params(dimension_semantics=("parallel",)),
    )(recv, w, m, v)


def _flat_rows(n_elems, cols, row_tile):
    rows = -(-n_elems // cols)
    return -(-rows // row_tile) * row_tile


def _pack(parts, cols, row_tile, dtype):
    flat = jnp.concatenate([p.reshape(-1).astype(dtype) for p in parts])
    rows = _flat_rows(flat.shape[0], cols, row_tile)
    return jnp.pad(flat, (0, rows * cols - flat.shape[0])).reshape(rows, cols)


def _unpack(flat2d, shapes):
    lead = flat2d.shape[:-2]
    flat = flat2d.reshape(*lead, -1)
    out, off = [], 0
    for shp in shapes:
        n = 1
        for s in shp:
            n *= s
        out.append(flat[..., off:off + n].reshape(*lead, *shp))
        off += n
    return out


def _full_from_blocks(blocks, axis):
    if axis is None:
        return blocks[0]
    moved = jnp.moveaxis(blocks, 0, axis)
    shp = list(moved.shape)
    return moved.reshape(shp[:axis] + [shp[axis] * shp[axis + 1]] + shp[axis + 2:])


def _blocks_from_full(full, axis):
    if axis is None:
        return jnp.broadcast_to(full, (N_DEV,) + full.shape)
    shp = list(full.shape)
    split = full.reshape(shp[:axis] + [N_DEV, shp[axis] // N_DEV] + shp[axis + 1:])
    return jnp.moveaxis(split, axis, 0)


def kernel(x, p, norm_g, w_attn_in, b_forget, w_attn_out, w_conv_in, conv_w, w_conv_out, w_mlp_up, w_mlp_down, w_ple_proj, w_ple_gate, loss_target, m_norm_g, m_w_attn_in, m_b_forget, m_w_attn_out, m_w_conv_in, m_conv_w, m_w_conv_out, m_w_mlp_up, m_w_mlp_down, m_w_ple_proj, m_w_ple_gate, v_norm_g, v_w_attn_in, v_b_forget, v_w_attn_out, v_w_conv_in, v_conv_w, v_w_conv_out, v_w_mlp_up, v_w_mlp_down, v_w_ple_proj, v_w_ple_gate):
    shards = dict(norm_g=norm_g, w_attn_in=w_attn_in, b_forget=b_forget, w_attn_out=w_attn_out,
                  w_conv_in=w_conv_in, conv_w=conv_w, w_conv_out=w_conv_out, w_mlp_up=w_mlp_up,
                  w_mlp_down=w_mlp_down, w_ple_proj=w_ple_proj, w_ple_gate=w_ple_gate)
    m_shards = dict(norm_g=m_norm_g, w_attn_in=m_w_attn_in, b_forget=m_b_forget, w_attn_out=m_w_attn_out,
                    w_conv_in=m_w_conv_in, conv_w=m_conv_w, w_conv_out=m_w_conv_out, w_mlp_up=m_w_mlp_up,
                    w_mlp_down=m_w_mlp_down, w_ple_proj=m_w_ple_proj, w_ple_gate=m_w_ple_gate)
    v_shards = dict(norm_g=v_norm_g, w_attn_in=v_w_attn_in, b_forget=v_b_forget, w_attn_out=v_w_attn_out,
                    w_conv_in=v_w_conv_in, conv_w=v_conv_w, w_conv_out=v_w_conv_out, w_mlp_up=v_w_mlp_up,
                    w_mlp_down=v_w_mlp_down, w_ple_proj=v_w_ple_proj, w_ple_gate=v_w_ple_gate)
    shard_shapes = [shards[n].shape for n in WEIGHT_NAMES]
    t_dim, d = x.shape[-2:]
    depth = p.shape[0]
    heads = b_forget.shape[1]
    assert d == heads * HEAD_DIM and x.shape[0] == 1
    tb = min(256, t_dim // 2)
    x0 = x.reshape(t_dim, d)
    target = loss_target.reshape(t_dim, d)

    small_names = ('norm_g', 'conv_w')
    gathered = _exchange(_pack([shards[n] for n in WEIGHT_NAMES], FLAT_COLS, FLAT_ROW_TILE, BF16),
                         gather=True, name="gather_weights")
    gathered_small = _exchange(_pack([shards[n] for n in small_names], LANES, 8, F32),
                               gather=True, name="gather_gains")
    full = {n: _full_from_blocks(blk, SHARD_AXIS[n])
            for n, blk in zip(WEIGHT_NAMES, _unpack(gathered, shard_shapes))}
    for n, blk in zip(small_names, _unpack(gathered_small, [shards[n].shape for n in small_names])):
        full[n] = _full_from_blocks(blk, SHARD_AXIS[n])
    gains = full['norm_g']
    taps = full['conv_w']
    w_in_pad = jnp.pad(full['w_attn_in'], ((0, 0), (0, 0), (0, LANES - heads)))
    bias_pad = jnp.pad(b_forget, ((0, 0), (0, LANES - heads)))

    def gain(i, k):
        return gains[i, k].reshape(1, d)

    def add_norm(x_prev, branch, g_branch, g_next, name):
        def fn(rows, vecs):
            x_new = rows[0] + _norm(rows[1], vecs[0])
            return [x_new, _norm(x_new, vecs[1])], []
        return _rows_call(fn, [x_prev, branch], [g_branch, g_next], [(d, F32), (d, BF16)], [], name=name)

    saved = []
    x_cur = x0
    hn = _rows_call(lambda rows, vecs: ([_norm(rows[0], vecs[0])], []), [x0], [gain(0, 0)], [(d, BF16)], [],
                    name="norm_in")[0]
    loss_rows = dy = None
    for i in range(depth):
        j = i // 2
        s = dict(x0=x_cur, hn=hn)
        if i % 2 == 0:
            w_in = w_in_pad[j]
            s['qkv'] = _mm(hn, w_in[:, :3 * d], out_dtypes=(BF16,), name=f"attn_in_{i}")
            s['fl'] = _mm(hn, w_in[:, 3 * d:], name=f"attn_gate_{i}")
            c = _cumsum_fwd(s['fl'], bias_pad[j:j + 1], name=f"gate_cumsum_{i}")
            c_t = c[:, :heads].T
            s['prune'] = _prune_table(c_t, _head_norms(s['qkv'], name=f"head_norms_{i}"), tb)
            c_t = c_t.reshape(heads // 2, 2, t_dim)
            s['c_hb'] = jnp.broadcast_to(c_t[:, :, :, None], (heads // 2, 2, t_dim, LANES))
            s['c_rows'] = c_t.reshape(heads // 2, 2, 1, t_dim)
            s['o'], s['lse_hb'] = _flash_fwd(s['qkv'], s['c_hb'], s['c_rows'], s['prune'], tb=tb, name=f"attn_fwd_{i}")
            s['m'] = _mm(s['o'], full['w_attn_out'][j], name=f"attn_out_{i}")
        else:
            s['proj'] = _mm(hn, full['w_conv_in'][j], name=f"conv_in_{i}")
            s['y'] = _conv_fwd(s['proj'], taps[j], name=f"conv_fwd_{i}")
            s['m'] = _mm(s['y'], full['w_conv_out'][j], name=f"conv_out_{i}")
        s['x1'], s['h2'] = add_norm(x_cur, s['m'], gain(i, 1), gain(i, 2), f"mix_norm_{i}")
        s['u'], s['a'] = _mm(s['h2'], full['w_mlp_up'][i], out_dtypes=(BF16, BF16), name=f"mlp_up_{i}",
                             epi=lambda acc: (acc, jnp.square(jnp.maximum(acc, 0.0))))
        s['f'] = _mm(s['a'], full['w_mlp_down'][i], name=f"mlp_down_{i}")
        s['x2'], s['h4'] = add_norm(s['x1'], s['f'], gain(i, 3), gain(i, 4), f"mlp_norm_{i}")
        s['pi'] = p[i].reshape(t_dim, -1)
        s['pp'] = _mm(s['pi'], full['w_ple_proj'][i], name=f"ple_proj_{i}")
        s['gl'], s['e'] = _mm(s['h4'], full['w_ple_gate'][i], extras=(s['pp'],), out_dtypes=(F32, F32),
                              name=f"ple_gate_{i}", epi=lambda acc, pp: (acc, pp * _sigmoid(acc)))
        if i + 1 < depth:
            x_cur, hn = add_norm(s['x2'], s['e'], gain(i, 5), gain(i + 1, 0), f"ple_norm_{i}")
        else:
            def loss_fn(rows, vecs):
                err = rows[0] + _norm(rows[1], vecs[0]) - rows[2]
                part = 0.5 * jnp.sum(jnp.sum(err * err, axis=1, keepdims=True), axis=0, keepdims=True) / d
                return [err / d], [jnp.broadcast_to(part, (1, LANES))]
            dy, loss_rows = _rows_call(loss_fn, [s['x2'], s['e'], target], [gain(i, 5)], [(d, F32)],
                                       [(1, LANES)], name="loss")
        saved.append(s)
    loss = lax.psum(loss_rows[0, 0], ("x", "y", "c"))

    grads = {n: [None] * full[n].shape[0] for n in WEIGHT_NAMES}
    d_gains = [[None] * 6 for _ in range(depth)]
    dx = dy
    for i in reversed(range(depth)):
        j = i // 2
        s = saved[i]

        def ple_fn(rows, vecs):
            de, dg = _norm_bwd(rows[0], vecs[0], rows[1])
            sg = _sigmoid(rows[2])
            return [de * sg, de * rows[3] * sg * (1.0 - sg)], [dg]
        dpp, dgl, d_gains[i][5] = _rows_call(ple_fn, [s['e'], dx, s['gl'], s['pp']], [gain(i, 5)],
                                             [(d, BF16), (d, BF16)], [(1, d)], name=f"ple_bwd_{i}")
        grads['w_ple_proj'][i] = _mm(s['pi'], dpp, ta=True, name=f"ple_proj_dw_{i}")
        grads['w_ple_gate'][i] = _mm(s['h4'], dgl, ta=True, name=f"ple_gate_dw_{i}")
        dh4 = _mm(dgl, full['w_ple_gate'][i], tb=True, name=f"ple_gate_dx_{i}")

        def two_norm_bwd(x_res, dh, dx_in, branch, g_res, g_branch, name):
            def fn(rows, vecs):
                d_res, dg_res = _norm_bwd(rows[0], vecs[0], rows[1])
                dx_out = rows[2] + d_res
                d_branch, dg_branch = _norm_bwd(rows[3], vecs[1], dx_out)
                return [dx_out, d_branch], [dg_res, dg_branch]
            return _rows_call(fn, [x_res, dh, dx_in, branch], [g_res, g_branch], [(d, F32), (d, BF16)],
                              [(1, d), (1, d)], name=name)

        dx2, df, d_gains[i][4], d_gains[i][3] = two_norm_bwd(s['x2'], dh4, dx, s['f'], gain(i, 4), gain(i, 3),
                                                            f"mlp_norm_bwd_{i}")
        grads['w_mlp_down'][i] = _mm(s['a'], df, ta=True, name=f"mlp_down_dw_{i}")
        du = _mm(df, full['w_mlp_down'][i], tb=True, extras=(s['u'],), out_dtypes=(BF16,), name=f"mlp_down_dx_{i}",
                 epi=lambda acc, u: (acc * (2.0 * jnp.maximum(u.astype(F32), 0.0)),))
        grads['w_mlp_up'][i] = _mm(s['h2'], du, ta=True, name=f"mlp_up_dw_{i}")
        dh2 = _mm(du, full['w_mlp_up'][i], tb=True, name=f"mlp_up_dx_{i}")
        dx1, dm, d_gains[i][2], d_gains[i][1] = two_norm_bwd(s['x1'], dh2, dx2, s['m'], gain(i, 2), gain(i, 1),
                                                            f"mix_norm_bwd_{i}")
        if i % 2 == 0:
            grads['w_attn_out'][j] = _mm(s['o'], dm, ta=True, name=f"attn_out_dw_{i}")
            do = _mm(dm, full['w_attn_out'][j], tb=True, out_dtypes=(BF16,), name=f"attn_out_dx_{i}")
            dq, delta_hb, rsum_hb = _flash_dq(s['qkv'], s['o'], do, s['c_hb'], s['c_rows'], s['lse_hb'], s['prune'], tb=tb,
                                              name=f"attn_dq_{i}")
            dk, dv, csum_hb = _flash_dkv(s['qkv'], do, s['c_hb'], s['c_rows'], _rows_of(s['lse_hb']),
                                         _rows_of(delta_hb), s['prune'], tb=tb, name=f"attn_dkv_{i}")
            dc = (rsum_hb[:, :, :, 0] - csum_hb[:, :, :, 0]).reshape(heads, t_dim).T
            dfl, db = _cumsum_bwd(jnp.pad(dc, ((0, 0), (0, LANES - heads))), s['fl'], bias_pad[j:j + 1],
                                  name=f"gate_cumsum_bwd_{i}")
            grads['b_forget'][j] = db[0, :heads]
            dproj = jnp.concatenate([dq, dk, dv, dfl], axis=1)
            grads['w_attn_in'][j] = _mm(s['hn'], dproj, ta=True, name=f"attn_in_dw_{i}")[:, :3 * d + heads]
            dhn = _mm(dproj, w_in_pad[j], tb=True, name=f"attn_in_dx_{i}")
        else:
            grads['w_conv_out'][j] = _mm(s['y'], dm, ta=True, name=f"conv_out_dw_{i}")
            dyc = _mm(dm, full['w_conv_out'][j], tb=True, name=f"conv_out_dx_{i}")
            dproj, dtaps = _conv_bwd(s['proj'], dyc, taps[j], name=f"conv_bwd_{i}")
            grads['conv_w'][j] = dtaps[:3]
            grads['w_conv_in'][j] = _mm(s['hn'], dproj, ta=True, name=f"conv_in_dw_{i}")
            dhn = _mm(dproj, full['w_conv_in'][j], tb=True, name=f"conv_in_dx_{i}")

        def in_fn(rows, vecs):
            d_res, dg = _norm_bwd(rows[0], vecs[0], rows[1])
            return [rows[2] + d_res], [dg]
        dx, d_gains[i][0] = _rows_call(in_fn, [s['x0'], dhn, dx1], [gain(i, 0)], [(d, F32)], [(1, d)],
                                       name=f"in_norm_bwd_{i}")
    grad_x = dx.reshape(x.shape)
    grads['norm_g'] = [jnp.concatenate(row, axis=0) for row in d_gains]
    grad_full = {n: jnp.stack(grads[n]) for n in WEIGHT_NAMES}

    send = jnp.concatenate([_blocks_from_full(grad_full[n], SHARD_AXIS[n]).reshape(N_DEV, -1).astype(BF16)
                            for n in WEIGHT_NAMES], axis=1)
    flat_rows = _flat_rows(send.shape[1], FLAT_COLS, FLAT_ROW_TILE)
    send = jnp.pad(send, ((0, 0), (0, flat_rows * FLAT_COLS - send.shape[1]))).reshape(N_DEV, flat_rows, FLAT_COLS)
    recv = _exchange(send, gather=False, name="exchange_grads")
    flat = lambda group: _pack([group[n] for n in WEIGHT_NAMES], FLAT_COLS, FLAT_ROW_TILE, F32)
    g_flat, delta_flat, m_flat, v_flat = _adamw(recv, flat(shards), flat(m_shards), flat(v_shards), name="adamw")
    outs = [_unpack(a, shard_shapes) for a in (g_flat, delta_flat, m_flat, v_flat)]
    return (loss, grad_x, *outs[0], *outs[1], *outs[2], *outs[3])
```

```python
import functools

import jax
import jax.numpy as jnp
from jax import lax
from jax.experimental import pallas as pl
from jax.experimental.pallas import tpu as pltpu

F32 = jnp.float32
BF16 = jnp.bfloat16

N_DEV = 8
LANES = 128
HEAD_DIM = 64
VMEM_LIMIT_BYTES = 56 * 1024 * 1024
RMS_EPS = 1e-6
NEG_INF = -1e30
ADAM_LR = 0.001
ADAM_B1 = 0.9
ADAM_B2 = 0.999
ADAM_EPS = 1e-08
ADAM_WD = 0.01
ADAM_STEP = 10
WEIGHT_NAMES = ('norm_g', 'w_attn_in', 'b_forget', 'w_attn_out', 'w_conv_in', 'conv_w', 'w_conv_out',
                'w_mlp_up', 'w_mlp_down', 'w_ple_proj', 'w_ple_gate')
SHARD_AXIS = {'norm_g': 2, 'w_attn_in': 2, 'b_forget': None, 'w_attn_out': 1, 'w_conv_in': 2, 'conv_w': 2,
              'w_conv_out': 1, 'w_mlp_up': 2, 'w_mlp_down': 1, 'w_ple_proj': 2, 'w_ple_gate': 1}


def _params(**kw):
    return pltpu.CompilerParams(vmem_limit_bytes=VMEM_LIMIT_BYTES, **kw)


def _tile(n, cap):
    if n <= cap:
        return n
    t = (cap // LANES) * LANES
    while n % t:
        t -= LANES
    return t


def _mm(a, b, *, ta=False, tb=False, extras=(), epi=None, out_dtypes=(F32,), name):
    m_dim, k_dim = (a.shape[1], a.shape[0]) if ta else a.shape
    n_dim = b.shape[0] if tb else b.shape[1]
    assert k_dim == (b.shape[1] if tb else b.shape[0])
    tk = _tile(k_dim, 1024)
    nk = k_dim // tk
    simple = not extras and len(out_dtypes) == 1
    tm = _tile(m_dim, 1024 if (nk > 1 and simple) else 512)
    tn = _tile(n_dim, 1024)
    grid = (n_dim // tn, m_dim // tm, nk)
    a_spec = (pl.BlockSpec((tk, tm), lambda j, i, k: (k, i)) if ta
              else pl.BlockSpec((tm, tk), lambda j, i, k: (i, k)))
    b_spec = (pl.BlockSpec((tn, tk), lambda j, i, k: (j, k)) if tb
              else pl.BlockSpec((tk, tn), lambda j, i, k: (k, j)))
    mn_spec = pl.BlockSpec((tm, tn), lambda j, i, k: (i, j))
    dims = (((0 if ta else 1,), (1 if tb else 0,)), ((), ()))
    n_extra, n_out = len(extras), len(out_dtypes)
    if epi is None:
        epi = lambda acc: (acc,)

    def body(a_ref, b_ref, *rest):
        e_refs, o_refs = rest[:n_extra], rest[n_extra:n_extra + n_out]
        part = lax.dot_general(a_ref[...].astype(BF16), b_ref[...].astype(BF16), dims,
                               preferred_element_type=F32)

        def finish(acc):
            for o_ref, val in zip(o_refs, epi(acc, *[e[...] for e in e_refs])):
                o_ref[...] = val.astype(o_ref.dtype)

        if nk == 1:
            finish(part)
        else:
            acc_ref = rest[-1]
            k = pl.program_id(2)

            @pl.when(k == 0)
            def _():
                acc_ref[...] = part

            @pl.when(k > 0)
            def _():
                acc_ref[...] += part

            @pl.when(k == nk - 1)
            def _():
                finish(acc_ref[...])

    outs = pl.pallas_call(
        body, name=name, grid=grid,
        in_specs=[a_spec, b_spec] + [mn_spec] * n_extra,
        out_specs=[mn_spec] * n_out,
        out_shape=[jax.ShapeDtypeStruct((m_dim, n_dim), dt) for dt in out_dtypes],
        scratch_shapes=[pltpu.VMEM((tm, tn), F32)] if nk > 1 else [],
        compiler_params=_params(dimension_semantics=("parallel", "parallel", "arbitrary")),
    )(a, b, *extras)
    return outs[0] if n_out == 1 else outs


def _rows(fn, row_ins, vec_ins, row_outs, vec_outs, *, name, tt=256, reverse=False):
    t_dim = row_ins[0].shape[0]
    tt = min(tt, t_dim)
    n = t_dim // tt
    n_ri, n_vi, n_ro, n_vo = len(row_ins), len(vec_ins), len(row_outs), len(vec_outs)
    pos = (lambda i: (n - 1 - i, 0)) if reverse else (lambda i: (i, 0))
    fixed = lambda i: (0, 0)

    def body(*refs):
        ri = refs[:n_ri]
        vi = refs[n_ri:n_ri + n_vi]
        ro = refs[n_ri + n_vi:n_ri + n_vi + n_ro]
        vo = refs[n_ri + n_vi + n_ro:n_ri + n_vi + n_ro + n_vo]
        scratch = refs[n_ri + n_vi + n_ro + n_vo:]
        r_out, v_out = fn([r[...] for r in ri], [v[...] for v in vi], *scratch)
        for o_ref, val in zip(ro, r_out):
            o_ref[...] = val.astype(o_ref.dtype)
        i = pl.program_id(0)
        for o_ref, val in zip(vo, v_out):
            @pl.when(i == 0)
            def _(o_ref=o_ref, val=val):
                o_ref[...] = val

            @pl.when(i > 0)
            def _(o_ref=o_ref, val=val):
                o_ref[...] += val

    return body, dict(
        grid=(n,),
        in_specs=[pl.BlockSpec((tt, r.shape[1]), pos) for r in row_ins]
        + [pl.BlockSpec(v.shape, fixed) for v in vec_ins],
        out_specs=[pl.BlockSpec((tt, w), pos) for w, _ in row_outs]
        + [pl.BlockSpec(s, fixed) for s in vec_outs],
        out_shape=[jax.ShapeDtypeStruct((t_dim, w), dt) for w, dt in row_outs]
        + [jax.ShapeDtypeStruct(s, F32) for s in vec_outs],
        name=name,
        compiler_params=_params(dimension_semantics=("arbitrary",)),
    )


def _rows_call(fn, row_ins, vec_ins, row_outs, vec_outs, *, name, tt=256, reverse=False, scratch=()):
    body, kw = _rows(fn, row_ins, vec_ins, row_outs, vec_outs, name=name, tt=tt, reverse=reverse)
    return pl.pallas_call(body, scratch_shapes=list(scratch), **kw)(*row_ins, *vec_ins)


def _rstd(x):
    return lax.rsqrt(jnp.mean(x * x, axis=-1, keepdims=True) + RMS_EPS)


def _norm(x, g):
    return x * _rstd(x) * g


def _norm_bwd(x, g, dy):
    xh = x * _rstd(x)
    gy = dy * g
    dx = _rstd(x) * (gy - xh * jnp.mean(gy * xh, axis=-1, keepdims=True))
    return dx, jnp.sum(dy * xh, axis=0, keepdims=True)


def _sigmoid(x):
    return 1.0 / (1.0 + jnp.exp(-x))


def _log_sigmoid(x):
    return jnp.minimum(x, 0.0) - jnp.log(1.0 + jnp.exp(-jnp.abs(x)))


def _split3(x):
    hi = x.astype(BF16)
    r1 = x - hi.astype(F32)
    mid = r1.astype(BF16)
    lo = (r1 - mid.astype(F32)).astype(BF16)
    return hi, mid, lo


def _cumsum_fwd(fl, bias, *, name):
    w = fl.shape[1]
    tt = min(512, fl.shape[0])

    def fn(rows, vecs, carry_ref):
        i = pl.program_id(0)

        @pl.when(i == 0)
        def _():
            carry_ref[...] = jnp.zeros_like(carry_ref)

        lf = _log_sigmoid(rows[0] + vecs[0])
        r = lax.broadcasted_iota(jnp.int32, (tt, tt), 0)
        c = lax.broadcasted_iota(jnp.int32, (tt, tt), 1)
        tri = (c <= r).astype(BF16)
        acc = carry_ref[0:1, :]
        for part in _split3(lf):
            acc = acc + jnp.dot(tri, part, preferred_element_type=F32)
        carry_ref[0:1, :] = acc[tt - 1:tt, :]
        return [acc], []

    return _rows_call(fn, [fl], [bias], [(w, F32)], [], name=name, tt=tt,
                      scratch=[pltpu.VMEM((8, w), F32)])[0]


def _cumsum_bwd(dc, fl, bias, *, name):
    w = fl.shape[1]
    tt = min(512, fl.shape[0])

    def fn(rows, vecs, carry_ref):
        i = pl.program_id(0)

        @pl.when(i == 0)
        def _():
            carry_ref[...] = jnp.zeros_like(carry_ref)

        r = lax.broadcasted_iota(jnp.int32, (tt, tt), 0)
        c = lax.broadcasted_iota(jnp.int32, (tt, tt), 1)
        tri = (c >= r).astype(BF16)
        acc = carry_ref[0:1, :]
        for part in _split3(rows[0]):
            acc = acc + jnp.dot(tri, part, preferred_element_type=F32)
        carry_ref[0:1, :] = acc[0:1, :]
        dfl = acc * _sigmoid(-(rows[1] + vecs[0]))
        return [dfl], [jnp.sum(dfl, axis=0, keepdims=True)]

    return _rows_call(fn, [dc, fl], [bias], [(w, BF16)], [(1, w)], name=name, tt=tt, reverse=True,
                      scratch=[pltpu.VMEM((8, w), F32)])


def _head_masks(tb):
    lane = lax.broadcasted_iota(jnp.int32, (tb, LANES), 1)
    return [lane < HEAD_DIM, lane >= HEAD_DIM]


PRUNE_MARGIN = 40.0


def _head_norms(qkv, *, name):
    t_dim = qkv.shape[0]
    d = qkv.shape[1] // 3
    heads = d // HEAD_DIM
    tt = min(512, t_dim)

    def body(q_ref, k_ref, o_ref):
        col = lax.broadcasted_iota(jnp.int32, (d, LANES), 0) // HEAD_DIM
        lane = lax.broadcasted_iota(jnp.int32, (d, LANES), 1)
        tile_max = None
        for ref, first in ((q_ref, 0), (k_ref, heads)):
            x = ref[...].astype(F32)
            sums = jnp.dot((x * x).astype(BF16), (col + first == lane).astype(BF16), preferred_element_type=F32)
            part = jnp.max(sums, axis=0, keepdims=True)
            tile_max = part if tile_max is None else jnp.maximum(tile_max, part)
        i = pl.program_id(0)

        @pl.when(i == 0)
        def _():
            o_ref[...] = tile_max

        @pl.when(i > 0)
        def _():
            o_ref[...] = jnp.maximum(o_ref[...], tile_max)

    return pl.pallas_call(
        body, name=name, grid=(t_dim // tt,),
        in_specs=[pl.BlockSpec((tt, d), lambda i: (i, 0)), pl.BlockSpec((tt, d), lambda i: (i, 1))],
        out_specs=pl.BlockSpec((1, LANES), lambda i: (0, 0)),
        out_shape=jax.ShapeDtypeStruct((1, LANES), F32),
        compiler_params=_params(dimension_semantics=("arbitrary",)),
    )(qkv, qkv)


def _prune_table(c_t, norms, tb):
    heads = c_t.shape[0]
    bound = 1.02 * HEAD_DIM ** -0.5 * jnp.sqrt(norms[0, :heads] * norms[0, heads:2 * heads])
    return jnp.concatenate([c_t[:, ::tb], c_t[:, tb - 1::tb], -(PRUNE_MARGIN + 2.0 * bound)[:, None]], axis=1)


def _kept_before(prune_ref, h, i, nq):
    first, thr = prune_ref[h, i], prune_ref[h, 2 * nq]
    return lax.fori_loop(0, i, lambda j, n: n + (first - prune_ref[h, nq + j] >= thr).astype(jnp.int32),
                         jnp.int32(0))


def _kept_after(prune_ref, h, j, nq):
    last, thr = prune_ref[h, nq + j], prune_ref[h, 2 * nq]
    return lax.fori_loop(j + 1, nq, lambda i, n: n + (prune_ref[h, i] - last >= thr).astype(jnp.int32),
                         jnp.int32(0))


def _as_row(col, tb):
    return jnp.transpose(jnp.broadcast_to(col, (tb, LANES)))[0:1, :]


def _flash_fwd(qkv, c_hb, c_rows, prune, *, tb, name):
    t_dim = qkv.shape[0]
    d = qkv.shape[1] // 3
    heads = d // HEAD_DIM
    cb = d // LANES
    nq = t_dim // tb

    def body(prune_ref, q_ref, k_ref, v_ref, cc_ref, cr_ref, o_ref, lse_ref, lser_ref):
        i = pl.program_id(1)
        h0 = 2 * pl.program_id(0)
        q = q_ref[...] * jnp.asarray(HEAD_DIM ** -0.5, BF16)
        masks = _head_masks(tb)
        row = lax.broadcasted_iota(jnp.int32, (tb, tb), 0)
        col = lax.broadcasted_iota(jnp.int32, (tb, tb), 1)
        qs = [jnp.where(masks[e], q, jnp.zeros_like(q)) for e in range(2)]
        ccols = [cc_ref[0, e][:, 0:1] for e in range(2)]

        def step(j, carry, diagonal):
            off = pl.multiple_of(j * tb, tb)
            kj = k_ref[pl.ds(off, tb), :]
            vj = v_ref[pl.ds(off, tb), :]
            out = []
            for e in range(2):
                m, l, acc = carry[e]
                crow = cr_ref[0, e, :, pl.ds(off, tb)]
                s = lax.dot_general(qs[e], kj, (((1,), (1,)), ((), ())), preferred_element_type=F32)
                s = s + (ccols[e] - crow)
                if diagonal:
                    s = jnp.where(col <= row, s, NEG_INF)
                m_new = jnp.maximum(m, jnp.max(s, axis=1, keepdims=True))
                p = jnp.exp(s - m_new)
                alpha = jnp.exp(m - m_new)
                l = alpha * l + jnp.sum(p, axis=1, keepdims=True)
                acc = alpha * acc + jnp.dot(p.astype(BF16), vj, preferred_element_type=F32)
                out.append((m_new, l, acc))
            return tuple(out)

        init = (jnp.full((tb, 1), NEG_INF, F32), jnp.zeros((tb, 1), F32), jnp.zeros((tb, LANES), F32))
        kept = jnp.maximum(_kept_before(prune_ref, h0, i, nq), _kept_before(prune_ref, h0 + 1, i, nq))
        carry = lax.fori_loop(i - kept, i, functools.partial(step, diagonal=False), (init, init))
        carry = step(i, carry, True)
        outs = []
        for e in range(2):
            m, l, acc = carry[e]
            outs.append(acc / l)
            lse = m + jnp.log(l)
            lse_ref[0, e] = jnp.broadcast_to(lse, (tb, LANES))
            lser_ref[0, e] = _as_row(lse, tb)
        o_ref[...] = jnp.where(masks[0], outs[0], outs[1]).astype(o_ref.dtype)

    hb_spec = pl.BlockSpec((1, 2, tb, LANES), lambda h, i: (h, 0, i, 0))
    row_spec = pl.BlockSpec((1, 2, 1, t_dim), lambda h, i: (h, 0, 0, 0))
    row_blk = pl.BlockSpec((1, 2, 1, tb), lambda h, i: (h, 0, 0, i))
    return pl.pallas_call(
        body, name=name, grid=(heads // 2, nq),
        in_specs=[pl.BlockSpec(memory_space=pltpu.SMEM),
                  pl.BlockSpec((tb, LANES), lambda h, i: (i, h)),
                  pl.BlockSpec((t_dim, LANES), lambda h, i: (0, cb + h)),
                  pl.BlockSpec((t_dim, LANES), lambda h, i: (0, 2 * cb + h)),
                  hb_spec, row_spec],
        out_specs=[pl.BlockSpec((tb, LANES), lambda h, i: (i, h)), hb_spec, row_blk],
        out_shape=[jax.ShapeDtypeStruct((t_dim, d), BF16),
                   jax.ShapeDtypeStruct((heads // 2, 2, t_dim, LANES), F32),
                   jax.ShapeDtypeStruct((heads // 2, 2, 1, t_dim), F32)],
        compiler_params=_params(dimension_semantics=("parallel", "arbitrary")),
    )(prune, qkv, qkv, qkv, c_hb, c_rows)


def _flash_dq(qkv, o, do, c_hb, c_rows, lse_hb, prune, *, tb, name):
    t_dim = qkv.shape[0]
    d = qkv.shape[1] // 3
    heads = d // HEAD_DIM
    cb = d // LANES
    scale = HEAD_DIM ** -0.5
    nq = t_dim // tb

    def body(prune_ref, q_ref, k_ref, v_ref, o_ref, do_ref, cc_ref, cr_ref, lse_ref, dq_ref, dl_ref, rs_ref):
        i = pl.program_id(1)
        h0 = 2 * pl.program_id(0)
        q = q_ref[...] * jnp.asarray(scale, BF16)
        do_blk = do_ref[...]
        prod = do_blk.astype(F32) * o_ref[...].astype(F32)
        masks = _head_masks(tb)
        row = lax.broadcasted_iota(jnp.int32, (tb, tb), 0)
        col = lax.broadcasted_iota(jnp.int32, (tb, tb), 1)
        qs = [jnp.where(masks[e], q, jnp.zeros_like(q)) for e in range(2)]
        dos = [jnp.where(masks[e], do_blk, jnp.zeros_like(do_blk)) for e in range(2)]
        deltas = [jnp.sum(jnp.where(masks[e], prod, 0.0), axis=1, keepdims=True) for e in range(2)]
        ccols = [cc_ref[0, e][:, 0:1] for e in range(2)]
        lses = [lse_ref[0, e][:, 0:1] for e in range(2)]

        def step(j, carry, diagonal):
            off = pl.multiple_of(j * tb, tb)
            kj = k_ref[pl.ds(off, tb), :]
            vj = v_ref[pl.ds(off, tb), :]
            out = []
            for e in range(2):
                acc, rsum = carry[e]
                crow = cr_ref[0, e, :, pl.ds(off, tb)]
                s = lax.dot_general(qs[e], kj, (((1,), (1,)), ((), ())), preferred_element_type=F32)
                s = s + (ccols[e] - crow)
                if diagonal:
                    s = jnp.where(col <= row, s, NEG_INF)
                p = jnp.exp(s - lses[e])
                dp = lax.dot_general(dos[e], vj, (((1,), (1,)), ((), ())), preferred_element_type=F32)
                ds = p * (dp - deltas[e])
                out.append((acc + jnp.dot(ds.astype(BF16), kj, preferred_element_type=F32),
                            rsum + jnp.sum(ds, axis=1, keepdims=True)))
            return tuple(out)

        init = (jnp.zeros((tb, LANES), F32), jnp.zeros((tb, 1), F32))
        kept = jnp.maximum(_kept_before(prune_ref, h0, i, nq), _kept_before(prune_ref, h0 + 1, i, nq))
        carry = lax.fori_loop(i - kept, i, functools.partial(step, diagonal=False), (init, init))
        carry = step(i, carry, True)
        for e in range(2):
            dl_ref[0, e] = _as_row(deltas[e], tb)
            rs_ref[0, e] = _as_row(carry[e][1], tb)
        dq_ref[...] = (jnp.where(masks[0], carry[0][0], carry[1][0]) * scale).astype(dq_ref.dtype)

    blk = pl.BlockSpec((tb, LANES), lambda h, i: (i, h))
    hb_spec = pl.BlockSpec((1, 2, tb, LANES), lambda h, i: (h, 0, i, 0))
    row_spec = pl.BlockSpec((1, 2, 1, t_dim), lambda h, i: (h, 0, 0, 0))
    row_blk = pl.BlockSpec((1, 2, 1, tb), lambda h, i: (h, 0, 0, i))
    row_shape = jax.ShapeDtypeStruct((heads // 2, 2, 1, t_dim), F32)
    return pl.pallas_call(
        body, name=name, grid=(heads // 2, nq),
        in_specs=[pl.BlockSpec(memory_space=pltpu.SMEM), blk,
                  pl.BlockSpec((t_dim, LANES), lambda h, i: (0, cb + h)),
                  pl.BlockSpec((t_dim, LANES), lambda h, i: (0, 2 * cb + h)),
                  blk, blk, hb_spec, row_spec, hb_spec],
        out_specs=[blk, row_blk, row_blk],
        out_shape=[jax.ShapeDtypeStruct((t_dim, d), BF16), row_shape, row_shape],
        compiler_params=_params(dimension_semantics=("parallel", "arbitrary")),
    )(prune, qkv, qkv, qkv, o, do, c_hb, c_rows, lse_hb)


def _flash_dkv(qkv, do, c_hb, c_rows, lse_rows, delta_rows, prune, *, tb, name):
    t_dim = qkv.shape[0]
    d = qkv.shape[1] // 3
    heads = d // HEAD_DIM
    cb = d // LANES
    scale = HEAD_DIM ** -0.5
    nq = t_dim // tb

    def body(prune_ref, q_ref, k_ref, v_ref, do_ref, cc_ref, cr_ref, lr_ref, dr_ref, dk_ref, dv_ref, dsum_ref):
        j = pl.program_id(1)
        h0 = 2 * pl.program_id(0)
        k_blk = k_ref[...] * jnp.asarray(scale, BF16)
        v_blk = v_ref[...]
        masks = _head_masks(tb)
        row = lax.broadcasted_iota(jnp.int32, (tb, tb), 0)
        col = lax.broadcasted_iota(jnp.int32, (tb, tb), 1)
        ks = [jnp.where(masks[e], k_blk, jnp.zeros_like(k_blk)) for e in range(2)]
        vs = [jnp.where(masks[e], v_blk, jnp.zeros_like(v_blk)) for e in range(2)]
        ccols = [cc_ref[0, e][:, 0:1] for e in range(2)]

        def step(i, carry, diagonal):
            off = pl.multiple_of(i * tb, tb)
            qi = q_ref[pl.ds(off, tb), :]
            doi = do_ref[pl.ds(off, tb), :]
            out = []
            for e in range(2):
                dk, dv, dsum = carry[e]
                crow = cr_ref[0, e, :, pl.ds(off, tb)]
                lse = lr_ref[0, e, :, pl.ds(off, tb)]
                delta = dr_ref[0, e, :, pl.ds(off, tb)]
                st = lax.dot_general(ks[e], qi, (((1,), (1,)), ((), ())), preferred_element_type=F32)
                st = st + (crow - ccols[e])
                if diagonal:
                    st = jnp.where(col >= row, st, NEG_INF)
                pt = jnp.exp(st - lse)
                dpt = lax.dot_general(vs[e], doi, (((1,), (1,)), ((), ())), preferred_element_type=F32)
                dst = pt * (dpt - delta)
                out.append((dk + jnp.dot(dst.astype(BF16), qi, preferred_element_type=F32),
                            dv + jnp.dot(pt.astype(BF16), doi, preferred_element_type=F32),
                            dsum + jnp.sum(dst, axis=1, keepdims=True)))
            return tuple(out)

        zero = jnp.zeros((tb, LANES), F32)
        init = (zero, zero, jnp.zeros((tb, 1), F32))
        carry = step(j, (init, init), True)
        kept = jnp.maximum(_kept_after(prune_ref, h0, j, nq), _kept_after(prune_ref, h0 + 1, j, nq))
        carry = lax.fori_loop(j + 1, j + 1 + kept, functools.partial(step, diagonal=False), carry)
        for e in range(2):
            dsum_ref[0, e] = _as_row(carry[e][2], tb)
        dk_ref[...] = (jnp.where(masks[0], carry[0][0], carry[1][0]) * scale).astype(dk_ref.dtype)
        dv_ref[...] = jnp.where(masks[0], carry[0][1], carry[1][1]).astype(dv_ref.dtype)

    blk = pl.BlockSpec((tb, LANES), lambda h, j: (j, h))
    hb_spec = pl.BlockSpec((1, 2, tb, LANES), lambda h, j: (h, 0, j, 0))
    row_spec = pl.BlockSpec((1, 2, 1, t_dim), lambda h, j: (h, 0, 0, 0))
    row_blk = pl.BlockSpec((1, 2, 1, tb), lambda h, j: (h, 0, 0, j))
    return pl.pallas_call(
        body, name=name, grid=(heads // 2, nq),
        in_specs=[pl.BlockSpec(memory_space=pltpu.SMEM),
                  pl.BlockSpec((t_dim, LANES), lambda h, j: (0, h)),
                  pl.BlockSpec((tb, LANES), lambda h, j: (j, cb + h)),
                  pl.BlockSpec((tb, LANES), lambda h, j: (j, 2 * cb + h)),
                  pl.BlockSpec((t_dim, LANES), lambda h, j: (0, h)),
                  hb_spec, row_spec, row_spec, row_spec],
        out_specs=[blk, blk, row_blk],
        out_shape=[jax.ShapeDtypeStruct((t_dim, d), BF16), jax.ShapeDtypeStruct((t_dim, d), BF16),
                   jax.ShapeDtypeStruct((heads // 2, 2, 1, t_dim), F32)],
        compiler_params=_params(dimension_semantics=("parallel", "arbitrary")),
    )(prune, qkv, qkv, qkv, do, c_hb, c_rows, lse_rows, delta_rows)


def _shift_down(z, prev, n, tt):
    out = pltpu.roll(z, n, axis=0)
    row = lax.broadcasted_iota(jnp.int32, z.shape, 0)
    for r in range(n):
        out = jnp.where(row == r, prev[8 - n + r:8 - n + r + 1, :], out)
    return out


def _shift_up(z, nxt, n, tt):
    out = pltpu.roll(z, tt - n, axis=0)
    row = lax.broadcasted_iota(jnp.int32, z.shape, 0)
    for r in range(n):
        out = jnp.where(row == tt - n + r, nxt[r:r + 1, :], out)
    return out


def _conv_fwd(proj, conv_w, *, name, tt=256):
    t_dim, d3 = proj.shape
    d = d3 // 3
    tt = min(tt, t_dim)

    def body(p_ref, prev_ref, w_ref, y_ref):
        i = pl.program_id(0)
        p = p_ref[...]
        pp = prev_ref[...]
        z = p[:, d:2 * d] * p[:, 2 * d:]
        zp = jnp.where(i > 0, pp[:, d:2 * d] * pp[:, 2 * d:], 0.0)
        w = w_ref[...]
        zc = w[2:3, :] * z + w[1:2, :] * _shift_down(z, zp, 1, tt) + w[0:1, :] * _shift_down(z, zp, 2, tt)
        y_ref[...] = (p[:, :d] * zc).astype(y_ref.dtype)

    return pl.pallas_call(
        body, name=name, grid=(t_dim // tt,),
        in_specs=[pl.BlockSpec((tt, d3), lambda i: (i, 0)),
                  pl.BlockSpec((8, d3), lambda i: (jnp.maximum(i * (tt // 8) - 1, 0), 0)),
                  pl.BlockSpec(conv_w.shape, lambda i: (0, 0))],
        out_specs=pl.BlockSpec((tt, d), lambda i: (i, 0)),
        out_shape=jax.ShapeDtypeStruct((t_dim, d), BF16),
        compiler_params=_params(dimension_semantics=("arbitrary",)),
    )(proj, proj, conv_w)


def _conv_bwd(proj, dy, conv_w, *, name, tt=256):
    t_dim, d3 = proj.shape
    d = d3 // 3
    tt = min(tt, t_dim)
    n = t_dim // tt

    def body(p_ref, prev_ref, next_ref, dy_ref, dyn_ref, w_ref, dp_ref, dw_ref):
        i = pl.program_id(0)
        p = p_ref[...]
        pp = prev_ref[...]
        pn = next_ref[...]
        bg, cg, u = p[:, :d], p[:, d:2 * d], p[:, 2 * d:]
        z = cg * u
        zp = jnp.where(i > 0, pp[:, d:2 * d] * pp[:, 2 * d:], 0.0)
        w = w_ref[...]
        z1 = _shift_down(z, zp, 1, tt)
        z2 = _shift_down(z, zp, 2, tt)
        zc = w[2:3, :] * z + w[1:2, :] * z1 + w[0:1, :] * z2
        dy_blk = dy_ref[...]
        dzc = dy_blk * bg
        dzn = jnp.where(i < n - 1, dyn_ref[...] * pn[:, :d], 0.0)
        dz = w[2:3, :] * dzc + w[1:2, :] * _shift_up(dzc, dzn, 1, tt) + w[0:1, :] * _shift_up(dzc, dzn, 2, tt)
        dp_ref[:, :d] = (dy_blk * zc).astype(dp_ref.dtype)
        dp_ref[:, d:2 * d] = (dz * u).astype(dp_ref.dtype)
        dp_ref[:, 2 * d:] = (dz * cg).astype(dp_ref.dtype)
        part = jnp.concatenate([jnp.sum(dzc * z2, axis=0, keepdims=True),
                                jnp.sum(dzc * z1, axis=0, keepdims=True),
                                jnp.sum(dzc * z, axis=0, keepdims=True),
                                jnp.zeros((5, d), F32)], axis=0)

        @pl.when(i == 0)
        def _():
            dw_ref[...] = part

        @pl.when(i > 0)
        def _():
            dw_ref[...] += part

    last8 = t_dim // 8 - 1
    return pl.pallas_call(
        body, name=name, grid=(n,),
        in_specs=[pl.BlockSpec((tt, d3), lambda i: (i, 0)),
                  pl.BlockSpec((8, d3), lambda i: (jnp.maximum(i * (tt // 8) - 1, 0), 0)),
                  pl.BlockSpec((8, d3), lambda i: (jnp.minimum((i + 1) * (tt // 8), last8), 0)),
                  pl.BlockSpec((tt, d), lambda i: (i, 0)),
                  pl.BlockSpec((8, d), lambda i: (jnp.minimum((i + 1) * (tt // 8), last8), 0)),
                  pl.BlockSpec(conv_w.shape, lambda i: (0, 0))],
        out_specs=[pl.BlockSpec((tt, d3), lambda i: (i, 0)), pl.BlockSpec((8, d), lambda i: (0, 0))],
        out_shape=[jax.ShapeDtypeStruct((t_dim, d3), BF16), jax.ShapeDtypeStruct((8, d), F32)],
        compiler_params=_params(dimension_semantics=("arbitrary",)),
    )(proj, proj, proj, dy, dy, conv_w)


def _window(ref, axis, n, idx):
    if axis is None:
        return ref
    sel = [slice(None)] * len(ref.shape)
    sel[axis] = pl.ds(pl.multiple_of(idx * n, n), n)
    return ref.at[tuple(sel)]


def _exchange(items, *, gather, name):
    n_w = len(items)
    widths = []
    out_shape = []
    for arr, axis in items:
        shp = list(arr.shape)
        if gather:
            widths.append(shp[axis])
            shp[axis] *= N_DEV
        else:
            if axis is not None:
                shp[axis] //= N_DEV
                widths.append(shp[axis])
            else:
                widths.append(None)
            shp = [N_DEV] + shp
        out_shape.append(jax.ShapeDtypeStruct(tuple(shp), arr.dtype))

    def body(*refs):
        src, dst = refs[:n_w], refs[n_w:2 * n_w]
        send_sems, recv_sems, local_sems = refs[2 * n_w:]
        x, y, c = lax.axis_index("x"), lax.axis_index("y"), lax.axis_index("c")
        me = 4 * x + 2 * y + c

        def ends(w, to):
            axis = items[w][1]
            if gather:
                return src[w], _window(dst[w], axis, widths[w], me)
            return _window(src[w], axis, widths[w], to), dst[w].at[me]

        copies = []
        for w in range(n_w):
            cp = pltpu.make_async_copy(*ends(w, me), local_sems.at[w])
            cp.start()
            copies.append(cp)
        for k in range(1, N_DEV):
            px = 1 - x if k & 4 else x
            py = 1 - y if k & 2 else y
            pc = 1 - c if k & 1 else c
            for w in range(n_w):
                s_ref, d_ref = ends(w, 4 * px + 2 * py + pc)
                cp = pltpu.make_async_remote_copy(
                    src_ref=s_ref, dst_ref=d_ref, send_sem=send_sems.at[w, k - 1], recv_sem=recv_sems.at[w, k - 1],
                    device_id=(px, py, pc), device_id_type=pl.DeviceIdType.MESH)
                cp.start()
                copies.append(cp)
        for cp in copies:
            cp.wait()

    return pl.pallas_call(
        body, name=name,
        in_specs=[pl.BlockSpec(memory_space=pl.ANY)] * n_w,
        out_specs=[pl.BlockSpec(memory_space=pl.ANY)] * n_w,
        out_shape=out_shape,
        scratch_shapes=[pltpu.SemaphoreType.DMA((n_w, N_DEV - 1)), pltpu.SemaphoreType.DMA((n_w, N_DEV - 1)),
                        pltpu.SemaphoreType.DMA((n_w,))],
    )(*[arr for arr, _ in items])


def _row_tile(rows, cols):
    tr = rows
    while tr % 16 == 0 and tr * cols > 256 * 1024:
        tr //= 2
    return tr


def _sum_parts(parts, *, name):
    n_parts, rows, cols = parts.shape
    tr = _row_tile(rows, cols)

    def body(p_ref, o_ref):
        g = p_ref[0].astype(F32)
        for s in range(1, n_parts):
            g = g + p_ref[s].astype(F32)
        o_ref[...] = g

    return pl.pallas_call(
        body, name=name, grid=(rows // tr,),
        in_specs=[pl.BlockSpec((n_parts, tr, cols), lambda i: (0, i, 0))],
        out_specs=pl.BlockSpec((tr, cols), lambda i: (i, 0)),
        out_shape=jax.ShapeDtypeStruct((rows, cols), F32),
        compiler_params=_params(dimension_semantics=("parallel",)),
    )(parts)


def _adamw(parts, w, m, v, *, name):
    n_parts, rows, cols = parts.shape
    tr = _row_tile(rows, cols)

    def body(p_ref, w_ref, m_ref, v_ref, g_ref, d_ref, nm_ref, nv_ref):
        g = p_ref[0].astype(F32)
        for s in range(1, n_parts):
            g = g + p_ref[s].astype(F32)
        m_new = ADAM_B1 * m_ref[...] + (1.0 - ADAM_B1) * g
        v_new = ADAM_B2 * v_ref[...] + (1.0 - ADAM_B2) * (g * g)
        m_hat = m_new / (1.0 - ADAM_B1 ** ADAM_STEP)
        v_hat = v_new / (1.0 - ADAM_B2 ** ADAM_STEP)
        g_ref[...] = g
        d_ref[...] = -ADAM_LR * (m_hat / (jnp.sqrt(v_hat) + ADAM_EPS) + ADAM_WD * w_ref[...])
        nm_ref[...] = m_new
        nv_ref[...] = v_new

    spec = pl.BlockSpec((tr, cols), lambda i: (i, 0))
    return pl.pallas_call(
        body, name=name, grid=(rows // tr,),
        in_specs=[pl.BlockSpec((n_parts, tr, cols), lambda i: (0, i, 0)), spec, spec, spec],
        out_specs=[spec] * 4,
        out_shape=[jax.ShapeDtypeStruct((rows, cols), F32)] * 4,
        compiler_params=_params(dimension_semantics=("parallel",)),
    )(parts, w, m, v)


def _pad_rows(a, axis, to):
    pad = [(0, 0)] * a.ndim
    pad[axis] = (0, to - a.shape[axis])
    return jnp.pad(a, pad)


def kernel(x, p, norm_g, w_attn_in, b_forget, w_attn_out, w_conv_in, conv_w, w_conv_out, w_mlp_up, w_mlp_down, w_ple_proj, w_ple_gate, loss_target, m_norm_g, m_w_attn_in, m_b_forget, m_w_attn_out, m_w_conv_in, m_conv_w, m_w_conv_out, m_w_mlp_up, m_w_mlp_down, m_w_ple_proj, m_w_ple_gate, v_norm_g, v_w_attn_in, v_b_forget, v_w_attn_out, v_w_conv_in, v_conv_w, v_w_conv_out, v_w_mlp_up, v_w_mlp_down, v_w_ple_proj, v_w_ple_gate):
    shards = dict(norm_g=norm_g, w_attn_in=w_attn_in, b_forget=b_forget, w_attn_out=w_attn_out,
                  w_conv_in=w_conv_in, conv_w=conv_w, w_conv_out=w_conv_out, w_mlp_up=w_mlp_up,
                  w_mlp_down=w_mlp_down, w_ple_proj=w_ple_proj, w_ple_gate=w_ple_gate)
    m_shards = dict(norm_g=m_norm_g, w_attn_in=m_w_attn_in, b_forget=m_b_forget, w_attn_out=m_w_attn_out,
                    w_conv_in=m_w_conv_in, conv_w=m_conv_w, w_conv_out=m_w_conv_out, w_mlp_up=m_w_mlp_up,
                    w_mlp_down=m_w_mlp_down, w_ple_proj=m_w_ple_proj, w_ple_gate=m_w_ple_gate)
    v_shards = dict(norm_g=v_norm_g, w_attn_in=v_w_attn_in, b_forget=v_b_forget, w_attn_out=v_w_attn_out,
                    w_conv_in=v_w_conv_in, conv_w=v_conv_w, w_conv_out=v_w_conv_out, w_mlp_up=v_w_mlp_up,
                    w_mlp_down=v_w_mlp_down, w_ple_proj=v_w_ple_proj, w_ple_gate=v_w_ple_gate)
    t_dim, d = x.shape[-2:]
    depth = p.shape[0]
    n_attn, heads = b_forget.shape
    assert d == heads * HEAD_DIM and x.shape[0] == 1
    tb = min(256, t_dim // 2)
    x0 = x.reshape(t_dim, d)
    target = loss_target.reshape(t_dim, d)

    in_cols = w_attn_in.shape[2]
    in_cols_pad = -(-in_cols // 16) * 16
    gather_names = [n for n in WEIGHT_NAMES if n != 'b_forget']
    gather_items = []
    for n in gather_names:
        if n == 'w_attn_in':
            gather_items.append((_pad_rows(jnp.swapaxes(w_attn_in, 1, 2), 1, in_cols_pad).astype(BF16), 1))
        elif n in ('norm_g', 'conv_w'):
            gather_items.append((shards[n], SHARD_AXIS[n]))
        else:
            gather_items.append((shards[n].astype(BF16), SHARD_AXIS[n]))
    full = dict(zip(gather_names, _exchange(gather_items, gather=True, name="gather_weights")))
    gains = full['norm_g']
    taps = full['conv_w']
    w_in_t = full['w_attn_in'].reshape(n_attn, N_DEV, in_cols_pad, d)[:, :, :in_cols]
    w_in_t = _pad_rows(w_in_t.reshape(n_attn, N_DEV * in_cols, d), 1, 3 * d + LANES)
    bias_pad = jnp.pad(b_forget, ((0, 0), (0, LANES - heads)))

    def gain(i, k):
        return gains[i, k].reshape(1, d)

    def add_norm(x_prev, branch, g_branch, g_next, name):
        def fn(rows, vecs):
            x_new = rows[0] + _norm(rows[1], vecs[0])
            return [x_new, _norm(x_new, vecs[1])], []
        return _rows_call(fn, [x_prev, branch], [g_branch, g_next], [(d, F32), (d, BF16)], [], name=name)

    saved = []
    x_cur = x0
    hn = _rows_call(lambda rows, vecs: ([_norm(rows[0], vecs[0])], []), [x0], [gain(0, 0)], [(d, BF16)], [],
                    name="norm_in")[0]
    loss_rows = dy = None
    for i in range(depth):
        j = i // 2
        s = dict(x0=x_cur, hn=hn)
        if i % 2 == 0:
            s['qkv'] = _mm(hn, w_in_t[j, :3 * d], tb=True, out_dtypes=(BF16,), name=f"attn_in_{i}")
            s['fl'] = _mm(hn, w_in_t[j, 3 * d:], tb=True, name=f"attn_gate_{i}")
            c = _cumsum_fwd(s['fl'], bias_pad[j:j + 1], name=f"gate_cumsum_{i}")
            c_t = c[:, :heads].T
            s['prune'] = _prune_table(c_t, _head_norms(s['qkv'], name=f"head_norms_{i}"), tb)
            c_t = c_t.reshape(heads // 2, 2, t_dim)
            s['c_hb'] = jnp.broadcast_to(c_t[:, :, :, None], (heads // 2, 2, t_dim, LANES))
            s['c_rows'] = c_t.reshape(heads // 2, 2, 1, t_dim)
            s['o'], s['lse_hb'], s['lse_rows'] = _flash_fwd(s['qkv'], s['c_hb'], s['c_rows'], s['prune'], tb=tb,
                                                           name=f"attn_fwd_{i}")
            s['m'] = _mm(s['o'], full['w_attn_out'][j], name=f"attn_out_{i}")
        else:
            s['proj'] = _mm(hn, full['w_conv_in'][j], name=f"conv_in_{i}")
            s['y'] = _conv_fwd(s['proj'], taps[j], name=f"conv_fwd_{i}")
            s['m'] = _mm(s['y'], full['w_conv_out'][j], name=f"conv_out_{i}")
        s['x1'], s['h2'] = add_norm(x_cur, s['m'], gain(i, 1), gain(i, 2), f"mix_norm_{i}")
        s['u'], s['a'] = _mm(s['h2'], full['w_mlp_up'][i], out_dtypes=(BF16, BF16), name=f"mlp_up_{i}",
                             epi=lambda acc: (acc, jnp.square(jnp.maximum(acc, 0.0))))
        s['f'] = _mm(s['a'], full['w_mlp_down'][i], name=f"mlp_down_{i}")
        s['x2'], s['h4'] = add_norm(s['x1'], s['f'], gain(i, 3), gain(i, 4), f"mlp_norm_{i}")
        s['pi'] = p[i].reshape(t_dim, -1)
        s['pp'] = _mm(s['pi'], full['w_ple_proj'][i], name=f"ple_proj_{i}")
        s['gl'], s['e'] = _mm(s['h4'], full['w_ple_gate'][i], extras=(s['pp'],), out_dtypes=(F32, F32),
                              name=f"ple_gate_{i}", epi=lambda acc, pp: (acc, pp * _sigmoid(acc)))
        if i + 1 < depth:
            x_cur, hn = add_norm(s['x2'], s['e'], gain(i, 5), gain(i + 1, 0), f"ple_norm_{i}")
        else:
            def loss_fn(rows, vecs):
                err = rows[0] + _norm(rows[1], vecs[0]) - rows[2]
                part = 0.5 * jnp.sum(jnp.sum(err * err, axis=1, keepdims=True), axis=0, keepdims=True) / d
                return [err / d], [jnp.broadcast_to(part, (1, LANES))]
            dy, loss_rows = _rows_call(loss_fn, [s['x2'], s['e'], target], [gain(i, 5)], [(d, F32)],
                                       [(1, LANES)], name="loss")
        saved.append(s)
    loss = lax.psum(loss_rows[0, 0], ("x", "y", "c"))

    grads = {n: [None] * shards[n].shape[0] for n in WEIGHT_NAMES}
    d_gains = [[None] * 6 for _ in range(depth)]
    wgrad = functools.partial(_mm, ta=True, out_dtypes=(BF16,))
    dx = dy
    for i in reversed(range(depth)):
        j = i // 2
        s = saved[i]

        def ple_fn(rows, vecs):
            de, dg = _norm_bwd(rows[0], vecs[0], rows[1])
            sg = _sigmoid(rows[2])
            return [de * sg, de * rows[3] * sg * (1.0 - sg)], [dg]
        dpp, dgl, d_gains[i][5] = _rows_call(ple_fn, [s['e'], dx, s['gl'], s['pp']], [gain(i, 5)],
                                             [(d, BF16), (d, BF16)], [(1, d)], name=f"ple_bwd_{i}")
        grads['w_ple_proj'][i] = wgrad(s['pi'], dpp, name=f"ple_proj_dw_{i}")
        grads['w_ple_gate'][i] = wgrad(s['h4'], dgl, name=f"ple_gate_dw_{i}")
        dh4 = _mm(dgl, full['w_ple_gate'][i], tb=True, name=f"ple_gate_dx_{i}")

        def two_norm_bwd(x_res, dh, dx_in, branch, g_res, g_branch, name):
            def fn(rows, vecs):
                d_res, dg_res = _norm_bwd(rows[0], vecs[0], rows[1])
                dx_out = rows[2] + d_res
                d_branch, dg_branch = _norm_bwd(rows[3], vecs[1], dx_out)
                return [dx_out, d_branch], [dg_res, dg_branch]
            return _rows_call(fn, [x_res, dh, dx_in, branch], [g_res, g_branch], [(d, F32), (d, BF16)],
                              [(1, d), (1, d)], name=name)

        dx2, df, d_gains[i][4], d_gains[i][3] = two_norm_bwd(s['x2'], dh4, dx, s['f'], gain(i, 4), gain(i, 3),
                                                            f"mlp_norm_bwd_{i}")
        grads['w_mlp_down'][i] = wgrad(s['a'], df, name=f"mlp_down_dw_{i}")
        du = _mm(df, full['w_mlp_down'][i], tb=True, extras=(s['u'],), out_dtypes=(BF16,), name=f"mlp_down_dx_{i}",
                 epi=lambda acc, u: (acc * (2.0 * jnp.maximum(u.astype(F32), 0.0)),))
        grads['w_mlp_up'][i] = wgrad(s['h2'], du, name=f"mlp_up_dw_{i}")
        dh2 = _mm(du, full['w_mlp_up'][i], tb=True, name=f"mlp_up_dx_{i}")
        dx1, dm, d_gains[i][2], d_gains[i][1] = two_norm_bwd(s['x1'], dh2, dx2, s['m'], gain(i, 2), gain(i, 1),
                                                            f"mix_norm_bwd_{i}")
        if i % 2 == 0:
            grads['w_attn_out'][j] = wgrad(s['o'], dm, name=f"attn_out_dw_{i}")
            do = _mm(dm, full['w_attn_out'][j], tb=True, out_dtypes=(BF16,), name=f"attn_out_dx_{i}")
            dq, delta_rows, rsum_rows = _flash_dq(s['qkv'], s['o'], do, s['c_hb'], s['c_rows'], s['lse_hb'],
                                                  s['prune'], tb=tb, name=f"attn_dq_{i}")
            dk, dv, csum_rows = _flash_dkv(s['qkv'], do, s['c_hb'], s['c_rows'], s['lse_rows'], delta_rows,
                                           s['prune'], tb=tb, name=f"attn_dkv_{i}")
            dc = (rsum_rows - csum_rows).reshape(heads, t_dim).T
            dfl, db = _cumsum_bwd(jnp.pad(dc, ((0, 0), (0, LANES - heads))), s['fl'], bias_pad[j:j + 1],
                                  name=f"gate_cumsum_bwd_{i}")
            grads['b_forget'][j] = db[0, :heads]
            dproj = jnp.concatenate([dq, dk, dv, dfl], axis=1)
            grads['w_attn_in'][j] = wgrad(dproj, s['hn'], name=f"attn_in_dw_{i}")
            dhn = _mm(dproj, w_in_t[j], name=f"attn_in_dx_{i}")
        else:
            grads['w_conv_out'][j] = wgrad(s['y'], dm, name=f"conv_out_dw_{i}")
            dyc = _mm(dm, full['w_conv_out'][j], tb=True, name=f"conv_out_dx_{i}")
            dproj, dtaps = _conv_bwd(s['proj'], dyc, taps[j], name=f"conv_bwd_{i}")
            grads['conv_w'][j] = dtaps[:3].astype(BF16)
            grads['w_conv_in'][j] = wgrad(s['hn'], dproj, name=f"conv_in_dw_{i}")
            dhn = _mm(dproj, full['w_conv_in'][j], tb=True, name=f"conv_in_dx_{i}")

        def in_fn(rows, vecs):
            d_res, dg = _norm_bwd(rows[0], vecs[0], rows[1])
            return [rows[2] + d_res], [dg]
        dx, d_gains[i][0] = _rows_call(in_fn, [s['x0'], dhn, dx1], [gain(i, 0)], [(d, F32)], [(1, d)],
                                       name=f"in_norm_bwd_{i}")
    grad_x = dx.reshape(x.shape)

    local = {n: jnp.stack(grads[n]) for n in WEIGHT_NAMES if n not in ('norm_g', 'b_forget')}
    local['norm_g'] = jnp.stack([jnp.concatenate(row, axis=0) for row in d_gains]).astype(BF16)
    g_in_t = local['w_attn_in'][:, :N_DEV * in_cols].reshape(n_attn, N_DEV, in_cols, d)
    local['w_attn_in'] = _pad_rows(g_in_t, 2, in_cols_pad).reshape(n_attn, N_DEV * in_cols_pad, d)
    local['b_forget'] = jnp.zeros((8, LANES), F32).at[:n_attn, :heads].set(jnp.stack(grads['b_forget']))
    axis_of = dict(SHARD_AXIS, w_attn_in=1)
    recv = dict(zip(WEIGHT_NAMES, _exchange([(local[n], axis_of[n]) for n in WEIGHT_NAMES], gather=False,
                                            name="exchange_grads")))
    recv['b_forget'] = recv['b_forget'][:, :n_attn, :heads]
    g_in_t = _sum_parts(recv['w_attn_in'].reshape(N_DEV, n_attn * in_cols_pad, d), name="sum_attn_in")
    recv['w_attn_in'] = jnp.swapaxes(g_in_t.reshape(n_attn, in_cols_pad, d)[:, :in_cols], 1, 2)[None]
    results = {}
    for n in WEIGHT_NAMES:
        shp = shards[n].shape
        flat = lambda a: a.reshape(a.shape[:a.ndim - len(shp)] + (-1, shp[-1]))
        outs = _adamw(flat(recv[n]), flat(shards[n]), flat(m_shards[n]), flat(v_shards[n]), name=f"adamw_{n}")
        results[n] = [o.reshape(shp) for o in outs]
    return (loss, grad_x, *[results[n][k] for k in range(4) for n in WEIGHT_NAMES])
```

```python
import functools

import jax
import jax.numpy as jnp
from jax import lax
from jax.experimental import pallas as pl
from jax.experimental.pallas import tpu as pltpu

F32 = jnp.float32
BF16 = jnp.bfloat16

N_DEV = 8
LANES = 128
HEAD_DIM = 64
VMEM_LIMIT_BYTES = 56 * 1024 * 1024
RMS_EPS = 1e-6
NEG_INF = -1e30
ADAM_LR = 0.001
ADAM_B1 = 0.9
ADAM_B2 = 0.999
ADAM_EPS = 1e-08
ADAM_WD = 0.01
ADAM_STEP = 10
WEIGHT_NAMES = ('norm_g', 'w_attn_in', 'b_forget', 'w_attn_out', 'w_conv_in', 'conv_w', 'w_conv_out',
                'w_mlp_up', 'w_mlp_down', 'w_ple_proj', 'w_ple_gate')
SHARD_AXIS = {'norm_g': 2, 'w_attn_in': 2, 'b_forget': None, 'w_attn_out': 1, 'w_conv_in': 2, 'conv_w': 2,
              'w_conv_out': 1, 'w_mlp_up': 2, 'w_mlp_down': 1, 'w_ple_proj': 2, 'w_ple_gate': 1}


def _params(**kw):
    return pltpu.CompilerParams(vmem_limit_bytes=VMEM_LIMIT_BYTES, **kw)


def _tile(n, cap):
    if n <= cap:
        return n
    t = (cap // LANES) * LANES
    while n % t:
        t -= LANES
    return t


def _mm(a, b, *, ta=False, tb=False, extras=(), epi=None, out_dtypes=(F32,), name):
    m_dim, k_dim = (a.shape[1], a.shape[0]) if ta else a.shape
    n_dim = b.shape[0] if tb else b.shape[1]
    assert k_dim == (b.shape[1] if tb else b.shape[0])
    tk = _tile(k_dim, 1024 if k_dim <= 1024 else 2048)
    nk = k_dim // tk
    simple = not extras and len(out_dtypes) == 1
    tm = _tile(m_dim, 1024 if (nk > 1 and simple) else 512)
    tn = _tile(n_dim, 1024)
    grid = (n_dim // tn, m_dim // tm, nk)
    a_spec = (pl.BlockSpec((tk, tm), lambda j, i, k: (k, i)) if ta
              else pl.BlockSpec((tm, tk), lambda j, i, k: (i, k)))
    b_spec = (pl.BlockSpec((tn, tk), lambda j, i, k: (j, k)) if tb
              else pl.BlockSpec((tk, tn), lambda j, i, k: (k, j)))
    mn_spec = pl.BlockSpec((tm, tn), lambda j, i, k: (i, j))
    dims = (((0 if ta else 1,), (1 if tb else 0,)), ((), ()))
    n_extra, n_out = len(extras), len(out_dtypes)
    if epi is None:
        epi = lambda acc: (acc,)

    def body(a_ref, b_ref, *rest):
        e_refs, o_refs = rest[:n_extra], rest[n_extra:n_extra + n_out]
        part = lax.dot_general(a_ref[...].astype(BF16), b_ref[...].astype(BF16), dims,
                               preferred_element_type=F32)

        def finish(acc):
            for o_ref, val in zip(o_refs, epi(acc, *[e[...] for e in e_refs])):
                o_ref[...] = val.astype(o_ref.dtype)

        if nk == 1:
            finish(part)
        else:
            acc_ref = rest[-1]
            k = pl.program_id(2)

            @pl.when(k == 0)
            def _():
                acc_ref[...] = part

            @pl.when(k > 0)
            def _():
                acc_ref[...] += part

            @pl.when(k == nk - 1)
            def _():
                finish(acc_ref[...])

    outs = pl.pallas_call(
        body, name=name, grid=grid,
        in_specs=[a_spec, b_spec] + [mn_spec] * n_extra,
        out_specs=[mn_spec] * n_out,
        out_shape=[jax.ShapeDtypeStruct((m_dim, n_dim), dt) for dt in out_dtypes],
        scratch_shapes=[pltpu.VMEM((tm, tn), F32)] if nk > 1 else [],
        compiler_params=_params(dimension_semantics=("parallel", "parallel", "arbitrary")),
    )(a, b, *extras)
    return outs[0] if n_out == 1 else outs


def _rows(fn, row_ins, vec_ins, row_outs, vec_outs, *, name, tt=256, reverse=False):
    t_dim = row_ins[0].shape[0]
    tt = min(tt, t_dim)
    n = t_dim // tt
    n_ri, n_vi, n_ro, n_vo = len(row_ins), len(vec_ins), len(row_outs), len(vec_outs)
    pos = (lambda i: (n - 1 - i, 0)) if reverse else (lambda i: (i, 0))
    fixed = lambda i: (0, 0)

    def body(*refs):
        ri = refs[:n_ri]
        vi = refs[n_ri:n_ri + n_vi]
        ro = refs[n_ri + n_vi:n_ri + n_vi + n_ro]
        vo = refs[n_ri + n_vi + n_ro:n_ri + n_vi + n_ro + n_vo]
        scratch = refs[n_ri + n_vi + n_ro + n_vo:]
        r_out, v_out = fn([r[...] for r in ri], [v[...] for v in vi], *scratch)
        for o_ref, val in zip(ro, r_out):
            o_ref[...] = val.astype(o_ref.dtype)
        i = pl.program_id(0)
        for o_ref, val in zip(vo, v_out):
            @pl.when(i == 0)
            def _(o_ref=o_ref, val=val):
                o_ref[...] = val

            @pl.when(i > 0)
            def _(o_ref=o_ref, val=val):
                o_ref[...] += val

    return body, dict(
        grid=(n,),
        in_specs=[pl.BlockSpec((tt, r.shape[1]), pos) for r in row_ins]
        + [pl.BlockSpec(v.shape, fixed) for v in vec_ins],
        out_specs=[pl.BlockSpec((tt, w), pos) for w, _ in row_outs]
        + [pl.BlockSpec(s, fixed) for s in vec_outs],
        out_shape=[jax.ShapeDtypeStruct((t_dim, w), dt) for w, dt in row_outs]
        + [jax.ShapeDtypeStruct(s, F32) for s in vec_outs],
        name=name,
        compiler_params=_params(dimension_semantics=("arbitrary",)),
    )


def _rows_call(fn, row_ins, vec_ins, row_outs, vec_outs, *, name, tt=256, reverse=False, scratch=()):
    body, kw = _rows(fn, row_ins, vec_ins, row_outs, vec_outs, name=name, tt=tt, reverse=reverse)
    return pl.pallas_call(body, scratch_shapes=list(scratch), **kw)(*row_ins, *vec_ins)


def _rstd(x):
    return lax.rsqrt(jnp.mean(x * x, axis=-1, keepdims=True) + RMS_EPS)


def _norm(x, g):
    return x * _rstd(x) * g


def _norm_bwd(x, g, dy):
    xh = x * _rstd(x)
    gy = dy * g
    dx = _rstd(x) * (gy - xh * jnp.mean(gy * xh, axis=-1, keepdims=True))
    return dx, jnp.sum(dy * xh, axis=0, keepdims=True)


def _sigmoid(x):
    return 1.0 / (1.0 + jnp.exp(-x))


def _log_sigmoid(x):
    return jnp.minimum(x, 0.0) - jnp.log(1.0 + jnp.exp(-jnp.abs(x)))


def _split3(x):
    hi = x.astype(BF16)
    r1 = x - hi.astype(F32)
    mid = r1.astype(BF16)
    lo = (r1 - mid.astype(F32)).astype(BF16)
    return hi, mid, lo


def _cumsum_fwd(fl, bias, *, name):
    w = fl.shape[1]
    tt = min(512, fl.shape[0])

    def fn(rows, vecs, carry_ref):
        i = pl.program_id(0)

        @pl.when(i == 0)
        def _():
            carry_ref[...] = jnp.zeros_like(carry_ref)

        lf = _log_sigmoid(rows[0] + vecs[0])
        r = lax.broadcasted_iota(jnp.int32, (tt, tt), 0)
        c = lax.broadcasted_iota(jnp.int32, (tt, tt), 1)
        tri = (c <= r).astype(BF16)
        acc = carry_ref[0:1, :]
        for part in _split3(lf):
            acc = acc + jnp.dot(tri, part, preferred_element_type=F32)
        carry_ref[0:1, :] = acc[tt - 1:tt, :]
        return [acc], []

    return _rows_call(fn, [fl], [bias], [(w, F32)], [], name=name, tt=tt,
                      scratch=[pltpu.VMEM((8, w), F32)])[0]


def _cumsum_bwd(dc, fl, bias, *, name):
    w = fl.shape[1]
    tt = min(512, fl.shape[0])

    def fn(rows, vecs, carry_ref):
        i = pl.program_id(0)

        @pl.when(i == 0)
        def _():
            carry_ref[...] = jnp.zeros_like(carry_ref)

        r = lax.broadcasted_iota(jnp.int32, (tt, tt), 0)
        c = lax.broadcasted_iota(jnp.int32, (tt, tt), 1)
        tri = (c >= r).astype(BF16)
        acc = carry_ref[0:1, :]
        for part in _split3(rows[0]):
            acc = acc + jnp.dot(tri, part, preferred_element_type=F32)
        carry_ref[0:1, :] = acc[0:1, :]
        dfl = acc * _sigmoid(-(rows[1] + vecs[0]))
        return [dfl], [jnp.sum(dfl, axis=0, keepdims=True)]

    return _rows_call(fn, [dc, fl], [bias], [(w, BF16)], [(1, w)], name=name, tt=tt, reverse=True,
                      scratch=[pltpu.VMEM((8, w), F32)])


def _head_masks(tb):
    lane = lax.broadcasted_iota(jnp.int32, (tb, LANES), 1)
    return [lane < HEAD_DIM, lane >= HEAD_DIM]


PRUNE_MARGIN = 40.0


def _head_norms(qkv, *, name):
    t_dim = qkv.shape[0]
    d = qkv.shape[1] // 3
    heads = d // HEAD_DIM
    tt = min(512, t_dim)

    def body(q_ref, k_ref, o_ref):
        col = lax.broadcasted_iota(jnp.int32, (d, LANES), 0) // HEAD_DIM
        lane = lax.broadcasted_iota(jnp.int32, (d, LANES), 1)
        tile_max = None
        for ref, first in ((q_ref, 0), (k_ref, heads)):
            x = ref[...].astype(F32)
            sums = jnp.dot((x * x).astype(BF16), (col + first == lane).astype(BF16), preferred_element_type=F32)
            part = jnp.max(sums, axis=0, keepdims=True)
            tile_max = part if tile_max is None else jnp.maximum(tile_max, part)
        i = pl.program_id(0)

        @pl.when(i == 0)
        def _():
            o_ref[...] = tile_max

        @pl.when(i > 0)
        def _():
            o_ref[...] = jnp.maximum(o_ref[...], tile_max)

    return pl.pallas_call(
        body, name=name, grid=(t_dim // tt,),
        in_specs=[pl.BlockSpec((tt, d), lambda i: (i, 0)), pl.BlockSpec((tt, d), lambda i: (i, 1))],
        out_specs=pl.BlockSpec((1, LANES), lambda i: (0, 0)),
        out_shape=jax.ShapeDtypeStruct((1, LANES), F32),
        compiler_params=_params(dimension_semantics=("arbitrary",)),
    )(qkv, qkv)


def _prune_table(c_t, norms, tb):
    heads = c_t.shape[0]
    bound = 1.02 * HEAD_DIM ** -0.5 * jnp.sqrt(norms[0, :heads] * norms[0, heads:2 * heads])
    return jnp.concatenate([c_t[:, ::tb], c_t[:, tb - 1::tb], -(PRUNE_MARGIN + 2.0 * bound)[:, None]], axis=1)


def _kept_before(prune_ref, h, i, nq):
    first, thr = prune_ref[h, i], prune_ref[h, 2 * nq]
    return lax.fori_loop(0, i, lambda j, n: n + (first - prune_ref[h, nq + j] >= thr).astype(jnp.int32),
                         jnp.int32(0))


def _kept_after(prune_ref, h, j, nq):
    last, thr = prune_ref[h, nq + j], prune_ref[h, 2 * nq]
    return lax.fori_loop(j + 1, nq, lambda i, n: n + (prune_ref[h, i] - last >= thr).astype(jnp.int32),
                         jnp.int32(0))


def _as_row(col, tb):
    return jnp.transpose(jnp.broadcast_to(col, (tb, LANES)))[0:1, :]


def _flash_fwd(qkv, c_hb, c_rows, prune, *, tb, name):
    t_dim = qkv.shape[0]
    d = qkv.shape[1] // 3
    heads = d // HEAD_DIM
    cb = d // LANES
    nq = t_dim // tb

    def body(prune_ref, q_ref, k_ref, v_ref, cc_ref, cr_ref, o_ref, lse_ref, lser_ref):
        i = pl.program_id(1)
        h0 = 2 * pl.program_id(0)
        q = q_ref[...] * jnp.asarray(HEAD_DIM ** -0.5, BF16)
        masks = _head_masks(tb)
        row = lax.broadcasted_iota(jnp.int32, (tb, tb), 0)
        col = lax.broadcasted_iota(jnp.int32, (tb, tb), 1)
        qs = [jnp.where(masks[e], q, jnp.zeros_like(q)) for e in range(2)]
        ccols = [cc_ref[0, e][:, 0:1] for e in range(2)]

        def step(j, carry, diagonal):
            off = pl.multiple_of(j * tb, tb)
            kj = k_ref[pl.ds(off, tb), :]
            vj = v_ref[pl.ds(off, tb), :]
            out = []
            for e in range(2):
                m, l, acc = carry[e]
                crow = cr_ref[0, e, :, pl.ds(off, tb)]
                s = lax.dot_general(qs[e], kj, (((1,), (1,)), ((), ())), preferred_element_type=F32)
                s = s + (ccols[e] - crow)
                if diagonal:
                    s = jnp.where(col <= row, s, NEG_INF)
                m_new = jnp.maximum(m, jnp.max(s, axis=1, keepdims=True))
                p = jnp.exp(s - m_new)
                alpha = jnp.exp(m - m_new)
                l = alpha * l + jnp.sum(p, axis=1, keepdims=True)
                acc = alpha * acc + jnp.dot(p.astype(BF16), vj, preferred_element_type=F32)
                out.append((m_new, l, acc))
            return tuple(out)

        init = (jnp.full((tb, 1), NEG_INF, F32), jnp.zeros((tb, 1), F32), jnp.zeros((tb, LANES), F32))
        kept = jnp.maximum(_kept_before(prune_ref, h0, i, nq), _kept_before(prune_ref, h0 + 1, i, nq))
        carry = lax.fori_loop(i - kept, i, functools.partial(step, diagonal=False), (init, init))
        carry = step(i, carry, True)
        outs = []
        for e in range(2):
            m, l, acc = carry[e]
            outs.append(acc / l)
            lse = m + jnp.log(l)
            lse_ref[0, e] = jnp.broadcast_to(lse, (tb, LANES))
            lser_ref[0, e] = _as_row(lse, tb)
        o_ref[...] = jnp.where(masks[0], outs[0], outs[1]).astype(o_ref.dtype)

    hb_spec = pl.BlockSpec((1, 2, tb, LANES), lambda h, i: (h, 0, i, 0))
    row_spec = pl.BlockSpec((1, 2, 1, t_dim), lambda h, i: (h, 0, 0, 0))
    row_blk = pl.BlockSpec((1, 2, 1, tb), lambda h, i: (h, 0, 0, i))
    return pl.pallas_call(
        body, name=name, grid=(heads // 2, nq),
        in_specs=[pl.BlockSpec(memory_space=pltpu.SMEM),
                  pl.BlockSpec((tb, LANES), lambda h, i: (i, h)),
                  pl.BlockSpec((t_dim, LANES), lambda h, i: (0, cb + h)),
                  pl.BlockSpec((t_dim, LANES), lambda h, i: (0, 2 * cb + h)),
                  hb_spec, row_spec],
        out_specs=[pl.BlockSpec((tb, LANES), lambda h, i: (i, h)), hb_spec, row_blk],
        out_shape=[jax.ShapeDtypeStruct((t_dim, d), BF16),
                   jax.ShapeDtypeStruct((heads // 2, 2, t_dim, LANES), F32),
                   jax.ShapeDtypeStruct((heads // 2, 2, 1, t_dim), F32)],
        compiler_params=_params(dimension_semantics=("parallel", "arbitrary")),
    )(prune, qkv, qkv, qkv, c_hb, c_rows)


def _flash_dq(qkv, o, do, c_hb, c_rows, lse_hb, prune, *, tb, name):
    t_dim = qkv.shape[0]
    d = qkv.shape[1] // 3
    heads = d // HEAD_DIM
    cb = d // LANES
    scale = HEAD_DIM ** -0.5
    nq = t_dim // tb

    def body(prune_ref, q_ref, k_ref, v_ref, o_ref, do_ref, cc_ref, cr_ref, lse_ref, dq_ref, dl_ref, rs_ref):
        i = pl.program_id(1)
        h0 = 2 * pl.program_id(0)
        q = q_ref[...] * jnp.asarray(scale, BF16)
        do_blk = do_ref[...]
        prod = do_blk.astype(F32) * o_ref[...].astype(F32)
        masks = _head_masks(tb)
        row = lax.broadcasted_iota(jnp.int32, (tb, tb), 0)
        col = lax.broadcasted_iota(jnp.int32, (tb, tb), 1)
        qs = [jnp.where(masks[e], q, jnp.zeros_like(q)) for e in range(2)]
        dos = [jnp.where(masks[e], do_blk, jnp.zeros_like(do_blk)) for e in range(2)]
        deltas = [jnp.sum(jnp.where(masks[e], prod, 0.0), axis=1, keepdims=True) for e in range(2)]
        ccols = [cc_ref[0, e][:, 0:1] for e in range(2)]
        lses = [lse_ref[0, e][:, 0:1] for e in range(2)]

        def step(j, carry, diagonal):
            off = pl.multiple_of(j * tb, tb)
            kj = k_ref[pl.ds(off, tb), :]
            vj = v_ref[pl.ds(off, tb), :]
            out = []
            for e in range(2):
                acc, rsum = carry[e]
                crow = cr_ref[0, e, :, pl.ds(off, tb)]
                s = lax.dot_general(qs[e], kj, (((1,), (1,)), ((), ())), preferred_element_type=F32)
                s = s + (ccols[e] - crow)
                if diagonal:
                    s = jnp.where(col <= row, s, NEG_INF)
                p = jnp.exp(s - lses[e])
                dp = lax.dot_general(dos[e], vj, (((1,), (1,)), ((), ())), preferred_element_type=F32)
                ds = p * (dp - deltas[e])
                out.append((acc + jnp.dot(ds.astype(BF16), kj, preferred_element_type=F32),
                            rsum + jnp.sum(ds, axis=1, keepdims=True)))
            return tuple(out)

        init = (jnp.zeros((tb, LANES), F32), jnp.zeros((tb, 1), F32))
        kept = jnp.maximum(_kept_before(prune_ref, h0, i, nq), _kept_before(prune_ref, h0 + 1, i, nq))
        carry = lax.fori_loop(i - kept, i, functools.partial(step, diagonal=False), (init, init))
        carry = step(i, carry, True)
        for e in range(2):
            dl_ref[0, e] = _as_row(deltas[e], tb)
            rs_ref[0, e] = _as_row(carry[e][1], tb)
        dq_ref[...] = (jnp.where(masks[0], carry[0][0], carry[1][0]) * scale).astype(dq_ref.dtype)

    blk = pl.BlockSpec((tb, LANES), lambda h, i: (i, h))
    hb_spec = pl.BlockSpec((1, 2, tb, LANES), lambda h, i: (h, 0, i, 0))
    row_spec = pl.BlockSpec((1, 2, 1, t_dim), lambda h, i: (h, 0, 0, 0))
    row_blk = pl.BlockSpec((1, 2, 1, tb), lambda h, i: (h, 0, 0, i))
    row_shape = jax.ShapeDtypeStruct((heads // 2, 2, 1, t_dim), F32)
    return pl.pallas_call(
        body, name=name, grid=(heads // 2, nq),
        in_specs=[pl.BlockSpec(memory_space=pltpu.SMEM), blk,
                  pl.BlockSpec((t_dim, LANES), lambda h, i: (0, cb + h)),
                  pl.BlockSpec((t_dim, LANES), lambda h, i: (0, 2 * cb + h)),
                  blk, blk, hb_spec, row_spec, hb_spec],
        out_specs=[blk, row_blk, row_blk],
        out_shape=[jax.ShapeDtypeStruct((t_dim, d), BF16), row_shape, row_shape],
        compiler_params=_params(dimension_semantics=("parallel", "arbitrary")),
    )(prune, qkv, qkv, qkv, o, do, c_hb, c_rows, lse_hb)


def _flash_dkv(qkv, do, c_hb, c_rows, lse_rows, delta_rows, prune, *, tb, name):
    t_dim = qkv.shape[0]
    d = qkv.shape[1] // 3
    heads = d // HEAD_DIM
    cb = d // LANES
    scale = HEAD_DIM ** -0.5
    nq = t_dim // tb

    def body(prune_ref, q_ref, k_ref, v_ref, do_ref, cc_ref, cr_ref, lr_ref, dr_ref, dk_ref, dv_ref, dsum_ref):
        j = pl.program_id(1)
        h0 = 2 * pl.program_id(0)
        k_blk = k_ref[...] * jnp.asarray(scale, BF16)
        v_blk = v_ref[...]
        masks = _head_masks(tb)
        row = lax.broadcasted_iota(jnp.int32, (tb, tb), 0)
        col = lax.broadcasted_iota(jnp.int32, (tb, tb), 1)
        ks = [jnp.where(masks[e], k_blk, jnp.zeros_like(k_blk)) for e in range(2)]
        vs = [jnp.where(masks[e], v_blk, jnp.zeros_like(v_blk)) for e in range(2)]
        ccols = [cc_ref[0, e][:, 0:1] for e in range(2)]

        def step(i, carry, diagonal):
            off = pl.multiple_of(i * tb, tb)
            qi = q_ref[pl.ds(off, tb), :]
            doi = do_ref[pl.ds(off, tb), :]
            out = []
            for e in range(2):
                dk, dv, dsum = carry[e]
                crow = cr_ref[0, e, :, pl.ds(off, tb)]
                lse = lr_ref[0, e, :, pl.ds(off, tb)]
                delta = dr_ref[0, e, :, pl.ds(off, tb)]
                st = lax.dot_general(ks[e], qi, (((1,), (1,)), ((), ())), preferred_element_type=F32)
                st = st + (crow - ccols[e])
                if diagonal:
                    st = jnp.where(col >= row, st, NEG_INF)
                pt = jnp.exp(st - lse)
                dpt = lax.dot_general(vs[e], doi, (((1,), (1,)), ((), ())), preferred_element_type=F32)
                dst = pt * (dpt - delta)
                out.append((dk + jnp.dot(dst.astype(BF16), qi, preferred_element_type=F32),
                            dv + jnp.dot(pt.astype(BF16), doi, preferred_element_type=F32),
                            dsum + jnp.sum(dst, axis=1, keepdims=True)))
            return tuple(out)

        zero = jnp.zeros((tb, LANES), F32)
        init = (zero, zero, jnp.zeros((tb, 1), F32))
        carry = step(j, (init, init), True)
        kept = jnp.maximum(_kept_after(prune_ref, h0, j, nq), _kept_after(prune_ref, h0 + 1, j, nq))
        carry = lax.fori_loop(j + 1, j + 1 + kept, functools.partial(step, diagonal=False), carry)
        for e in range(2):
            dsum_ref[0, e] = _as_row(carry[e][2], tb)
        dk_ref[...] = (jnp.where(masks[0], carry[0][0], carry[1][0]) * scale).astype(dk_ref.dtype)
        dv_ref[...] = jnp.where(masks[0], carry[0][1], carry[1][1]).astype(dv_ref.dtype)

    blk = pl.BlockSpec((tb, LANES), lambda h, j: (j, h))
    hb_spec = pl.BlockSpec((1, 2, tb, LANES), lambda h, j: (h, 0, j, 0))
    row_spec = pl.BlockSpec((1, 2, 1, t_dim), lambda h, j: (h, 0, 0, 0))
    row_blk = pl.BlockSpec((1, 2, 1, tb), lambda h, j: (h, 0, 0, j))
    return pl.pallas_call(
        body, name=name, grid=(heads // 2, nq),
        in_specs=[pl.BlockSpec(memory_space=pltpu.SMEM),
                  pl.BlockSpec((t_dim, LANES), lambda h, j: (0, h)),
                  pl.BlockSpec((tb, LANES), lambda h, j: (j, cb + h)),
                  pl.BlockSpec((tb, LANES), lambda h, j: (j, 2 * cb + h)),
                  pl.BlockSpec((t_dim, LANES), lambda h, j: (0, h)),
                  hb_spec, row_spec, row_spec, row_spec],
        out_specs=[blk, blk, row_blk],
        out_shape=[jax.ShapeDtypeStruct((t_dim, d), BF16), jax.ShapeDtypeStruct((t_dim, d), BF16),
                   jax.ShapeDtypeStruct((heads // 2, 2, 1, t_dim), F32)],
        compiler_params=_params(dimension_semantics=("parallel", "arbitrary")),
    )(prune, qkv, qkv, qkv, do, c_hb, c_rows, lse_rows, delta_rows)


def _shift_down(z, prev, n, tt):
    out = pltpu.roll(z, n, axis=0)
    row = lax.broadcasted_iota(jnp.int32, z.shape, 0)
    for r in range(n):
        out = jnp.where(row == r, prev[8 - n + r:8 - n + r + 1, :], out)
    return out


def _shift_up(z, nxt, n, tt):
    out = pltpu.roll(z, tt - n, axis=0)
    row = lax.broadcasted_iota(jnp.int32, z.shape, 0)
    for r in range(n):
        out = jnp.where(row == tt - n + r, nxt[r:r + 1, :], out)
    return out


def _conv_fwd(proj, conv_w, *, name, tt=256):
    t_dim, d3 = proj.shape
    d = d3 // 3
    tt = min(tt, t_dim)

    def body(p_ref, prev_ref, w_ref, y_ref):
        i = pl.program_id(0)
        p = p_ref[...]
        pp = prev_ref[...]
        z = p[:, d:2 * d] * p[:, 2 * d:]
        zp = jnp.where(i > 0, pp[:, d:2 * d] * pp[:, 2 * d:], 0.0)
        w = w_ref[...]
        zc = w[2:3, :] * z + w[1:2, :] * _shift_down(z, zp, 1, tt) + w[0:1, :] * _shift_down(z, zp, 2, tt)
        y_ref[...] = (p[:, :d] * zc).astype(y_ref.dtype)

    return pl.pallas_call(
        body, name=name, grid=(t_dim // tt,),
        in_specs=[pl.BlockSpec((tt, d3), lambda i: (i, 0)),
                  pl.BlockSpec((8, d3), lambda i: (jnp.maximum(i * (tt // 8) - 1, 0), 0)),
                  pl.BlockSpec(conv_w.shape, lambda i: (0, 0))],
        out_specs=pl.BlockSpec((tt, d), lambda i: (i, 0)),
        out_shape=jax.ShapeDtypeStruct((t_dim, d), BF16),
        compiler_params=_params(dimension_semantics=("arbitrary",)),
    )(proj, proj, conv_w)


def _conv_bwd(proj, dy, conv_w, *, name, tt=256):
    t_dim, d3 = proj.shape
    d = d3 // 3
    tt = min(tt, t_dim)
    n = t_dim // tt

    def body(p_ref, prev_ref, next_ref, dy_ref, dyn_ref, w_ref, dp_ref, dw_ref):
        i = pl.program_id(0)
        p = p_ref[...]
        pp = prev_ref[...]
        pn = next_ref[...]
        bg, cg, u = p[:, :d], p[:, d:2 * d], p[:, 2 * d:]
        z = cg * u
        zp = jnp.where(i > 0, pp[:, d:2 * d] * pp[:, 2 * d:], 0.0)
        w = w_ref[...]
        z1 = _shift_down(z, zp, 1, tt)
        z2 = _shift_down(z, zp, 2, tt)
        zc = w[2:3, :] * z + w[1:2, :] * z1 + w[0:1, :] * z2
        dy_blk = dy_ref[...]
        dzc = dy_blk * bg
        dzn = jnp.where(i < n - 1, dyn_ref[...] * pn[:, :d], 0.0)
        dz = w[2:3, :] * dzc + w[1:2, :] * _shift_up(dzc, dzn, 1, tt) + w[0:1, :] * _shift_up(dzc, dzn, 2, tt)
        dp_ref[:, :d] = (dy_blk * zc).astype(dp_ref.dtype)
        dp_ref[:, d:2 * d] = (dz * u).astype(dp_ref.dtype)
        dp_ref[:, 2 * d:] = (dz * cg).astype(dp_ref.dtype)
        part = jnp.concatenate([jnp.sum(dzc * z2, axis=0, keepdims=True),
                                jnp.sum(dzc * z1, axis=0, keepdims=True),
                                jnp.sum(dzc * z, axis=0, keepdims=True),
                                jnp.zeros((5, d), F32)], axis=0)

        @pl.when(i == 0)
        def _():
            dw_ref[...] = part

        @pl.when(i > 0)
        def _():
            dw_ref[...] += part

    last8 = t_dim // 8 - 1
    return pl.pallas_call(
        body, name=name, grid=(n,),
        in_specs=[pl.BlockSpec((tt, d3), lambda i: (i, 0)),
                  pl.BlockSpec((8, d3), lambda i: (jnp.maximum(i * (tt // 8) - 1, 0), 0)),
                  pl.BlockSpec((8, d3), lambda i: (jnp.minimum((i + 1) * (tt // 8), last8), 0)),
                  pl.BlockSpec((tt, d), lambda i: (i, 0)),
                  pl.BlockSpec((8, d), lambda i: (jnp.minimum((i + 1) * (tt // 8), last8), 0)),
                  pl.BlockSpec(conv_w.shape, lambda i: (0, 0))],
        out_specs=[pl.BlockSpec((tt, d3), lambda i: (i, 0)), pl.BlockSpec((8, d), lambda i: (0, 0))],
        out_shape=[jax.ShapeDtypeStruct((t_dim, d3), BF16), jax.ShapeDtypeStruct((8, d), F32)],
        compiler_params=_params(dimension_semantics=("arbitrary",)),
    )(proj, proj, proj, dy, dy, conv_w)


def _window(ref, axis, n, idx):
    if axis is None:
        return ref
    sel = [slice(None)] * len(ref.shape)
    sel[axis] = pl.ds(pl.multiple_of(idx * n, n), n)
    return ref.at[tuple(sel)]


def _exchange(items, *, gather, name):
    n_w = len(items)
    widths = []
    out_shape = []
    for arr, axis in items:
        shp = list(arr.shape)
        if gather:
            widths.append(shp[axis])
            shp[axis] *= N_DEV
        else:
            if axis is not None:
                shp[axis] //= N_DEV
                widths.append(shp[axis])
            else:
                widths.append(None)
            shp = [N_DEV] + shp
        out_shape.append(jax.ShapeDtypeStruct(tuple(shp), arr.dtype))

    def body(*refs):
        src, dst = refs[:n_w], refs[n_w:2 * n_w]
        send_sems, recv_sems, local_sems = refs[2 * n_w:]
        x, y, c = lax.axis_index("x"), lax.axis_index("y"), lax.axis_index("c")
        me = 4 * x + 2 * y + c

        def ends(w, to):
            axis = items[w][1]
            if gather:
                return src[w], _window(dst[w], axis, widths[w], me)
            return _window(src[w], axis, widths[w], to), dst[w].at[me]

        copies = []
        for w in range(n_w):
            cp = pltpu.make_async_copy(*ends(w, me), local_sems.at[w])
            cp.start()
            copies.append(cp)
        if gather:
            chips = [(1 - x, y), (x, 1 - y), (1 - x, 1 - y)]

            def block_copy(w, n, origin, to, from_shard):
                place = _window(dst[w], items[w][1], widths[w], 4 * origin[0] + 2 * origin[1] + origin[2])
                return pltpu.make_async_remote_copy(
                    src_ref=src[w] if from_shard else place, dst_ref=place,
                    send_sem=send_sems.at[w, n], recv_sem=recv_sems.at[w, n],
                    device_id=to, device_id_type=pl.DeviceIdType.MESH)

            sent = []
            for w in range(n_w):
                sent.append(block_copy(w, 0, (x, y, c), (x, y, 1 - c), True))
                sent += [block_copy(w, 1 + n, (x, y, c), (*chip, c), True) for n, chip in enumerate(chips)]
            for cp in sent:
                cp.start()
            for n, chip in enumerate(chips):
                for w in range(n_w):
                    block_copy(w, 1 + n, (*chip, c), (x, y, c), True).wait_recv()
                    passed = block_copy(w, 4 + n, (*chip, c), (x, y, 1 - c), False)
                    passed.start()
                    sent.append(passed)
            for w in range(n_w):
                block_copy(w, 0, (x, y, 1 - c), (x, y, c), True).wait_recv()
                for n, chip in enumerate(chips):
                    block_copy(w, 4 + n, (*chip, 1 - c), (x, y, c), False).wait_recv()
            for cp in sent:
                cp.wait_send()
            for cp in copies:
                cp.wait()
            return
        for k in range(1, N_DEV):
            px = 1 - x if k & 4 else x
            py = 1 - y if k & 2 else y
            pc = 1 - c if k & 1 else c
            for w in range(n_w):
                s_ref, d_ref = ends(w, 4 * px + 2 * py + pc)
                cp = pltpu.make_async_remote_copy(
                    src_ref=s_ref, dst_ref=d_ref, send_sem=send_sems.at[w, k - 1], recv_sem=recv_sems.at[w, k - 1],
                    device_id=(px, py, pc), device_id_type=pl.DeviceIdType.MESH)
                cp.start()
                copies.append(cp)
        for cp in copies:
            cp.wait()

    return pl.pallas_call(
        body, name=name,
        in_specs=[pl.BlockSpec(memory_space=pl.ANY)] * n_w,
        out_specs=[pl.BlockSpec(memory_space=pl.ANY)] * n_w,
        out_shape=out_shape,
        scratch_shapes=[pltpu.SemaphoreType.DMA((n_w, N_DEV - 1)), pltpu.SemaphoreType.DMA((n_w, N_DEV - 1)),
                        pltpu.SemaphoreType.DMA((n_w,))],
    )(*[arr for arr, _ in items])


def _row_tile(rows, cols):
    tr = rows
    while tr % 16 == 0 and tr * cols > 256 * 1024:
        tr //= 2
    return tr


def _sum_parts(parts, *, name):
    n_parts, rows, cols = parts.shape
    tr = _row_tile(rows, cols)

    def body(p_ref, o_ref):
        g = p_ref[0].astype(F32)
        for s in range(1, n_parts):
            g = g + p_ref[s].astype(F32)
        o_ref[...] = g

    return pl.pallas_call(
        body, name=name, grid=(rows // tr,),
        in_specs=[pl.BlockSpec((n_parts, tr, cols), lambda i: (0, i, 0))],
        out_specs=pl.BlockSpec((tr, cols), lambda i: (i, 0)),
        out_shape=jax.ShapeDtypeStruct((rows, cols), F32),
        compiler_params=_params(dimension_semantics=("parallel",)),
    )(parts)


def _adamw(parts, w, m, v, *, name):
    n_parts, rows, cols = parts.shape
    tr = _row_tile(rows, cols)

    def body(p_ref, w_ref, m_ref, v_ref, g_ref, d_ref, nm_ref, nv_ref):
        g = p_ref[0].astype(F32)
        for s in range(1, n_parts):
            g = g + p_ref[s].astype(F32)
        m_new = ADAM_B1 * m_ref[...] + (1.0 - ADAM_B1) * g
        v_new = ADAM_B2 * v_ref[...] + (1.0 - ADAM_B2) * (g * g)
        m_hat = m_new / (1.0 - ADAM_B1 ** ADAM_STEP)
        v_hat = v_new / (1.0 - ADAM_B2 ** ADAM_STEP)
        g_ref[...] = g
        d_ref[...] = -ADAM_LR * (m_hat / (jnp.sqrt(v_hat) + ADAM_EPS) + ADAM_WD * w_ref[...])
        nm_ref[...] = m_new
        nv_ref[...] = v_new

    spec = pl.BlockSpec((tr, cols), lambda i: (i, 0))
    return pl.pallas_call(
        body, name=name, grid=(rows // tr,),
        in_specs=[pl.BlockSpec((n_parts, tr, cols), lambda i: (0, i, 0)), spec, spec, spec],
        out_specs=[spec] * 4,
        out_shape=[jax.ShapeDtypeStruct((rows, cols), F32)] * 4,
        compiler_params=_params(dimension_semantics=("parallel",)),
    )(parts, w, m, v)


def _pad_rows(a, axis, to):
    pad = [(0, 0)] * a.ndim
    pad[axis] = (0, to - a.shape[axis])
    return jnp.pad(a, pad)


def kernel(x, p, norm_g, w_attn_in, b_forget, w_attn_out, w_conv_in, conv_w, w_conv_out, w_mlp_up, w_mlp_down, w_ple_proj, w_ple_gate, loss_target, m_norm_g, m_w_attn_in, m_b_forget, m_w_attn_out, m_w_conv_in, m_conv_w, m_w_conv_out, m_w_mlp_up, m_w_mlp_down, m_w_ple_proj, m_w_ple_gate, v_norm_g, v_w_attn_in, v_b_forget, v_w_attn_out, v_w_conv_in, v_conv_w, v_w_conv_out, v_w_mlp_up, v_w_mlp_down, v_w_ple_proj, v_w_ple_gate):
    shards = dict(norm_g=norm_g, w_attn_in=w_attn_in, b_forget=b_forget, w_attn_out=w_attn_out,
                  w_conv_in=w_conv_in, conv_w=conv_w, w_conv_out=w_conv_out, w_mlp_up=w_mlp_up,
                  w_mlp_down=w_mlp_down, w_ple_proj=w_ple_proj, w_ple_gate=w_ple_gate)
    m_shards = dict(norm_g=m_norm_g, w_attn_in=m_w_attn_in, b_forget=m_b_forget, w_attn_out=m_w_attn_out,
                    w_conv_in=m_w_conv_in, conv_w=m_conv_w, w_conv_out=m_w_conv_out, w_mlp_up=m_w_mlp_up,
                    w_mlp_down=m_w_mlp_down, w_ple_proj=m_w_ple_proj, w_ple_gate=m_w_ple_gate)
    v_shards = dict(norm_g=v_norm_g, w_attn_in=v_w_attn_in, b_forget=v_b_forget, w_attn_out=v_w_attn_out,
                    w_conv_in=v_w_conv_in, conv_w=v_conv_w, w_conv_out=v_w_conv_out, w_mlp_up=v_w_mlp_up,
                    w_mlp_down=v_w_mlp_down, w_ple_proj=v_w_ple_proj, w_ple_gate=v_w_ple_gate)
    t_dim, d = x.shape[-2:]
    depth = p.shape[0]
    n_attn, heads = b_forget.shape
    assert d == heads * HEAD_DIM and x.shape[0] == 1
    tb = min(512, t_dim // 2)
    x0 = x.reshape(t_dim, d)
    target = loss_target.reshape(t_dim, d)

    in_cols = w_attn_in.shape[2]
    in_cols_pad = -(-in_cols // 16) * 16
    gather_names = [n for n in WEIGHT_NAMES if n != 'b_forget']
    gather_items = []
    for n in gather_names:
        if n == 'w_attn_in':
            gather_items.append((_pad_rows(jnp.swapaxes(w_attn_in, 1, 2), 1, in_cols_pad).astype(BF16), 1))
        elif n in ('norm_g', 'conv_w'):
            gather_items.append((shards[n], SHARD_AXIS[n]))
        else:
            gather_items.append((shards[n].astype(BF16), SHARD_AXIS[n]))
    full = dict(zip(gather_names, _exchange(gather_items, gather=True, name="gather_weights")))
    gains = full['norm_g']
    taps = full['conv_w']
    w_in_t = full['w_attn_in'].reshape(n_attn, N_DEV, in_cols_pad, d)[:, :, :in_cols]
    w_in_t = _pad_rows(w_in_t.reshape(n_attn, N_DEV * in_cols, d), 1, 3 * d + LANES)
    bias_pad = jnp.pad(b_forget, ((0, 0), (0, LANES - heads)))

    def gain(i, k):
        return gains[i, k].reshape(1, d)

    def add_norm(x_prev, branch, g_branch, g_next, name):
        def fn(rows, vecs):
            x_new = rows[0] + _norm(rows[1], vecs[0])
            return [x_new, _norm(x_new, vecs[1])], []
        return _rows_call(fn, [x_prev, branch], [g_branch, g_next], [(d, F32), (d, BF16)], [], name=name)

    saved = []
    x_cur = x0
    hn = _rows_call(lambda rows, vecs: ([_norm(rows[0], vecs[0])], []), [x0], [gain(0, 0)], [(d, BF16)], [],
                    name="norm_in")[0]
    loss_rows = dy = None
    for i in range(depth):
        j = i // 2
        s = dict(x0=x_cur, hn=hn)
        if i % 2 == 0:
            s['qkv'] = _mm(hn, w_in_t[j, :3 * d], tb=True, out_dtypes=(BF16,), name=f"attn_in_{i}")
            s['fl'] = _mm(hn, w_in_t[j, 3 * d:], tb=True, name=f"attn_gate_{i}")
            c = _cumsum_fwd(s['fl'], bias_pad[j:j + 1], name=f"gate_cumsum_{i}")
            c_t = c[:, :heads].T
            s['prune'] = _prune_table(c_t, _head_norms(s['qkv'], name=f"head_norms_{i}"), tb)
            c_t = c_t.reshape(heads // 2, 2, t_dim)
            s['c_hb'] = jnp.broadcast_to(c_t[:, :, :, None], (heads // 2, 2, t_dim, LANES))
            s['c_rows'] = c_t.reshape(heads // 2, 2, 1, t_dim)
            s['o'], s['lse_hb'], s['lse_rows'] = _flash_fwd(s['qkv'], s['c_hb'], s['c_rows'], s['prune'], tb=tb,
                                                           name=f"attn_fwd_{i}")
            s['m'] = _mm(s['o'], full['w_attn_out'][j], name=f"attn_out_{i}")
        else:
            s['proj'] = _mm(hn, full['w_conv_in'][j], name=f"conv_in_{i}")
            s['y'] = _conv_fwd(s['proj'], taps[j], name=f"conv_fwd_{i}")
            s['m'] = _mm(s['y'], full['w_conv_out'][j], name=f"conv_out_{i}")
        s['x1'], s['h2'] = add_norm(x_cur, s['m'], gain(i, 1), gain(i, 2), f"mix_norm_{i}")
        s['u'], s['a'] = _mm(s['h2'], full['w_mlp_up'][i], out_dtypes=(BF16, BF16), name=f"mlp_up_{i}",
                             epi=lambda acc: (acc, jnp.square(jnp.maximum(acc, 0.0))))
        s['f'] = _mm(s['a'], full['w_mlp_down'][i], name=f"mlp_down_{i}")
        s['x2'], s['h4'] = add_norm(s['x1'], s['f'], gain(i, 3), gain(i, 4), f"mlp_norm_{i}")
        s['pi'] = p[i].reshape(t_dim, -1)
        s['pp'] = _mm(s['pi'], full['w_ple_proj'][i], name=f"ple_proj_{i}")
        s['gl'], s['e'] = _mm(s['h4'], full['w_ple_gate'][i], extras=(s['pp'],), out_dtypes=(F32, F32),
                              name=f"ple_gate_{i}", epi=lambda acc, pp: (acc, pp * _sigmoid(acc)))
        if i + 1 < depth:
            x_cur, hn = add_norm(s['x2'], s['e'], gain(i, 5), gain(i + 1, 0), f"ple_norm_{i}")
        else:
            def loss_fn(rows, vecs):
                err = rows[0] + _norm(rows[1], vecs[0]) - rows[2]
                part = 0.5 * jnp.sum(jnp.sum(err * err, axis=1, keepdims=True), axis=0, keepdims=True) / d
                return [err / d], [jnp.broadcast_to(part, (1, LANES))]
            dy, loss_rows = _rows_call(loss_fn, [s['x2'], s['e'], target], [gain(i, 5)], [(d, F32)],
                                       [(1, LANES)], name="loss")
        saved.append(s)
    loss = lax.psum(loss_rows[0, 0], ("x", "y", "c"))

    grads = {n: [None] * shards[n].shape[0] for n in WEIGHT_NAMES}
    d_gains = [[None] * 6 for _ in range(depth)]
    wgrad = functools.partial(_mm, ta=True, out_dtypes=(BF16,))
    dx = dy
    for i in reversed(range(depth)):
        j = i // 2
        s = saved[i]

        def ple_fn(rows, vecs):
            de, dg = _norm_bwd(rows[0], vecs[0], rows[1])
            sg = _sigmoid(rows[2])
            return [de * sg, de * rows[3] * sg * (1.0 - sg)], [dg]
        dpp, dgl, d_gains[i][5] = _rows_call(ple_fn, [s['e'], dx, s['gl'], s['pp']], [gain(i, 5)],
                                             [(d, BF16), (d, BF16)], [(1, d)], name=f"ple_bwd_{i}")
        grads['w_ple_proj'][i] = wgrad(s['pi'], dpp, name=f"ple_proj_dw_{i}")
        grads['w_ple_gate'][i] = wgrad(s['h4'], dgl, name=f"ple_gate_dw_{i}")
        dh4 = _mm(dgl, full['w_ple_gate'][i], tb=True, name=f"ple_gate_dx_{i}")

        def two_norm_bwd(x_res, dh, dx_in, branch, g_res, g_branch, name):
            def fn(rows, vecs):
                d_res, dg_res = _norm_bwd(rows[0], vecs[0], rows[1])
                dx_out = rows[2] + d_res
                d_branch, dg_branch = _norm_bwd(rows[3], vecs[1], dx_out)
                return [dx_out, d_branch], [dg_res, dg_branch]
            return _rows_call(fn, [x_res, dh, dx_in, branch], [g_res, g_branch], [(d, F32), (d, BF16)],
                              [(1, d), (1, d)], name=name)

        dx2, df, d_gains[i][4], d_gains[i][3] = two_norm_bwd(s['x2'], dh4, dx, s['f'], gain(i, 4), gain(i, 3),
                                                            f"mlp_norm_bwd_{i}")
        grads['w_mlp_down'][i] = wgrad(s['a'], df, name=f"mlp_down_dw_{i}")
        du = _mm(df, full['w_mlp_down'][i], tb=True, extras=(s['u'],), out_dtypes=(BF16,), name=f"mlp_down_dx_{i}",
                 epi=lambda acc, u: (acc * (2.0 * jnp.maximum(u.astype(F32), 0.0)),))
        grads['w_mlp_up'][i] = wgrad(s['h2'], du, name=f"mlp_up_dw_{i}")
        dh2 = _mm(du, full['w_mlp_up'][i], tb=True, name=f"mlp_up_dx_{i}")
        dx1, dm, d_gains[i][2], d_gains[i][1] = two_norm_bwd(s['x1'], dh2, dx2, s['m'], gain(i, 2), gain(i, 1),
                                                            f"mix_norm_bwd_{i}")
        if i % 2 == 0:
            grads['w_attn_out'][j] = wgrad(s['o'], dm, name=f"attn_out_dw_{i}")
            do = _mm(dm, full['w_attn_out'][j], tb=True, out_dtypes=(BF16,), name=f"attn_out_dx_{i}")
            dq, delta_rows, rsum_rows = _flash_dq(s['qkv'], s['o'], do, s['c_hb'], s['c_rows'], s['lse_hb'],
                                                  s['prune'], tb=tb, name=f"attn_dq_{i}")
            dk, dv, csum_rows = _flash_dkv(s['qkv'], do, s['c_hb'], s['c_rows'], s['lse_rows'], delta_rows,
                                           s['prune'], tb=tb, name=f"attn_dkv_{i}")
            dc = (rsum_rows - csum_rows).reshape(heads, t_dim).T
            dfl, db = _cumsum_bwd(jnp.pad(dc, ((0, 0), (0, LANES - heads))), s['fl'], bias_pad[j:j + 1],
                                  name=f"gate_cumsum_bwd_{i}")
            grads['b_forget'][j] = db[0, :heads]
            dproj = jnp.concatenate([dq, dk, dv, dfl], axis=1)
            grads['w_attn_in'][j] = wgrad(dproj, s['hn'], name=f"attn_in_dw_{i}")
            dhn = _mm(dproj, w_in_t[j], name=f"attn_in_dx_{i}")
        else:
            grads['w_conv_out'][j] = wgrad(s['y'], dm, name=f"conv_out_dw_{i}")
            dyc = _mm(dm, full['w_conv_out'][j], tb=True, name=f"conv_out_dx_{i}")
            dproj, dtaps = _conv_bwd(s['proj'], dyc, taps[j], name=f"conv_bwd_{i}")
            grads['conv_w'][j] = dtaps[:3].astype(BF16)
            grads['w_conv_in'][j] = wgrad(s['hn'], dproj, name=f"conv_in_dw_{i}")
            dhn = _mm(dproj, full['w_conv_in'][j], tb=True, name=f"conv_in_dx_{i}")

        def in_fn(rows, vecs):
            d_res, dg = _norm_bwd(rows[0], vecs[0], rows[1])
            return [rows[2] + d_res], [dg]
        dx, d_gains[i][0] = _rows_call(in_fn, [s['x0'], dhn, dx1], [gain(i, 0)], [(d, F32)], [(1, d)],
                                       name=f"in_norm_bwd_{i}")
    grad_x = dx.reshape(x.shape)

    local = {n: jnp.stack(grads[n]) for n in WEIGHT_NAMES if n not in ('norm_g', 'b_forget')}
    local['norm_g'] = jnp.stack([jnp.concatenate(row, axis=0) for row in d_gains]).astype(BF16)
    g_in_t = local['w_attn_in'][:, :N_DEV * in_cols].reshape(n_attn, N_DEV, in_cols, d)
    local['w_attn_in'] = _pad_rows(g_in_t, 2, in_cols_pad).reshape(n_attn, N_DEV * in_cols_pad, d)
    local['b_forget'] = jnp.zeros((8, LANES), F32).at[:n_attn, :heads].set(jnp.stack(grads['b_forget']))
    axis_of = dict(SHARD_AXIS, w_attn_in=1)
    recv = dict(zip(WEIGHT_NAMES, _exchange([(local[n], axis_of[n]) for n in WEIGHT_NAMES], gather=False,
                                            name="exchange_grads")))
    recv['b_forget'] = recv['b_forget'][:, :n_attn, :heads]
    g_in_t = _sum_parts(recv['w_attn_in'].reshape(N_DEV, n_attn * in_cols_pad, d), name="sum_attn_in")
    recv['w_attn_in'] = jnp.swapaxes(g_in_t.reshape(n_attn, in_cols_pad, d)[:, :in_cols], 1, 2)[None]
    results = {}
    for n in WEIGHT_NAMES:
        shp = shards[n].shape
        flat = lambda a: a.reshape(a.shape[:a.ndim - len(shp)] + (-1, shp[-1]))
        outs = _adamw(flat(recv[n]), flat(shards[n]), flat(m_shards[n]), flat(v_shards[n]), name=f"adamw_{n}")
        results[n] = [o.reshape(shp) for o in outs]
    return (loss, grad_x, *[results[n][k] for k in range(4) for n in WEIGHT_NAMES])
```

```python
import functools

import jax
import jax.numpy as jnp
from jax import lax
from jax.experimental import pallas as pl
from jax.experimental.pallas import tpu as pltpu

F32 = jnp.float32
BF16 = jnp.bfloat16

N_DEV = 8
LANES = 128
HEAD_DIM = 64
VMEM_LIMIT_BYTES = 56 * 1024 * 1024
RMS_EPS = 1e-6
NEG_INF = -1e30
ADAM_LR = 0.001
ADAM_B1 = 0.9
ADAM_B2 = 0.999
ADAM_EPS = 1e-08
ADAM_WD = 0.01
ADAM_STEP = 10
WEIGHT_NAMES = ('norm_g', 'w_attn_in', 'b_forget', 'w_attn_out', 'w_conv_in', 'conv_w', 'w_conv_out',
                'w_mlp_up', 'w_mlp_down', 'w_ple_proj', 'w_ple_gate')
SHARD_AXIS = {'norm_g': 2, 'w_attn_in': 2, 'b_forget': None, 'w_attn_out': 1, 'w_conv_in': 2, 'conv_w': 2,
              'w_conv_out': 1, 'w_mlp_up': 2, 'w_mlp_down': 1, 'w_ple_proj': 2, 'w_ple_gate': 1}


def _params(**kw):
    return pltpu.CompilerParams(vmem_limit_bytes=VMEM_LIMIT_BYTES, **kw)


def _tile(n, cap):
    if n <= cap:
        return n
    t = (cap // LANES) * LANES
    while n % t:
        t -= LANES
    return t


def _mm(a, b, *, ta=False, tb=False, extras=(), epi=None, out_dtypes=(F32,), name):
    m_dim, k_dim = (a.shape[1], a.shape[0]) if ta else a.shape
    n_dim = b.shape[0] if tb else b.shape[1]
    assert k_dim == (b.shape[1] if tb else b.shape[0])
    tk = _tile(k_dim, 1024 if k_dim <= 1024 else 2048)
    nk = k_dim // tk
    simple = not extras and len(out_dtypes) == 1
    tm = _tile(m_dim, 1024 if (nk > 1 and simple) else 512)
    tn = _tile(n_dim, 1024)
    grid = (n_dim // tn, m_dim // tm, nk)
    a_spec = (pl.BlockSpec((tk, tm), lambda j, i, k: (k, i)) if ta
              else pl.BlockSpec((tm, tk), lambda j, i, k: (i, k)))
    b_spec = (pl.BlockSpec((tn, tk), lambda j, i, k: (j, k)) if tb
              else pl.BlockSpec((tk, tn), lambda j, i, k: (k, j)))
    mn_spec = pl.BlockSpec((tm, tn), lambda j, i, k: (i, j))
    dims = (((0 if ta else 1,), (1 if tb else 0,)), ((), ()))
    n_extra, n_out = len(extras), len(out_dtypes)
    if epi is None:
        epi = lambda acc: (acc,)

    def body(a_ref, b_ref, *rest):
        e_refs, o_refs = rest[:n_extra], rest[n_extra:n_extra + n_out]
        part = lax.dot_general(a_ref[...].astype(BF16), b_ref[...].astype(BF16), dims,
                               preferred_element_type=F32)

        def finish(acc):
            for o_ref, val in zip(o_refs, epi(acc, *[e[...] for e in e_refs])):
                o_ref[...] = val.astype(o_ref.dtype)

        if nk == 1:
            finish(part)
        else:
            acc_ref = rest[-1]
            k = pl.program_id(2)

            @pl.when(k == 0)
            def _():
                acc_ref[...] = part

            @pl.when(k > 0)
            def _():
                acc_ref[...] += part

            @pl.when(k == nk - 1)
            def _():
                finish(acc_ref[...])

    outs = pl.pallas_call(
        body, name=name, grid=grid,
        in_specs=[a_spec, b_spec] + [mn_spec] * n_extra,
        out_specs=[mn_spec] * n_out,
        out_shape=[jax.ShapeDtypeStruct((m_dim, n_dim), dt) for dt in out_dtypes],
        scratch_shapes=[pltpu.VMEM((tm, tn), F32)] if nk > 1 else [],
        compiler_params=_params(dimension_semantics=("parallel", "parallel", "arbitrary")),
    )(a, b, *extras)
    return outs[0] if n_out == 1 else outs


def _rows(fn, row_ins, vec_ins, row_outs, vec_outs, *, name, tt=512, reverse=False):
    t_dim = row_ins[0].shape[0]
    tt = min(tt, t_dim)
    n = t_dim // tt
    n_ri, n_vi, n_ro, n_vo = len(row_ins), len(vec_ins), len(row_outs), len(vec_outs)
    pos = (lambda i: (n - 1 - i, 0)) if reverse else (lambda i: (i, 0))
    fixed = lambda i: (0, 0)

    def body(*refs):
        ri = refs[:n_ri]
        vi = refs[n_ri:n_ri + n_vi]
        ro = refs[n_ri + n_vi:n_ri + n_vi + n_ro]
        vo = refs[n_ri + n_vi + n_ro:n_ri + n_vi + n_ro + n_vo]
        scratch = refs[n_ri + n_vi + n_ro + n_vo:]
        r_out, v_out = fn([r[...] for r in ri], [v[...] for v in vi], *scratch)
        for o_ref, val in zip(ro, r_out):
            o_ref[...] = val.astype(o_ref.dtype)
        i = pl.program_id(0)
        for o_ref, val in zip(vo, v_out):
            @pl.when(i == 0)
            def _(o_ref=o_ref, val=val):
                o_ref[...] = val

            @pl.when(i > 0)
            def _(o_ref=o_ref, val=val):
                o_ref[...] += val

    return body, dict(
        grid=(n,),
        in_specs=[pl.BlockSpec((tt, r.shape[1]), pos) for r in row_ins]
        + [pl.BlockSpec(v.shape, fixed) for v in vec_ins],
        out_specs=[pl.BlockSpec((tt, w), pos) for w, _ in row_outs]
        + [pl.BlockSpec(s, fixed) for s in vec_outs],
        out_shape=[jax.ShapeDtypeStruct((t_dim, w), dt) for w, dt in row_outs]
        + [jax.ShapeDtypeStruct(s, F32) for s in vec_outs],
        name=name,
        compiler_params=_params(dimension_semantics=("arbitrary",)),
    )


def _rows_call(fn, row_ins, vec_ins, row_outs, vec_outs, *, name, tt=512, reverse=False, scratch=()):
    body, kw = _rows(fn, row_ins, vec_ins, row_outs, vec_outs, name=name, tt=tt, reverse=reverse)
    return pl.pallas_call(body, scratch_shapes=list(scratch), **kw)(*row_ins, *vec_ins)


def _rstd(x):
    return lax.rsqrt(jnp.mean(x * x, axis=-1, keepdims=True) + RMS_EPS)


def _norm(x, g):
    return x * _rstd(x) * g


def _norm_bwd(x, g, dy):
    xh = x * _rstd(x)
    gy = dy * g
    dx = _rstd(x) * (gy - xh * jnp.mean(gy * xh, axis=-1, keepdims=True))
    return dx, jnp.sum(dy * xh, axis=0, keepdims=True)


def _sigmoid(x):
    return 1.0 / (1.0 + jnp.exp(-x))


def _log_sigmoid(x):
    return jnp.minimum(x, 0.0) - jnp.log(1.0 + jnp.exp(-jnp.abs(x)))


def _split3(x):
    hi = x.astype(BF16)
    r1 = x - hi.astype(F32)
    mid = r1.astype(BF16)
    lo = (r1 - mid.astype(F32)).astype(BF16)
    return hi, mid, lo


def _cumsum_fwd(fl, bias, *, name):
    w = fl.shape[1]
    tt = min(512, fl.shape[0])

    def fn(rows, vecs, carry_ref):
        i = pl.program_id(0)

        @pl.when(i == 0)
        def _():
            carry_ref[...] = jnp.zeros_like(carry_ref)

        lf = _log_sigmoid(rows[0] + vecs[0])
        r = lax.broadcasted_iota(jnp.int32, (tt, tt), 0)
        c = lax.broadcasted_iota(jnp.int32, (tt, tt), 1)
        tri = (c <= r).astype(BF16)
        acc = carry_ref[0:1, :]
        for part in _split3(lf):
            acc = acc + jnp.dot(tri, part, preferred_element_type=F32)
        carry_ref[0:1, :] = acc[tt - 1:tt, :]
        return [acc], []

    return _rows_call(fn, [fl], [bias], [(w, F32)], [], name=name, tt=tt,
                      scratch=[pltpu.VMEM((8, w), F32)])[0]


def _cumsum_bwd(dc, fl, bias, *, name):
    w = fl.shape[1]
    tt = min(512, fl.shape[0])

    def fn(rows, vecs, carry_ref):
        i = pl.program_id(0)

        @pl.when(i == 0)
        def _():
            carry_ref[...] = jnp.zeros_like(carry_ref)

        r = lax.broadcasted_iota(jnp.int32, (tt, tt), 0)
        c = lax.broadcasted_iota(jnp.int32, (tt, tt), 1)
        tri = (c >= r).astype(BF16)
        acc = carry_ref[0:1, :]
        for part in _split3(rows[0]):
            acc = acc + jnp.dot(tri, part, preferred_element_type=F32)
        carry_ref[0:1, :] = acc[0:1, :]
        dfl = acc * _sigmoid(-(rows[1] + vecs[0]))
        return [dfl], [jnp.sum(dfl, axis=0, keepdims=True)]

    return _rows_call(fn, [dc, fl], [bias], [(w, BF16)], [(1, w)], name=name, tt=tt, reverse=True,
                      scratch=[pltpu.VMEM((8, w), F32)])


def _head_masks(tb):
    lane = lax.broadcasted_iota(jnp.int32, (tb, LANES), 1)
    return [lane < HEAD_DIM, lane >= HEAD_DIM]


PRUNE_MARGIN = 40.0


def _head_norms(qkv, *, name):
    t_dim = qkv.shape[0]
    d = qkv.shape[1] // 3
    heads = d // HEAD_DIM
    tt = min(512, t_dim)

    def body(q_ref, k_ref, o_ref):
        col = lax.broadcasted_iota(jnp.int32, (d, LANES), 0) // HEAD_DIM
        lane = lax.broadcasted_iota(jnp.int32, (d, LANES), 1)
        tile_max = None
        for ref, first in ((q_ref, 0), (k_ref, heads)):
            x = ref[...].astype(F32)
            sums = jnp.dot((x * x).astype(BF16), (col + first == lane).astype(BF16), preferred_element_type=F32)
            part = jnp.max(sums, axis=0, keepdims=True)
            tile_max = part if tile_max is None else jnp.maximum(tile_max, part)
        i = pl.program_id(0)

        @pl.when(i == 0)
        def _():
            o_ref[...] = tile_max

        @pl.when(i > 0)
        def _():
            o_ref[...] = jnp.maximum(o_ref[...], tile_max)

    return pl.pallas_call(
        body, name=name, grid=(t_dim // tt,),
        in_specs=[pl.BlockSpec((tt, d), lambda i: (i, 0)), pl.BlockSpec((tt, d), lambda i: (i, 1))],
        out_specs=pl.BlockSpec((1, LANES), lambda i: (0, 0)),
        out_shape=jax.ShapeDtypeStruct((1, LANES), F32),
        compiler_params=_params(dimension_semantics=("arbitrary",)),
    )(qkv, qkv)


def _prune_table(c_t, norms, tb):
    heads = c_t.shape[0]
    bound = 1.02 * HEAD_DIM ** -0.5 * jnp.sqrt(norms[0, :heads] * norms[0, heads:2 * heads])
    return jnp.concatenate([c_t[:, ::tb], c_t[:, tb - 1::tb], -(PRUNE_MARGIN + 2.0 * bound)[:, None]], axis=1)


def _kept_before(prune_ref, h, i, nq):
    first, thr = prune_ref[h, i], prune_ref[h, 2 * nq]
    return lax.fori_loop(0, i, lambda j, n: n + (first - prune_ref[h, nq + j] >= thr).astype(jnp.int32),
                         jnp.int32(0))


def _kept_after(prune_ref, h, j, nq):
    last, thr = prune_ref[h, nq + j], prune_ref[h, 2 * nq]
    return lax.fori_loop(j + 1, nq, lambda i, n: n + (prune_ref[h, i] - last >= thr).astype(jnp.int32),
                         jnp.int32(0))


def _as_row(col, tb):
    return jnp.transpose(jnp.broadcast_to(col, (tb, LANES)))[0:1, :]


def _flash_fwd(qkv, c_rows, prune, *, tb, name):
    t_dim = qkv.shape[0]
    d = qkv.shape[1] // 3
    heads = d // HEAD_DIM
    cb = d // LANES
    nq = t_dim // tb

    def body(prune_ref, q_ref, k_ref, v_ref, cr_ref, o_ref, lse_ref, lser_ref):
        i = pl.program_id(1)
        h0 = 2 * pl.program_id(0)
        q = q_ref[...] * jnp.asarray(HEAD_DIM ** -0.5, BF16)
        masks = _head_masks(tb)
        row = lax.broadcasted_iota(jnp.int32, (tb, tb), 0)
        col = lax.broadcasted_iota(jnp.int32, (tb, tb), 1)
        qs = [jnp.where(masks[e], q, jnp.zeros_like(q)) for e in range(2)]

        def step(j, carry, diagonal):
            off = pl.multiple_of(j * tb, tb)
            kj = k_ref[pl.ds(off, tb), :]
            vj = v_ref[pl.ds(off, tb), :]
            out = []
            for e in range(2):
                m, l, acc = carry[e]
                crow = cr_ref[0, e, :, pl.ds(off, tb)]
                s = lax.dot_general(qs[e], kj, (((1,), (1,)), ((), ())), preferred_element_type=F32) - crow
                if diagonal:
                    s = jnp.where(col <= row, s, NEG_INF)
                m_new = jnp.maximum(m, jnp.max(s, axis=1, keepdims=True))
                p = jnp.exp(s - m_new)
                alpha = jnp.exp(m - m_new)
                l = alpha * l + jnp.sum(p, axis=1, keepdims=True)
                acc = alpha * acc + jnp.dot(p.astype(BF16), vj, preferred_element_type=F32)
                out.append((m_new, l, acc))
            return tuple(out)

        init = (jnp.full((tb, 1), NEG_INF, F32), jnp.zeros((tb, 1), F32), jnp.zeros((tb, LANES), F32))
        kept = jnp.maximum(_kept_before(prune_ref, h0, i, nq), _kept_before(prune_ref, h0 + 1, i, nq))
        carry = lax.fori_loop(i - kept, i, functools.partial(step, diagonal=False), (init, init))
        carry = step(i, carry, True)
        outs = []
        for e in range(2):
            m, l, acc = carry[e]
            outs.append(acc / l)
            lse = m + jnp.log(l)
            lse_ref[0, e] = jnp.broadcast_to(lse, (tb, LANES))
            lser_ref[0, e] = _as_row(lse, tb)
        o_ref[...] = jnp.where(masks[0], outs[0], outs[1]).astype(o_ref.dtype)

    hb_spec = pl.BlockSpec((1, 2, tb, LANES), lambda h, i: (h, 0, i, 0))
    row_spec = pl.BlockSpec((1, 2, 1, t_dim), lambda h, i: (h, 0, 0, 0))
    row_blk = pl.BlockSpec((1, 2, 1, tb), lambda h, i: (h, 0, 0, i))
    return pl.pallas_call(
        body, name=name, grid=(heads // 2, nq),
        in_specs=[pl.BlockSpec(memory_space=pltpu.SMEM),
                  pl.BlockSpec((tb, LANES), lambda h, i: (i, h)),
                  pl.BlockSpec((t_dim, LANES), lambda h, i: (0, cb + h)),
                  pl.BlockSpec((t_dim, LANES), lambda h, i: (0, 2 * cb + h)),
                  row_spec],
        out_specs=[pl.BlockSpec((tb, LANES), lambda h, i: (i, h)), hb_spec, row_blk],
        out_shape=[jax.ShapeDtypeStruct((t_dim, d), BF16),
                   jax.ShapeDtypeStruct((heads // 2, 2, t_dim, LANES), F32),
                   jax.ShapeDtypeStruct((heads // 2, 2, 1, t_dim), F32)],
        compiler_params=_params(dimension_semantics=("parallel", "arbitrary")),
    )(prune, qkv, qkv, qkv, c_rows)


def _flash_dq(qkv, o, do, c_rows, lse_hb, prune, *, tb, name):
    t_dim = qkv.shape[0]
    d = qkv.shape[1] // 3
    heads = d // HEAD_DIM
    cb = d // LANES
    scale = HEAD_DIM ** -0.5
    nq = t_dim // tb

    def body(prune_ref, q_ref, k_ref, v_ref, o_ref, do_ref, cr_ref, lse_ref, dq_ref, dl_ref, rs_ref):
        i = pl.program_id(1)
        h0 = 2 * pl.program_id(0)
        q = q_ref[...] * jnp.asarray(scale, BF16)
        do_blk = do_ref[...]
        prod = do_blk.astype(F32) * o_ref[...].astype(F32)
        masks = _head_masks(tb)
        row = lax.broadcasted_iota(jnp.int32, (tb, tb), 0)
        col = lax.broadcasted_iota(jnp.int32, (tb, tb), 1)
        qs = [jnp.where(masks[e], q, jnp.zeros_like(q)) for e in range(2)]
        dos = [jnp.where(masks[e], do_blk, jnp.zeros_like(do_blk)) for e in range(2)]
        deltas = [jnp.sum(jnp.where(masks[e], prod, 0.0), axis=1, keepdims=True) for e in range(2)]
        lses = [lse_ref[0, e][:, 0:1] for e in range(2)]

        def step(j, carry, diagonal):
            off = pl.multiple_of(j * tb, tb)
            kj = k_ref[pl.ds(off, tb), :]
            vj = v_ref[pl.ds(off, tb), :]
            out = []
            for e in range(2):
                acc, rsum = carry[e]
                crow = cr_ref[0, e, :, pl.ds(off, tb)]
                s = lax.dot_general(qs[e], kj, (((1,), (1,)), ((), ())), preferred_element_type=F32) - crow
                if diagonal:
                    s = jnp.where(col <= row, s, NEG_INF)
                p = jnp.exp(s - lses[e])
                dp = lax.dot_general(dos[e], vj, (((1,), (1,)), ((), ())), preferred_element_type=F32)
                ds = p * (dp - deltas[e])
                out.append((acc + jnp.dot(ds.astype(BF16), kj, preferred_element_type=F32),
                            rsum + jnp.sum(ds, axis=1, keepdims=True)))
            return tuple(out)

        init = (jnp.zeros((tb, LANES), F32), jnp.zeros((tb, 1), F32))
        kept = jnp.maximum(_kept_before(prune_ref, h0, i, nq), _kept_before(prune_ref, h0 + 1, i, nq))
        carry = lax.fori_loop(i - kept, i, functools.partial(step, diagonal=False), (init, init))
        carry = step(i, carry, True)
        for e in range(2):
            dl_ref[0, e] = _as_row(deltas[e], tb)
            rs_ref[0, e] = _as_row(carry[e][1], tb)
        dq_ref[...] = (jnp.where(masks[0], carry[0][0], carry[1][0]) * scale).astype(dq_ref.dtype)

    blk = pl.BlockSpec((tb, LANES), lambda h, i: (i, h))
    hb_spec = pl.BlockSpec((1, 2, tb, LANES), lambda h, i: (h, 0, i, 0))
    row_spec = pl.BlockSpec((1, 2, 1, t_dim), lambda h, i: (h, 0, 0, 0))
    row_blk = pl.BlockSpec((1, 2, 1, tb), lambda h, i: (h, 0, 0, i))
    row_shape = jax.ShapeDtypeStruct((heads // 2, 2, 1, t_dim), F32)
    return pl.pallas_call(
        body, name=name, grid=(heads // 2, nq),
        in_specs=[pl.BlockSpec(memory_space=pltpu.SMEM), blk,
                  pl.BlockSpec((t_dim, LANES), lambda h, i: (0, cb + h)),
                  pl.BlockSpec((t_dim, LANES), lambda h, i: (0, 2 * cb + h)),
                  blk, blk, row_spec, hb_spec],
        out_specs=[blk, row_blk, row_blk],
        out_shape=[jax.ShapeDtypeStruct((t_dim, d), BF16), row_shape, row_shape],
        compiler_params=_params(dimension_semantics=("parallel", "arbitrary")),
    )(prune, qkv, qkv, qkv, o, do, c_rows, lse_hb)


def _flash_dkv(qkv, do, c_hb, lse_rows, delta_rows, prune, *, tb, name):
    t_dim = qkv.shape[0]
    d = qkv.shape[1] // 3
    heads = d // HEAD_DIM
    cb = d // LANES
    scale = HEAD_DIM ** -0.5
    nq = t_dim // tb

    def body(prune_ref, q_ref, k_ref, v_ref, do_ref, cc_ref, lr_ref, dr_ref, dk_ref, dv_ref, dsum_ref):
        j = pl.program_id(1)
        h0 = 2 * pl.program_id(0)
        k_blk = k_ref[...] * jnp.asarray(scale, BF16)
        v_blk = v_ref[...]
        masks = _head_masks(tb)
        row = lax.broadcasted_iota(jnp.int32, (tb, tb), 0)
        col = lax.broadcasted_iota(jnp.int32, (tb, tb), 1)
        ks = [jnp.where(masks[e], k_blk, jnp.zeros_like(k_blk)) for e in range(2)]
        vs = [jnp.where(masks[e], v_blk, jnp.zeros_like(v_blk)) for e in range(2)]
        ccols = [cc_ref[0, e][:, 0:1] for e in range(2)]

        def step(i, carry, diagonal):
            off = pl.multiple_of(i * tb, tb)
            qi = q_ref[pl.ds(off, tb), :]
            doi = do_ref[pl.ds(off, tb), :]
            out = []
            for e in range(2):
                dk, dv, dsum = carry[e]
                lse = lr_ref[0, e, :, pl.ds(off, tb)]
                delta = dr_ref[0, e, :, pl.ds(off, tb)]
                st = lax.dot_general(ks[e], qi, (((1,), (1,)), ((), ())), preferred_element_type=F32) - ccols[e]
                if diagonal:
                    st = jnp.where(col >= row, st, NEG_INF)
                pt = jnp.exp(st - lse)
                dpt = lax.dot_general(vs[e], doi, (((1,), (1,)), ((), ())), preferred_element_type=F32)
                dst = pt * (dpt - delta)
                out.append((dk + jnp.dot(dst.astype(BF16), qi, preferred_element_type=F32),
                            dv + jnp.dot(pt.astype(BF16), doi, preferred_element_type=F32),
                            dsum + jnp.sum(dst, axis=1, keepdims=True)))
            return tuple(out)

        zero = jnp.zeros((tb, LANES), F32)
        init = (zero, zero, jnp.zeros((tb, 1), F32))
        carry = step(j, (init, init), True)
        kept = jnp.maximum(_kept_after(prune_ref, h0, j, nq), _kept_after(prune_ref, h0 + 1, j, nq))
        carry = lax.fori_loop(j + 1, j + 1 + kept, functools.partial(step, diagonal=False), carry)
        for e in range(2):
            dsum_ref[0, e] = _as_row(carry[e][2], tb)
        dk_ref[...] = (jnp.where(masks[0], carry[0][0], carry[1][0]) * scale).astype(dk_ref.dtype)
        dv_ref[...] = jnp.where(masks[0], carry[0][1], carry[1][1]).astype(dv_ref.dtype)

    blk = pl.BlockSpec((tb, LANES), lambda h, j: (j, h))
    hb_spec = pl.BlockSpec((1, 2, tb, LANES), lambda h, j: (h, 0, j, 0))
    row_spec = pl.BlockSpec((1, 2, 1, t_dim), lambda h, j: (h, 0, 0, 0))
    row_blk = pl.BlockSpec((1, 2, 1, tb), lambda h, j: (h, 0, 0, j))
    return pl.pallas_call(
        body, name=name, grid=(heads // 2, nq),
        in_specs=[pl.BlockSpec(memory_space=pltpu.SMEM),
                  pl.BlockSpec((t_dim, LANES), lambda h, j: (0, h)),
                  pl.BlockSpec((tb, LANES), lambda h, j: (j, cb + h)),
                  pl.BlockSpec((tb, LANES), lambda h, j: (j, 2 * cb + h)),
                  pl.BlockSpec((t_dim, LANES), lambda h, j: (0, h)),
                  hb_spec, row_spec, row_spec],
        out_specs=[blk, blk, row_blk],
        out_shape=[jax.ShapeDtypeStruct((t_dim, d), BF16), jax.ShapeDtypeStruct((t_dim, d), BF16),
                   jax.ShapeDtypeStruct((heads // 2, 2, 1, t_dim), F32)],
        compiler_params=_params(dimension_semantics=("parallel", "arbitrary")),
    )(prune, qkv, qkv, qkv, do, c_hb, lse_rows, delta_rows)


def _shift_down(z, prev, n, tt):
    out = pltpu.roll(z, n, axis=0)
    row = lax.broadcasted_iota(jnp.int32, z.shape, 0)
    for r in range(n):
        out = jnp.where(row == r, prev[8 - n + r:8 - n + r + 1, :], out)
    return out


def _shift_up(z, nxt, n, tt):
    out = pltpu.roll(z, tt - n, axis=0)
    row = lax.broadcasted_iota(jnp.int32, z.shape, 0)
    for r in range(n):
        out = jnp.where(row == tt - n + r, nxt[r:r + 1, :], out)
    return out


def _conv_fwd(proj, conv_w, *, name, tt=256):
    t_dim, d3 = proj.shape
    d = d3 // 3
    tt = min(tt, t_dim)

    def body(p_ref, prev_ref, w_ref, y_ref):
        i = pl.program_id(0)
        p = p_ref[...]
        pp = prev_ref[...]
        z = p[:, d:2 * d] * p[:, 2 * d:]
        zp = jnp.where(i > 0, pp[:, d:2 * d] * pp[:, 2 * d:], 0.0)
        w = w_ref[...]
        zc = w[2:3, :] * z + w[1:2, :] * _shift_down(z, zp, 1, tt) + w[0:1, :] * _shift_down(z, zp, 2, tt)
        y_ref[...] = (p[:, :d] * zc).astype(y_ref.dtype)

    return pl.pallas_call(
        body, name=name, grid=(t_dim // tt,),
        in_specs=[pl.BlockSpec((tt, d3), lambda i: (i, 0)),
                  pl.BlockSpec((8, d3), lambda i: (jnp.maximum(i * (tt // 8) - 1, 0), 0)),
                  pl.BlockSpec(conv_w.shape, lambda i: (0, 0))],
        out_specs=pl.BlockSpec((tt, d), lambda i: (i, 0)),
        out_shape=jax.ShapeDtypeStruct((t_dim, d), BF16),
        compiler_params=_params(dimension_semantics=("arbitrary",)),
    )(proj, proj, conv_w)


def _conv_bwd(proj, dy, conv_w, *, name, tt=256):
    t_dim, d3 = proj.shape
    d = d3 // 3
    tt = min(tt, t_dim)
    n = t_dim // tt

    def body(p_ref, prev_ref, next_ref, dy_ref, dyn_ref, w_ref, dp_ref, dw_ref):
        i = pl.program_id(0)
        p = p_ref[...]
        pp = prev_ref[...]
        pn = next_ref[...]
        bg, cg, u = p[:, :d], p[:, d:2 * d], p[:, 2 * d:]
        z = cg * u
        zp = jnp.where(i > 0, pp[:, d:2 * d] * pp[:, 2 * d:], 0.0)
        w = w_ref[...]
        z1 = _shift_down(z, zp, 1, tt)
        z2 = _shift_down(z, zp, 2, tt)
        zc = w[2:3, :] * z + w[1:2, :] * z1 + w[0:1, :] * z2
        dy_blk = dy_ref[...]
        dzc = dy_blk * bg
        dzn = jnp.where(i < n - 1, dyn_ref[...] * pn[:, :d], 0.0)
        dz = w[2:3, :] * dzc + w[1:2, :] * _shift_up(dzc, dzn, 1, tt) + w[0:1, :] * _shift_up(dzc, dzn, 2, tt)
        dp_ref[:, :d] = (dy_blk * zc).astype(dp_ref.dtype)
        dp_ref[:, d:2 * d] = (dz * u).astype(dp_ref.dtype)
        dp_ref[:, 2 * d:] = (dz * cg).astype(dp_ref.dtype)
        part = jnp.concatenate([jnp.sum(dzc * z2, axis=0, keepdims=True),
                                jnp.sum(dzc * z1, axis=0, keepdims=True),
                                jnp.sum(dzc * z, axis=0, keepdims=True),
                                jnp.zeros((5, d), F32)], axis=0)

        @pl.when(i == 0)
        def _():
            dw_ref[...] = part

        @pl.when(i > 0)
        def _():
            dw_ref[...] += part

    last8 = t_dim // 8 - 1
    return pl.pallas_call(
        body, name=name, grid=(n,),
        in_specs=[pl.BlockSpec((tt, d3), lambda i: (i, 0)),
                  pl.BlockSpec((8, d3), lambda i: (jnp.maximum(i * (tt // 8) - 1, 0), 0)),
                  pl.BlockSpec((8, d3), lambda i: (jnp.minimum((i + 1) * (tt // 8), last8), 0)),
                  pl.BlockSpec((tt, d), lambda i: (i, 0)),
                  pl.BlockSpec((8, d), lambda i: (jnp.minimum((i + 1) * (tt // 8), last8), 0)),
                  pl.BlockSpec(conv_w.shape, lambda i: (0, 0))],
        out_specs=[pl.BlockSpec((tt, d3), lambda i: (i, 0)), pl.BlockSpec((8, d), lambda i: (0, 0))],
        out_shape=[jax.ShapeDtypeStruct((t_dim, d3), BF16), jax.ShapeDtypeStruct((8, d), F32)],
        compiler_params=_params(dimension_semantics=("arbitrary",)),
    )(proj, proj, proj, dy, dy, conv_w)


def _window(ref, axis, n, idx):
    if axis is None:
        return ref
    sel = [slice(None)] * len(ref.shape)
    sel[axis] = pl.ds(pl.multiple_of(idx * n, n), n)
    return ref.at[tuple(sel)]


def _exchange(items, *, gather, name):
    n_w = len(items)
    widths = []
    out_shape = []
    for arr, axis in items:
        shp = list(arr.shape)
        if gather:
            widths.append(shp[axis])
            shp[axis] *= N_DEV
        else:
            if axis is not None:
                shp[axis] //= N_DEV
                widths.append(shp[axis])
            else:
                widths.append(None)
            shp = [N_DEV] + shp
        out_shape.append(jax.ShapeDtypeStruct(tuple(shp), arr.dtype))

    def body(*refs):
        src, dst = refs[:n_w], refs[n_w:2 * n_w]
        send_sems, recv_sems, local_sems = refs[2 * n_w:]
        x, y, c = lax.axis_index("x"), lax.axis_index("y"), lax.axis_index("c")
        me = 4 * x + 2 * y + c

        def ends(w, to):
            axis = items[w][1]
            if gather:
                return src[w], _window(dst[w], axis, widths[w], me)
            return _window(src[w], axis, widths[w], to), dst[w].at[me]

        copies = []
        for w in range(n_w):
            cp = pltpu.make_async_copy(*ends(w, me), local_sems.at[w])
            cp.start()
            copies.append(cp)
        if gather:
            chips = [(1 - x, y), (x, 1 - y), (1 - x, 1 - y)]

            def block_copy(w, n, origin, to, from_shard):
                place = _window(dst[w], items[w][1], widths[w], 4 * origin[0] + 2 * origin[1] + origin[2])
                return pltpu.make_async_remote_copy(
                    src_ref=src[w] if from_shard else place, dst_ref=place,
                    send_sem=send_sems.at[w, n], recv_sem=recv_sems.at[w, n],
                    device_id=to, device_id_type=pl.DeviceIdType.MESH)

            sent = []
            for w in range(n_w):
                sent.append(block_copy(w, 0, (x, y, c), (x, y, 1 - c), True))
                sent += [block_copy(w, 1 + n, (x, y, c), (*chip, c), True) for n, chip in enumerate(chips)]
            for cp in sent:
                cp.start()
            for n, chip in enumerate(chips):
                for w in range(n_w):
                    block_copy(w, 1 + n, (*chip, c), (x, y, c), True).wait_recv()
                    passed = block_copy(w, 4 + n, (*chip, c), (x, y, 1 - c), False)
                    passed.start()
                    sent.append(passed)
            for w in range(n_w):
                block_copy(w, 0, (x, y, 1 - c), (x, y, c), True).wait_recv()
                for n, chip in enumerate(chips):
                    block_copy(w, 4 + n, (*chip, 1 - c), (x, y, c), False).wait_recv()
            for cp in sent:
                cp.wait_send()
            for cp in copies:
                cp.wait()
            return
        for k in range(1, N_DEV):
            px = 1 - x if k & 4 else x
            py = 1 - y if k & 2 else y
            pc = 1 - c if k & 1 else c
            for w in range(n_w):
                s_ref, d_ref = ends(w, 4 * px + 2 * py + pc)
                cp = pltpu.make_async_remote_copy(
                    src_ref=s_ref, dst_ref=d_ref, send_sem=send_sems.at[w, k - 1], recv_sem=recv_sems.at[w, k - 1],
                    device_id=(px, py, pc), device_id_type=pl.DeviceIdType.MESH)
                cp.start()
                copies.append(cp)
        for cp in copies:
            cp.wait()

    return pl.pallas_call(
        body, name=name,
        in_specs=[pl.BlockSpec(memory_space=pl.ANY)] * n_w,
        out_specs=[pl.BlockSpec(memory_space=pl.ANY)] * n_w,
        out_shape=out_shape,
        scratch_shapes=[pltpu.SemaphoreType.DMA((n_w, N_DEV - 1)), pltpu.SemaphoreType.DMA((n_w, N_DEV - 1)),
                        pltpu.SemaphoreType.DMA((n_w,))],
    )(*[arr for arr, _ in items])


def _row_tile(rows, cols):
    tr = rows
    while tr % 16 == 0 and tr * cols > 256 * 1024:
        tr //= 2
    return tr


def _sum_parts(parts, *, name):
    n_parts, rows, cols = parts.shape
    tr = _row_tile(rows, cols)

    def body(p_ref, o_ref):
        g = p_ref[0].astype(F32)
        for s in range(1, n_parts):
            g = g + p_ref[s].astype(F32)
        o_ref[...] = g

    return pl.pallas_call(
        body, name=name, grid=(rows // tr,),
        in_specs=[pl.BlockSpec((n_parts, tr, cols), lambda i: (0, i, 0))],
        out_specs=pl.BlockSpec((tr, cols), lambda i: (i, 0)),
        out_shape=jax.ShapeDtypeStruct((rows, cols), F32),
        compiler_params=_params(dimension_semantics=("parallel",)),
    )(parts)


def _adamw(parts, w, m, v, *, name):
    n_parts, rows, cols = parts.shape
    tr = _row_tile(rows, cols)

    def body(p_ref, w_ref, m_ref, v_ref, g_ref, d_ref, nm_ref, nv_ref):
        g = p_ref[0].astype(F32)
        for s in range(1, n_parts):
            g = g + p_ref[s].astype(F32)
        m_new = ADAM_B1 * m_ref[...] + (1.0 - ADAM_B1) * g
        v_new = ADAM_B2 * v_ref[...] + (1.0 - ADAM_B2) * (g * g)
        m_hat = m_new / (1.0 - ADAM_B1 ** ADAM_STEP)
        v_hat = v_new / (1.0 - ADAM_B2 ** ADAM_STEP)
        g_ref[...] = g
        d_ref[...] = -ADAM_LR * (m_hat / (jnp.sqrt(v_hat) + ADAM_EPS) + ADAM_WD * w_ref[...])
        nm_ref[...] = m_new
        nv_ref[...] = v_new

    spec = pl.BlockSpec((tr, cols), lambda i: (i, 0))
    return pl.pallas_call(
        body, name=name, grid=(rows // tr,),
        in_specs=[pl.BlockSpec((n_parts, tr, cols), lambda i: (0, i, 0)), spec, spec, spec],
        out_specs=[spec] * 4,
        out_shape=[jax.ShapeDtypeStruct((rows, cols), F32)] * 4,
        compiler_params=_params(dimension_semantics=("parallel",)),
    )(parts, w, m, v)


def _pad_rows(a, axis, to):
    pad = [(0, 0)] * a.ndim
    pad[axis] = (0, to - a.shape[axis])
    return jnp.pad(a, pad)


def kernel(x, p, norm_g, w_attn_in, b_forget, w_attn_out, w_conv_in, conv_w, w_conv_out, w_mlp_up, w_mlp_down, w_ple_proj, w_ple_gate, loss_target, m_norm_g, m_w_attn_in, m_b_forget, m_w_attn_out, m_w_conv_in, m_conv_w, m_w_conv_out, m_w_mlp_up, m_w_mlp_down, m_w_ple_proj, m_w_ple_gate, v_norm_g, v_w_attn_in, v_b_forget, v_w_attn_out, v_w_conv_in, v_conv_w, v_w_conv_out, v_w_mlp_up, v_w_mlp_down, v_w_ple_proj, v_w_ple_gate):
    shards = dict(norm_g=norm_g, w_attn_in=w_attn_in, b_forget=b_forget, w_attn_out=w_attn_out,
                  w_conv_in=w_conv_in, conv_w=conv_w, w_conv_out=w_conv_out, w_mlp_up=w_mlp_up,
                  w_mlp_down=w_mlp_down, w_ple_proj=w_ple_proj, w_ple_gate=w_ple_gate)
    m_shards = dict(norm_g=m_norm_g, w_attn_in=m_w_attn_in, b_forget=m_b_forget, w_attn_out=m_w_attn_out,
                    w_conv_in=m_w_conv_in, conv_w=m_conv_w, w_conv_out=m_w_conv_out, w_mlp_up=m_w_mlp_up,
                    w_mlp_down=m_w_mlp_down, w_ple_proj=m_w_ple_proj, w_ple_gate=m_w_ple_gate)
    v_shards = dict(norm_g=v_norm_g, w_attn_in=v_w_attn_in, b_forget=v_b_forget, w_attn_out=v_w_attn_out,
                    w_conv_in=v_w_conv_in, conv_w=v_conv_w, w_conv_out=v_w_conv_out, w_mlp_up=v_w_mlp_up,
                    w_mlp_down=v_w_mlp_down, w_ple_proj=v_w_ple_proj, w_ple_gate=v_w_ple_gate)
    t_dim, d = x.shape[-2:]
    depth = p.shape[0]
    n_attn, heads = b_forget.shape
    assert d == heads * HEAD_DIM and x.shape[0] == 1
    tb = min(512, t_dim // 2)
    x0 = x.reshape(t_dim, d)
    target = loss_target.reshape(t_dim, d)

    in_cols = w_attn_in.shape[2]
    in_cols_pad = -(-in_cols // 16) * 16
    gather_names = [n for n in WEIGHT_NAMES if n != 'b_forget']
    gather_items = []
    for n in gather_names:
        if n == 'w_attn_in':
            gather_items.append((_pad_rows(jnp.swapaxes(w_attn_in, 1, 2), 1, in_cols_pad).astype(BF16), 1))
        elif n in ('norm_g', 'conv_w'):
            gather_items.append((shards[n], SHARD_AXIS[n]))
        else:
            gather_items.append((shards[n].astype(BF16), SHARD_AXIS[n]))
    full = dict(zip(gather_names, _exchange(gather_items, gather=True, name="gather_weights")))
    gains = full['norm_g']
    taps = full['conv_w']
    w_in_t = full['w_attn_in'].reshape(n_attn, N_DEV, in_cols_pad, d)[:, :, :in_cols]
    w_in_t = _pad_rows(w_in_t.reshape(n_attn, N_DEV * in_cols, d), 1, 3 * d + LANES)
    bias_pad = jnp.pad(b_forget, ((0, 0), (0, LANES - heads)))

    def gain(i, k):
        return gains[i, k].reshape(1, d)

    def add_norm(x_prev, branch, g_branch, g_next, name):
        def fn(rows, vecs):
            x_new = rows[0] + _norm(rows[1], vecs[0])
            return [x_new, _norm(x_new, vecs[1])], []
        return _rows_call(fn, [x_prev, branch], [g_branch, g_next], [(d, F32), (d, BF16)], [], name=name)

    saved = []
    x_cur = x0
    hn = _rows_call(lambda rows, vecs: ([_norm(rows[0], vecs[0])], []), [x0], [gain(0, 0)], [(d, BF16)], [],
                    name="norm_in")[0]
    loss_rows = dy = None
    for i in range(depth):
        j = i // 2
        s = dict(x0=x_cur, hn=hn)
        if i % 2 == 0:
            s['qkv'] = _mm(hn, w_in_t[j, :3 * d], tb=True, out_dtypes=(BF16,), name=f"attn_in_{i}")
            s['fl'] = _mm(hn, w_in_t[j, 3 * d:], tb=True, name=f"attn_gate_{i}")
            c = _cumsum_fwd(s['fl'], bias_pad[j:j + 1], name=f"gate_cumsum_{i}")
            c_t = c[:, :heads].T
            s['prune'] = _prune_table(c_t, _head_norms(s['qkv'], name=f"head_norms_{i}"), tb)
            c_t = c_t.reshape(heads // 2, 2, t_dim)
            s['c_hb'] = jnp.broadcast_to(c_t[:, :, :, None], (heads // 2, 2, t_dim, LANES))
            s['c_rows'] = c_t.reshape(heads // 2, 2, 1, t_dim)
            s['o'], s['lse_hb'], s['lse_rows'] = _flash_fwd(s['qkv'], s['c_rows'], s['prune'], tb=tb,
                                                           name=f"attn_fwd_{i}")
            s['m'] = _mm(s['o'], full['w_attn_out'][j], name=f"attn_out_{i}")
        else:
            s['proj'] = _mm(hn, full['w_conv_in'][j], name=f"conv_in_{i}")
            s['y'] = _conv_fwd(s['proj'], taps[j], name=f"conv_fwd_{i}")
            s['m'] = _mm(s['y'], full['w_conv_out'][j], name=f"conv_out_{i}")
        s['x1'], s['h2'] = add_norm(x_cur, s['m'], gain(i, 1), gain(i, 2), f"mix_norm_{i}")
        s['u'], s['a'] = _mm(s['h2'], full['w_mlp_up'][i], out_dtypes=(BF16, BF16), name=f"mlp_up_{i}",
                             epi=lambda acc: (acc, jnp.square(jnp.maximum(acc, 0.0))))
        s['f'] = _mm(s['a'], full['w_mlp_down'][i], name=f"mlp_down_{i}")
        s['x2'], s['h4'] = add_norm(s['x1'], s['f'], gain(i, 3), gain(i, 4), f"mlp_norm_{i}")
        s['pi'] = p[i].reshape(t_dim, -1)
        s['pp'] = _mm(s['pi'], full['w_ple_proj'][i], name=f"ple_proj_{i}")
        s['gl'], s['e'] = _mm(s['h4'], full['w_ple_gate'][i], extras=(s['pp'],), out_dtypes=(F32, F32),
                              name=f"ple_gate_{i}", epi=lambda acc, pp: (acc, pp * _sigmoid(acc)))
        if i + 1 < depth:
            x_cur, hn = add_norm(s['x2'], s['e'], gain(i, 5), gain(i + 1, 0), f"ple_norm_{i}")
        else:
            def loss_fn(rows, vecs):
                err = rows[0] + _norm(rows[1], vecs[0]) - rows[2]
                part = 0.5 * jnp.sum(jnp.sum(err * err, axis=1, keepdims=True), axis=0, keepdims=True) / d
                return [err / d], [jnp.broadcast_to(part, (1, LANES))]
            dy, loss_rows = _rows_call(loss_fn, [s['x2'], s['e'], target], [gain(i, 5)], [(d, F32)],
                                       [(1, LANES)], name="loss")
        saved.append(s)
    loss = lax.psum(loss_rows[0, 0], ("x", "y", "c"))

    grads = {n: [None] * shards[n].shape[0] for n in WEIGHT_NAMES}
    d_gains = [[None] * 6 for _ in range(depth)]
    wgrad = functools.partial(_mm, ta=True, out_dtypes=(BF16,))
    dx = dy
    for i in reversed(range(depth)):
        j = i // 2
        s = saved[i]

        def ple_fn(rows, vecs):
            de, dg = _norm_bwd(rows[0], vecs[0], rows[1])
            sg = _sigmoid(rows[2])
            return [de * sg, de * rows[3] * sg * (1.0 - sg)], [dg]
        dpp, dgl, d_gains[i][5] = _rows_call(ple_fn, [s['e'], dx, s['gl'], s['pp']], [gain(i, 5)],
                                             [(d, BF16), (d, BF16)], [(1, d)], name=f"ple_bwd_{i}")
        grads['w_ple_proj'][i] = wgrad(s['pi'], dpp, name=f"ple_proj_dw_{i}")
        grads['w_ple_gate'][i] = wgrad(s['h4'], dgl, name=f"ple_gate_dw_{i}")
        dh4 = _mm(dgl, full['w_ple_gate'][i], tb=True, name=f"ple_gate_dx_{i}")

        def two_norm_bwd(x_res, dh, dx_in, branch, g_res, g_branch, name):
            def fn(rows, vecs):
                d_res, dg_res = _norm_bwd(rows[0], vecs[0], rows[1])
                dx_out = rows[2] + d_res
                d_branch, dg_branch = _norm_bwd(rows[3], vecs[1], dx_out)
                return [dx_out, d_branch], [dg_res, dg_branch]
            return _rows_call(fn, [x_res, dh, dx_in, branch], [g_res, g_branch], [(d, F32), (d, BF16)],
                              [(1, d), (1, d)], name=name)

        dx2, df, d_gains[i][4], d_gains[i][3] = two_norm_bwd(s['x2'], dh4, dx, s['f'], gain(i, 4), gain(i, 3),
                                                            f"mlp_norm_bwd_{i}")
        grads['w_mlp_down'][i] = wgrad(s['a'], df, name=f"mlp_down_dw_{i}")
        du = _mm(df, full['w_mlp_down'][i], tb=True, extras=(s['u'],), out_dtypes=(BF16,), name=f"mlp_down_dx_{i}",
                 epi=lambda acc, u: (acc * (2.0 * jnp.maximum(u.astype(F32), 0.0)),))
        grads['w_mlp_up'][i] = wgrad(s['h2'], du, name=f"mlp_up_dw_{i}")
        dh2 = _mm(du, full['w_mlp_up'][i], tb=True, name=f"mlp_up_dx_{i}")
        dx1, dm, d_gains[i][2], d_gains[i][1] = two_norm_bwd(s['x1'], dh2, dx2, s['m'], gain(i, 2), gain(i, 1),
                                                            f"mix_norm_bwd_{i}")
        if i % 2 == 0:
            grads['w_attn_out'][j] = wgrad(s['o'], dm, name=f"attn_out_dw_{i}")
            do = _mm(dm, full['w_attn_out'][j], tb=True, out_dtypes=(BF16,), name=f"attn_out_dx_{i}")
            dq, delta_rows, rsum_rows = _flash_dq(s['qkv'], s['o'], do, s['c_rows'], s['lse_hb'],
                                                  s['prune'], tb=tb, name=f"attn_dq_{i}")
            dk, dv, csum_rows = _flash_dkv(s['qkv'], do, s['c_hb'], s['lse_rows'], delta_rows,
                                           s['prune'], tb=tb, name=f"attn_dkv_{i}")
            dc = (rsum_rows - csum_rows).reshape(heads, t_dim).T
            dfl, db = _cumsum_bwd(jnp.pad(dc, ((0, 0), (0, LANES - heads))), s['fl'], bias_pad[j:j + 1],
                                  name=f"gate_cumsum_bwd_{i}")
            grads['b_forget'][j] = db[0, :heads]
            dproj = jnp.concatenate([dq, dk, dv, dfl], axis=1)
            grads['w_attn_in'][j] = wgrad(dproj, s['hn'], name=f"attn_in_dw_{i}")
            dhn = _mm(dproj, w_in_t[j], name=f"attn_in_dx_{i}")
        else:
            grads['w_conv_out'][j] = wgrad(s['y'], dm, name=f"conv_out_dw_{i}")
            dyc = _mm(dm, full['w_conv_out'][j], tb=True, name=f"conv_out_dx_{i}")
            dproj, dtaps = _conv_bwd(s['proj'], dyc, taps[j], name=f"conv_bwd_{i}")
            grads['conv_w'][j] = dtaps[:3].astype(BF16)
            grads['w_conv_in'][j] = wgrad(s['hn'], dproj, name=f"conv_in_dw_{i}")
            dhn = _mm(dproj, full['w_conv_in'][j], tb=True, name=f"conv_in_dx_{i}")

        def in_fn(rows, vecs):
            d_res, dg = _norm_bwd(rows[0], vecs[0], rows[1])
            return [rows[2] + d_res], [dg]
        dx, d_gains[i][0] = _rows_call(in_fn, [s['x0'], dhn, dx1], [gain(i, 0)], [(d, F32)], [(1, d)],
                                       name=f"in_norm_bwd_{i}")
    grad_x = dx.reshape(x.shape)

    local = {n: jnp.stack(grads[n]) for n in WEIGHT_NAMES if n not in ('norm_g', 'b_forget')}
    local['norm_g'] = jnp.stack([jnp.concatenate(row, axis=0) for row in d_gains]).astype(BF16)
    g_in_t = local['w_attn_in'][:, :N_DEV * in_cols].reshape(n_attn, N_DEV, in_cols, d)
    local['w_attn_in'] = _pad_rows(g_in_t, 2, in_cols_pad).reshape(n_attn, N_DEV * in_cols_pad, d)
    local['b_forget'] = jnp.zeros((8, LANES), F32).at[:n_attn, :heads].set(jnp.stack(grads['b_forget']))
    axis_of = dict(SHARD_AXIS, w_attn_in=1)
    recv = dict(zip(WEIGHT_NAMES, _exchange([(local[n], axis_of[n]) for n in WEIGHT_NAMES], gather=False,
                                            name="exchange_grads")))
    recv['b_forget'] = recv['b_forget'][:, :n_attn, :heads]
    g_in_t = _sum_parts(recv['w_attn_in'].reshape(N_DEV, n_attn * in_cols_pad, d), name="sum_attn_in")
    recv['w_attn_in'] = jnp.swapaxes(g_in_t.reshape(n_attn, in_cols_pad, d)[:, :in_cols], 1, 2)[None]
    results = {}
    for n in WEIGHT_NAMES:
        shp = shards[n].shape
        flat = lambda a: a.reshape(a.shape[:a.ndim - len(shp)] + (-1, shp[-1]))
        outs = _adamw(flat(recv[n]), flat(shards[n]), flat(m_shards[n]), flat(v_shards[n]), name=f"adamw_{n}")
        results[n] = [o.reshape(shp) for o in outs]
    return (loss, grad_x, *[results[n][k] for k in range(4) for n in WEIGHT_NAMES])
```

```python
import functools

import jax
import jax.numpy as jnp
from jax import lax
from jax.experimental import pallas as pl
from jax.experimental.pallas import tpu as pltpu

F32 = jnp.float32
BF16 = jnp.bfloat16

N_DEV = 8
LANES = 128
HEAD_DIM = 64
VMEM_LIMIT_BYTES = 56 * 1024 * 1024
RMS_EPS = 1e-6
NEG_INF = -1e30
ADAM_LR = 0.001
ADAM_B1 = 0.9
ADAM_B2 = 0.999
ADAM_EPS = 1e-08
ADAM_WD = 0.01
ADAM_STEP = 10
WEIGHT_NAMES = ('norm_g', 'w_attn_in', 'b_forget', 'w_attn_out', 'w_conv_in', 'conv_w', 'w_conv_out',
                'w_mlp_up', 'w_mlp_down', 'w_ple_proj', 'w_ple_gate')
SHARD_AXIS = {'norm_g': 2, 'w_attn_in': 2, 'b_forget': None, 'w_attn_out': 1, 'w_conv_in': 2, 'conv_w': 2,
              'w_conv_out': 1, 'w_mlp_up': 2, 'w_mlp_down': 1, 'w_ple_proj': 2, 'w_ple_gate': 1}


def _params(**kw):
    return pltpu.CompilerParams(vmem_limit_bytes=VMEM_LIMIT_BYTES, **kw)


def _tile(n, cap):
    if n <= cap:
        return n
    t = (cap // LANES) * LANES
    while n % t:
        t -= LANES
    return t


def _mm(a, b, *, ta=False, tb=False, extras=(), epi=None, out_dtypes=(F32,), name):
    m_dim, k_dim = (a.shape[1], a.shape[0]) if ta else a.shape
    n_dim = b.shape[0] if tb else b.shape[1]
    assert k_dim == (b.shape[1] if tb else b.shape[0])
    tk = _tile(k_dim, 1024 if k_dim <= 1024 else 2048)
    nk = k_dim // tk
    simple = not extras and len(out_dtypes) == 1
    tm = _tile(m_dim, 1024 if (nk > 1 and simple) else 512)
    tn = _tile(n_dim, 1024)
    grid = (n_dim // tn, m_dim // tm, nk)
    a_spec = (pl.BlockSpec((tk, tm), lambda j, i, k: (k, i)) if ta
              else pl.BlockSpec((tm, tk), lambda j, i, k: (i, k)))
    b_spec = (pl.BlockSpec((tn, tk), lambda j, i, k: (j, k)) if tb
              else pl.BlockSpec((tk, tn), lambda j, i, k: (k, j)))
    mn_spec = pl.BlockSpec((tm, tn), lambda j, i, k: (i, j))
    dims = (((0 if ta else 1,), (1 if tb else 0,)), ((), ()))
    n_extra, n_out = len(extras), len(out_dtypes)
    if epi is None:
        epi = lambda acc: (acc,)

    def body(a_ref, b_ref, *rest):
        e_refs, o_refs = rest[:n_extra], rest[n_extra:n_extra + n_out]
        part = lax.dot_general(a_ref[...].astype(BF16), b_ref[...].astype(BF16), dims,
                               preferred_element_type=F32)

        def finish(acc):
            for o_ref, val in zip(o_refs, epi(acc, *[e[...] for e in e_refs])):
                o_ref[...] = val.astype(o_ref.dtype)

        if nk == 1:
            finish(part)
        else:
            acc_ref = rest[-1]
            k = pl.program_id(2)

            @pl.when(k == 0)
            def _():
                acc_ref[...] = part

            @pl.when(k > 0)
            def _():
                acc_ref[...] += part

            @pl.when(k == nk - 1)
            def _():
                finish(acc_ref[...])

    outs = pl.pallas_call(
        body, name=name, grid=grid,
        in_specs=[a_spec, b_spec] + [mn_spec] * n_extra,
        out_specs=[mn_spec] * n_out,
        out_shape=[jax.ShapeDtypeStruct((m_dim, n_dim), dt) for dt in out_dtypes],
        scratch_shapes=[pltpu.VMEM((tm, tn), F32)] if nk > 1 else [],
        compiler_params=_params(dimension_semantics=("parallel", "parallel", "arbitrary")),
    )(a, b, *extras)
    return outs[0] if n_out == 1 else outs


def _rows(fn, row_ins, vec_ins, row_outs, vec_outs, *, name, tt=512, reverse=False):
    t_dim = row_ins[0].shape[0]
    tt = min(tt, t_dim)
    n = t_dim // tt
    n_ri, n_vi, n_ro, n_vo = len(row_ins), len(vec_ins), len(row_outs), len(vec_outs)
    pos = (lambda i: (n - 1 - i, 0)) if reverse else (lambda i: (i, 0))
    fixed = lambda i: (0, 0)

    def body(*refs):
        ri = refs[:n_ri]
        vi = refs[n_ri:n_ri + n_vi]
        ro = refs[n_ri + n_vi:n_ri + n_vi + n_ro]
        vo = refs[n_ri + n_vi + n_ro:n_ri + n_vi + n_ro + n_vo]
        scratch = refs[n_ri + n_vi + n_ro + n_vo:]
        r_out, v_out = fn([r[...] for r in ri], [v[...] for v in vi], *scratch)
        for o_ref, val in zip(ro, r_out):
            o_ref[...] = val.astype(o_ref.dtype)
        i = pl.program_id(0)
        for o_ref, val in zip(vo, v_out):
            @pl.when(i == 0)
            def _(o_ref=o_ref, val=val):
                o_ref[...] = val

            @pl.when(i > 0)
            def _(o_ref=o_ref, val=val):
                o_ref[...] += val

    return body, dict(
        grid=(n,),
        in_specs=[pl.BlockSpec((tt, r.shape[1]), pos) for r in row_ins]
        + [pl.BlockSpec(v.shape, fixed) for v in vec_ins],
        out_specs=[pl.BlockSpec((tt, w), pos) for w, _ in row_outs]
        + [pl.BlockSpec(s, fixed) for s in vec_outs],
        out_shape=[jax.ShapeDtypeStruct((t_dim, w), dt) for w, dt in row_outs]
        + [jax.ShapeDtypeStruct(s, F32) for s in vec_outs],
        name=name,
        compiler_params=_params(dimension_semantics=("arbitrary",)),
    )


def _rows_call(fn, row_ins, vec_ins, row_outs, vec_outs, *, name, tt=512, reverse=False, scratch=()):
    body, kw = _rows(fn, row_ins, vec_ins, row_outs, vec_outs, name=name, tt=tt, reverse=reverse)
    return pl.pallas_call(body, scratch_shapes=list(scratch), **kw)(*row_ins, *vec_ins)


def _rstd(x):
    return lax.rsqrt(jnp.mean(x * x, axis=-1, keepdims=True) + RMS_EPS)


def _norm(x, g):
    return x * _rstd(x) * g


def _norm_bwd(x, g, dy):
    xh = x * _rstd(x)
    gy = dy * g
    dx = _rstd(x) * (gy - xh * jnp.mean(gy * xh, axis=-1, keepdims=True))
    return dx, jnp.sum(dy * xh, axis=0, keepdims=True)


def _sigmoid(x):
    return 1.0 / (1.0 + jnp.exp(-x))


def _log_sigmoid(x):
    return jnp.minimum(x, 0.0) - jnp.log(1.0 + jnp.exp(-jnp.abs(x)))


def _split3(x):
    hi = x.astype(BF16)
    r1 = x - hi.astype(F32)
    mid = r1.astype(BF16)
    lo = (r1 - mid.astype(F32)).astype(BF16)
    return hi, mid, lo


def _cumsum_fwd(fl, bias, *, name):
    w = fl.shape[1]
    tt = min(512, fl.shape[0])

    def fn(rows, vecs, carry_ref):
        i = pl.program_id(0)

        @pl.when(i == 0)
        def _():
            carry_ref[...] = jnp.zeros_like(carry_ref)

        lf = _log_sigmoid(rows[0] + vecs[0])
        r = lax.broadcasted_iota(jnp.int32, (tt, tt), 0)
        c = lax.broadcasted_iota(jnp.int32, (tt, tt), 1)
        tri = (c <= r).astype(BF16)
        acc = carry_ref[0:1, :]
        for part in _split3(lf):
            acc = acc + jnp.dot(tri, part, preferred_element_type=F32)
        carry_ref[0:1, :] = acc[tt - 1:tt, :]
        return [acc], []

    return _rows_call(fn, [fl], [bias], [(w, F32)], [], name=name, tt=tt,
                      scratch=[pltpu.VMEM((8, w), F32)])[0]


def _cumsum_bwd(dc, fl, bias, *, name):
    w = fl.shape[1]
    tt = min(512, fl.shape[0])

    def fn(rows, vecs, carry_ref):
        i = pl.program_id(0)

        @pl.when(i == 0)
        def _():
            carry_ref[...] = jnp.zeros_like(carry_ref)

        r = lax.broadcasted_iota(jnp.int32, (tt, tt), 0)
        c = lax.broadcasted_iota(jnp.int32, (tt, tt), 1)
        tri = (c >= r).astype(BF16)
        acc = carry_ref[0:1, :]
        for part in _split3(rows[0]):
            acc = acc + jnp.dot(tri, part, preferred_element_type=F32)
        carry_ref[0:1, :] = acc[0:1, :]
        dfl = acc * _sigmoid(-(rows[1] + vecs[0]))
        return [dfl], [jnp.sum(dfl, axis=0, keepdims=True)]

    return _rows_call(fn, [dc, fl], [bias], [(w, BF16)], [(1, w)], name=name, tt=tt, reverse=True,
                      scratch=[pltpu.VMEM((8, w), F32)])


def _head_masks(tb):
    lane = lax.broadcasted_iota(jnp.int32, (tb, LANES), 1)
    return [lane < HEAD_DIM, lane >= HEAD_DIM]


PRUNE_MARGIN = 30.0


def _head_norms(qkv, *, name):
    t_dim = qkv.shape[0]
    d = qkv.shape[1] // 3
    heads = d // HEAD_DIM
    tt = min(512, t_dim)

    def body(q_ref, k_ref, o_ref):
        col = lax.broadcasted_iota(jnp.int32, (d, LANES), 0) // HEAD_DIM
        lane = lax.broadcasted_iota(jnp.int32, (d, LANES), 1)
        tile_max = None
        for ref, first in ((q_ref, 0), (k_ref, heads)):
            x = ref[...].astype(F32)
            sums = jnp.dot((x * x).astype(BF16), (col + first == lane).astype(BF16), preferred_element_type=F32)
            part = jnp.max(sums, axis=0, keepdims=True)
            tile_max = part if tile_max is None else jnp.maximum(tile_max, part)
        i = pl.program_id(0)

        @pl.when(i == 0)
        def _():
            o_ref[...] = tile_max

        @pl.when(i > 0)
        def _():
            o_ref[...] = jnp.maximum(o_ref[...], tile_max)

    return pl.pallas_call(
        body, name=name, grid=(t_dim // tt,),
        in_specs=[pl.BlockSpec((tt, d), lambda i: (i, 0)), pl.BlockSpec((tt, d), lambda i: (i, 1))],
        out_specs=pl.BlockSpec((1, LANES), lambda i: (0, 0)),
        out_shape=jax.ShapeDtypeStruct((1, LANES), F32),
        compiler_params=_params(dimension_semantics=("arbitrary",)),
    )(qkv, qkv)


def _prune_table(c_t, norms, tb):
    heads = c_t.shape[0]
    bound = 1.02 * HEAD_DIM ** -0.5 * jnp.sqrt(norms[0, :heads] * norms[0, heads:2 * heads])
    return jnp.concatenate([c_t[:, ::tb], c_t[:, tb - 1::tb], -(PRUNE_MARGIN + 2.0 * bound)[:, None]], axis=1)


def _kept_before(prune_ref, h, i, nq):
    first, thr = prune_ref[h, i], prune_ref[h, 2 * nq]
    return lax.fori_loop(0, i, lambda j, n: n + (first - prune_ref[h, nq + j] >= thr).astype(jnp.int32),
                         jnp.int32(0))


def _kept_after(prune_ref, h, j, nq):
    last, thr = prune_ref[h, nq + j], prune_ref[h, 2 * nq]
    return lax.fori_loop(j + 1, nq, lambda i, n: n + (prune_ref[h, i] - last >= thr).astype(jnp.int32),
                         jnp.int32(0))


def _as_row(col, tb):
    return jnp.transpose(jnp.broadcast_to(col, (tb, LANES)))[0:1, :]


def _flash_fwd(qkv, c_rows, prune, *, tb, name):
    t_dim = qkv.shape[0]
    d = qkv.shape[1] // 3
    heads = d // HEAD_DIM
    cb = d // LANES
    nq = t_dim // tb

    def body(prune_ref, q_ref, k_ref, v_ref, cr_ref, o_ref, lse_ref, lser_ref):
        i = pl.program_id(1)
        h0 = 2 * pl.program_id(0)
        q = q_ref[...] * jnp.asarray(HEAD_DIM ** -0.5, BF16)
        masks = _head_masks(tb)
        row = lax.broadcasted_iota(jnp.int32, (tb, tb), 0)
        col = lax.broadcasted_iota(jnp.int32, (tb, tb), 1)
        qs = [jnp.where(masks[e], q, jnp.zeros_like(q)) for e in range(2)]

        def step(j, carry, diagonal):
            off = pl.multiple_of(j * tb, tb)
            kj = k_ref[pl.ds(off, tb), :]
            vj = v_ref[pl.ds(off, tb), :]
            out = []
            for e in range(2):
                m, l, acc = carry[e]
                crow = cr_ref[0, e, :, pl.ds(off, tb)]
                s = lax.dot_general(qs[e], kj, (((1,), (1,)), ((), ())), preferred_element_type=F32) - crow
                if diagonal:
                    s = jnp.where(col <= row, s, NEG_INF)
                m_new = jnp.maximum(m, jnp.max(s, axis=1, keepdims=True))
                p = jnp.exp(s - m_new)
                alpha = jnp.exp(m - m_new)
                l = alpha * l + jnp.sum(p, axis=1, keepdims=True)
                acc = alpha * acc + jnp.dot(p.astype(BF16), vj, preferred_element_type=F32)
                out.append((m_new, l, acc))
            return tuple(out)

        init = (jnp.full((tb, 1), NEG_INF, F32), jnp.zeros((tb, 1), F32), jnp.zeros((tb, LANES), F32))
        kept = jnp.maximum(_kept_before(prune_ref, h0, i, nq), _kept_before(prune_ref, h0 + 1, i, nq))
        carry = lax.fori_loop(i - kept, i, functools.partial(step, diagonal=False), (init, init))
        carry = step(i, carry, True)
        outs = []
        for e in range(2):
            m, l, acc = carry[e]
            outs.append(acc / l)
            lse = m + jnp.log(l)
            lse_ref[0, e] = jnp.broadcast_to(lse, (tb, LANES))
            lser_ref[0, e] = _as_row(lse, tb)
        o_ref[...] = jnp.where(masks[0], outs[0], outs[1]).astype(o_ref.dtype)

    hb_spec = pl.BlockSpec((1, 2, tb, LANES), lambda h, i: (h, 0, i, 0))
    row_spec = pl.BlockSpec((1, 2, 1, t_dim), lambda h, i: (h, 0, 0, 0))
    row_blk = pl.BlockSpec((1, 2, 1, tb), lambda h, i: (h, 0, 0, i))
    return pl.pallas_call(
        body, name=name, grid=(heads // 2, nq),
        in_specs=[pl.BlockSpec(memory_space=pltpu.SMEM),
                  pl.BlockSpec((tb, LANES), lambda h, i: (i, h)),
                  pl.BlockSpec((t_dim, LANES), lambda h, i: (0, cb + h)),
                  pl.BlockSpec((t_dim, LANES), lambda h, i: (0, 2 * cb + h)),
                  row_spec],
        out_specs=[pl.BlockSpec((tb, LANES), lambda h, i: (i, h)), hb_spec, row_blk],
        out_shape=[jax.ShapeDtypeStruct((t_dim, d), BF16),
                   jax.ShapeDtypeStruct((heads // 2, 2, t_dim, LANES), F32),
                   jax.ShapeDtypeStruct((heads // 2, 2, 1, t_dim), F32)],
        compiler_params=_params(dimension_semantics=("parallel", "arbitrary")),
    )(prune, qkv, qkv, qkv, c_rows)


def _flash_dq(qkv, o, do, c_rows, lse_hb, prune, *, tb, name):
    t_dim = qkv.shape[0]
    d = qkv.shape[1] // 3
    heads = d // HEAD_DIM
    cb = d // LANES
    scale = HEAD_DIM ** -0.5
    nq = t_dim // tb

    def body(prune_ref, q_ref, k_ref, v_ref, o_ref, do_ref, cr_ref, lse_ref, dq_ref, dl_ref, rs_ref):
        i = pl.program_id(1)
        h0 = 2 * pl.program_id(0)
        q = q_ref[...] * jnp.asarray(scale, BF16)
        do_blk = do_ref[...]
        prod = do_blk.astype(F32) * o_ref[...].astype(F32)
        masks = _head_masks(tb)
        row = lax.broadcasted_iota(jnp.int32, (tb, tb), 0)
        col = lax.broadcasted_iota(jnp.int32, (tb, tb), 1)
        qs = [jnp.where(masks[e], q, jnp.zeros_like(q)) for e in range(2)]
        dos = [jnp.where(masks[e], do_blk, jnp.zeros_like(do_blk)) for e in range(2)]
        deltas = [jnp.sum(jnp.where(masks[e], prod, 0.0), axis=1, keepdims=True) for e in range(2)]
        lses = [lse_ref[0, e][:, 0:1] for e in range(2)]

        def step(j, carry, diagonal):
            off = pl.multiple_of(j * tb, tb)
            kj = k_ref[pl.ds(off, tb), :]
            vj = v_ref[pl.ds(off, tb), :]
            out = []
            for e in range(2):
                acc, rsum = carry[e]
                crow = cr_ref[0, e, :, pl.ds(off, tb)]
                s = lax.dot_general(qs[e], kj, (((1,), (1,)), ((), ())), preferred_element_type=F32) - crow
                if diagonal:
                    s = jnp.where(col <= row, s, NEG_INF)
                p = jnp.exp(s - lses[e])
                dp = lax.dot_general(dos[e], vj, (((1,), (1,)), ((), ())), preferred_element_type=F32)
                ds = p * (dp - deltas[e])
                out.append((acc + jnp.dot(ds.astype(BF16), kj, preferred_element_type=F32),
                            rsum + jnp.sum(ds, axis=1, keepdims=True)))
            return tuple(out)

        init = (jnp.zeros((tb, LANES), F32), jnp.zeros((tb, 1), F32))
        kept = jnp.maximum(_kept_before(prune_ref, h0, i, nq), _kept_before(prune_ref, h0 + 1, i, nq))
        carry = lax.fori_loop(i - kept, i, functools.partial(step, diagonal=False), (init, init))
        carry = step(i, carry, True)
        for e in range(2):
            dl_ref[0, e] = _as_row(deltas[e], tb)
            rs_ref[0, e] = _as_row(carry[e][1], tb)
        dq_ref[...] = (jnp.where(masks[0], carry[0][0], carry[1][0]) * scale).astype(dq_ref.dtype)

    blk = pl.BlockSpec((tb, LANES), lambda h, i: (i, h))
    hb_spec = pl.BlockSpec((1, 2, tb, LANES), lambda h, i: (h, 0, i, 0))
    row_spec = pl.BlockSpec((1, 2, 1, t_dim), lambda h, i: (h, 0, 0, 0))
    row_blk = pl.BlockSpec((1, 2, 1, tb), lambda h, i: (h, 0, 0, i))
    row_shape = jax.ShapeDtypeStruct((heads // 2, 2, 1, t_dim), F32)
    return pl.pallas_call(
        body, name=name, grid=(heads // 2, nq),
        in_specs=[pl.BlockSpec(memory_space=pltpu.SMEM), blk,
                  pl.BlockSpec((t_dim, LANES), lambda h, i: (0, cb + h)),
                  pl.BlockSpec((t_dim, LANES), lambda h, i: (0, 2 * cb + h)),
                  blk, blk, row_spec, hb_spec],
        out_specs=[blk, row_blk, row_blk],
        out_shape=[jax.ShapeDtypeStruct((t_dim, d), BF16), row_shape, row_shape],
        compiler_params=_params(dimension_semantics=("parallel", "arbitrary")),
    )(prune, qkv, qkv, qkv, o, do, c_rows, lse_hb)


def _flash_dkv(qkv, do, c_hb, lse_rows, delta_rows, prune, *, tb, name, scatter=()):
    t_dim = qkv.shape[0]
    d = qkv.shape[1] // 3
    heads = d // HEAD_DIM
    cb = d // LANES
    scale = HEAD_DIM ** -0.5
    nq = t_dim // tb
    n_w = len(scatter)
    widths, scatter_shapes = _scatter_shapes(scatter)

    def body(prune_ref, q_ref, k_ref, v_ref, do_ref, cc_ref, lr_ref, dr_ref, *rest):
        src, (dk_ref, dv_ref, dsum_ref), dst = rest[:n_w], rest[n_w:n_w + 3], rest[n_w + 3:2 * n_w + 3]
        j = pl.program_id(1)
        h0 = 2 * pl.program_id(0)
        if scatter:
            travel = lambda: _scatter_copies(scatter, widths, src, dst, *rest[2 * n_w + 3:])

            @pl.when((pl.program_id(0) == 0) & (j == 0))
            def _():
                for cp in travel():
                    cp.start()
        k_blk = k_ref[...] * jnp.asarray(scale, BF16)
        v_blk = v_ref[...]
        masks = _head_masks(tb)
        row = lax.broadcasted_iota(jnp.int32, (tb, tb), 0)
        col = lax.broadcasted_iota(jnp.int32, (tb, tb), 1)
        ks = [jnp.where(masks[e], k_blk, jnp.zeros_like(k_blk)) for e in range(2)]
        vs = [jnp.where(masks[e], v_blk, jnp.zeros_like(v_blk)) for e in range(2)]
        ccols = [cc_ref[0, e][:, 0:1] for e in range(2)]

        def step(i, carry, diagonal):
            off = pl.multiple_of(i * tb, tb)
            qi = q_ref[pl.ds(off, tb), :]
            doi = do_ref[pl.ds(off, tb), :]
            out = []
            for e in range(2):
                dk, dv, dsum = carry[e]
                lse = lr_ref[0, e, :, pl.ds(off, tb)]
                delta = dr_ref[0, e, :, pl.ds(off, tb)]
                st = lax.dot_general(ks[e], qi, (((1,), (1,)), ((), ())), preferred_element_type=F32) - ccols[e]
                if diagonal:
                    st = jnp.where(col >= row, st, NEG_INF)
                pt = jnp.exp(st - lse)
                dpt = lax.dot_general(vs[e], doi, (((1,), (1,)), ((), ())), preferred_element_type=F32)
                dst = pt * (dpt - delta)
                out.append((dk + jnp.dot(dst.astype(BF16), qi, preferred_element_type=F32),
                            dv + jnp.dot(pt.astype(BF16), doi, preferred_element_type=F32),
                            dsum + jnp.sum(dst, axis=1, keepdims=True)))
            return tuple(out)

        zero = jnp.zeros((tb, LANES), F32)
        init = (zero, zero, jnp.zeros((tb, 1), F32))
        carry = step(j, (init, init), True)
        kept = jnp.maximum(_kept_after(prune_ref, h0, j, nq), _kept_after(prune_ref, h0 + 1, j, nq))
        carry = lax.fori_loop(j + 1, j + 1 + kept, functools.partial(step, diagonal=False), carry)
        for e in range(2):
            dsum_ref[0, e] = _as_row(carry[e][2], tb)
        dk_ref[...] = (jnp.where(masks[0], carry[0][0], carry[1][0]) * scale).astype(dk_ref.dtype)
        dv_ref[...] = jnp.where(masks[0], carry[0][1], carry[1][1]).astype(dv_ref.dtype)
        if scatter:
            @pl.when((pl.program_id(0) == heads // 2 - 1) & (j == nq - 1))
            def _():
                for cp in travel():
                    cp.wait()

    blk = pl.BlockSpec((tb, LANES), lambda h, j: (j, h))
    hb_spec = pl.BlockSpec((1, 2, tb, LANES), lambda h, j: (h, 0, j, 0))
    row_spec = pl.BlockSpec((1, 2, 1, t_dim), lambda h, j: (h, 0, 0, 0))
    row_blk = pl.BlockSpec((1, 2, 1, tb), lambda h, j: (h, 0, 0, j))
    hbm = pl.BlockSpec(memory_space=pl.ANY)
    outs = pl.pallas_call(
        body, name=name, grid=(heads // 2, nq),
        in_specs=[pl.BlockSpec(memory_space=pltpu.SMEM),
                  pl.BlockSpec((t_dim, LANES), lambda h, j: (0, h)),
                  pl.BlockSpec((tb, LANES), lambda h, j: (j, cb + h)),
                  pl.BlockSpec((tb, LANES), lambda h, j: (j, 2 * cb + h)),
                  pl.BlockSpec((t_dim, LANES), lambda h, j: (0, h)),
                  hb_spec, row_spec, row_spec] + [hbm] * n_w,
        out_specs=[blk, blk, row_blk] + [hbm] * n_w,
        out_shape=[jax.ShapeDtypeStruct((t_dim, d), BF16), jax.ShapeDtypeStruct((t_dim, d), BF16),
                   jax.ShapeDtypeStruct((heads // 2, 2, 1, t_dim), F32)] + scatter_shapes,
        scratch_shapes=_scatter_sems(n_w) if scatter else [],
        compiler_params=_params(dimension_semantics=("arbitrary", "arbitrary")),
    )(prune, qkv, qkv, qkv, do, c_hb, lse_rows, delta_rows, *[arr for arr, _ in scatter])
    return outs[0], outs[1], outs[2], outs[3:]


def _shift_down(z, prev, n, tt):
    out = pltpu.roll(z, n, axis=0)
    row = lax.broadcasted_iota(jnp.int32, z.shape, 0)
    for r in range(n):
        out = jnp.where(row == r, prev[8 - n + r:8 - n + r + 1, :], out)
    return out


def _shift_up(z, nxt, n, tt):
    out = pltpu.roll(z, tt - n, axis=0)
    row = lax.broadcasted_iota(jnp.int32, z.shape, 0)
    for r in range(n):
        out = jnp.where(row == tt - n + r, nxt[r:r + 1, :], out)
    return out


def _conv_fwd(proj, conv_w, *, name, tt=256):
    t_dim, d3 = proj.shape
    d = d3 // 3
    tt = min(tt, t_dim)

    def body(p_ref, prev_ref, w_ref, y_ref):
        i = pl.program_id(0)
        p = p_ref[...]
        pp = prev_ref[...]
        z = p[:, d:2 * d] * p[:, 2 * d:]
        zp = jnp.where(i > 0, pp[:, d:2 * d] * pp[:, 2 * d:], 0.0)
        w = w_ref[...]
        zc = w[2:3, :] * z + w[1:2, :] * _shift_down(z, zp, 1, tt) + w[0:1, :] * _shift_down(z, zp, 2, tt)
        y_ref[...] = (p[:, :d] * zc).astype(y_ref.dtype)

    return pl.pallas_call(
        body, name=name, grid=(t_dim // tt,),
        in_specs=[pl.BlockSpec((tt, d3), lambda i: (i, 0)),
                  pl.BlockSpec((8, d3), lambda i: (jnp.maximum(i * (tt // 8) - 1, 0), 0)),
                  pl.BlockSpec(conv_w.shape, lambda i: (0, 0))],
        out_specs=pl.BlockSpec((tt, d), lambda i: (i, 0)),
        out_shape=jax.ShapeDtypeStruct((t_dim, d), BF16),
        compiler_params=_params(dimension_semantics=("arbitrary",)),
    )(proj, proj, conv_w)


def _conv_bwd(proj, dy, conv_w, *, name, tt=256):
    t_dim, d3 = proj.shape
    d = d3 // 3
    tt = min(tt, t_dim)
    n = t_dim // tt

    def body(p_ref, prev_ref, next_ref, dy_ref, dyn_ref, w_ref, dp_ref, dw_ref):
        i = pl.program_id(0)
        p = p_ref[...]
        pp = prev_ref[...]
        pn = next_ref[...]
        bg, cg, u = p[:, :d], p[:, d:2 * d], p[:, 2 * d:]
        z = cg * u
        zp = jnp.where(i > 0, pp[:, d:2 * d] * pp[:, 2 * d:], 0.0)
        w = w_ref[...]
        z1 = _shift_down(z, zp, 1, tt)
        z2 = _shift_down(z, zp, 2, tt)
        zc = w[2:3, :] * z + w[1:2, :] * z1 + w[0:1, :] * z2
        dy_blk = dy_ref[...]
        dzc = dy_blk * bg
        dzn = jnp.where(i < n - 1, dyn_ref[...] * pn[:, :d], 0.0)
        dz = w[2:3, :] * dzc + w[1:2, :] * _shift_up(dzc, dzn, 1, tt) + w[0:1, :] * _shift_up(dzc, dzn, 2, tt)
        dp_ref[:, :d] = (dy_blk * zc).astype(dp_ref.dtype)
        dp_ref[:, d:2 * d] = (dz * u).astype(dp_ref.dtype)
        dp_ref[:, 2 * d:] = (dz * cg).astype(dp_ref.dtype)
        part = jnp.concatenate([jnp.sum(dzc * z2, axis=0, keepdims=True),
                                jnp.sum(dzc * z1, axis=0, keepdims=True),
                                jnp.sum(dzc * z, axis=0, keepdims=True),
                                jnp.zeros((5, d), F32)], axis=0)

        @pl.when(i == 0)
        def _():
            dw_ref[...] = part

        @pl.when(i > 0)
        def _():
            dw_ref[...] += part

    last8 = t_dim // 8 - 1
    return pl.pallas_call(
        body, name=name, grid=(n,),
        in_specs=[pl.BlockSpec((tt, d3), lambda i: (i, 0)),
                  pl.BlockSpec((8, d3), lambda i: (jnp.maximum(i * (tt // 8) - 1, 0), 0)),
                  pl.BlockSpec((8, d3), lambda i: (jnp.minimum((i + 1) * (tt // 8), last8), 0)),
                  pl.BlockSpec((tt, d), lambda i: (i, 0)),
                  pl.BlockSpec((8, d), lambda i: (jnp.minimum((i + 1) * (tt // 8), last8), 0)),
                  pl.BlockSpec(conv_w.shape, lambda i: (0, 0))],
        out_specs=[pl.BlockSpec((tt, d3), lambda i: (i, 0)), pl.BlockSpec((8, d), lambda i: (0, 0))],
        out_shape=[jax.ShapeDtypeStruct((t_dim, d3), BF16), jax.ShapeDtypeStruct((8, d), F32)],
        compiler_params=_params(dimension_semantics=("arbitrary",)),
    )(proj, proj, proj, dy, dy, conv_w)


def _window(ref, axis, n, idx):
    if axis is None:
        return ref
    sel = [slice(None)] * len(ref.shape)
    sel[axis] = pl.ds(pl.multiple_of(idx * n, n), n)
    return ref.at[tuple(sel)]


def _scatter_shapes(items):
    widths, shapes = [], []
    for arr, axis in items:
        shp = list(arr.shape)
        if axis is not None:
            shp[axis] //= N_DEV
        widths.append(None if axis is None else shp[axis])
        shapes.append(jax.ShapeDtypeStruct((N_DEV, *shp), arr.dtype))
    return widths, shapes


def _scatter_copies(items, widths, src, dst, send_sems, recv_sems, local_sems):
    x, y, c = lax.axis_index("x"), lax.axis_index("y"), lax.axis_index("c")
    me = 4 * x + 2 * y + c
    copies = [pltpu.make_async_copy(_window(src[w], items[w][1], widths[w], me), dst[w].at[me], local_sems.at[w])
              for w in range(len(items))]
    for k in range(1, N_DEV):
        px = 1 - x if k & 4 else x
        py = 1 - y if k & 2 else y
        pc = 1 - c if k & 1 else c
        for w in range(len(items)):
            copies.append(pltpu.make_async_remote_copy(
                src_ref=_window(src[w], items[w][1], widths[w], 4 * px + 2 * py + pc), dst_ref=dst[w].at[me],
                send_sem=send_sems.at[w, k - 1], recv_sem=recv_sems.at[w, k - 1],
                device_id=(px, py, pc), device_id_type=pl.DeviceIdType.MESH))
    return copies


def _scatter_sems(n_w):
    return [pltpu.SemaphoreType.DMA((n_w, N_DEV - 1)), pltpu.SemaphoreType.DMA((n_w, N_DEV - 1)),
            pltpu.SemaphoreType.DMA((n_w,))]


def _exchange(items, *, gather, name):
    n_w = len(items)
    if gather:
        widths = [arr.shape[axis] for arr, axis in items]
        out_shape = [jax.ShapeDtypeStruct(tuple(s * N_DEV if a == axis else s for a, s in enumerate(arr.shape)),
                                          arr.dtype) for arr, axis in items]
    else:
        widths, out_shape = _scatter_shapes(items)

    def body(*refs):
        src, dst = refs[:n_w], refs[n_w:2 * n_w]
        send_sems, recv_sems, local_sems = refs[2 * n_w:]
        if not gather:
            copies = _scatter_copies(items, widths, src, dst, send_sems, recv_sems, local_sems)
            for cp in copies:
                cp.start()
            for cp in copies:
                cp.wait()
            return
        x, y, c = lax.axis_index("x"), lax.axis_index("y"), lax.axis_index("c")
        me = 4 * x + 2 * y + c
        copies = []
        for w in range(n_w):
            cp = pltpu.make_async_copy(src[w], _window(dst[w], items[w][1], widths[w], me), local_sems.at[w])
            cp.start()
            copies.append(cp)
        if gather:
            chips = [(1 - x, y), (x, 1 - y), (1 - x, 1 - y)]

            def block_copy(w, n, origin, to, from_shard):
                place = _window(dst[w], items[w][1], widths[w], 4 * origin[0] + 2 * origin[1] + origin[2])
                return pltpu.make_async_remote_copy(
                    src_ref=src[w] if from_shard else place, dst_ref=place,
                    send_sem=send_sems.at[w, n], recv_sem=recv_sems.at[w, n],
                    device_id=to, device_id_type=pl.DeviceIdType.MESH)

            sent = []
            for w in range(n_w):
                sent.append(block_copy(w, 0, (x, y, c), (x, y, 1 - c), True))
                sent += [block_copy(w, 1 + n, (x, y, c), (*chip, c), True) for n, chip in enumerate(chips)]
            for cp in sent:
                cp.start()
            for n, chip in enumerate(chips):
                for w in range(n_w):
                    block_copy(w, 1 + n, (*chip, c), (x, y, c), True).wait_recv()
                    passed = block_copy(w, 4 + n, (*chip, c), (x, y, 1 - c), False)
                    passed.start()
                    sent.append(passed)
            for w in range(n_w):
                block_copy(w, 0, (x, y, 1 - c), (x, y, c), True).wait_recv()
                for n, chip in enumerate(chips):
                    block_copy(w, 4 + n, (*chip, 1 - c), (x, y, c), False).wait_recv()
            for cp in sent:
                cp.wait_send()
            for cp in copies:
                cp.wait()
            return

    return pl.pallas_call(
        body, name=name,
        in_specs=[pl.BlockSpec(memory_space=pl.ANY)] * n_w,
        out_specs=[pl.BlockSpec(memory_space=pl.ANY)] * n_w,
        out_shape=out_shape,
        scratch_shapes=[pltpu.SemaphoreType.DMA((n_w, N_DEV - 1)), pltpu.SemaphoreType.DMA((n_w, N_DEV - 1)),
                        pltpu.SemaphoreType.DMA((n_w,))],
    )(*[arr for arr, _ in items])


def _row_tile(rows, cols):
    tr = rows
    while tr % 16 == 0 and tr * cols > 256 * 1024:
        tr //= 2
    return tr


def _sum_parts(parts, *, name):
    n_parts, rows, cols = parts.shape
    tr = _row_tile(rows, cols)

    def body(p_ref, o_ref):
        g = p_ref[0].astype(F32)
        for s in range(1, n_parts):
            g = g + p_ref[s].astype(F32)
        o_ref[...] = g

    return pl.pallas_call(
        body, name=name, grid=(rows // tr,),
        in_specs=[pl.BlockSpec((n_parts, tr, cols), lambda i: (0, i, 0))],
        out_specs=pl.BlockSpec((tr, cols), lambda i: (i, 0)),
        out_shape=jax.ShapeDtypeStruct((rows, cols), F32),
        compiler_params=_params(dimension_semantics=("parallel",)),
    )(parts)


def _adamw(parts, w, m, v, *, name):
    n_parts, rows, cols = parts.shape
    tr = _row_tile(rows, cols)

    def body(p_ref, w_ref, m_ref, v_ref, g_ref, d_ref, nm_ref, nv_ref):
        g = p_ref[0].astype(F32)
        for s in range(1, n_parts):
            g = g + p_ref[s].astype(F32)
        m_new = ADAM_B1 * m_ref[...] + (1.0 - ADAM_B1) * g
        v_new = ADAM_B2 * v_ref[...] + (1.0 - ADAM_B2) * (g * g)
        m_hat = m_new / (1.0 - ADAM_B1 ** ADAM_STEP)
        v_hat = v_new / (1.0 - ADAM_B2 ** ADAM_STEP)
        g_ref[...] = g
        d_ref[...] = -ADAM_LR * (m_hat / (jnp.sqrt(v_hat) + ADAM_EPS) + ADAM_WD * w_ref[...])
        nm_ref[...] = m_new
        nv_ref[...] = v_new

    spec = pl.BlockSpec((tr, cols), lambda i: (i, 0))
    return pl.pallas_call(
        body, name=name, grid=(rows // tr,),
        in_specs=[pl.BlockSpec((n_parts, tr, cols), lambda i: (0, i, 0)), spec, spec, spec],
        out_specs=[spec] * 4,
        out_shape=[jax.ShapeDtypeStruct((rows, cols), F32)] * 4,
        compiler_params=_params(dimension_semantics=("parallel",)),
    )(parts, w, m, v)


def _pad_rows(a, axis, to):
    pad = [(0, 0)] * a.ndim
    pad[axis] = (0, to - a.shape[axis])
    return jnp.pad(a, pad)


def kernel(x, p, norm_g, w_attn_in, b_forget, w_attn_out, w_conv_in, conv_w, w_conv_out, w_mlp_up, w_mlp_down, w_ple_proj, w_ple_gate, loss_target, m_norm_g, m_w_attn_in, m_b_forget, m_w_attn_out, m_w_conv_in, m_conv_w, m_w_conv_out, m_w_mlp_up, m_w_mlp_down, m_w_ple_proj, m_w_ple_gate, v_norm_g, v_w_attn_in, v_b_forget, v_w_attn_out, v_w_conv_in, v_conv_w, v_w_conv_out, v_w_mlp_up, v_w_mlp_down, v_w_ple_proj, v_w_ple_gate):
    shards = dict(norm_g=norm_g, w_attn_in=w_attn_in, b_forget=b_forget, w_attn_out=w_attn_out,
                  w_conv_in=w_conv_in, conv_w=conv_w, w_conv_out=w_conv_out, w_mlp_up=w_mlp_up,
                  w_mlp_down=w_mlp_down, w_ple_proj=w_ple_proj, w_ple_gate=w_ple_gate)
    m_shards = dict(norm_g=m_norm_g, w_attn_in=m_w_attn_in, b_forget=m_b_forget, w_attn_out=m_w_attn_out,
                    w_conv_in=m_w_conv_in, conv_w=m_conv_w, w_conv_out=m_w_conv_out, w_mlp_up=m_w_mlp_up,
                    w_mlp_down=m_w_mlp_down, w_ple_proj=m_w_ple_proj, w_ple_gate=m_w_ple_gate)
    v_shards = dict(norm_g=v_norm_g, w_attn_in=v_w_attn_in, b_forget=v_b_forget, w_attn_out=v_w_attn_out,
                    w_conv_in=v_w_conv_in, conv_w=v_conv_w, w_conv_out=v_w_conv_out, w_mlp_up=v_w_mlp_up,
                    w_mlp_down=v_w_mlp_down, w_ple_proj=v_w_ple_proj, w_ple_gate=v_w_ple_gate)
    t_dim, d = x.shape[-2:]
    depth = p.shape[0]
    n_attn, heads = b_forget.shape
    assert d == heads * HEAD_DIM and x.shape[0] == 1
    tb = min(512, t_dim // 2)
    x0 = x.reshape(t_dim, d)
    target = loss_target.reshape(t_dim, d)

    in_cols = w_attn_in.shape[2]
    in_cols_pad = -(-in_cols // 16) * 16
    gather_names = [n for n in WEIGHT_NAMES if n != 'b_forget']
    gather_items = []
    for n in gather_names:
        if n == 'w_attn_in':
            gather_items.append((_pad_rows(jnp.swapaxes(w_attn_in, 1, 2), 1, in_cols_pad).astype(BF16), 1))
        elif n in ('norm_g', 'conv_w'):
            gather_items.append((shards[n], SHARD_AXIS[n]))
        else:
            gather_items.append((shards[n].astype(BF16), SHARD_AXIS[n]))
    full = dict(zip(gather_names, _exchange(gather_items, gather=True, name="gather_weights")))
    gains = full['norm_g']
    taps = full['conv_w']
    w_in_t = full['w_attn_in'].reshape(n_attn, N_DEV, in_cols_pad, d)[:, :, :in_cols]
    w_in_t = _pad_rows(w_in_t.reshape(n_attn, N_DEV * in_cols, d), 1, 3 * d + LANES)
    bias_pad = jnp.pad(b_forget, ((0, 0), (0, LANES - heads)))

    def gain(i, k):
        return gains[i, k].reshape(1, d)

    def add_norm(x_prev, branch, g_branch, g_next, name):
        def fn(rows, vecs):
            x_new = rows[0] + _norm(rows[1], vecs[0])
            return [x_new, _norm(x_new, vecs[1])], []
        return _rows_call(fn, [x_prev, branch], [g_branch, g_next], [(d, F32), (d, BF16)], [], name=name)

    saved = []
    x_cur = x0
    hn = _rows_call(lambda rows, vecs: ([_norm(rows[0], vecs[0])], []), [x0], [gain(0, 0)], [(d, BF16)], [],
                    name="norm_in")[0]
    loss_rows = dy = None
    for i in range(depth):
        j = i // 2
        s = dict(x0=x_cur, hn=hn)
        if i % 2 == 0:
            s['qkv'] = _mm(hn, w_in_t[j, :3 * d], tb=True, out_dtypes=(BF16,), name=f"attn_in_{i}")
            s['fl'] = _mm(hn, w_in_t[j, 3 * d:], tb=True, name=f"attn_gate_{i}")
            c = _cumsum_fwd(s['fl'], bias_pad[j:j + 1], name=f"gate_cumsum_{i}")
            c_t = c[:, :heads].T
            s['prune'] = _prune_table(c_t, _head_norms(s['qkv'], name=f"head_norms_{i}"), tb)
            c_t = c_t.reshape(heads // 2, 2, t_dim)
            s['c_hb'] = jnp.broadcast_to(c_t[:, :, :, None], (heads // 2, 2, t_dim, LANES))
            s['c_rows'] = c_t.reshape(heads // 2, 2, 1, t_dim)
            s['o'], s['lse_hb'], s['lse_rows'] = _flash_fwd(s['qkv'], s['c_rows'], s['prune'], tb=tb,
                                                           name=f"attn_fwd_{i}")
            s['m'] = _mm(s['o'], full['w_attn_out'][j], name=f"attn_out_{i}")
        else:
            s['proj'] = _mm(hn, full['w_conv_in'][j], name=f"conv_in_{i}")
            s['y'] = _conv_fwd(s['proj'], taps[j], name=f"conv_fwd_{i}")
            s['m'] = _mm(s['y'], full['w_conv_out'][j], name=f"conv_out_{i}")
        s['x1'], s['h2'] = add_norm(x_cur, s['m'], gain(i, 1), gain(i, 2), f"mix_norm_{i}")
        s['u'], s['a'] = _mm(s['h2'], full['w_mlp_up'][i], out_dtypes=(BF16, BF16), name=f"mlp_up_{i}",
                             epi=lambda acc: (acc, jnp.square(jnp.maximum(acc, 0.0))))
        s['f'] = _mm(s['a'], full['w_mlp_down'][i], name=f"mlp_down_{i}")
        s['x2'], s['h4'] = add_norm(s['x1'], s['f'], gain(i, 3), gain(i, 4), f"mlp_norm_{i}")
        s['pi'] = p[i].reshape(t_dim, -1)
        s['pp'] = _mm(s['pi'], full['w_ple_proj'][i], name=f"ple_proj_{i}")
        s['gl'], s['e'] = _mm(s['h4'], full['w_ple_gate'][i], extras=(s['pp'],), out_dtypes=(F32, F32),
                              name=f"ple_gate_{i}", epi=lambda acc, pp: (acc, pp * _sigmoid(acc)))
        if i + 1 < depth:
            x_cur, hn = add_norm(s['x2'], s['e'], gain(i, 5), gain(i + 1, 0), f"ple_norm_{i}")
        else:
            def loss_fn(rows, vecs):
                err = rows[0] + _norm(rows[1], vecs[0]) - rows[2]
                part = 0.5 * jnp.sum(jnp.sum(err * err, axis=1, keepdims=True), axis=0, keepdims=True) / d
                return [err / d], [jnp.broadcast_to(part, (1, LANES))]
            dy, loss_rows = _rows_call(loss_fn, [s['x2'], s['e'], target], [gain(i, 5)], [(d, F32)],
                                       [(1, LANES)], name="loss")
        saved.append(s)
    loss = lax.psum(loss_rows[0, 0], ("x", "y", "c"))

    grads = {n: [None] * shards[n].shape[0] for n in WEIGHT_NAMES}
    d_gains = [[None] * 6 for _ in range(depth)]
    wgrad = functools.partial(_mm, ta=True, out_dtypes=(BF16,))
    axis_of = dict(SHARD_AXIS, w_attn_in=1)

    def in_t_blocks(layers):
        g = jnp.stack(layers)[:, :N_DEV * in_cols].reshape(len(layers), N_DEV, in_cols, d)
        return _pad_rows(g, 2, in_cols_pad).reshape(len(layers), N_DEV * in_cols_pad, d)

    early_names = early_items = early = None
    dx = dy
    for i in reversed(range(depth)):
        j = i // 2
        s = saved[i]

        def ple_fn(rows, vecs):
            de, dg = _norm_bwd(rows[0], vecs[0], rows[1])
            sg = _sigmoid(rows[2])
            return [de * sg, de * rows[3] * sg * (1.0 - sg)], [dg]
        dpp, dgl, d_gains[i][5] = _rows_call(ple_fn, [s['e'], dx, s['gl'], s['pp']], [gain(i, 5)],
                                             [(d, BF16), (d, BF16)], [(1, d)], name=f"ple_bwd_{i}")
        grads['w_ple_proj'][i] = wgrad(s['pi'], dpp, name=f"ple_proj_dw_{i}")
        grads['w_ple_gate'][i] = wgrad(s['h4'], dgl, name=f"ple_gate_dw_{i}")
        dh4 = _mm(dgl, full['w_ple_gate'][i], tb=True, name=f"ple_gate_dx_{i}")

        def two_norm_bwd(x_res, dh, dx_in, branch, g_res, g_branch, name):
            def fn(rows, vecs):
                d_res, dg_res = _norm_bwd(rows[0], vecs[0], rows[1])
                dx_out = rows[2] + d_res
                d_branch, dg_branch = _norm_bwd(rows[3], vecs[1], dx_out)
                return [dx_out, d_branch], [dg_res, dg_branch]
            return _rows_call(fn, [x_res, dh, dx_in, branch], [g_res, g_branch], [(d, F32), (d, BF16)],
                              [(1, d), (1, d)], name=name)

        dx2, df, d_gains[i][4], d_gains[i][3] = two_norm_bwd(s['x2'], dh4, dx, s['f'], gain(i, 4), gain(i, 3),
                                                            f"mlp_norm_bwd_{i}")
        grads['w_mlp_down'][i] = wgrad(s['a'], df, name=f"mlp_down_dw_{i}")
        du = _mm(df, full['w_mlp_down'][i], tb=True, extras=(s['u'],), out_dtypes=(BF16,), name=f"mlp_down_dx_{i}",
                 epi=lambda acc, u: (acc * (2.0 * jnp.maximum(u.astype(F32), 0.0)),))
        grads['w_mlp_up'][i] = wgrad(s['h2'], du, name=f"mlp_up_dw_{i}")
        dh2 = _mm(du, full['w_mlp_up'][i], tb=True, name=f"mlp_up_dx_{i}")
        dx1, dm, d_gains[i][2], d_gains[i][1] = two_norm_bwd(s['x1'], dh2, dx2, s['m'], gain(i, 2), gain(i, 1),
                                                            f"mix_norm_bwd_{i}")
        if i % 2 == 0:
            grads['w_attn_out'][j] = wgrad(s['o'], dm, name=f"attn_out_dw_{i}")
            do = _mm(dm, full['w_attn_out'][j], tb=True, out_dtypes=(BF16,), name=f"attn_out_dx_{i}")
            dq, delta_rows, rsum_rows = _flash_dq(s['qkv'], s['o'], do, s['c_rows'], s['lse_hb'],
                                                  s['prune'], tb=tb, name=f"attn_dq_{i}")
            if i == 0:
                early_names = [n for n in WEIGHT_NAMES if n not in ('norm_g', 'b_forget')]
                early_items = [(in_t_blocks(grads[n][1:]) if n == 'w_attn_in' else jnp.stack(grads[n]), axis_of[n])
                               for n in early_names]
            dk, dv, csum_rows, early = _flash_dkv(s['qkv'], do, s['c_hb'], s['lse_rows'], delta_rows, s['prune'],
                                                  tb=tb, name=f"attn_dkv_{i}", scatter=early_items if i == 0 else ())
            dc = (rsum_rows - csum_rows).reshape(heads, t_dim).T
            dfl, db = _cumsum_bwd(jnp.pad(dc, ((0, 0), (0, LANES - heads))), s['fl'], bias_pad[j:j + 1],
                                  name=f"gate_cumsum_bwd_{i}")
            grads['b_forget'][j] = db[0, :heads]
            dproj = jnp.concatenate([dq, dk, dv, dfl], axis=1)
            grads['w_attn_in'][j] = wgrad(dproj, s['hn'], name=f"attn_in_dw_{i}")
            dhn = _mm(dproj, w_in_t[j], name=f"attn_in_dx_{i}")
        else:
            grads['w_conv_out'][j] = wgrad(s['y'], dm, name=f"conv_out_dw_{i}")
            dyc = _mm(dm, full['w_conv_out'][j], tb=True, name=f"conv_out_dx_{i}")
            dproj, dtaps = _conv_bwd(s['proj'], dyc, taps[j], name=f"conv_bwd_{i}")
            grads['conv_w'][j] = dtaps[:3].astype(BF16)
            grads['w_conv_in'][j] = wgrad(s['hn'], dproj, name=f"conv_in_dw_{i}")
            dhn = _mm(dproj, full['w_conv_in'][j], tb=True, name=f"conv_in_dx_{i}")

        def in_fn(rows, vecs):
            d_res, dg = _norm_bwd(rows[0], vecs[0], rows[1])
            return [rows[2] + d_res], [dg]
        dx, d_gains[i][0] = _rows_call(in_fn, [s['x0'], dhn, dx1], [gain(i, 0)], [(d, F32)], [(1, d)],
                                       name=f"in_norm_bwd_{i}")
    grad_x = dx.reshape(x.shape)

    recv = dict(zip(early_names, early))
    late = _exchange([(in_t_blocks(grads['w_attn_in'][:1]), 1),
                      (jnp.stack([jnp.concatenate(row, axis=0) for row in d_gains]).astype(BF16), SHARD_AXIS['norm_g']),
                      (jnp.zeros((8, LANES), F32).at[:n_attn, :heads].set(jnp.stack(grads['b_forget'])), None)],
                     gather=False, name="exchange_late")
    recv['norm_g'] = late[1]
    recv['b_forget'] = late[2][:, :n_attn, :heads]
    g_in_t = jnp.concatenate([_sum_parts(part.reshape(N_DEV, -1, d), name=f"sum_attn_in_{k}")
                              for k, part in enumerate((late[0], recv['w_attn_in']))], axis=0)
    recv['w_attn_in'] = jnp.swapaxes(g_in_t.reshape(n_attn, in_cols_pad, d)[:, :in_cols], 1, 2)[None]
    results = {}
    for n in WEIGHT_NAMES:
        shp = shards[n].shape
        flat = lambda a: a.reshape(a.shape[:a.ndim - len(shp)] + (-1, shp[-1]))
        outs = _adamw(flat(recv[n]), flat(shards[n]), flat(m_shards[n]), flat(v_shards[n]), name=f"adamw_{n}")
        results[n] = [o.reshape(shp) for o in outs]
    return (loss, grad_x, *[results[n][k] for k in range(4) for n in WEIGHT_NAMES])
```

```python
import functools

import jax
import jax.numpy as jnp
from jax import lax
from jax.experimental import pallas as pl
from jax.experimental.pallas import tpu as pltpu

F32 = jnp.float32
BF16 = jnp.bfloat16

N_DEV = 8
LANES = 128
HEAD_DIM = 64
VMEM_LIMIT_BYTES = 56 * 1024 * 1024
RMS_EPS = 1e-6
NEG_INF = -1e30
ADAM_LR = 0.001
ADAM_B1 = 0.9
ADAM_B2 = 0.999
ADAM_EPS = 1e-08
ADAM_WD = 0.01
ADAM_STEP = 10
WEIGHT_NAMES = ('norm_g', 'w_attn_in', 'b_forget', 'w_attn_out', 'w_conv_in', 'conv_w', 'w_conv_out',
                'w_mlp_up', 'w_mlp_down', 'w_ple_proj', 'w_ple_gate')
SHARD_AXIS = {'norm_g': 2, 'w_attn_in': 2, 'b_forget': None, 'w_attn_out': 1, 'w_conv_in': 2, 'conv_w': 2,
              'w_conv_out': 1, 'w_mlp_up': 2, 'w_mlp_down': 1, 'w_ple_proj': 2, 'w_ple_gate': 1}


def _params(**kw):
    return pltpu.CompilerParams(vmem_limit_bytes=VMEM_LIMIT_BYTES, **kw)


def _tile(n, cap):
    if n <= cap:
        return n
    t = (cap // LANES) * LANES
    while n % t:
        t -= LANES
    return t


def _mm(a, b, *, ta=False, tb=False, extras=(), epi=None, out_dtypes=(F32,), name):
    m_dim, k_dim = (a.shape[1], a.shape[0]) if ta else a.shape
    n_dim = b.shape[0] if tb else b.shape[1]
    assert k_dim == (b.shape[1] if tb else b.shape[0])
    tk = _tile(k_dim, 1024 if k_dim <= 1024 else 2048)
    nk = k_dim // tk
    simple = not extras and len(out_dtypes) == 1
    tm = _tile(m_dim, 1024 if (nk > 1 and simple) else 512)
    tn = _tile(n_dim, 1024)
    grid = (n_dim // tn, m_dim // tm, nk)
    a_spec = (pl.BlockSpec((tk, tm), lambda j, i, k: (k, i)) if ta
              else pl.BlockSpec((tm, tk), lambda j, i, k: (i, k)))
    b_spec = (pl.BlockSpec((tn, tk), lambda j, i, k: (j, k)) if tb
              else pl.BlockSpec((tk, tn), lambda j, i, k: (k, j)))
    mn_spec = pl.BlockSpec((tm, tn), lambda j, i, k: (i, j))
    dims = (((0 if ta else 1,), (1 if tb else 0,)), ((), ()))
    n_extra, n_out = len(extras), len(out_dtypes)
    if epi is None:
        epi = lambda acc: (acc,)

    def body(a_ref, b_ref, *rest):
        e_refs, o_refs = rest[:n_extra], rest[n_extra:n_extra + n_out]
        part = lax.dot_general(a_ref[...].astype(BF16), b_ref[...].astype(BF16), dims,
                               preferred_element_type=F32)

        def finish(acc):
            for o_ref, val in zip(o_refs, epi(acc, *[e[...] for e in e_refs])):
                o_ref[...] = val.astype(o_ref.dtype)

        if nk == 1:
            finish(part)
        else:
            acc_ref = rest[-1]
            k = pl.program_id(2)

            @pl.when(k == 0)
            def _():
                acc_ref[...] = part

            @pl.when(k > 0)
            def _():
                acc_ref[...] += part

            @pl.when(k == nk - 1)
            def _():
                finish(acc_ref[...])

    outs = pl.pallas_call(
        body, name=name, grid=grid,
        in_specs=[a_spec, b_spec] + [mn_spec] * n_extra,
        out_specs=[mn_spec] * n_out,
        out_shape=[jax.ShapeDtypeStruct((m_dim, n_dim), dt) for dt in out_dtypes],
        scratch_shapes=[pltpu.VMEM((tm, tn), F32)] if nk > 1 else [],
        compiler_params=_params(dimension_semantics=("parallel", "parallel", "arbitrary")),
    )(a, b, *extras)
    return outs[0] if n_out == 1 else outs


def _rows(fn, row_ins, vec_ins, row_outs, vec_outs, *, name, tt=512, reverse=False):
    t_dim = row_ins[0].shape[0]
    tt = min(tt, t_dim)
    n = t_dim // tt
    n_ri, n_vi, n_ro, n_vo = len(row_ins), len(vec_ins), len(row_outs), len(vec_outs)
    pos = (lambda i: (n - 1 - i, 0)) if reverse else (lambda i: (i, 0))
    fixed = lambda i: (0, 0)

    def body(*refs):
        ri = refs[:n_ri]
        vi = refs[n_ri:n_ri + n_vi]
        ro = refs[n_ri + n_vi:n_ri + n_vi + n_ro]
        vo = refs[n_ri + n_vi + n_ro:n_ri + n_vi + n_ro + n_vo]
        scratch = refs[n_ri + n_vi + n_ro + n_vo:]
        r_out, v_out = fn([r[...] for r in ri], [v[...] for v in vi], *scratch)
        for o_ref, val in zip(ro, r_out):
            o_ref[...] = val.astype(o_ref.dtype)
        i = pl.program_id(0)
        for o_ref, val in zip(vo, v_out):
            @pl.when(i == 0)
            def _(o_ref=o_ref, val=val):
                o_ref[...] = val

            @pl.when(i > 0)
            def _(o_ref=o_ref, val=val):
                o_ref[...] += val

    return body, dict(
        grid=(n,),
        in_specs=[pl.BlockSpec((tt, r.shape[1]), pos) for r in row_ins]
        + [pl.BlockSpec(v.shape, fixed) for v in vec_ins],
        out_specs=[pl.BlockSpec((tt, w), pos) for w, _ in row_outs]
        + [pl.BlockSpec(s, fixed) for s in vec_outs],
        out_shape=[jax.ShapeDtypeStruct((t_dim, w), dt) for w, dt in row_outs]
        + [jax.ShapeDtypeStruct(s, F32) for s in vec_outs],
        name=name,
        compiler_params=_params(dimension_semantics=("arbitrary",)),
    )


def _rows_call(fn, row_ins, vec_ins, row_outs, vec_outs, *, name, tt=512, reverse=False, scratch=()):
    body, kw = _rows(fn, row_ins, vec_ins, row_outs, vec_outs, name=name, tt=tt, reverse=reverse)
    return pl.pallas_call(body, scratch_shapes=list(scratch), **kw)(*row_ins, *vec_ins)


def _rstd(x):
    return lax.rsqrt(jnp.mean(x * x, axis=-1, keepdims=True) + RMS_EPS)


def _norm(x, g):
    return x * _rstd(x) * g


def _norm_bwd(x, g, dy):
    xh = x * _rstd(x)
    gy = dy * g
    dx = _rstd(x) * (gy - xh * jnp.mean(gy * xh, axis=-1, keepdims=True))
    return dx, jnp.sum(dy * xh, axis=0, keepdims=True)


def _sigmoid(x):
    return 1.0 / (1.0 + jnp.exp(-x))


def _log_sigmoid(x):
    return jnp.minimum(x, 0.0) - jnp.log(1.0 + jnp.exp(-jnp.abs(x)))


def _split3(x):
    hi = x.astype(BF16)
    r1 = x - hi.astype(F32)
    mid = r1.astype(BF16)
    lo = (r1 - mid.astype(F32)).astype(BF16)
    return hi, mid, lo


def _cumsum_fwd(fl, bias, *, name):
    w = fl.shape[1]
    tt = min(512, fl.shape[0])

    def fn(rows, vecs, carry_ref):
        i = pl.program_id(0)

        @pl.when(i == 0)
        def _():
            carry_ref[...] = jnp.zeros_like(carry_ref)

        lf = _log_sigmoid(rows[0] + vecs[0])
        r = lax.broadcasted_iota(jnp.int32, (tt, tt), 0)
        c = lax.broadcasted_iota(jnp.int32, (tt, tt), 1)
        tri = (c <= r).astype(BF16)
        acc = carry_ref[0:1, :]
        for part in _split3(lf):
            acc = acc + jnp.dot(tri, part, preferred_element_type=F32)
        carry_ref[0:1, :] = acc[tt - 1:tt, :]
        return [acc], []

    return _rows_call(fn, [fl], [bias], [(w, F32)], [], name=name, tt=tt,
                      scratch=[pltpu.VMEM((8, w), F32)])[0]


def _cumsum_bwd(dc, fl, bias, *, name):
    w = fl.shape[1]
    tt = min(512, fl.shape[0])

    def fn(rows, vecs, carry_ref):
        i = pl.program_id(0)

        @pl.when(i == 0)
        def _():
            carry_ref[...] = jnp.zeros_like(carry_ref)

        r = lax.broadcasted_iota(jnp.int32, (tt, tt), 0)
        c = lax.broadcasted_iota(jnp.int32, (tt, tt), 1)
        tri = (c >= r).astype(BF16)
        acc = carry_ref[0:1, :]
        for part in _split3(rows[0]):
            acc = acc + jnp.dot(tri, part, preferred_element_type=F32)
        carry_ref[0:1, :] = acc[0:1, :]
        dfl = acc * _sigmoid(-(rows[1] + vecs[0]))
        return [dfl], [jnp.sum(dfl, axis=0, keepdims=True)]

    return _rows_call(fn, [dc, fl], [bias], [(w, BF16)], [(1, w)], name=name, tt=tt, reverse=True,
                      scratch=[pltpu.VMEM((8, w), F32)])


def _head_masks(tb):
    lane = lax.broadcasted_iota(jnp.int32, (tb, LANES), 1)
    return [lane < HEAD_DIM, lane >= HEAD_DIM]


PRUNE_MARGIN = 30.0


def _head_norms(qkv, *, name):
    t_dim = qkv.shape[0]
    d = qkv.shape[1] // 3
    heads = d // HEAD_DIM
    tt = min(512, t_dim)

    def body(q_ref, k_ref, o_ref):
        col = lax.broadcasted_iota(jnp.int32, (d, LANES), 0) // HEAD_DIM
        lane = lax.broadcasted_iota(jnp.int32, (d, LANES), 1)
        tile_max = None
        for ref, first in ((q_ref, 0), (k_ref, heads)):
            x = ref[...].astype(F32)
            sums = jnp.dot((x * x).astype(BF16), (col + first == lane).astype(BF16), preferred_element_type=F32)
            part = jnp.max(sums, axis=0, keepdims=True)
            tile_max = part if tile_max is None else jnp.maximum(tile_max, part)
        i = pl.program_id(0)

        @pl.when(i == 0)
        def _():
            o_ref[...] = tile_max

        @pl.when(i > 0)
        def _():
            o_ref[...] = jnp.maximum(o_ref[...], tile_max)

    return pl.pallas_call(
        body, name=name, grid=(t_dim // tt,),
        in_specs=[pl.BlockSpec((tt, d), lambda i: (i, 0)), pl.BlockSpec((tt, d), lambda i: (i, 1))],
        out_specs=pl.BlockSpec((1, LANES), lambda i: (0, 0)),
        out_shape=jax.ShapeDtypeStruct((1, LANES), F32),
        compiler_params=_params(dimension_semantics=("arbitrary",)),
    )(qkv, qkv)


def _prune_table(c_t, norms, tb):
    heads = c_t.shape[0]
    bound = 1.02 * HEAD_DIM ** -0.5 * jnp.sqrt(norms[0, :heads] * norms[0, heads:2 * heads])
    return jnp.concatenate([c_t[:, ::tb], c_t[:, tb - 1::tb], -(PRUNE_MARGIN + 2.0 * bound)[:, None]], axis=1)


def _kept_before(prune_ref, h, i, nq):
    first, thr = prune_ref[h, i], prune_ref[h, 2 * nq]
    return lax.fori_loop(0, i, lambda j, n: n + (first - prune_ref[h, nq + j] >= thr).astype(jnp.int32),
                         jnp.int32(0))


def _kept_after(prune_ref, h, j, nq):
    last, thr = prune_ref[h, nq + j], prune_ref[h, 2 * nq]
    return lax.fori_loop(j + 1, nq, lambda i, n: n + (prune_ref[h, i] - last >= thr).astype(jnp.int32),
                         jnp.int32(0))


def _as_row(col, tb):
    return jnp.transpose(jnp.broadcast_to(col, (tb, LANES)))[0:1, :]


def _flash_fwd(qkv, c_rows, prune, *, tb, name, gather=()):
    t_dim = qkv.shape[0]
    d = qkv.shape[1] // 3
    heads = d // HEAD_DIM
    cb = d // LANES
    nq = t_dim // tb
    n_w = len(gather)
    widths, gather_shapes = _gather_shapes(gather)

    def body(prune_ref, q_ref, k_ref, v_ref, cr_ref, *rest):
        src, (o_ref, lse_ref, lser_ref), dst = rest[:n_w], rest[n_w:n_w + 3], rest[n_w + 3:2 * n_w + 3]
        i = pl.program_id(1)
        h0 = 2 * pl.program_id(0)
        if gather:
            start, relay, finish = _gather_phases(gather, widths, src, dst, *rest[2 * n_w + 3:])
            pl.when((pl.program_id(0) == 0) & (i == 0))(start)
            pl.when((pl.program_id(0) == (3 * heads) // 8) & (i == 0))(relay)
        q = q_ref[...] * jnp.asarray(HEAD_DIM ** -0.5, BF16)
        masks = _head_masks(tb)
        row = lax.broadcasted_iota(jnp.int32, (tb, tb), 0)
        col = lax.broadcasted_iota(jnp.int32, (tb, tb), 1)
        qs = [jnp.where(masks[e], q, jnp.zeros_like(q)) for e in range(2)]

        def step(j, carry, diagonal):
            off = pl.multiple_of(j * tb, tb)
            kj = k_ref[pl.ds(off, tb), :]
            vj = v_ref[pl.ds(off, tb), :]
            out = []
            for e in range(2):
                m, l, acc = carry[e]
                crow = cr_ref[0, e, :, pl.ds(off, tb)]
                s = lax.dot_general(qs[e], kj, (((1,), (1,)), ((), ())), preferred_element_type=F32) - crow
                if diagonal:
                    s = jnp.where(col <= row, s, NEG_INF)
                m_new = jnp.maximum(m, jnp.max(s, axis=1, keepdims=True))
                p = jnp.exp(s - m_new)
                alpha = jnp.exp(m - m_new)
                l = alpha * l + jnp.sum(p, axis=1, keepdims=True)
                acc = alpha * acc + jnp.dot(p.astype(BF16), vj, preferred_element_type=F32)
                out.append((m_new, l, acc))
            return tuple(out)

        init = (jnp.full((tb, 1), NEG_INF, F32), jnp.zeros((tb, 1), F32), jnp.zeros((tb, LANES), F32))
        kept = jnp.maximum(_kept_before(prune_ref, h0, i, nq), _kept_before(prune_ref, h0 + 1, i, nq))
        carry = lax.fori_loop(i - kept, i, functools.partial(step, diagonal=False), (init, init))
        carry = step(i, carry, True)
        outs = []
        for e in range(2):
            m, l, acc = carry[e]
            outs.append(acc / l)
            lse = m + jnp.log(l)
            lse_ref[0, e] = jnp.broadcast_to(lse, (tb, LANES))
            lser_ref[0, e] = _as_row(lse, tb)
        o_ref[...] = jnp.where(masks[0], outs[0], outs[1]).astype(o_ref.dtype)
        if gather:
            pl.when((pl.program_id(0) == heads // 2 - 1) & (i == nq - 1))(finish)

    hb_spec = pl.BlockSpec((1, 2, tb, LANES), lambda h, i: (h, 0, i, 0))
    row_spec = pl.BlockSpec((1, 2, 1, t_dim), lambda h, i: (h, 0, 0, 0))
    row_blk = pl.BlockSpec((1, 2, 1, tb), lambda h, i: (h, 0, 0, i))
    hbm = pl.BlockSpec(memory_space=pl.ANY)
    outs = pl.pallas_call(
        body, name=name, grid=(heads // 2, nq),
        in_specs=[pl.BlockSpec(memory_space=pltpu.SMEM),
                  pl.BlockSpec((tb, LANES), lambda h, i: (i, h)),
                  pl.BlockSpec((t_dim, LANES), lambda h, i: (0, cb + h)),
                  pl.BlockSpec((t_dim, LANES), lambda h, i: (0, 2 * cb + h)),
                  row_spec] + [hbm] * n_w,
        out_specs=[pl.BlockSpec((tb, LANES), lambda h, i: (i, h)), hb_spec, row_blk] + [hbm] * n_w,
        out_shape=[jax.ShapeDtypeStruct((t_dim, d), BF16),
                   jax.ShapeDtypeStruct((heads // 2, 2, t_dim, LANES), F32),
                   jax.ShapeDtypeStruct((heads // 2, 2, 1, t_dim), F32)] + gather_shapes,
        scratch_shapes=_scatter_sems(n_w) if gather else [],
        compiler_params=_params(dimension_semantics=("arbitrary", "arbitrary")),
    )(prune, qkv, qkv, qkv, c_rows, *[arr for arr, _ in gather])
    return outs[0], outs[1], outs[2], outs[3:]


def _flash_dq(qkv, o, do, c_rows, lse_hb, prune, *, tb, name):
    t_dim = qkv.shape[0]
    d = qkv.shape[1] // 3
    heads = d // HEAD_DIM
    cb = d // LANES
    scale = HEAD_DIM ** -0.5
    nq = t_dim // tb

    def body(prune_ref, q_ref, k_ref, v_ref, o_ref, do_ref, cr_ref, lse_ref, dq_ref, dl_ref, rs_ref):
        i = pl.program_id(1)
        h0 = 2 * pl.program_id(0)
        q = q_ref[...] * jnp.asarray(scale, BF16)
        do_blk = do_ref[...]
        prod = do_blk.astype(F32) * o_ref[...].astype(F32)
        masks = _head_masks(tb)
        row = lax.broadcasted_iota(jnp.int32, (tb, tb), 0)
        col = lax.broadcasted_iota(jnp.int32, (tb, tb), 1)
        qs = [jnp.where(masks[e], q, jnp.zeros_like(q)) for e in range(2)]
        dos = [jnp.where(masks[e], do_blk, jnp.zeros_like(do_blk)) for e in range(2)]
        deltas = [jnp.sum(jnp.where(masks[e], prod, 0.0), axis=1, keepdims=True) for e in range(2)]
        lses = [lse_ref[0, e][:, 0:1] for e in range(2)]

        def step(j, carry, diagonal):
            off = pl.multiple_of(j * tb, tb)
            kj = k_ref[pl.ds(off, tb), :]
            vj = v_ref[pl.ds(off, tb), :]
            out = []
            for e in range(2):
                acc, rsum = carry[e]
                crow = cr_ref[0, e, :, pl.ds(off, tb)]
                s = lax.dot_general(qs[e], kj, (((1,), (1,)), ((), ())), preferred_element_type=F32) - crow
                if diagonal:
                    s = jnp.where(col <= row, s, NEG_INF)
                p = jnp.exp(s - lses[e])
                dp = lax.dot_general(dos[e], vj, (((1,), (1,)), ((), ())), preferred_element_type=F32)
                ds = p * (dp - deltas[e])
                out.append((acc + jnp.dot(ds.astype(BF16), kj, preferred_element_type=F32),
                            rsum + jnp.sum(ds, axis=1, keepdims=True)))
            return tuple(out)

        init = (jnp.zeros((tb, LANES), F32), jnp.zeros((tb, 1), F32))
        kept = jnp.maximum(_kept_before(prune_ref, h0, i, nq), _kept_before(prune_ref, h0 + 1, i, nq))
        carry = lax.fori_loop(i - kept, i, functools.partial(step, diagonal=False), (init, init))
        carry = step(i, carry, True)
        for e in range(2):
            dl_ref[0, e] = _as_row(deltas[e], tb)
            rs_ref[0, e] = _as_row(carry[e][1], tb)
        dq_ref[...] = (jnp.where(masks[0], carry[0][0], carry[1][0]) * scale).astype(dq_ref.dtype)

    blk = pl.BlockSpec((tb, LANES), lambda h, i: (i, h))
    hb_spec = pl.BlockSpec((1, 2, tb, LANES), lambda h, i: (h, 0, i, 0))
    row_spec = pl.BlockSpec((1, 2, 1, t_dim), lambda h, i: (h, 0, 0, 0))
    row_blk = pl.BlockSpec((1, 2, 1, tb), lambda h, i: (h, 0, 0, i))
    row_shape = jax.ShapeDtypeStruct((heads // 2, 2, 1, t_dim), F32)
    return pl.pallas_call(
        body, name=name, grid=(heads // 2, nq),
        in_specs=[pl.BlockSpec(memory_space=pltpu.SMEM), blk,
                  pl.BlockSpec((t_dim, LANES), lambda h, i: (0, cb + h)),
                  pl.BlockSpec((t_dim, LANES), lambda h, i: (0, 2 * cb + h)),
                  blk, blk, row_spec, hb_spec],
        out_specs=[blk, row_blk, row_blk],
        out_shape=[jax.ShapeDtypeStruct((t_dim, d), BF16), row_shape, row_shape],
        compiler_params=_params(dimension_semantics=("parallel", "arbitrary")),
    )(prune, qkv, qkv, qkv, o, do, c_rows, lse_hb)


def _flash_dkv(qkv, do, c_hb, lse_rows, delta_rows, prune, *, tb, name, scatter=()):
    t_dim = qkv.shape[0]
    d = qkv.shape[1] // 3
    heads = d // HEAD_DIM
    cb = d // LANES
    scale = HEAD_DIM ** -0.5
    nq = t_dim // tb
    n_w = len(scatter)
    widths, scatter_shapes = _scatter_shapes(scatter)

    def body(prune_ref, q_ref, k_ref, v_ref, do_ref, cc_ref, lr_ref, dr_ref, *rest):
        src, (dk_ref, dv_ref, dsum_ref), dst = rest[:n_w], rest[n_w:n_w + 3], rest[n_w + 3:2 * n_w + 3]
        j = pl.program_id(1)
        h0 = 2 * pl.program_id(0)
        if scatter:
            travel = lambda: _scatter_copies(scatter, widths, src, dst, *rest[2 * n_w + 3:])

            @pl.when((pl.program_id(0) == 0) & (j == 0))
            def _():
                for cp in travel():
                    cp.start()
        k_blk = k_ref[...] * jnp.asarray(scale, BF16)
        v_blk = v_ref[...]
        masks = _head_masks(tb)
        row = lax.broadcasted_iota(jnp.int32, (tb, tb), 0)
        col = lax.broadcasted_iota(jnp.int32, (tb, tb), 1)
        ks = [jnp.where(masks[e], k_blk, jnp.zeros_like(k_blk)) for e in range(2)]
        vs = [jnp.where(masks[e], v_blk, jnp.zeros_like(v_blk)) for e in range(2)]
        ccols = [cc_ref[0, e][:, 0:1] for e in range(2)]

        def step(i, carry, diagonal):
            off = pl.multiple_of(i * tb, tb)
            qi = q_ref[pl.ds(off, tb), :]
            doi = do_ref[pl.ds(off, tb), :]
            out = []
            for e in range(2):
                dk, dv, dsum = carry[e]
                lse = lr_ref[0, e, :, pl.ds(off, tb)]
                delta = dr_ref[0, e, :, pl.ds(off, tb)]
                st = lax.dot_general(ks[e], qi, (((1,), (1,)), ((), ())), preferred_element_type=F32) - ccols[e]
                if diagonal:
                    st = jnp.where(col >= row, st, NEG_INF)
                pt = jnp.exp(st - lse)
                dpt = lax.dot_general(vs[e], doi, (((1,), (1,)), ((), ())), preferred_element_type=F32)
                dst = pt * (dpt - delta)
                out.append((dk + jnp.dot(dst.astype(BF16), qi, preferred_element_type=F32),
                            dv + jnp.dot(pt.astype(BF16), doi, preferred_element_type=F32),
                            dsum + jnp.sum(dst, axis=1, keepdims=True)))
            return tuple(out)

        zero = jnp.zeros((tb, LANES), F32)
        init = (zero, zero, jnp.zeros((tb, 1), F32))
        carry = step(j, (init, init), True)
        kept = jnp.maximum(_kept_after(prune_ref, h0, j, nq), _kept_after(prune_ref, h0 + 1, j, nq))
        carry = lax.fori_loop(j + 1, j + 1 + kept, functools.partial(step, diagonal=False), carry)
        for e in range(2):
            dsum_ref[0, e] = _as_row(carry[e][2], tb)
        dk_ref[...] = (jnp.where(masks[0], carry[0][0], carry[1][0]) * scale).astype(dk_ref.dtype)
        dv_ref[...] = jnp.where(masks[0], carry[0][1], carry[1][1]).astype(dv_ref.dtype)
        if scatter:
            @pl.when((pl.program_id(0) == heads // 2 - 1) & (j == nq - 1))
            def _():
                for cp in travel():
                    cp.wait()

    blk = pl.BlockSpec((tb, LANES), lambda h, j: (j, h))
    hb_spec = pl.BlockSpec((1, 2, tb, LANES), lambda h, j: (h, 0, j, 0))
    row_spec = pl.BlockSpec((1, 2, 1, t_dim), lambda h, j: (h, 0, 0, 0))
    row_blk = pl.BlockSpec((1, 2, 1, tb), lambda h, j: (h, 0, 0, j))
    hbm = pl.BlockSpec(memory_space=pl.ANY)
    outs = pl.pallas_call(
        body, name=name, grid=(heads // 2, nq),
        in_specs=[pl.BlockSpec(memory_space=pltpu.SMEM),
                  pl.BlockSpec((t_dim, LANES), lambda h, j: (0, h)),
                  pl.BlockSpec((tb, LANES), lambda h, j: (j, cb + h)),
                  pl.BlockSpec((tb, LANES), lambda h, j: (j, 2 * cb + h)),
                  pl.BlockSpec((t_dim, LANES), lambda h, j: (0, h)),
                  hb_spec, row_spec, row_spec] + [hbm] * n_w,
        out_specs=[blk, blk, row_blk] + [hbm] * n_w,
        out_shape=[jax.ShapeDtypeStruct((t_dim, d), BF16), jax.ShapeDtypeStruct((t_dim, d), BF16),
                   jax.ShapeDtypeStruct((heads // 2, 2, 1, t_dim), F32)] + scatter_shapes,
        scratch_shapes=_scatter_sems(n_w) if scatter else [],
        compiler_params=_params(dimension_semantics=("arbitrary", "arbitrary")),
    )(prune, qkv, qkv, qkv, do, c_hb, lse_rows, delta_rows, *[arr for arr, _ in scatter])
    return outs[0], outs[1], outs[2], outs[3:]


def _shift_down(z, prev, n, tt):
    out = pltpu.roll(z, n, axis=0)
    row = lax.broadcasted_iota(jnp.int32, z.shape, 0)
    for r in range(n):
        out = jnp.where(row == r, prev[8 - n + r:8 - n + r + 1, :], out)
    return out


def _shift_up(z, nxt, n, tt):
    out = pltpu.roll(z, tt - n, axis=0)
    row = lax.broadcasted_iota(jnp.int32, z.shape, 0)
    for r in range(n):
        out = jnp.where(row == tt - n + r, nxt[r:r + 1, :], out)
    return out


def _conv_fwd(proj, conv_w, *, name, tt=256):
    t_dim, d3 = proj.shape
    d = d3 // 3
    tt = min(tt, t_dim)

    def body(p_ref, prev_ref, w_ref, y_ref):
        i = pl.program_id(0)
        p = p_ref[...]
        pp = prev_ref[...]
        z = p[:, d:2 * d] * p[:, 2 * d:]
        zp = jnp.where(i > 0, pp[:, d:2 * d] * pp[:, 2 * d:], 0.0)
        w = w_ref[...]
        zc = w[2:3, :] * z + w[1:2, :] * _shift_down(z, zp, 1, tt) + w[0:1, :] * _shift_down(z, zp, 2, tt)
        y_ref[...] = (p[:, :d] * zc).astype(y_ref.dtype)

    return pl.pallas_call(
        body, name=name, grid=(t_dim // tt,),
        in_specs=[pl.BlockSpec((tt, d3), lambda i: (i, 0)),
                  pl.BlockSpec((8, d3), lambda i: (jnp.maximum(i * (tt // 8) - 1, 0), 0)),
                  pl.BlockSpec(conv_w.shape, lambda i: (0, 0))],
        out_specs=pl.BlockSpec((tt, d), lambda i: (i, 0)),
        out_shape=jax.ShapeDtypeStruct((t_dim, d), BF16),
        compiler_params=_params(dimension_semantics=("arbitrary",)),
    )(proj, proj, conv_w)


def _conv_bwd(proj, dy, conv_w, *, name, tt=256):
    t_dim, d3 = proj.shape
    d = d3 // 3
    tt = min(tt, t_dim)
    n = t_dim // tt

    def body(p_ref, prev_ref, next_ref, dy_ref, dyn_ref, w_ref, dp_ref, dw_ref):
        i = pl.program_id(0)
        p = p_ref[...]
        pp = prev_ref[...]
        pn = next_ref[...]
        bg, cg, u = p[:, :d], p[:, d:2 * d], p[:, 2 * d:]
        z = cg * u
        zp = jnp.where(i > 0, pp[:, d:2 * d] * pp[:, 2 * d:], 0.0)
        w = w_ref[...]
        z1 = _shift_down(z, zp, 1, tt)
        z2 = _shift_down(z, zp, 2, tt)
        zc = w[2:3, :] * z + w[1:2, :] * z1 + w[0:1, :] * z2
        dy_blk = dy_ref[...]
        dzc = dy_blk * bg
        dzn = jnp.where(i < n - 1, dyn_ref[...] * pn[:, :d], 0.0)
        dz = w[2:3, :] * dzc + w[1:2, :] * _shift_up(dzc, dzn, 1, tt) + w[0:1, :] * _shift_up(dzc, dzn, 2, tt)
        dp_ref[:, :d] = (dy_blk * zc).astype(dp_ref.dtype)
        dp_ref[:, d:2 * d] = (dz * u).astype(dp_ref.dtype)
        dp_ref[:, 2 * d:] = (dz * cg).astype(dp_ref.dtype)
        part = jnp.concatenate([jnp.sum(dzc * z2, axis=0, keepdims=True),
                                jnp.sum(dzc * z1, axis=0, keepdims=True),
                                jnp.sum(dzc * z, axis=0, keepdims=True),
                                jnp.zeros((5, d), F32)], axis=0)

        @pl.when(i == 0)
        def _():
            dw_ref[...] = part

        @pl.when(i > 0)
        def _():
            dw_ref[...] += part

    last8 = t_dim // 8 - 1
    return pl.pallas_call(
        body, name=name, grid=(n,),
        in_specs=[pl.BlockSpec((tt, d3), lambda i: (i, 0)),
                  pl.BlockSpec((8, d3), lambda i: (jnp.maximum(i * (tt // 8) - 1, 0), 0)),
                  pl.BlockSpec((8, d3), lambda i: (jnp.minimum((i + 1) * (tt // 8), last8), 0)),
                  pl.BlockSpec((tt, d), lambda i: (i, 0)),
                  pl.BlockSpec((8, d), lambda i: (jnp.minimum((i + 1) * (tt // 8), last8), 0)),
                  pl.BlockSpec(conv_w.shape, lambda i: (0, 0))],
        out_specs=[pl.BlockSpec((tt, d3), lambda i: (i, 0)), pl.BlockSpec((8, d), lambda i: (0, 0))],
        out_shape=[jax.ShapeDtypeStruct((t_dim, d3), BF16), jax.ShapeDtypeStruct((8, d), F32)],
        compiler_params=_params(dimension_semantics=("arbitrary",)),
    )(proj, proj, proj, dy, dy, conv_w)


def _window(ref, axis, n, idx):
    if axis is None:
        return ref
    sel = [slice(None)] * len(ref.shape)
    sel[axis] = pl.ds(pl.multiple_of(idx * n, n), n)
    return ref.at[tuple(sel)]


def _scatter_shapes(items):
    widths, shapes = [], []
    for arr, axis in items:
        shp = list(arr.shape)
        if axis is not None:
            shp[axis] //= N_DEV
        widths.append(None if axis is None else shp[axis])
        shapes.append(jax.ShapeDtypeStruct((N_DEV, *shp), arr.dtype))
    return widths, shapes


def _scatter_copies(items, widths, src, dst, send_sems, recv_sems, local_sems):
    x, y, c = lax.axis_index("x"), lax.axis_index("y"), lax.axis_index("c")
    me = 4 * x + 2 * y + c
    copies = [pltpu.make_async_copy(_window(src[w], items[w][1], widths[w], me), dst[w].at[me], local_sems.at[w])
              for w in range(len(items))]
    for k in range(1, N_DEV):
        px = 1 - x if k & 4 else x
        py = 1 - y if k & 2 else y
        pc = 1 - c if k & 1 else c
        for w in range(len(items)):
            copies.append(pltpu.make_async_remote_copy(
                src_ref=_window(src[w], items[w][1], widths[w], 4 * px + 2 * py + pc), dst_ref=dst[w].at[me],
                send_sem=send_sems.at[w, k - 1], recv_sem=recv_sems.at[w, k - 1],
                device_id=(px, py, pc), device_id_type=pl.DeviceIdType.MESH))
    return copies


def _scatter_sems(n_w):
    return [pltpu.SemaphoreType.DMA((n_w, N_DEV - 1)), pltpu.SemaphoreType.DMA((n_w, N_DEV - 1)),
            pltpu.SemaphoreType.DMA((n_w,))]


def _gather_shapes(items):
    widths = [arr.shape[axis] for arr, axis in items]
    shapes = [jax.ShapeDtypeStruct(tuple(s * N_DEV if a == axis else s for a, s in enumerate(arr.shape)), arr.dtype)
              for arr, axis in items]
    return widths, shapes


def _gather_phases(items, widths, src, dst, send_sems, recv_sems, local_sems):
    n_w = len(items)
    x, y, c = lax.axis_index("x"), lax.axis_index("y"), lax.axis_index("c")
    chips = [(1 - x, y), (x, 1 - y), (1 - x, 1 - y)]

    def place(w, origin):
        return _window(dst[w], items[w][1], widths[w], 4 * origin[0] + 2 * origin[1] + origin[2])

    def block_copy(w, n, origin, to, from_shard):
        return pltpu.make_async_remote_copy(
            src_ref=src[w] if from_shard else place(w, origin), dst_ref=place(w, origin),
            send_sem=send_sems.at[w, n], recv_sem=recv_sems.at[w, n],
            device_id=to, device_id_type=pl.DeviceIdType.MESH)

    def own(w):
        return pltpu.make_async_copy(src[w], place(w, (x, y, c)), local_sems.at[w])

    def first(w):
        return ([block_copy(w, 0, (x, y, c), (x, y, 1 - c), True)]
                + [block_copy(w, 1 + n, (x, y, c), (*chip, c), True) for n, chip in enumerate(chips)])

    def passed(w, n):
        return block_copy(w, 4 + n, (*chips[n], c), (x, y, 1 - c), False)

    def start():
        for w in range(n_w):
            own(w).start()
            for cp in first(w):
                cp.start()

    def relay():
        for n, chip in enumerate(chips):
            for w in range(n_w):
                block_copy(w, 1 + n, (*chip, c), (x, y, c), True).wait_recv()
                passed(w, n).start()

    def finish():
        for w in range(n_w):
            block_copy(w, 0, (x, y, 1 - c), (x, y, c), True).wait_recv()
            for n, chip in enumerate(chips):
                block_copy(w, 4 + n, (*chip, 1 - c), (x, y, c), False).wait_recv()
            for cp in first(w) + [passed(w, n) for n in range(3)]:
                cp.wait_send()
            own(w).wait()

    return start, relay, finish


def _exchange(items, *, gather, name):
    n_w = len(items)
    widths, out_shape = _gather_shapes(items) if gather else _scatter_shapes(items)

    def body(*refs):
        src, dst = refs[:n_w], refs[n_w:2 * n_w]
        send_sems, recv_sems, local_sems = refs[2 * n_w:]
        if not gather:
            copies = _scatter_copies(items, widths, src, dst, send_sems, recv_sems, local_sems)
            for cp in copies:
                cp.start()
            for cp in copies:
                cp.wait()
            return
        for phase in _gather_phases(items, widths, src, dst, send_sems, recv_sems, local_sems):
            phase()

    return pl.pallas_call(
        body, name=name,
        in_specs=[pl.BlockSpec(memory_space=pl.ANY)] * n_w,
        out_specs=[pl.BlockSpec(memory_space=pl.ANY)] * n_w,
        out_shape=out_shape,
        scratch_shapes=_scatter_sems(n_w),
    )(*[arr for arr, _ in items])


def _row_tile(rows, cols):
    tr = rows
    while tr % 16 == 0 and tr * cols > 256 * 1024:
        tr //= 2
    return tr


def _sum_parts(parts, *, name):
    n_parts, rows, cols = parts.shape
    tr = _row_tile(rows, cols)

    def body(p_ref, o_ref):
        g = p_ref[0].astype(F32)
        for s in range(1, n_parts):
            g = g + p_ref[s].astype(F32)
        o_ref[...] = g

    return pl.pallas_call(
        body, name=name, grid=(rows // tr,),
        in_specs=[pl.BlockSpec((n_parts, tr, cols), lambda i: (0, i, 0))],
        out_specs=pl.BlockSpec((tr, cols), lambda i: (i, 0)),
        out_shape=jax.ShapeDtypeStruct((rows, cols), F32),
        compiler_params=_params(dimension_semantics=("parallel",)),
    )(parts)


def _adamw(parts, w, m, v, *, name):
    n_parts, rows, cols = parts.shape
    tr = _row_tile(rows, cols)

    def body(p_ref, w_ref, m_ref, v_ref, g_ref, d_ref, nm_ref, nv_ref):
        g = p_ref[0].astype(F32)
        for s in range(1, n_parts):
            g = g + p_ref[s].astype(F32)
        m_new = ADAM_B1 * m_ref[...] + (1.0 - ADAM_B1) * g
        v_new = ADAM_B2 * v_ref[...] + (1.0 - ADAM_B2) * (g * g)
        m_hat = m_new / (1.0 - ADAM_B1 ** ADAM_STEP)
        v_hat = v_new / (1.0 - ADAM_B2 ** ADAM_STEP)
        g_ref[...] = g
        d_ref[...] = -ADAM_LR * (m_hat / (jnp.sqrt(v_hat) + ADAM_EPS) + ADAM_WD * w_ref[...])
        nm_ref[...] = m_new
        nv_ref[...] = v_new

    spec = pl.BlockSpec((tr, cols), lambda i: (i, 0))
    return pl.pallas_call(
        body, name=name, grid=(rows // tr,),
        in_specs=[pl.BlockSpec((n_parts, tr, cols), lambda i: (0, i, 0)), spec, spec, spec],
        out_specs=[spec] * 4,
        out_shape=[jax.ShapeDtypeStruct((rows, cols), F32)] * 4,
        compiler_params=_params(dimension_semantics=("parallel",)),
    )(parts, w, m, v)


def _pad_rows(a, axis, to):
    pad = [(0, 0)] * a.ndim
    pad[axis] = (0, to - a.shape[axis])
    return jnp.pad(a, pad)


def kernel(x, p, norm_g, w_attn_in, b_forget, w_attn_out, w_conv_in, conv_w, w_conv_out, w_mlp_up, w_mlp_down, w_ple_proj, w_ple_gate, loss_target, m_norm_g, m_w_attn_in, m_b_forget, m_w_attn_out, m_w_conv_in, m_conv_w, m_w_conv_out, m_w_mlp_up, m_w_mlp_down, m_w_ple_proj, m_w_ple_gate, v_norm_g, v_w_attn_in, v_b_forget, v_w_attn_out, v_w_conv_in, v_conv_w, v_w_conv_out, v_w_mlp_up, v_w_mlp_down, v_w_ple_proj, v_w_ple_gate):
    shards = dict(norm_g=norm_g, w_attn_in=w_attn_in, b_forget=b_forget, w_attn_out=w_attn_out,
                  w_conv_in=w_conv_in, conv_w=conv_w, w_conv_out=w_conv_out, w_mlp_up=w_mlp_up,
                  w_mlp_down=w_mlp_down, w_ple_proj=w_ple_proj, w_ple_gate=w_ple_gate)
    m_shards = dict(norm_g=m_norm_g, w_attn_in=m_w_attn_in, b_forget=m_b_forget, w_attn_out=m_w_attn_out,
                    w_conv_in=m_w_conv_in, conv_w=m_conv_w, w_conv_out=m_w_conv_out, w_mlp_up=m_w_mlp_up,
                    w_mlp_down=m_w_mlp_down, w_ple_proj=m_w_ple_proj, w_ple_gate=m_w_ple_gate)
    v_shards = dict(norm_g=v_norm_g, w_attn_in=v_w_attn_in, b_forget=v_b_forget, w_attn_out=v_w_attn_out,
                    w_conv_in=v_w_conv_in, conv_w=v_conv_w, w_conv_out=v_w_conv_out, w_mlp_up=v_w_mlp_up,
                    w_mlp_down=v_w_mlp_down, w_ple_proj=v_w_ple_proj, w_ple_gate=v_w_ple_gate)
    t_dim, d = x.shape[-2:]
    depth = p.shape[0]
    n_attn, heads = b_forget.shape
    assert d == heads * HEAD_DIM and x.shape[0] == 1
    tb = min(512, t_dim // 2)
    x0 = x.reshape(t_dim, d)
    target = loss_target.reshape(t_dim, d)

    in_cols = w_attn_in.shape[2]
    in_cols_pad = -(-in_cols // 16) * 16
    first_names = ['norm_g', 'w_attn_in', 'conv_w']
    rest_names = [n for n in WEIGHT_NAMES if n not in first_names + ['b_forget']]

    def gather_item(n):
        if n == 'w_attn_in':
            return _pad_rows(jnp.swapaxes(w_attn_in, 1, 2), 1, in_cols_pad).astype(BF16), 1
        return (shards[n] if n in ('norm_g', 'conv_w') else shards[n].astype(BF16)), SHARD_AXIS[n]

    full = dict(zip(first_names, _exchange([gather_item(n) for n in first_names], gather=True, name="gather_first")))
    gains = full['norm_g']
    taps = full['conv_w']
    w_in_t = full['w_attn_in'].reshape(n_attn, N_DEV, in_cols_pad, d)[:, :, :in_cols]
    w_in_t = _pad_rows(w_in_t.reshape(n_attn, N_DEV * in_cols, d), 1, 3 * d + LANES)
    bias_pad = jnp.pad(b_forget, ((0, 0), (0, LANES - heads)))

    def gain(i, k):
        return gains[i, k].reshape(1, d)

    def add_norm(x_prev, branch, g_branch, g_next, name):
        def fn(rows, vecs):
            x_new = rows[0] + _norm(rows[1], vecs[0])
            return [x_new, _norm(x_new, vecs[1])], []
        return _rows_call(fn, [x_prev, branch], [g_branch, g_next], [(d, F32), (d, BF16)], [], name=name)

    saved = []
    x_cur = x0
    hn = _rows_call(lambda rows, vecs: ([_norm(rows[0], vecs[0])], []), [x0], [gain(0, 0)], [(d, BF16)], [],
                    name="norm_in")[0]
    loss_rows = dy = None
    for i in range(depth):
        j = i // 2
        s = dict(x0=x_cur, hn=hn)
        if i % 2 == 0:
            s['qkv'] = _mm(hn, w_in_t[j, :3 * d], tb=True, out_dtypes=(BF16,), name=f"attn_in_{i}")
            s['fl'] = _mm(hn, w_in_t[j, 3 * d:], tb=True, name=f"attn_gate_{i}")
            c = _cumsum_fwd(s['fl'], bias_pad[j:j + 1], name=f"gate_cumsum_{i}")
            c_t = c[:, :heads].T
            s['prune'] = _prune_table(c_t, _head_norms(s['qkv'], name=f"head_norms_{i}"), tb)
            c_t = c_t.reshape(heads // 2, 2, t_dim)
            s['c_hb'] = jnp.broadcast_to(c_t[:, :, :, None], (heads // 2, 2, t_dim, LANES))
            s['c_rows'] = c_t.reshape(heads // 2, 2, 1, t_dim)
            s['o'], s['lse_hb'], s['lse_rows'], rest = _flash_fwd(
                s['qkv'], s['c_rows'], s['prune'], tb=tb, name=f"attn_fwd_{i}",
                gather=[gather_item(n) for n in rest_names] if i == 0 else ())
            if i == 0:
                full.update(zip(rest_names, rest))
            s['m'] = _mm(s['o'], full['w_attn_out'][j], name=f"attn_out_{i}")
        else:
            s['proj'] = _mm(hn, full['w_conv_in'][j], name=f"conv_in_{i}")
            s['y'] = _conv_fwd(s['proj'], taps[j], name=f"conv_fwd_{i}")
            s['m'] = _mm(s['y'], full['w_conv_out'][j], name=f"conv_out_{i}")
        s['x1'], s['h2'] = add_norm(x_cur, s['m'], gain(i, 1), gain(i, 2), f"mix_norm_{i}")
        s['u'], s['a'] = _mm(s['h2'], full['w_mlp_up'][i], out_dtypes=(BF16, BF16), name=f"mlp_up_{i}",
                             epi=lambda acc: (acc, jnp.square(jnp.maximum(acc, 0.0))))
        s['f'] = _mm(s['a'], full['w_mlp_down'][i], name=f"mlp_down_{i}")
        s['x2'], s['h4'] = add_norm(s['x1'], s['f'], gain(i, 3), gain(i, 4), f"mlp_norm_{i}")
        s['pi'] = p[i].reshape(t_dim, -1)
        s['pp'] = _mm(s['pi'], full['w_ple_proj'][i], name=f"ple_proj_{i}")
        s['gl'], s['e'] = _mm(s['h4'], full['w_ple_gate'][i], extras=(s['pp'],), out_dtypes=(F32, F32),
                              name=f"ple_gate_{i}", epi=lambda acc, pp: (acc, pp * _sigmoid(acc)))
        if i + 1 < depth:
            x_cur, hn = add_norm(s['x2'], s['e'], gain(i, 5), gain(i + 1, 0), f"ple_norm_{i}")
        else:
            def loss_fn(rows, vecs):
                err = rows[0] + _norm(rows[1], vecs[0]) - rows[2]
                part = 0.5 * jnp.sum(jnp.sum(err * err, axis=1, keepdims=True), axis=0, keepdims=True) / d
                return [err / d], [jnp.broadcast_to(part, (1, LANES))]
            dy, loss_rows = _rows_call(loss_fn, [s['x2'], s['e'], target], [gain(i, 5)], [(d, F32)],
                                       [(1, LANES)], name="loss")
        saved.append(s)
    loss = lax.psum(loss_rows[0, 0], ("x", "y", "c"))

    grads = {n: [None] * shards[n].shape[0] for n in WEIGHT_NAMES}
    d_gains = [[None] * 6 for _ in range(depth)]
    wgrad = functools.partial(_mm, ta=True, out_dtypes=(BF16,))
    axis_of = dict(SHARD_AXIS, w_attn_in=1)

    def in_t_blocks(layers):
        g = jnp.stack(layers)[:, :N_DEV * in_cols].reshape(len(layers), N_DEV, in_cols, d)
        return _pad_rows(g, 2, in_cols_pad).reshape(len(layers), N_DEV * in_cols_pad, d)

    early_names = early_items = early = None
    dx = dy
    for i in reversed(range(depth)):
        j = i // 2
        s = saved[i]

        def ple_fn(rows, vecs):
            de, dg = _norm_bwd(rows[0], vecs[0], rows[1])
            sg = _sigmoid(rows[2])
            return [de * sg, de * rows[3] * sg * (1.0 - sg)], [dg]
        dpp, dgl, d_gains[i][5] = _rows_call(ple_fn, [s['e'], dx, s['gl'], s['pp']], [gain(i, 5)],
                                             [(d, BF16), (d, BF16)], [(1, d)], name=f"ple_bwd_{i}")
        grads['w_ple_proj'][i] = wgrad(s['pi'], dpp, name=f"ple_proj_dw_{i}")
        grads['w_ple_gate'][i] = wgrad(s['h4'], dgl, name=f"ple_gate_dw_{i}")
        dh4 = _mm(dgl, full['w_ple_gate'][i], tb=True, name=f"ple_gate_dx_{i}")

        def two_norm_bwd(x_res, dh, dx_in, branch, g_res, g_branch, name):
            def fn(rows, vecs):
                d_res, dg_res = _norm_bwd(rows[0], vecs[0], rows[1])
                dx_out = rows[2] + d_res
                d_branch, dg_branch = _norm_bwd(rows[3], vecs[1], dx_out)
                return [dx_out, d_branch], [dg_res, dg_branch]
            return _rows_call(fn, [x_res, dh, dx_in, branch], [g_res, g_branch], [(d, F32), (d, BF16)],
                              [(1, d), (1, d)], name=name)

        dx2, df, d_gains[i][4], d_gains[i][3] = two_norm_bwd(s['x2'], dh4, dx, s['f'], gain(i, 4), gain(i, 3),
                                                            f"mlp_norm_bwd_{i}")
        grads['w_mlp_down'][i] = wgrad(s['a'], df, name=f"mlp_down_dw_{i}")
        du = _mm(df, full['w_mlp_down'][i], tb=True, extras=(s['u'],), out_dtypes=(BF16,), name=f"mlp_down_dx_{i}",
                 epi=lambda acc, u: (acc * (2.0 * jnp.maximum(u.astype(F32), 0.0)),))
        grads['w_mlp_up'][i] = wgrad(s['h2'], du, name=f"mlp_up_dw_{i}")
        dh2 = _mm(du, full['w_mlp_up'][i], tb=True, name=f"mlp_up_dx_{i}")
        dx1, dm, d_gains[i][2], d_gains[i][1] = two_norm_bwd(s['x1'], dh2, dx2, s['m'], gain(i, 2), gain(i, 1),
                                                            f"mix_norm_bwd_{i}")
        if i % 2 == 0:
            grads['w_attn_out'][j] = wgrad(s['o'], dm, name=f"attn_out_dw_{i}")
            do = _mm(dm, full['w_attn_out'][j], tb=True, out_dtypes=(BF16,), name=f"attn_out_dx_{i}")
            dq, delta_rows, rsum_rows = _flash_dq(s['qkv'], s['o'], do, s['c_rows'], s['lse_hb'],
                                                  s['prune'], tb=tb, name=f"attn_dq_{i}")
            if i == 0:
                early_names = [n for n in WEIGHT_NAMES if n not in ('norm_g', 'b_forget')]
                early_items = [(in_t_blocks(grads[n][1:]) if n == 'w_attn_in' else jnp.stack(grads[n]), axis_of[n])
                               for n in early_names]
            dk, dv, csum_rows, early = _flash_dkv(s['qkv'], do, s['c_hb'], s['lse_rows'], delta_rows, s['prune'],
                                                  tb=tb, name=f"attn_dkv_{i}", scatter=early_items if i == 0 else ())
            dc = (rsum_rows - csum_rows).reshape(heads, t_dim).T
            dfl, db = _cumsum_bwd(jnp.pad(dc, ((0, 0), (0, LANES - heads))), s['fl'], bias_pad[j:j + 1],
                                  name=f"gate_cumsum_bwd_{i}")
            grads['b_forget'][j] = db[0, :heads]
            dproj = jnp.concatenate([dq, dk, dv, dfl], axis=1)
            grads['w_attn_in'][j] = wgrad(dproj, s['hn'], name=f"attn_in_dw_{i}")
            dhn = _mm(dproj, w_in_t[j], name=f"attn_in_dx_{i}")
        else:
            grads['w_conv_out'][j] = wgrad(s['y'], dm, name=f"conv_out_dw_{i}")
            dyc = _mm(dm, full['w_conv_out'][j], tb=True, name=f"conv_out_dx_{i}")
            dproj, dtaps = _conv_bwd(s['proj'], dyc, taps[j], name=f"conv_bwd_{i}")
            grads['conv_w'][j] = dtaps[:3].astype(BF16)
            grads['w_conv_in'][j] = wgrad(s['hn'], dproj, name=f"conv_in_dw_{i}")
            dhn = _mm(dproj, full['w_conv_in'][j], tb=True, name=f"conv_in_dx_{i}")

        def in_fn(rows, vecs):
            d_res, dg = _norm_bwd(rows[0], vecs[0], rows[1])
            return [rows[2] + d_res], [dg]
        dx, d_gains[i][0] = _rows_call(in_fn, [s['x0'], dhn, dx1], [gain(i, 0)], [(d, F32)], [(1, d)],
                                       name=f"in_norm_bwd_{i}")
    grad_x = dx.reshape(x.shape)

    recv = dict(zip(early_names, early))
    late = _exchange([(in_t_blocks(grads['w_attn_in'][:1]), 1),
                      (jnp.stack([jnp.concatenate(row, axis=0) for row in d_gains]).astype(BF16), SHARD_AXIS['norm_g']),
                      (jnp.zeros((8, LANES), F32).at[:n_attn, :heads].set(jnp.stack(grads['b_forget'])), None)],
                     gather=False, name="exchange_late")
    recv['norm_g'] = late[1]
    recv['b_forget'] = late[2][:, :n_attn, :heads]
    g_in_t = jnp.concatenate([_sum_parts(part.reshape(N_DEV, -1, d), name=f"sum_attn_in_{k}")
                              for k, part in enumerate((late[0], recv['w_attn_in']))], axis=0)
    recv['w_attn_in'] = jnp.swapaxes(g_in_t.reshape(n_attn, in_cols_pad, d)[:, :in_cols], 1, 2)[None]
    results = {}
    for n in WEIGHT_NAMES:
        shp = shards[n].shape
        flat = lambda a: a.reshape(a.shape[:a.ndim - len(shp)] + (-1, shp[-1]))
        outs = _adamw(flat(recv[n]), flat(shards[n]), flat(m_shards[n]), flat(v_shards[n]), name=f"adamw_{n}")
        results[n] = [o.reshape(shp) for o in outs]
    return (loss, grad_x, *[results[n][k] for k in range(4) for n in WEIGHT_NAMES])
```

```python
import functools

import jax
import jax.numpy as jnp
from jax import lax
from jax.experimental import pallas as pl
from jax.experimental.pallas import tpu as pltpu

F32 = jnp.float32
BF16 = jnp.bfloat16

N_DEV = 8
LANES = 128
HEAD_DIM = 64
VMEM_LIMIT_BYTES = 56 * 1024 * 1024
RMS_EPS = 1e-6
NEG_INF = -1e30
ADAM_LR = 0.001
ADAM_B1 = 0.9
ADAM_B2 = 0.999
ADAM_EPS = 1e-08
ADAM_WD = 0.01
ADAM_STEP = 10
WEIGHT_NAMES = ('norm_g', 'w_attn_in', 'b_forget', 'w_attn_out', 'w_conv_in', 'conv_w', 'w_conv_out',
                'w_mlp_up', 'w_mlp_down', 'w_ple_proj', 'w_ple_gate')
SHARD_AXIS = {'norm_g': 2, 'w_attn_in': 2, 'b_forget': None, 'w_attn_out': 1, 'w_conv_in': 2, 'conv_w': 2,
              'w_conv_out': 1, 'w_mlp_up': 2, 'w_mlp_down': 1, 'w_ple_proj': 2, 'w_ple_gate': 1}


def _params(**kw):
    return pltpu.CompilerParams(vmem_limit_bytes=VMEM_LIMIT_BYTES, **kw)


def _tile(n, cap):
    if n <= cap:
        return n
    t = (cap // LANES) * LANES
    while n % t:
        t -= LANES
    return t


MM_VMEM_BUDGET = 36 * 1024 * 1024


def _mm(a, b, *, ta=False, tb=False, extras=(), epi=None, out_dtypes=(F32,), name, a_rows=None, a_off=0):
    rows_a = a_rows or a.shape[0]
    m_dim, k_dim = (a.shape[1], rows_a) if ta else (rows_a, a.shape[1])
    n_dim = b.shape[0] if tb else b.shape[1]
    assert k_dim == (b.shape[1] if tb else b.shape[0]) and a_off % rows_a == 0
    tk = _tile(k_dim, 1024 if k_dim <= 1024 else 2048)
    nk = k_dim // tk
    tn = _tile(n_dim, 1024)

    def vmem_bytes(tm):
        per_mn = sum(jnp.dtype(dt).itemsize for dt in out_dtypes) + sum(e.dtype.itemsize for e in extras)
        return (2 * (tm * tk * a.dtype.itemsize + tk * tn * b.dtype.itemsize) + 2 * tm * tn * per_mn
                + tm * tn * 4 * (2 + (nk > 1)))

    tm = next(t for t in (_tile(m_dim, 1024), _tile(m_dim, 512)) if t <= 512 or vmem_bytes(t) <= MM_VMEM_BUDGET)
    grid = (n_dim // tn, m_dim // tm, nk)
    off_m, off_k = (0, a_off // tk) if ta else (a_off // tm, 0)
    a_spec = (pl.BlockSpec((tk, tm), lambda j, i, k: (k + off_k, i)) if ta
              else pl.BlockSpec((tm, tk), lambda j, i, k: (i + off_m, k)))
    b_spec = (pl.BlockSpec((tn, tk), lambda j, i, k: (j, k)) if tb
              else pl.BlockSpec((tk, tn), lambda j, i, k: (k, j)))
    mn_spec = pl.BlockSpec((tm, tn), lambda j, i, k: (i, j))
    dims = (((0 if ta else 1,), (1 if tb else 0,)), ((), ()))
    n_extra, n_out = len(extras), len(out_dtypes)
    if epi is None:
        epi = lambda acc: (acc,)

    def body(a_ref, b_ref, *rest):
        e_refs, o_refs = rest[:n_extra], rest[n_extra:n_extra + n_out]
        part = lax.dot_general(a_ref[...].astype(BF16), b_ref[...].astype(BF16), dims,
                               preferred_element_type=F32)

        def finish(acc):
            for o_ref, val in zip(o_refs, epi(acc, *[e[...] for e in e_refs])):
                o_ref[...] = val.astype(o_ref.dtype)

        if nk == 1:
            finish(part)
        else:
            acc_ref = rest[-1]
            k = pl.program_id(2)

            @pl.when(k == 0)
            def _():
                acc_ref[...] = part

            @pl.when(k > 0)
            def _():
                acc_ref[...] += part

            @pl.when(k == nk - 1)
            def _():
                finish(acc_ref[...])

    outs = pl.pallas_call(
        body, name=name, grid=grid,
        in_specs=[a_spec, b_spec] + [mn_spec] * n_extra,
        out_specs=[mn_spec] * n_out,
        out_shape=[jax.ShapeDtypeStruct((m_dim, n_dim), dt) for dt in out_dtypes],
        scratch_shapes=[pltpu.VMEM((tm, tn), F32)] if nk > 1 else [],
        compiler_params=_params(dimension_semantics=("parallel", "parallel", "arbitrary")),
    )(a, b, *extras)
    return outs[0] if n_out == 1 else outs


def _rows(fn, row_ins, vec_ins, row_outs, vec_outs, *, name, tt=512, reverse=False):
    t_dim = row_ins[0].shape[0]
    tt = min(tt, t_dim)
    n = t_dim // tt
    n_ri, n_vi, n_ro, n_vo = len(row_ins), len(vec_ins), len(row_outs), len(vec_outs)
    pos = (lambda i: (n - 1 - i, 0)) if reverse else (lambda i: (i, 0))
    fixed = lambda i: (0, 0)

    def body(*refs):
        ri = refs[:n_ri]
        vi = refs[n_ri:n_ri + n_vi]
        ro = refs[n_ri + n_vi:n_ri + n_vi + n_ro]
        vo = refs[n_ri + n_vi + n_ro:n_ri + n_vi + n_ro + n_vo]
        scratch = refs[n_ri + n_vi + n_ro + n_vo:]
        r_out, v_out = fn([r[...] for r in ri], [v[...] for v in vi], *scratch)
        for o_ref, val in zip(ro, r_out):
            o_ref[...] = val.astype(o_ref.dtype)
        i = pl.program_id(0)
        for o_ref, val in zip(vo, v_out):
            @pl.when(i == 0)
            def _(o_ref=o_ref, val=val):
                o_ref[...] = val

            @pl.when(i > 0)
            def _(o_ref=o_ref, val=val):
                o_ref[...] += val

    return body, dict(
        grid=(n,),
        in_specs=[pl.BlockSpec((tt, r.shape[1]), pos) for r in row_ins]
        + [pl.BlockSpec(v.shape, fixed) for v in vec_ins],
        out_specs=[pl.BlockSpec((tt, w), pos) for w, _ in row_outs]
        + [pl.BlockSpec(s, fixed) for s in vec_outs],
        out_shape=[jax.ShapeDtypeStruct((t_dim, w), dt) for w, dt in row_outs]
        + [jax.ShapeDtypeStruct(s, F32) for s in vec_outs],
        name=name,
        compiler_params=_params(dimension_semantics=("arbitrary",)),
    )


def _rows_call(fn, row_ins, vec_ins, row_outs, vec_outs, *, name, tt=512, reverse=False, scratch=()):
    body, kw = _rows(fn, row_ins, vec_ins, row_outs, vec_outs, name=name, tt=tt, reverse=reverse)
    return pl.pallas_call(body, scratch_shapes=list(scratch), **kw)(*row_ins, *vec_ins)


def _rstd(x):
    return lax.rsqrt(jnp.mean(x * x, axis=-1, keepdims=True) + RMS_EPS)


def _norm(x, g):
    return x * _rstd(x) * g


def _norm_bwd(x, g, dy):
    xh = x * _rstd(x)
    gy = dy * g
    dx = _rstd(x) * (gy - xh * jnp.mean(gy * xh, axis=-1, keepdims=True))
    return dx, jnp.sum(dy * xh, axis=0, keepdims=True)


def _sigmoid(x):
    return 1.0 / (1.0 + jnp.exp(-x))


def _log_sigmoid(x):
    return jnp.minimum(x, 0.0) - jnp.log(1.0 + jnp.exp(-jnp.abs(x)))


def _split3(x):
    hi = x.astype(BF16)
    r1 = x - hi.astype(F32)
    mid = r1.astype(BF16)
    lo = (r1 - mid.astype(F32)).astype(BF16)
    return hi, mid, lo


def _cumsum_fwd(fl, bias, *, name):
    w = fl.shape[1]
    tt = min(512, fl.shape[0])

    def fn(rows, vecs, carry_ref):
        i = pl.program_id(0)

        @pl.when(i == 0)
        def _():
            carry_ref[...] = jnp.zeros_like(carry_ref)

        lf = _log_sigmoid(rows[0] + vecs[0])
        r = lax.broadcasted_iota(jnp.int32, (tt, tt), 0)
        c = lax.broadcasted_iota(jnp.int32, (tt, tt), 1)
        tri = (c <= r).astype(BF16)
        acc = carry_ref[0:1, :]
        for part in _split3(lf):
            acc = acc + jnp.dot(tri, part, preferred_element_type=F32)
        carry_ref[0:1, :] = acc[tt - 1:tt, :]
        return [acc], []

    return _rows_call(fn, [fl], [bias], [(w, F32)], [], name=name, tt=tt,
                      scratch=[pltpu.VMEM((8, w), F32)])[0]


def _cumsum_bwd(dc, fl, bias, *, name):
    w = fl.shape[1]
    tt = min(512, fl.shape[0])

    def fn(rows, vecs, carry_ref):
        i = pl.program_id(0)

        @pl.when(i == 0)
        def _():
            carry_ref[...] = jnp.zeros_like(carry_ref)

        r = lax.broadcasted_iota(jnp.int32, (tt, tt), 0)
        c = lax.broadcasted_iota(jnp.int32, (tt, tt), 1)
        tri = (c >= r).astype(BF16)
        acc = carry_ref[0:1, :]
        for part in _split3(rows[0]):
            acc = acc + jnp.dot(tri, part, preferred_element_type=F32)
        carry_ref[0:1, :] = acc[0:1, :]
        dfl = acc * _sigmoid(-(rows[1] + vecs[0]))
        return [dfl], [jnp.sum(dfl, axis=0, keepdims=True)]

    return _rows_call(fn, [dc, fl], [bias], [(w, BF16)], [(1, w)], name=name, tt=tt, reverse=True,
                      scratch=[pltpu.VMEM((8, w), F32)])


def _head_masks(tb):
    lane = lax.broadcasted_iota(jnp.int32, (tb, LANES), 1)
    return [lane < HEAD_DIM, lane >= HEAD_DIM]


PRUNE_MARGIN = 30.0


def _head_norms(qkv, *, name):
    t_dim = qkv.shape[0]
    d = qkv.shape[1] // 3
    heads = d // HEAD_DIM
    tt = min(512, t_dim)

    def body(q_ref, k_ref, o_ref):
        col = lax.broadcasted_iota(jnp.int32, (d, LANES), 0) // HEAD_DIM
        lane = lax.broadcasted_iota(jnp.int32, (d, LANES), 1)
        tile_max = None
        for ref, first in ((q_ref, 0), (k_ref, heads)):
            x = ref[...].astype(F32)
            sums = jnp.dot((x * x).astype(BF16), (col + first == lane).astype(BF16), preferred_element_type=F32)
            part = jnp.max(sums, axis=0, keepdims=True)
            tile_max = part if tile_max is None else jnp.maximum(tile_max, part)
        i = pl.program_id(0)

        @pl.when(i == 0)
        def _():
            o_ref[...] = tile_max

        @pl.when(i > 0)
        def _():
            o_ref[...] = jnp.maximum(o_ref[...], tile_max)

    return pl.pallas_call(
        body, name=name, grid=(t_dim // tt,),
        in_specs=[pl.BlockSpec((tt, d), lambda i: (i, 0)), pl.BlockSpec((tt, d), lambda i: (i, 1))],
        out_specs=pl.BlockSpec((1, LANES), lambda i: (0, 0)),
        out_shape=jax.ShapeDtypeStruct((1, LANES), F32),
        compiler_params=_params(dimension_semantics=("arbitrary",)),
    )(qkv, qkv)


def _prune_table(c_t, norms, tb):
    heads = c_t.shape[0]
    bound = 1.02 * HEAD_DIM ** -0.5 * jnp.sqrt(norms[0, :heads] * norms[0, heads:2 * heads])
    return jnp.concatenate([c_t[:, ::tb], c_t[:, tb - 1::tb], -(PRUNE_MARGIN + 2.0 * bound)[:, None]], axis=1)


def _kept_before(prune_ref, h, i, nq):
    first, thr = prune_ref[h, i], prune_ref[h, 2 * nq]
    return lax.fori_loop(0, i, lambda j, n: n + (first - prune_ref[h, nq + j] >= thr).astype(jnp.int32),
                         jnp.int32(0))


def _kept_after(prune_ref, h, j, nq):
    last, thr = prune_ref[h, nq + j], prune_ref[h, 2 * nq]
    return lax.fori_loop(j + 1, nq, lambda i, n: n + (prune_ref[h, i] - last >= thr).astype(jnp.int32),
                         jnp.int32(0))


def _as_row(col, tb):
    return jnp.transpose(jnp.broadcast_to(col, (tb, LANES)))[0:1, :]


def _flash_fwd(qkv, c_rows, prune, *, tb, name, gather=()):
    t_dim = qkv.shape[0]
    d = qkv.shape[1] // 3
    heads = d // HEAD_DIM
    cb = d // LANES
    nq = t_dim // tb
    n_w = len(gather)
    widths, gather_shapes = _gather_shapes(gather)

    def body(prune_ref, q_ref, k_ref, v_ref, cr_ref, *rest):
        src, (o_ref, lse_ref, lser_ref), dst = rest[:n_w], rest[n_w:n_w + 3], rest[n_w + 3:2 * n_w + 3]
        i = pl.program_id(1)
        h0 = 2 * pl.program_id(0)
        if gather:
            start, relay, finish = _gather_phases(gather, widths, src, dst, *rest[2 * n_w + 3:])
            pl.when((pl.program_id(0) == 0) & (i == 0))(start)
            pl.when((pl.program_id(0) == (3 * heads) // 8) & (i == 0))(relay)
        q = q_ref[...] * jnp.asarray(HEAD_DIM ** -0.5, BF16)
        masks = _head_masks(tb)
        row = lax.broadcasted_iota(jnp.int32, (tb, tb), 0)
        col = lax.broadcasted_iota(jnp.int32, (tb, tb), 1)
        qs = [jnp.where(masks[e], q, jnp.zeros_like(q)) for e in range(2)]

        def step(j, carry, diagonal):
            off = pl.multiple_of(j * tb, tb)
            kj = k_ref[pl.ds(off, tb), :]
            vj = v_ref[pl.ds(off, tb), :]
            out = []
            for e in range(2):
                m, l, acc = carry[e]
                crow = cr_ref[0, e, :, pl.ds(off, tb)]
                s = lax.dot_general(qs[e], kj, (((1,), (1,)), ((), ())), preferred_element_type=F32) - crow
                if diagonal:
                    s = jnp.where(col <= row, s, NEG_INF)
                m_new = jnp.maximum(m, jnp.max(s, axis=1, keepdims=True))
                p = jnp.exp(s - m_new)
                alpha = jnp.exp(m - m_new)
                l = alpha * l + jnp.sum(p, axis=1, keepdims=True)
                acc = alpha * acc + jnp.dot(p.astype(BF16), vj, preferred_element_type=F32)
                out.append((m_new, l, acc))
            return tuple(out)

        init = (jnp.full((tb, 1), NEG_INF, F32), jnp.zeros((tb, 1), F32), jnp.zeros((tb, LANES), F32))
        kept = jnp.maximum(_kept_before(prune_ref, h0, i, nq), _kept_before(prune_ref, h0 + 1, i, nq))
        carry = lax.fori_loop(i - kept, i, functools.partial(step, diagonal=False), (init, init))
        carry = step(i, carry, True)
        outs = []
        for e in range(2):
            m, l, acc = carry[e]
            outs.append(acc / l)
            lse = m + jnp.log(l)
            lse_ref[0, e] = jnp.broadcast_to(lse, (tb, LANES))
            lser_ref[0, e] = _as_row(lse, tb)
        o_ref[...] = jnp.where(masks[0], outs[0], outs[1]).astype(o_ref.dtype)
        if gather:
            pl.when((pl.program_id(0) == heads // 2 - 1) & (i == nq - 1))(finish)

    hb_spec = pl.BlockSpec((1, 2, tb, LANES), lambda h, i: (h, 0, i, 0))
    row_spec = pl.BlockSpec((1, 2, 1, t_dim), lambda h, i: (h, 0, 0, 0))
    row_blk = pl.BlockSpec((1, 2, 1, tb), lambda h, i: (h, 0, 0, i))
    hbm = pl.BlockSpec(memory_space=pl.ANY)
    outs = pl.pallas_call(
        body, name=name, grid=(heads // 2, nq),
        in_specs=[pl.BlockSpec(memory_space=pltpu.SMEM),
                  pl.BlockSpec((tb, LANES), lambda h, i: (i, h)),
                  pl.BlockSpec((t_dim, LANES), lambda h, i: (0, cb + h)),
                  pl.BlockSpec((t_dim, LANES), lambda h, i: (0, 2 * cb + h)),
                  row_spec] + [hbm] * n_w,
        out_specs=[pl.BlockSpec((tb, LANES), lambda h, i: (i, h)), hb_spec, row_blk] + [hbm] * n_w,
        out_shape=[jax.ShapeDtypeStruct((t_dim, d), BF16),
                   jax.ShapeDtypeStruct((heads // 2, 2, t_dim, LANES), F32),
                   jax.ShapeDtypeStruct((heads // 2, 2, 1, t_dim), F32)] + gather_shapes,
        scratch_shapes=_scatter_sems(n_w) if gather else [],
        compiler_params=_params(dimension_semantics=("arbitrary", "arbitrary")),
    )(prune, qkv, qkv, qkv, c_rows, *[arr for arr, _ in gather])
    return outs[0], outs[1], outs[2], outs[3:]


def _flash_dq(qkv, o, do, c_rows, lse_hb, prune, *, tb, name):
    t_dim = qkv.shape[0]
    d = qkv.shape[1] // 3
    heads = d // HEAD_DIM
    cb = d // LANES
    scale = HEAD_DIM ** -0.5
    nq = t_dim // tb

    def body(prune_ref, q_ref, k_ref, v_ref, o_ref, do_ref, cr_ref, lse_ref, dq_ref, dl_ref, rs_ref):
        i = pl.program_id(1)
        h0 = 2 * pl.program_id(0)
        q = q_ref[...] * jnp.asarray(scale, BF16)
        do_blk = do_ref[...]
        prod = do_blk.astype(F32) * o_ref[...].astype(F32)
        masks = _head_masks(tb)
        row = lax.broadcasted_iota(jnp.int32, (tb, tb), 0)
        col = lax.broadcasted_iota(jnp.int32, (tb, tb), 1)
        qs = [jnp.where(masks[e], q, jnp.zeros_like(q)) for e in range(2)]
        dos = [jnp.where(masks[e], do_blk, jnp.zeros_like(do_blk)) for e in range(2)]
        deltas = [jnp.sum(jnp.where(masks[e], prod, 0.0), axis=1, keepdims=True) for e in range(2)]
        lses = [lse_ref[0, e][:, 0:1] for e in range(2)]

        def step(j, carry, diagonal):
            off = pl.multiple_of(j * tb, tb)
            kj = k_ref[pl.ds(off, tb), :]
            vj = v_ref[pl.ds(off, tb), :]
            out = []
            for e in range(2):
                acc, rsum = carry[e]
                crow = cr_ref[0, e, :, pl.ds(off, tb)]
                s = lax.dot_general(qs[e], kj, (((1,), (1,)), ((), ())), preferred_element_type=F32) - crow
                if diagonal:
                    s = jnp.where(col <= row, s, NEG_INF)
                p = jnp.exp(s - lses[e])
                dp = lax.dot_general(dos[e], vj, (((1,), (1,)), ((), ())), preferred_element_type=F32)
                ds = p * (dp - deltas[e])
                out.append((acc + jnp.dot(ds.astype(BF16), kj, preferred_element_type=F32),
                            rsum + jnp.sum(ds, axis=1, keepdims=True)))
            return tuple(out)

        init = (jnp.zeros((tb, LANES), F32), jnp.zeros((tb, 1), F32))
        kept = jnp.maximum(_kept_before(prune_ref, h0, i, nq), _kept_before(prune_ref, h0 + 1, i, nq))
        carry = lax.fori_loop(i - kept, i, functools.partial(step, diagonal=False), (init, init))
        carry = step(i, carry, True)
        for e in range(2):
            dl_ref[0, e] = _as_row(deltas[e], tb)
            rs_ref[0, e] = _as_row(carry[e][1], tb)
        dq_ref[...] = (jnp.where(masks[0], carry[0][0], carry[1][0]) * scale).astype(dq_ref.dtype)

    blk = pl.BlockSpec((tb, LANES), lambda h, i: (i, h))
    hb_spec = pl.BlockSpec((1, 2, tb, LANES), lambda h, i: (h, 0, i, 0))
    row_spec = pl.BlockSpec((1, 2, 1, t_dim), lambda h, i: (h, 0, 0, 0))
    row_blk = pl.BlockSpec((1, 2, 1, tb), lambda h, i: (h, 0, 0, i))
    row_shape = jax.ShapeDtypeStruct((heads // 2, 2, 1, t_dim), F32)
    return pl.pallas_call(
        body, name=name, grid=(heads // 2, nq),
        in_specs=[pl.BlockSpec(memory_space=pltpu.SMEM), blk,
                  pl.BlockSpec((t_dim, LANES), lambda h, i: (0, cb + h)),
                  pl.BlockSpec((t_dim, LANES), lambda h, i: (0, 2 * cb + h)),
                  blk, blk, row_spec, hb_spec],
        out_specs=[blk, row_blk, row_blk],
        out_shape=[jax.ShapeDtypeStruct((t_dim, d), BF16), row_shape, row_shape],
        compiler_params=_params(dimension_semantics=("parallel", "arbitrary")),
    )(prune, qkv, qkv, qkv, o, do, c_rows, lse_hb)


def _flash_dkv(qkv, do, c_rows, lse_rows, delta_rows, prune, *, tb, name, scatter=()):
    t_dim = qkv.shape[0]
    d = qkv.shape[1] // 3
    heads = d // HEAD_DIM
    cb = d // LANES
    scale = HEAD_DIM ** -0.5
    nq = t_dim // tb
    n_w = len(scatter)
    widths, scatter_shapes = _scatter_shapes(scatter)

    def body(prune_ref, q_ref, k_ref, v_ref, do_ref, cc_ref, lr_ref, dr_ref, *rest):
        src, (dk_ref, dv_ref, dsum_ref), dst = rest[:n_w], rest[n_w:n_w + 3], rest[n_w + 3:2 * n_w + 3]
        j = pl.program_id(1)
        h0 = 2 * pl.program_id(0)
        if scatter:
            travel = lambda: _scatter_copies(scatter, widths, src, dst, *rest[2 * n_w + 3:])

            @pl.when((pl.program_id(0) == 0) & (j == 0))
            def _():
                for cp in travel():
                    cp.start()
        k_blk = k_ref[...] * jnp.asarray(scale, BF16)
        v_blk = v_ref[...]
        masks = _head_masks(tb)
        row = lax.broadcasted_iota(jnp.int32, (tb, tb), 0)
        col = lax.broadcasted_iota(jnp.int32, (tb, tb), 1)
        ks = [jnp.where(masks[e], k_blk, jnp.zeros_like(k_blk)) for e in range(2)]
        vs = [jnp.where(masks[e], v_blk, jnp.zeros_like(v_blk)) for e in range(2)]
        ccols = [jnp.transpose(jnp.broadcast_to(cc_ref[0, e], (LANES, tb)))[:, 0:1] for e in range(2)]

        def step(i, carry, diagonal):
            off = pl.multiple_of(i * tb, tb)
            qi = q_ref[pl.ds(off, tb), :]
            doi = do_ref[pl.ds(off, tb), :]
            out = []
            for e in range(2):
                dk, dv, dsum = carry[e]
                lse = lr_ref[0, e, :, pl.ds(off, tb)]
                delta = dr_ref[0, e, :, pl.ds(off, tb)]
                st = lax.dot_general(ks[e], qi, (((1,), (1,)), ((), ())), preferred_element_type=F32) - ccols[e]
                if diagonal:
                    st = jnp.where(col >= row, st, NEG_INF)
                pt = jnp.exp(st - lse)
                dpt = lax.dot_general(vs[e], doi, (((1,), (1,)), ((), ())), preferred_element_type=F32)
                dst = pt * (dpt - delta)
                out.append((dk + jnp.dot(dst.astype(BF16), qi, preferred_element_type=F32),
                            dv + jnp.dot(pt.astype(BF16), doi, preferred_element_type=F32),
                            dsum + jnp.sum(dst, axis=1, keepdims=True)))
            return tuple(out)

        zero = jnp.zeros((tb, LANES), F32)
        init = (zero, zero, jnp.zeros((tb, 1), F32))
        carry = step(j, (init, init), True)
        kept = jnp.maximum(_kept_after(prune_ref, h0, j, nq), _kept_after(prune_ref, h0 + 1, j, nq))
        carry = lax.fori_loop(j + 1, j + 1 + kept, functools.partial(step, diagonal=False), carry)
        for e in range(2):
            dsum_ref[0, e] = _as_row(carry[e][2], tb)
        dk_ref[...] = (jnp.where(masks[0], carry[0][0], carry[1][0]) * scale).astype(dk_ref.dtype)
        dv_ref[...] = jnp.where(masks[0], carry[0][1], carry[1][1]).astype(dv_ref.dtype)
        if scatter:
            @pl.when((pl.program_id(0) == heads // 2 - 1) & (j == nq - 1))
            def _():
                for cp in travel():
                    cp.wait()

    blk = pl.BlockSpec((tb, LANES), lambda h, j: (j, h))
    row_spec = pl.BlockSpec((1, 2, 1, t_dim), lambda h, j: (h, 0, 0, 0))
    row_blk = pl.BlockSpec((1, 2, 1, tb), lambda h, j: (h, 0, 0, j))
    hbm = pl.BlockSpec(memory_space=pl.ANY)
    outs = pl.pallas_call(
        body, name=name, grid=(heads // 2, nq),
        in_specs=[pl.BlockSpec(memory_space=pltpu.SMEM),
                  pl.BlockSpec((t_dim, LANES), lambda h, j: (0, h)),
                  pl.BlockSpec((tb, LANES), lambda h, j: (j, cb + h)),
                  pl.BlockSpec((tb, LANES), lambda h, j: (j, 2 * cb + h)),
                  pl.BlockSpec((t_dim, LANES), lambda h, j: (0, h)),
                  row_blk, row_spec, row_spec] + [hbm] * n_w,
        out_specs=[blk, blk, row_blk] + [hbm] * n_w,
        out_shape=[jax.ShapeDtypeStruct((t_dim, d), BF16), jax.ShapeDtypeStruct((t_dim, d), BF16),
                   jax.ShapeDtypeStruct((heads // 2, 2, 1, t_dim), F32)] + scatter_shapes,
        scratch_shapes=_scatter_sems(n_w) if scatter else [],
        compiler_params=_params(dimension_semantics=("arbitrary", "arbitrary")),
    )(prune, qkv, qkv, qkv, do, c_rows, lse_rows, delta_rows, *[arr for arr, _ in scatter])
    return outs[0], outs[1], outs[2], outs[3:]


def _shift_down(z, prev, n, tt):
    out = pltpu.roll(z, n, axis=0)
    row = lax.broadcasted_iota(jnp.int32, z.shape, 0)
    for r in range(n):
        out = jnp.where(row == r, prev[8 - n + r:8 - n + r + 1, :], out)
    return out


def _shift_up(z, nxt, n, tt):
    out = pltpu.roll(z, tt - n, axis=0)
    row = lax.broadcasted_iota(jnp.int32, z.shape, 0)
    for r in range(n):
        out = jnp.where(row == tt - n + r, nxt[r:r + 1, :], out)
    return out


def _conv_fwd(proj, conv_w, *, name, tt=256):
    t_dim, d3 = proj.shape
    d = d3 // 3
    tt = min(tt, t_dim)

    def body(p_ref, prev_ref, w_ref, y_ref):
        i = pl.program_id(0)
        p = p_ref[...]
        pp = prev_ref[...]
        z = p[:, d:2 * d] * p[:, 2 * d:]
        zp = jnp.where(i > 0, pp[:, d:2 * d] * pp[:, 2 * d:], 0.0)
        w = w_ref[...]
        zc = w[2:3, :] * z + w[1:2, :] * _shift_down(z, zp, 1, tt) + w[0:1, :] * _shift_down(z, zp, 2, tt)
        y_ref[...] = (p[:, :d] * zc).astype(y_ref.dtype)

    return pl.pallas_call(
        body, name=name, grid=(t_dim // tt,),
        in_specs=[pl.BlockSpec((tt, d3), lambda i: (i, 0)),
                  pl.BlockSpec((8, d3), lambda i: (jnp.maximum(i * (tt // 8) - 1, 0), 0)),
                  pl.BlockSpec(conv_w.shape, lambda i: (0, 0))],
        out_specs=pl.BlockSpec((tt, d), lambda i: (i, 0)),
        out_shape=jax.ShapeDtypeStruct((t_dim, d), BF16),
        compiler_params=_params(dimension_semantics=("arbitrary",)),
    )(proj, proj, conv_w)


def _conv_bwd(proj, dy, conv_w, *, name, tt=256):
    t_dim, d3 = proj.shape
    d = d3 // 3
    tt = min(tt, t_dim)
    n = t_dim // tt

    def body(p_ref, prev_ref, next_ref, dy_ref, dyn_ref, w_ref, dp_ref, dw_ref):
        i = pl.program_id(0)
        p = p_ref[...]
        pp = prev_ref[...]
        pn = next_ref[...]
        bg, cg, u = p[:, :d], p[:, d:2 * d], p[:, 2 * d:]
        z = cg * u
        zp = jnp.where(i > 0, pp[:, d:2 * d] * pp[:, 2 * d:], 0.0)
        w = w_ref[...]
        z1 = _shift_down(z, zp, 1, tt)
        z2 = _shift_down(z, zp, 2, tt)
        zc = w[2:3, :] * z + w[1:2, :] * z1 + w[0:1, :] * z2
        dy_blk = dy_ref[...]
        dzc = dy_blk * bg
        dzn = jnp.where(i < n - 1, dyn_ref[...] * pn[:, :d], 0.0)
        dz = w[2:3, :] * dzc + w[1:2, :] * _shift_up(dzc, dzn, 1, tt) + w[0:1, :] * _shift_up(dzc, dzn, 2, tt)
        dp_ref[:, :d] = (dy_blk * zc).astype(dp_ref.dtype)
        dp_ref[:, d:2 * d] = (dz * u).astype(dp_ref.dtype)
        dp_ref[:, 2 * d:] = (dz * cg).astype(dp_ref.dtype)
        part = jnp.concatenate([jnp.sum(dzc * z2, axis=0, keepdims=True),
                                jnp.sum(dzc * z1, axis=0, keepdims=True),
                                jnp.sum(dzc * z, axis=0, keepdims=True),
                                jnp.zeros((5, d), F32)], axis=0)

        @pl.when(i == 0)
        def _():
            dw_ref[...] = part

        @pl.when(i > 0)
        def _():
            dw_ref[...] += part

    last8 = t_dim // 8 - 1
    return pl.pallas_call(
        body, name=name, grid=(n,),
        in_specs=[pl.BlockSpec((tt, d3), lambda i: (i, 0)),
                  pl.BlockSpec((8, d3), lambda i: (jnp.maximum(i * (tt // 8) - 1, 0), 0)),
                  pl.BlockSpec((8, d3), lambda i: (jnp.minimum((i + 1) * (tt // 8), last8), 0)),
                  pl.BlockSpec((tt, d), lambda i: (i, 0)),
                  pl.BlockSpec((8, d), lambda i: (jnp.minimum((i + 1) * (tt // 8), last8), 0)),
                  pl.BlockSpec(conv_w.shape, lambda i: (0, 0))],
        out_specs=[pl.BlockSpec((tt, d3), lambda i: (i, 0)), pl.BlockSpec((8, d), lambda i: (0, 0))],
        out_shape=[jax.ShapeDtypeStruct((t_dim, d3), BF16), jax.ShapeDtypeStruct((8, d), F32)],
        compiler_params=_params(dimension_semantics=("arbitrary",)),
    )(proj, proj, proj, dy, dy, conv_w)


def _window(ref, axis, n, idx):
    if axis is None:
        return ref
    sel = [slice(None)] * len(ref.shape)
    sel[axis] = pl.ds(pl.multiple_of(idx * n, n), n)
    return ref.at[tuple(sel)]


def _scatter_shapes(items):
    widths, shapes = [], []
    for arr, axis in items:
        shp = list(arr.shape)
        if axis is not None:
            shp[axis] //= N_DEV
        widths.append(None if axis is None else shp[axis])
        shapes.append(jax.ShapeDtypeStruct((N_DEV, *shp), arr.dtype))
    return widths, shapes


def _scatter_copies(items, widths, src, dst, send_sems, recv_sems, local_sems):
    x, y, c = lax.axis_index("x"), lax.axis_index("y"), lax.axis_index("c")
    me = 4 * x + 2 * y + c
    copies = [pltpu.make_async_copy(_window(src[w], items[w][1], widths[w], me), dst[w].at[me], local_sems.at[w])
              for w in range(len(items))]
    for k in range(1, N_DEV):
        px = 1 - x if k & 4 else x
        py = 1 - y if k & 2 else y
        pc = 1 - c if k & 1 else c
        for w in range(len(items)):
            copies.append(pltpu.make_async_remote_copy(
                src_ref=_window(src[w], items[w][1], widths[w], 4 * px + 2 * py + pc), dst_ref=dst[w].at[me],
                send_sem=send_sems.at[w, k - 1], recv_sem=recv_sems.at[w, k - 1],
                device_id=(px, py, pc), device_id_type=pl.DeviceIdType.MESH))
    return copies


def _scatter_sems(n_w):
    return [pltpu.SemaphoreType.DMA((n_w, N_DEV - 1)), pltpu.SemaphoreType.DMA((n_w, N_DEV - 1)),
            pltpu.SemaphoreType.DMA((n_w,))]


def _gather_shapes(items):
    widths = [arr.shape[axis] for arr, axis in items]
    shapes = [jax.ShapeDtypeStruct(tuple(s * N_DEV if a == axis else s for a, s in enumerate(arr.shape)), arr.dtype)
              for arr, axis in items]
    return widths, shapes


def _gather_phases(items, widths, src, dst, send_sems, recv_sems, local_sems):
    n_w = len(items)
    x, y, c = lax.axis_index("x"), lax.axis_index("y"), lax.axis_index("c")
    chips = [(1 - x, y), (x, 1 - y), (1 - x, 1 - y)]

    def place(w, origin):
        return _window(dst[w], items[w][1], widths[w], 4 * origin[0] + 2 * origin[1] + origin[2])

    def block_copy(w, n, origin, to, from_shard):
        return pltpu.make_async_remote_copy(
            src_ref=src[w] if from_shard else place(w, origin), dst_ref=place(w, origin),
            send_sem=send_sems.at[w, n], recv_sem=recv_sems.at[w, n],
            device_id=to, device_id_type=pl.DeviceIdType.MESH)

    def own(w):
        return pltpu.make_async_copy(src[w], place(w, (x, y, c)), local_sems.at[w])

    def first(w):
        return ([block_copy(w, 0, (x, y, c), (x, y, 1 - c), True)]
                + [block_copy(w, 1 + n, (x, y, c), (*chip, c), True) for n, chip in enumerate(chips)])

    def passed(w, n):
        return block_copy(w, 4 + n, (*chips[n], c), (x, y, 1 - c), False)

    def start():
        for w in range(n_w):
            own(w).start()
            for cp in first(w):
                cp.start()

    def relay():
        for n, chip in enumerate(chips):
            for w in range(n_w):
                block_copy(w, 1 + n, (*chip, c), (x, y, c), True).wait_recv()
                passed(w, n).start()

    def finish():
        for w in range(n_w):
            block_copy(w, 0, (x, y, 1 - c), (x, y, c), True).wait_recv()
            for n, chip in enumerate(chips):
                block_copy(w, 4 + n, (*chip, 1 - c), (x, y, c), False).wait_recv()
            for cp in first(w) + [passed(w, n) for n in range(3)]:
                cp.wait_send()
            own(w).wait()

    return start, relay, finish


def _exchange(items, *, gather, name):
    n_w = len(items)
    widths, out_shape = _gather_shapes(items) if gather else _scatter_shapes(items)

    def body(*refs):
        src, dst = refs[:n_w], refs[n_w:2 * n_w]
        send_sems, recv_sems, local_sems = refs[2 * n_w:]
        if not gather:
            copies = _scatter_copies(items, widths, src, dst, send_sems, recv_sems, local_sems)
            for cp in copies:
                cp.start()
            for cp in copies:
                cp.wait()
            return
        for phase in _gather_phases(items, widths, src, dst, send_sems, recv_sems, local_sems):
            phase()

    return pl.pallas_call(
        body, name=name,
        in_specs=[pl.BlockSpec(memory_space=pl.ANY)] * n_w,
        out_specs=[pl.BlockSpec(memory_space=pl.ANY)] * n_w,
        out_shape=out_shape,
        scratch_shapes=_scatter_sems(n_w),
    )(*[arr for arr, _ in items])


def _row_tile(rows, cols):
    tr = rows
    while tr % 16 == 0 and tr * cols > 256 * 1024:
        tr //= 2
    return tr


def _sum_parts(parts, *, name):
    n_parts, rows, cols = parts.shape
    tr = _row_tile(rows, cols)

    def body(p_ref, o_ref):
        g = p_ref[0].astype(F32)
        for s in range(1, n_parts):
            g = g + p_ref[s].astype(F32)
        o_ref[...] = g

    return pl.pallas_call(
        body, name=name, grid=(rows // tr,),
        in_specs=[pl.BlockSpec((n_parts, tr, cols), lambda i: (0, i, 0))],
        out_specs=pl.BlockSpec((tr, cols), lambda i: (i, 0)),
        out_shape=jax.ShapeDtypeStruct((rows, cols), F32),
        compiler_params=_params(dimension_semantics=("parallel",)),
    )(parts)


def _adamw(parts, w, m, v, *, name):
    n_parts, rows, cols = parts.shape
    tr = _row_tile(rows, cols)

    def body(p_ref, w_ref, m_ref, v_ref, g_ref, d_ref, nm_ref, nv_ref):
        g = p_ref[0].astype(F32)
        for s in range(1, n_parts):
            g = g + p_ref[s].astype(F32)
        m_new = ADAM_B1 * m_ref[...] + (1.0 - ADAM_B1) * g
        v_new = ADAM_B2 * v_ref[...] + (1.0 - ADAM_B2) * (g * g)
        m_hat = m_new / (1.0 - ADAM_B1 ** ADAM_STEP)
        v_hat = v_new / (1.0 - ADAM_B2 ** ADAM_STEP)
        g_ref[...] = g
        d_ref[...] = -ADAM_LR * (m_hat / (jnp.sqrt(v_hat) + ADAM_EPS) + ADAM_WD * w_ref[...])
        nm_ref[...] = m_new
        nv_ref[...] = v_new

    spec = pl.BlockSpec((tr, cols), lambda i: (i, 0))
    return pl.pallas_call(
        body, name=name, grid=(rows // tr,),
        in_specs=[pl.BlockSpec((n_parts, tr, cols), lambda i: (0, i, 0)), spec, spec, spec],
        out_specs=[spec] * 4,
        out_shape=[jax.ShapeDtypeStruct((rows, cols), F32)] * 4,
        compiler_params=_params(dimension_semantics=("parallel",)),
    )(parts, w, m, v)


def _pad_rows(a, axis, to):
    pad = [(0, 0)] * a.ndim
    pad[axis] = (0, to - a.shape[axis])
    return jnp.pad(a, pad)


def kernel(x, p, norm_g, w_attn_in, b_forget, w_attn_out, w_conv_in, conv_w, w_conv_out, w_mlp_up, w_mlp_down, w_ple_proj, w_ple_gate, loss_target, m_norm_g, m_w_attn_in, m_b_forget, m_w_attn_out, m_w_conv_in, m_conv_w, m_w_conv_out, m_w_mlp_up, m_w_mlp_down, m_w_ple_proj, m_w_ple_gate, v_norm_g, v_w_attn_in, v_b_forget, v_w_attn_out, v_w_conv_in, v_conv_w, v_w_conv_out, v_w_mlp_up, v_w_mlp_down, v_w_ple_proj, v_w_ple_gate):
    shards = dict(norm_g=norm_g, w_attn_in=w_attn_in, b_forget=b_forget, w_attn_out=w_attn_out,
                  w_conv_in=w_conv_in, conv_w=conv_w, w_conv_out=w_conv_out, w_mlp_up=w_mlp_up,
                  w_mlp_down=w_mlp_down, w_ple_proj=w_ple_proj, w_ple_gate=w_ple_gate)
    m_shards = dict(norm_g=m_norm_g, w_attn_in=m_w_attn_in, b_forget=m_b_forget, w_attn_out=m_w_attn_out,
                    w_conv_in=m_w_conv_in, conv_w=m_conv_w, w_conv_out=m_w_conv_out, w_mlp_up=m_w_mlp_up,
                    w_mlp_down=m_w_mlp_down, w_ple_proj=m_w_ple_proj, w_ple_gate=m_w_ple_gate)
    v_shards = dict(norm_g=v_norm_g, w_attn_in=v_w_attn_in, b_forget=v_b_forget, w_attn_out=v_w_attn_out,
                    w_conv_in=v_w_conv_in, conv_w=v_conv_w, w_conv_out=v_w_conv_out, w_mlp_up=v_w_mlp_up,
                    w_mlp_down=v_w_mlp_down, w_ple_proj=v_w_ple_proj, w_ple_gate=v_w_ple_gate)
    t_dim, d = x.shape[-2:]
    depth = p.shape[0]
    n_attn, heads = b_forget.shape
    assert d == heads * HEAD_DIM and x.shape[0] == 1
    tb = min(512, t_dim // 2)
    x0 = x.reshape(t_dim, d)
    target = loss_target.reshape(t_dim, d)
    p_rows = p.reshape(depth * t_dim, p.shape[-1])

    in_cols = w_attn_in.shape[2]
    in_cols_pad = -(-in_cols // 16) * 16
    first_names = ['norm_g', 'w_attn_in', 'conv_w']
    rest_names = [n for n in WEIGHT_NAMES if n not in first_names + ['b_forget']]

    def gather_item(n):
        if n == 'w_attn_in':
            return _pad_rows(jnp.swapaxes(w_attn_in, 1, 2), 1, in_cols_pad).astype(BF16), 1
        return (shards[n] if n in ('norm_g', 'conv_w') else shards[n].astype(BF16)), SHARD_AXIS[n]

    full = dict(zip(first_names, _exchange([gather_item(n) for n in first_names], gather=True, name="gather_first")))
    gains = full['norm_g']
    taps = full['conv_w']
    w_in_t = full['w_attn_in'].reshape(n_attn, N_DEV, in_cols_pad, d)[:, :, :in_cols]
    w_in_t = _pad_rows(w_in_t.reshape(n_attn, N_DEV * in_cols, d), 1, 3 * d + LANES)
    bias_pad = jnp.pad(b_forget, ((0, 0), (0, LANES - heads)))

    def gain(i, k):
        return gains[i, k].reshape(1, d)

    def add_norm(x_prev, branch, g_branch, g_next, name):
        def fn(rows, vecs):
            x_new = rows[0] + _norm(rows[1], vecs[0])
            return [x_new, _norm(x_new, vecs[1])], []
        return _rows_call(fn, [x_prev, branch], [g_branch, g_next], [(d, F32), (d, BF16)], [], name=name)

    saved = []
    x_cur = x0
    hn = _rows_call(lambda rows, vecs: ([_norm(rows[0], vecs[0])], []), [x0], [gain(0, 0)], [(d, BF16)], [],
                    name="norm_in")[0]
    loss_rows = dy = None
    for i in range(depth):
        j = i // 2
        s = dict(x0=x_cur, hn=hn)
        if i % 2 == 0:
            s['qkv'] = _mm(hn, w_in_t[j, :3 * d], tb=True, out_dtypes=(BF16,), name=f"attn_in_{i}")
            s['fl'] = _mm(hn, w_in_t[j, 3 * d:], tb=True, name=f"attn_gate_{i}")
            c = _cumsum_fwd(s['fl'], bias_pad[j:j + 1], name=f"gate_cumsum_{i}")
            c_t = c[:, :heads].T
            s['prune'] = _prune_table(c_t, _head_norms(s['qkv'], name=f"head_norms_{i}"), tb)
            s['c_rows'] = c_t.reshape(heads // 2, 2, 1, t_dim)
            s['o'], s['lse_hb'], s['lse_rows'], rest = _flash_fwd(
                s['qkv'], s['c_rows'], s['prune'], tb=tb, name=f"attn_fwd_{i}",
                gather=[gather_item(n) for n in rest_names] if i == 0 else ())
            if i == 0:
                full.update(zip(rest_names, rest))
            s['m'] = _mm(s['o'], full['w_attn_out'][j], name=f"attn_out_{i}")
        else:
            s['proj'] = _mm(hn, full['w_conv_in'][j], name=f"conv_in_{i}")
            s['y'] = _conv_fwd(s['proj'], taps[j], name=f"conv_fwd_{i}")
            s['m'] = _mm(s['y'], full['w_conv_out'][j], name=f"conv_out_{i}")
        s['x1'], s['h2'] = add_norm(x_cur, s['m'], gain(i, 1), gain(i, 2), f"mix_norm_{i}")
        s['u'], s['a'] = _mm(s['h2'], full['w_mlp_up'][i], out_dtypes=(BF16, BF16), name=f"mlp_up_{i}",
                             epi=lambda acc: (acc, jnp.square(jnp.maximum(acc, 0.0))))
        s['f'] = _mm(s['a'], full['w_mlp_down'][i], name=f"mlp_down_{i}")
        s['x2'], s['h4'] = add_norm(s['x1'], s['f'], gain(i, 3), gain(i, 4), f"mlp_norm_{i}")
        s['pp'] = _mm(p_rows, full['w_ple_proj'][i], a_rows=t_dim, a_off=i * t_dim, name=f"ple_proj_{i}")
        s['gl'], s['e'] = _mm(s['h4'], full['w_ple_gate'][i], extras=(s['pp'],), out_dtypes=(F32, F32),
                              name=f"ple_gate_{i}", epi=lambda acc, pp: (acc, pp * _sigmoid(acc)))
        if i + 1 < depth:
            x_cur, hn = add_norm(s['x2'], s['e'], gain(i, 5), gain(i + 1, 0), f"ple_norm_{i}")
        else:
            def loss_fn(rows, vecs):
                err = rows[0] + _norm(rows[1], vecs[0]) - rows[2]
                part = 0.5 * jnp.sum(jnp.sum(err * err, axis=1, keepdims=True), axis=0, keepdims=True) / d
                return [err / d], [jnp.broadcast_to(part, (1, LANES))]
            dy, loss_rows = _rows_call(loss_fn, [s['x2'], s['e'], target], [gain(i, 5)], [(d, F32)],
                                       [(1, LANES)], name="loss")
        saved.append(s)
    loss = lax.psum(loss_rows[0, 0], ("x", "y", "c"))

    grads = {n: [None] * shards[n].shape[0] for n in WEIGHT_NAMES}
    d_gains = [[None] * 6 for _ in range(depth)]
    wgrad = functools.partial(_mm, ta=True, out_dtypes=(BF16,))
    axis_of = dict(SHARD_AXIS, w_attn_in=1)

    def in_t_blocks(layers):
        g = jnp.stack(layers)[:, :N_DEV * in_cols].reshape(len(layers), N_DEV, in_cols, d)
        return _pad_rows(g, 2, in_cols_pad).reshape(len(layers), N_DEV * in_cols_pad, d)

    early_names = early_items = early = None
    dx = dy
    for i in reversed(range(depth)):
        j = i // 2
        s = saved[i]

        def ple_fn(rows, vecs):
            de, dg = _norm_bwd(rows[0], vecs[0], rows[1])
            sg = _sigmoid(rows[2])
            return [de * sg, de * rows[3] * sg * (1.0 - sg)], [dg]
        dpp, dgl, d_gains[i][5] = _rows_call(ple_fn, [s['e'], dx, s['gl'], s['pp']], [gain(i, 5)],
                                             [(d, BF16), (d, BF16)], [(1, d)], name=f"ple_bwd_{i}")
        grads['w_ple_proj'][i] = wgrad(p_rows, dpp, a_rows=t_dim, a_off=i * t_dim, name=f"ple_proj_dw_{i}")
        grads['w_ple_gate'][i] = wgrad(s['h4'], dgl, name=f"ple_gate_dw_{i}")
        dh4 = _mm(dgl, full['w_ple_gate'][i], tb=True, name=f"ple_gate_dx_{i}")

        def two_norm_bwd(x_res, dh, dx_in, branch, g_res, g_branch, name):
            def fn(rows, vecs):
                d_res, dg_res = _norm_bwd(rows[0], vecs[0], rows[1])
                dx_out = rows[2] + d_res
                d_branch, dg_branch = _norm_bwd(rows[3], vecs[1], dx_out)
                return [dx_out, d_branch], [dg_res, dg_branch]
            return _rows_call(fn, [x_res, dh, dx_in, branch], [g_res, g_branch], [(d, F32), (d, BF16)],
                              [(1, d), (1, d)], name=name)

        dx2, df, d_gains[i][4], d_gains[i][3] = two_norm_bwd(s['x2'], dh4, dx, s['f'], gain(i, 4), gain(i, 3),
                                                            f"mlp_norm_bwd_{i}")
        grads['w_mlp_down'][i] = wgrad(s['a'], df, name=f"mlp_down_dw_{i}")
        du = _mm(df, full['w_mlp_down'][i], tb=True, extras=(s['u'],), out_dtypes=(BF16,), name=f"mlp_down_dx_{i}",
                 epi=lambda acc, u: (acc * (2.0 * jnp.maximum(u.astype(F32), 0.0)),))
        grads['w_mlp_up'][i] = wgrad(s['h2'], du, name=f"mlp_up_dw_{i}")
        dh2 = _mm(du, full['w_mlp_up'][i], tb=True, name=f"mlp_up_dx_{i}")
        dx1, dm, d_gains[i][2], d_gains[i][1] = two_norm_bwd(s['x1'], dh2, dx2, s['m'], gain(i, 2), gain(i, 1),
                                                            f"mix_norm_bwd_{i}")
        if i % 2 == 0:
            grads['w_attn_out'][j] = wgrad(s['o'], dm, name=f"attn_out_dw_{i}")
            do = _mm(dm, full['w_attn_out'][j], tb=True, out_dtypes=(BF16,), name=f"attn_out_dx_{i}")
            dq, delta_rows, rsum_rows = _flash_dq(s['qkv'], s['o'], do, s['c_rows'], s['lse_hb'],
                                                  s['prune'], tb=tb, name=f"attn_dq_{i}")
            if i == 0:
                early_names = [n for n in WEIGHT_NAMES if n not in ('norm_g', 'b_forget')]
                early_items = [(in_t_blocks(grads[n][1:]) if n == 'w_attn_in' else jnp.stack(grads[n]), axis_of[n])
                               for n in early_names]
            dk, dv, csum_rows, early = _flash_dkv(s['qkv'], do, s['c_rows'], s['lse_rows'], delta_rows, s['prune'],
                                                  tb=tb, name=f"attn_dkv_{i}", scatter=early_items if i == 0 else ())
            dc = (rsum_rows - csum_rows).reshape(heads, t_dim).T
            dfl, db = _cumsum_bwd(jnp.pad(dc, ((0, 0), (0, LANES - heads))), s['fl'], bias_pad[j:j + 1],
                                  name=f"gate_cumsum_bwd_{i}")
            grads['b_forget'][j] = db[0, :heads]
            dproj = jnp.concatenate([dq, dk, dv, dfl], axis=1)
            grads['w_attn_in'][j] = wgrad(dproj, s['hn'], name=f"attn_in_dw_{i}")
            dhn = _mm(dproj, w_in_t[j], name=f"attn_in_dx_{i}")
        else:
            grads['w_conv_out'][j] = wgrad(s['y'], dm, name=f"conv_out_dw_{i}")
            dyc = _mm(dm, full['w_conv_out'][j], tb=True, name=f"conv_out_dx_{i}")
            dproj, dtaps = _conv_bwd(s['proj'], dyc, taps[j], name=f"conv_bwd_{i}")
            grads['conv_w'][j] = dtaps[:3].astype(BF16)
            grads['w_conv_in'][j] = wgrad(s['hn'], dproj, name=f"conv_in_dw_{i}")
            dhn = _mm(dproj, full['w_conv_in'][j], tb=True, name=f"conv_in_dx_{i}")

        def in_fn(rows, vecs):
            d_res, dg = _norm_bwd(rows[0], vecs[0], rows[1])
            return [rows[2] + d_res], [dg]
        dx, d_gains[i][0] = _rows_call(in_fn, [s['x0'], dhn, dx1], [gain(i, 0)], [(d, F32)], [(1, d)],
                                       name=f"in_norm_bwd_{i}")
    grad_x = dx.reshape(x.shape)

    recv = dict(zip(early_names, early))
    late = _exchange([(in_t_blocks(grads['w_attn_in'][:1]), 1),
                      (jnp.stack([jnp.concatenate(row, axis=0) for row in d_gains]).astype(BF16), SHARD_AXIS['norm_g']),
                      (jnp.zeros((8, LANES), F32).at[:n_attn, :heads].set(jnp.stack(grads['b_forget'])), None)],
                     gather=False, name="exchange_late")
    recv['norm_g'] = late[1]
    recv['b_forget'] = late[2][:, :n_attn, :heads]
    g_in_t = jnp.concatenate([_sum_parts(part.reshape(N_DEV, -1, d), name=f"sum_attn_in_{k}")
                              for k, part in enumerate((late[0], recv['w_attn_in']))], axis=0)
    recv['w_attn_in'] = jnp.swapaxes(g_in_t.reshape(n_attn, in_cols_pad, d)[:, :in_cols], 1, 2)[None]
    results = {}
    for n in WEIGHT_NAMES:
        shp = shards[n].shape
        flat = lambda a: a.reshape(a.shape[:a.ndim - len(shp)] + (-1, shp[-1]))
        outs = _adamw(flat(recv[n]), flat(shards[n]), flat(m_shards[n]), flat(v_shards[n]), name=f"adamw_{n}")
        results[n] = [o.reshape(shp) for o in outs]
    return (loss, grad_x, *[results[n][k] for k in range(4) for n in WEIGHT_NAMES])
```

```python
import functools

import jax
import jax.numpy as jnp
from jax import lax
from jax.experimental import pallas as pl
from jax.experimental.pallas import tpu as pltpu

F32 = jnp.float32
BF16 = jnp.bfloat16

N_DEV = 8
LANES = 128
HEAD_DIM = 64
VMEM_LIMIT_BYTES = 56 * 1024 * 1024
RMS_EPS = 1e-6
NEG_INF = -1e30
ADAM_LR = 0.001
ADAM_B1 = 0.9
ADAM_B2 = 0.999
ADAM_EPS = 1e-08
ADAM_WD = 0.01
ADAM_STEP = 10
WEIGHT_NAMES = ('norm_g', 'w_attn_in', 'b_forget', 'w_attn_out', 'w_conv_in', 'conv_w', 'w_conv_out',
                'w_mlp_up', 'w_mlp_down', 'w_ple_proj', 'w_ple_gate')
SHARD_AXIS = {'norm_g': 2, 'w_attn_in': 2, 'b_forget': None, 'w_attn_out': 1, 'w_conv_in': 2, 'conv_w': 2,
              'w_conv_out': 1, 'w_mlp_up': 2, 'w_mlp_down': 1, 'w_ple_proj': 2, 'w_ple_gate': 1}


def _params(**kw):
    return pltpu.CompilerParams(vmem_limit_bytes=VMEM_LIMIT_BYTES, **kw)


def _tile(n, cap):
    if n <= cap:
        return n
    t = (cap // LANES) * LANES
    while n % t:
        t -= LANES
    return t


MM_VMEM_BUDGET = 36 * 1024 * 1024


def _mm(a, b, *, ta=False, tb=False, extras=(), epi=None, out_dtypes=(F32,), name, a_rows=None, a_off=0):
    rows_a = a_rows or a.shape[0]
    m_dim, k_dim = (a.shape[1], rows_a) if ta else (rows_a, a.shape[1])
    n_dim = b.shape[0] if tb else b.shape[1]
    assert k_dim == (b.shape[1] if tb else b.shape[0]) and a_off % rows_a == 0
    tk = _tile(k_dim, 1024 if k_dim <= 1024 else 2048)
    nk = k_dim // tk
    tn = _tile(n_dim, 1024)

    def vmem_bytes(tm):
        per_mn = sum(jnp.dtype(dt).itemsize for dt in out_dtypes) + sum(e.dtype.itemsize for e in extras)
        return (2 * (tm * tk * a.dtype.itemsize + tk * tn * b.dtype.itemsize) + 2 * tm * tn * per_mn
                + tm * tn * 4 * (2 + (nk > 1)))

    tm = next(t for t in (_tile(m_dim, 1024), _tile(m_dim, 512)) if t <= 512 or vmem_bytes(t) <= MM_VMEM_BUDGET)
    grid = (n_dim // tn, m_dim // tm, nk)
    off_m, off_k = (0, a_off // tk) if ta else (a_off // tm, 0)
    a_spec = (pl.BlockSpec((tk, tm), lambda j, i, k: (k + off_k, i)) if ta
              else pl.BlockSpec((tm, tk), lambda j, i, k: (i + off_m, k)))
    b_spec = (pl.BlockSpec((tn, tk), lambda j, i, k: (j, k)) if tb
              else pl.BlockSpec((tk, tn), lambda j, i, k: (k, j)))
    mn_spec = pl.BlockSpec((tm, tn), lambda j, i, k: (i, j))
    dims = (((0 if ta else 1,), (1 if tb else 0,)), ((), ()))
    n_extra, n_out = len(extras), len(out_dtypes)
    if epi is None:
        epi = lambda acc: (acc,)

    def body(a_ref, b_ref, *rest):
        e_refs, o_refs = rest[:n_extra], rest[n_extra:n_extra + n_out]
        part = lax.dot_general(a_ref[...].astype(BF16), b_ref[...].astype(BF16), dims,
                               preferred_element_type=F32)

        def finish(acc):
            for o_ref, val in zip(o_refs, epi(acc, *[e[...] for e in e_refs])):
                o_ref[...] = val.astype(o_ref.dtype)

        if nk == 1:
            finish(part)
        else:
            acc_ref = rest[-1]
            k = pl.program_id(2)

            @pl.when(k == 0)
            def _():
                acc_ref[...] = part

            @pl.when(k > 0)
            def _():
                acc_ref[...] += part

            @pl.when(k == nk - 1)
            def _():
                finish(acc_ref[...])

    outs = pl.pallas_call(
        body, name=name, grid=grid,
        in_specs=[a_spec, b_spec] + [mn_spec] * n_extra,
        out_specs=[mn_spec] * n_out,
        out_shape=[jax.ShapeDtypeStruct((m_dim, n_dim), dt) for dt in out_dtypes],
        scratch_shapes=[pltpu.VMEM((tm, tn), F32)] if nk > 1 else [],
        compiler_params=_params(dimension_semantics=("parallel", "parallel", "arbitrary")),
    )(a, b, *extras)
    return outs[0] if n_out == 1 else outs


def _rows(fn, row_ins, vec_ins, row_outs, vec_outs, *, name, tt=512, reverse=False):
    t_dim = row_ins[0].shape[0]
    tt = min(tt, t_dim)
    n = t_dim // tt
    n_ri, n_vi, n_ro, n_vo = len(row_ins), len(vec_ins), len(row_outs), len(vec_outs)
    pos = (lambda i: (n - 1 - i, 0)) if reverse else (lambda i: (i, 0))
    fixed = lambda i: (0, 0)

    def body(*refs):
        ri = refs[:n_ri]
        vi = refs[n_ri:n_ri + n_vi]
        ro = refs[n_ri + n_vi:n_ri + n_vi + n_ro]
        vo = refs[n_ri + n_vi + n_ro:n_ri + n_vi + n_ro + n_vo]
        scratch = refs[n_ri + n_vi + n_ro + n_vo:]
        r_out, v_out = fn([r[...] for r in ri], [v[...] for v in vi], *scratch)
        for o_ref, val in zip(ro, r_out):
            o_ref[...] = val.astype(o_ref.dtype)
        i = pl.program_id(0)
        for o_ref, val in zip(vo, v_out):
            @pl.when(i == 0)
            def _(o_ref=o_ref, val=val):
                o_ref[...] = val

            @pl.when(i > 0)
            def _(o_ref=o_ref, val=val):
                o_ref[...] += val

    return body, dict(
        grid=(n,),
        in_specs=[pl.BlockSpec((tt, r.shape[1]), pos) for r in row_ins]
        + [pl.BlockSpec(v.shape, fixed) for v in vec_ins],
        out_specs=[pl.BlockSpec((tt, w), pos) for w, _ in row_outs]
        + [pl.BlockSpec(s, fixed) for s in vec_outs],
        out_shape=[jax.ShapeDtypeStruct((t_dim, w), dt) for w, dt in row_outs]
        + [jax.ShapeDtypeStruct(s, F32) for s in vec_outs],
        name=name,
        compiler_params=_params(dimension_semantics=("arbitrary",)),
    )


def _rows_call(fn, row_ins, vec_ins, row_outs, vec_outs, *, name, tt=512, reverse=False, scratch=()):
    body, kw = _rows(fn, row_ins, vec_ins, row_outs, vec_outs, name=name, tt=tt, reverse=reverse)
    return pl.pallas_call(body, scratch_shapes=list(scratch), **kw)(*row_ins, *vec_ins)


def _rstd(x):
    return lax.rsqrt(jnp.mean(x * x, axis=-1, keepdims=True) + RMS_EPS)


def _norm(x, g):
    return x * _rstd(x) * g


def _norm_bwd(x, g, dy):
    xh = x * _rstd(x)
    gy = dy * g
    dx = _rstd(x) * (gy - xh * jnp.mean(gy * xh, axis=-1, keepdims=True))
    return dx, jnp.sum(dy * xh, axis=0, keepdims=True)


def _sigmoid(x):
    return 1.0 / (1.0 + jnp.exp(-x))


def _log_sigmoid(x):
    return jnp.minimum(x, 0.0) - jnp.log(1.0 + jnp.exp(-jnp.abs(x)))


def _split3(x):
    hi = x.astype(BF16)
    r1 = x - hi.astype(F32)
    mid = r1.astype(BF16)
    lo = (r1 - mid.astype(F32)).astype(BF16)
    return hi, mid, lo


def _cumsum_fwd(fl, bias, *, name):
    w = fl.shape[1]
    tt = min(512, fl.shape[0])

    def fn(rows, vecs, carry_ref):
        i = pl.program_id(0)

        @pl.when(i == 0)
        def _():
            carry_ref[...] = jnp.zeros_like(carry_ref)

        lf = _log_sigmoid(rows[0] + vecs[0])
        r = lax.broadcasted_iota(jnp.int32, (tt, tt), 0)
        c = lax.broadcasted_iota(jnp.int32, (tt, tt), 1)
        tri = (c <= r).astype(BF16)
        acc = carry_ref[0:1, :]
        for part in _split3(lf):
            acc = acc + jnp.dot(tri, part, preferred_element_type=F32)
        carry_ref[0:1, :] = acc[tt - 1:tt, :]
        return [acc], []

    return _rows_call(fn, [fl], [bias], [(w, F32)], [], name=name, tt=tt,
                      scratch=[pltpu.VMEM((8, w), F32)])[0]


def _cumsum_bwd(dc, fl, bias, *, name):
    w = fl.shape[1]
    tt = min(512, fl.shape[0])

    def fn(rows, vecs, carry_ref):
        i = pl.program_id(0)

        @pl.when(i == 0)
        def _():
            carry_ref[...] = jnp.zeros_like(carry_ref)

        r = lax.broadcasted_iota(jnp.int32, (tt, tt), 0)
        c = lax.broadcasted_iota(jnp.int32, (tt, tt), 1)
        tri = (c >= r).astype(BF16)
        acc = carry_ref[0:1, :]
        for part in _split3(rows[0]):
            acc = acc + jnp.dot(tri, part, preferred_element_type=F32)
        carry_ref[0:1, :] = acc[0:1, :]
        dfl = acc * _sigmoid(-(rows[1] + vecs[0]))
        return [dfl], [jnp.sum(dfl, axis=0, keepdims=True)]

    return _rows_call(fn, [dc, fl], [bias], [(w, BF16)], [(1, w)], name=name, tt=tt, reverse=True,
                      scratch=[pltpu.VMEM((8, w), F32)])


def _head_masks(tb):
    lane = lax.broadcasted_iota(jnp.int32, (tb, LANES), 1)
    return [lane < HEAD_DIM, lane >= HEAD_DIM]


PRUNE_MARGIN = 30.0


def _head_norms(qkv, *, name):
    t_dim = qkv.shape[0]
    d = qkv.shape[1] // 3
    heads = d // HEAD_DIM
    tt = min(512, t_dim)

    def body(q_ref, k_ref, o_ref):
        col = lax.broadcasted_iota(jnp.int32, (d, LANES), 0) // HEAD_DIM
        lane = lax.broadcasted_iota(jnp.int32, (d, LANES), 1)
        tile_max = None
        for ref, first in ((q_ref, 0), (k_ref, heads)):
            x = ref[...].astype(F32)
            sums = jnp.dot((x * x).astype(BF16), (col + first == lane).astype(BF16), preferred_element_type=F32)
            part = jnp.max(sums, axis=0, keepdims=True)
            tile_max = part if tile_max is None else jnp.maximum(tile_max, part)
        i = pl.program_id(0)

        @pl.when(i == 0)
        def _():
            o_ref[...] = tile_max

        @pl.when(i > 0)
        def _():
            o_ref[...] = jnp.maximum(o_ref[...], tile_max)

    return pl.pallas_call(
        body, name=name, grid=(t_dim // tt,),
        in_specs=[pl.BlockSpec((tt, d), lambda i: (i, 0)), pl.BlockSpec((tt, d), lambda i: (i, 1))],
        out_specs=pl.BlockSpec((1, LANES), lambda i: (0, 0)),
        out_shape=jax.ShapeDtypeStruct((1, LANES), F32),
        compiler_params=_params(dimension_semantics=("arbitrary",)),
    )(qkv, qkv)


def _prune_table(c_t, norms, tb):
    heads = c_t.shape[0]
    bound = 1.02 * HEAD_DIM ** -0.5 * jnp.sqrt(norms[0, :heads] * norms[0, heads:2 * heads])
    return jnp.concatenate([c_t[:, ::tb], c_t[:, tb - 1::tb], -(PRUNE_MARGIN + 2.0 * bound)[:, None]], axis=1)


def _kept_before(prune_ref, h, i, nq):
    first, thr = prune_ref[h, i], prune_ref[h, 2 * nq]
    return lax.fori_loop(0, i, lambda j, n: n + (first - prune_ref[h, nq + j] >= thr).astype(jnp.int32),
                         jnp.int32(0))


def _kept_after(prune_ref, h, j, nq):
    last, thr = prune_ref[h, nq + j], prune_ref[h, 2 * nq]
    return lax.fori_loop(j + 1, nq, lambda i, n: n + (prune_ref[h, i] - last >= thr).astype(jnp.int32),
                         jnp.int32(0))


def _as_row(col, tb):
    return jnp.transpose(jnp.broadcast_to(col, (tb, LANES)))[0:1, :]


def _flash_fwd(qkv, c_rows, prune, *, tb, name, gather=()):
    t_dim = qkv.shape[0]
    d = qkv.shape[1] // 3
    heads = d // HEAD_DIM
    cb = d // LANES
    nq = t_dim // tb
    n_w = len(gather)
    widths, gather_shapes = _gather_shapes(gather)

    def body(prune_ref, q_ref, k_ref, v_ref, cr_ref, *rest):
        src, (o_ref, lse_ref, lser_ref), dst = rest[:n_w], rest[n_w:n_w + 3], rest[n_w + 3:2 * n_w + 3]
        i = pl.program_id(1)
        h0 = 2 * pl.program_id(0)
        if gather:
            start, relay, finish = _gather_phases(gather, widths, src, dst, *rest[2 * n_w + 3:])
            pl.when((pl.program_id(0) == 0) & (i == 0))(start)
            pl.when((pl.program_id(0) == (3 * heads) // 8) & (i == 0))(relay)
        q = q_ref[...] * jnp.asarray(HEAD_DIM ** -0.5, BF16)
        masks = _head_masks(tb)
        row = lax.broadcasted_iota(jnp.int32, (tb, tb), 0)
        col = lax.broadcasted_iota(jnp.int32, (tb, tb), 1)
        qs = [jnp.where(masks[e], q, jnp.zeros_like(q)) for e in range(2)]

        def step(j, carry, diagonal):
            off = pl.multiple_of(j * tb, tb)
            kj = k_ref[pl.ds(off, tb), :]
            vj = v_ref[pl.ds(off, tb), :]
            out = []
            for e in range(2):
                m, l, acc = carry[e]
                crow = cr_ref[0, e, :, pl.ds(off, tb)]
                s = lax.dot_general(qs[e], kj, (((1,), (1,)), ((), ())), preferred_element_type=F32) - crow
                if diagonal:
                    s = jnp.where(col <= row, s, NEG_INF)
                m_new = jnp.maximum(m, jnp.max(s, axis=1, keepdims=True))
                p = jnp.exp(s - m_new)
                alpha = jnp.exp(m - m_new)
                l = alpha * l + jnp.sum(p, axis=1, keepdims=True)
                acc = alpha * acc + jnp.dot(p.astype(BF16), vj, preferred_element_type=F32)
                out.append((m_new, l, acc))
            return tuple(out)

        init = (jnp.full((tb, 1), NEG_INF, F32), jnp.zeros((tb, 1), F32), jnp.zeros((tb, LANES), F32))
        kept = jnp.maximum(_kept_before(prune_ref, h0, i, nq), _kept_before(prune_ref, h0 + 1, i, nq))
        carry = lax.fori_loop(i - kept, i, functools.partial(step, diagonal=False), (init, init))
        carry = step(i, carry, True)
        outs = []
        for e in range(2):
            m, l, acc = carry[e]
            outs.append(acc / l)
            lse = m + jnp.log(l)
            lse_ref[0, e] = jnp.broadcast_to(lse, (tb, LANES))
            lser_ref[0, e] = _as_row(lse, tb)
        o_ref[...] = jnp.where(masks[0], outs[0], outs[1]).astype(o_ref.dtype)
        if gather:
            pl.when((pl.program_id(0) == heads // 2 - 1) & (i == nq - 1))(finish)

    hb_spec = pl.BlockSpec((1, 2, tb, LANES), lambda h, i: (h, 0, i, 0))
    row_spec = pl.BlockSpec((1, 2, 1, t_dim), lambda h, i: (h, 0, 0, 0))
    row_blk = pl.BlockSpec((1, 2, 1, tb), lambda h, i: (h, 0, 0, i))
    hbm = pl.BlockSpec(memory_space=pl.ANY)
    outs = pl.pallas_call(
        body, name=name, grid=(heads // 2, nq),
        in_specs=[pl.BlockSpec(memory_space=pltpu.SMEM),
                  pl.BlockSpec((tb, LANES), lambda h, i: (i, h)),
                  pl.BlockSpec((t_dim, LANES), lambda h, i: (0, cb + h)),
                  pl.BlockSpec((t_dim, LANES), lambda h, i: (0, 2 * cb + h)),
                  row_spec] + [hbm] * n_w,
        out_specs=[pl.BlockSpec((tb, LANES), lambda h, i: (i, h)), hb_spec, row_blk] + [hbm] * n_w,
        out_shape=[jax.ShapeDtypeStruct((t_dim, d), BF16),
                   jax.ShapeDtypeStruct((heads // 2, 2, t_dim, LANES), F32),
                   jax.ShapeDtypeStruct((heads // 2, 2, 1, t_dim), F32)] + gather_shapes,
        scratch_shapes=_scatter_sems(n_w) if gather else [],
        compiler_params=_params(dimension_semantics=("arbitrary", "arbitrary")),
    )(prune, qkv, qkv, qkv, c_rows, *[arr for arr, _ in gather])
    return outs[0], outs[1], outs[2], outs[3:]


def _flash_dq(qkv, o, do, c_rows, lse_hb, prune, *, tb, name):
    t_dim = qkv.shape[0]
    d = qkv.shape[1] // 3
    heads = d // HEAD_DIM
    cb = d // LANES
    scale = HEAD_DIM ** -0.5
    nq = t_dim // tb

    def body(prune_ref, q_ref, k_ref, v_ref, o_ref, do_ref, cr_ref, lse_ref, dq_ref, dl_ref, rs_ref):
        i = pl.program_id(1)
        h0 = 2 * pl.program_id(0)
        q = q_ref[...] * jnp.asarray(scale, BF16)
        do_blk = do_ref[...]
        prod = do_blk.astype(F32) * o_ref[...].astype(F32)
        masks = _head_masks(tb)
        row = lax.broadcasted_iota(jnp.int32, (tb, tb), 0)
        col = lax.broadcasted_iota(jnp.int32, (tb, tb), 1)
        qs = [jnp.where(masks[e], q, jnp.zeros_like(q)) for e in range(2)]
        dos = [jnp.where(masks[e], do_blk, jnp.zeros_like(do_blk)) for e in range(2)]
        deltas = [jnp.sum(jnp.where(masks[e], prod, 0.0), axis=1, keepdims=True) for e in range(2)]
        lses = [lse_ref[0, e][:, 0:1] for e in range(2)]

        def step(j, carry, diagonal):
            off = pl.multiple_of(j * tb, tb)
            kj = k_ref[pl.ds(off, tb), :]
            vj = v_ref[pl.ds(off, tb), :]
            out = []
            for e in range(2):
                acc, rsum = carry[e]
                crow = cr_ref[0, e, :, pl.ds(off, tb)]
                s = lax.dot_general(qs[e], kj, (((1,), (1,)), ((), ())), preferred_element_type=F32) - crow
                if diagonal:
                    s = jnp.where(col <= row, s, NEG_INF)
                p = jnp.exp(s - lses[e])
                dp = lax.dot_general(dos[e], vj, (((1,), (1,)), ((), ())), preferred_element_type=F32)
                ds = p * (dp - deltas[e])
                out.append((acc + jnp.dot(ds.astype(BF16), kj, preferred_element_type=F32),
                            rsum + jnp.sum(ds, axis=1, keepdims=True)))
            return tuple(out)

        init = (jnp.zeros((tb, LANES), F32), jnp.zeros((tb, 1), F32))
        kept = jnp.maximum(_kept_before(prune_ref, h0, i, nq), _kept_before(prune_ref, h0 + 1, i, nq))
        carry = lax.fori_loop(i - kept, i, functools.partial(step, diagonal=False), (init, init))
        carry = step(i, carry, True)
        for e in range(2):
            dl_ref[0, e] = _as_row(deltas[e], tb)
            rs_ref[0, e] = _as_row(carry[e][1], tb)
        dq_ref[...] = (jnp.where(masks[0], carry[0][0], carry[1][0]) * scale).astype(dq_ref.dtype)

    blk = pl.BlockSpec((tb, LANES), lambda h, i: (i, h))
    hb_spec = pl.BlockSpec((1, 2, tb, LANES), lambda h, i: (h, 0, i, 0))
    row_spec = pl.BlockSpec((1, 2, 1, t_dim), lambda h, i: (h, 0, 0, 0))
    row_blk = pl.BlockSpec((1, 2, 1, tb), lambda h, i: (h, 0, 0, i))
    row_shape = jax.ShapeDtypeStruct((heads // 2, 2, 1, t_dim), F32)
    return pl.pallas_call(
        body, name=name, grid=(heads // 2, nq),
        in_specs=[pl.BlockSpec(memory_space=pltpu.SMEM), blk,
                  pl.BlockSpec((t_dim, LANES), lambda h, i: (0, cb + h)),
                  pl.BlockSpec((t_dim, LANES), lambda h, i: (0, 2 * cb + h)),
                  blk, blk, row_spec, hb_spec],
        out_specs=[blk, row_blk, row_blk],
        out_shape=[jax.ShapeDtypeStruct((t_dim, d), BF16), row_shape, row_shape],
        compiler_params=_params(dimension_semantics=("parallel", "arbitrary")),
    )(prune, qkv, qkv, qkv, o, do, c_rows, lse_hb)


def _flash_dkv(qkv, do, c_rows, lse_rows, delta_rows, prune, *, tb, name, scatter=()):
    t_dim = qkv.shape[0]
    d = qkv.shape[1] // 3
    heads = d // HEAD_DIM
    cb = d // LANES
    scale = HEAD_DIM ** -0.5
    nq = t_dim // tb
    n_w = len(scatter)
    widths, scatter_shapes = _scatter_shapes(scatter)

    def body(prune_ref, q_ref, k_ref, v_ref, do_ref, cc_ref, lr_ref, dr_ref, *rest):
        src, (dk_ref, dv_ref, dsum_ref), dst = rest[:n_w], rest[n_w:n_w + 3], rest[n_w + 3:2 * n_w + 3]
        j = pl.program_id(1)
        h0 = 2 * pl.program_id(0)
        if scatter:
            travel = lambda: _scatter_copies(scatter, widths, src, dst, *rest[2 * n_w + 3:])

            @pl.when((pl.program_id(0) == 0) & (j == 0))
            def _():
                for cp in travel():
                    cp.start()
        k_blk = k_ref[...] * jnp.asarray(scale, BF16)
        v_blk = v_ref[...]
        masks = _head_masks(tb)
        row = lax.broadcasted_iota(jnp.int32, (tb, tb), 0)
        col = lax.broadcasted_iota(jnp.int32, (tb, tb), 1)
        ks = [jnp.where(masks[e], k_blk, jnp.zeros_like(k_blk)) for e in range(2)]
        vs = [jnp.where(masks[e], v_blk, jnp.zeros_like(v_blk)) for e in range(2)]
        ccols = [jnp.transpose(jnp.broadcast_to(cc_ref[0, e], (LANES, tb)))[:, 0:1] for e in range(2)]

        def step(i, carry, diagonal):
            off = pl.multiple_of(i * tb, tb)
            qi = q_ref[pl.ds(off, tb), :]
            doi = do_ref[pl.ds(off, tb), :]
            out = []
            for e in range(2):
                dk, dv, dsum = carry[e]
                lse = lr_ref[0, e, :, pl.ds(off, tb)]
                delta = dr_ref[0, e, :, pl.ds(off, tb)]
                st = lax.dot_general(ks[e], qi, (((1,), (1,)), ((), ())), preferred_element_type=F32) - ccols[e]
                if diagonal:
                    st = jnp.where(col >= row, st, NEG_INF)
                pt = jnp.exp(st - lse)
                dpt = lax.dot_general(vs[e], doi, (((1,), (1,)), ((), ())), preferred_element_type=F32)
                dst = pt * (dpt - delta)
                out.append((dk + jnp.dot(dst.astype(BF16), qi, preferred_element_type=F32),
                            dv + jnp.dot(pt.astype(BF16), doi, preferred_element_type=F32),
                            dsum + jnp.sum(dst, axis=1, keepdims=True)))
            return tuple(out)

        zero = jnp.zeros((tb, LANES), F32)
        init = (zero, zero, jnp.zeros((tb, 1), F32))
        carry = step(j, (init, init), True)
        kept = jnp.maximum(_kept_after(prune_ref, h0, j, nq), _kept_after(prune_ref, h0 + 1, j, nq))
        carry = lax.fori_loop(j + 1, j + 1 + kept, functools.partial(step, diagonal=False), carry)
        for e in range(2):
            dsum_ref[0, e] = _as_row(carry[e][2], tb)
        dk_ref[...] = (jnp.where(masks[0], carry[0][0], carry[1][0]) * scale).astype(dk_ref.dtype)
        dv_ref[...] = jnp.where(masks[0], carry[0][1], carry[1][1]).astype(dv_ref.dtype)
        if scatter:
            @pl.when((pl.program_id(0) == heads // 2 - 1) & (j == nq - 1))
            def _():
                for cp in travel():
                    cp.wait()

    blk = pl.BlockSpec((tb, LANES), lambda h, j: (j, h))
    row_spec = pl.BlockSpec((1, 2, 1, t_dim), lambda h, j: (h, 0, 0, 0))
    row_blk = pl.BlockSpec((1, 2, 1, tb), lambda h, j: (h, 0, 0, j))
    hbm = pl.BlockSpec(memory_space=pl.ANY)
    outs = pl.pallas_call(
        body, name=name, grid=(heads // 2, nq),
        in_specs=[pl.BlockSpec(memory_space=pltpu.SMEM),
                  pl.BlockSpec((t_dim, LANES), lambda h, j: (0, h)),
                  pl.BlockSpec((tb, LANES), lambda h, j: (j, cb + h)),
                  pl.BlockSpec((tb, LANES), lambda h, j: (j, 2 * cb + h)),
                  pl.BlockSpec((t_dim, LANES), lambda h, j: (0, h)),
                  row_blk, row_spec, row_spec] + [hbm] * n_w,
        out_specs=[blk, blk, row_blk] + [hbm] * n_w,
        out_shape=[jax.ShapeDtypeStruct((t_dim, d), BF16), jax.ShapeDtypeStruct((t_dim, d), BF16),
                   jax.ShapeDtypeStruct((heads // 2, 2, 1, t_dim), F32)] + scatter_shapes,
        scratch_shapes=_scatter_sems(n_w) if scatter else [],
        compiler_params=_params(dimension_semantics=("arbitrary", "arbitrary")),
    )(prune, qkv, qkv, qkv, do, c_rows, lse_rows, delta_rows, *[arr for arr, _ in scatter])
    return outs[0], outs[1], outs[2], outs[3:]


def _shift_down(z, prev, n, tt):
    out = pltpu.roll(z, n, axis=0)
    row = lax.broadcasted_iota(jnp.int32, z.shape, 0)
    for r in range(n):
        out = jnp.where(row == r, prev[8 - n + r:8 - n + r + 1, :], out)
    return out


def _shift_up(z, nxt, n, tt):
    out = pltpu.roll(z, tt - n, axis=0)
    row = lax.broadcasted_iota(jnp.int32, z.shape, 0)
    for r in range(n):
        out = jnp.where(row == tt - n + r, nxt[r:r + 1, :], out)
    return out


def _conv_fwd(proj, conv_w, *, name, tt=256):
    t_dim, d3 = proj.shape
    d = d3 // 3
    tt = min(tt, t_dim)

    def body(p_ref, prev_ref, w_ref, y_ref):
        i = pl.program_id(0)
        p = p_ref[...]
        pp = prev_ref[...]
        z = p[:, d:2 * d] * p[:, 2 * d:]
        zp = jnp.where(i > 0, pp[:, d:2 * d] * pp[:, 2 * d:], 0.0)
        w = w_ref[...]
        zc = w[2:3, :] * z + w[1:2, :] * _shift_down(z, zp, 1, tt) + w[0:1, :] * _shift_down(z, zp, 2, tt)
        y_ref[...] = (p[:, :d] * zc).astype(y_ref.dtype)

    return pl.pallas_call(
        body, name=name, grid=(t_dim // tt,),
        in_specs=[pl.BlockSpec((tt, d3), lambda i: (i, 0)),
                  pl.BlockSpec((8, d3), lambda i: (jnp.maximum(i * (tt // 8) - 1, 0), 0)),
                  pl.BlockSpec(conv_w.shape, lambda i: (0, 0))],
        out_specs=pl.BlockSpec((tt, d), lambda i: (i, 0)),
        out_shape=jax.ShapeDtypeStruct((t_dim, d), BF16),
        compiler_params=_params(dimension_semantics=("arbitrary",)),
    )(proj, proj, conv_w)


def _conv_bwd(proj, dy, conv_w, *, name, tt=256):
    t_dim, d3 = proj.shape
    d = d3 // 3
    tt = min(tt, t_dim)
    n = t_dim // tt

    def body(p_ref, prev_ref, next_ref, dy_ref, dyn_ref, w_ref, dp_ref, dw_ref):
        i = pl.program_id(0)
        p = p_ref[...]
        pp = prev_ref[...]
        pn = next_ref[...]
        bg, cg, u = p[:, :d], p[:, d:2 * d], p[:, 2 * d:]
        z = cg * u
        zp = jnp.where(i > 0, pp[:, d:2 * d] * pp[:, 2 * d:], 0.0)
        w = w_ref[...]
        z1 = _shift_down(z, zp, 1, tt)
        z2 = _shift_down(z, zp, 2, tt)
        zc = w[2:3, :] * z + w[1:2, :] * z1 + w[0:1, :] * z2
        dy_blk = dy_ref[...]
        dzc = dy_blk * bg
        dzn = jnp.where(i < n - 1, dyn_ref[...] * pn[:, :d], 0.0)
        dz = w[2:3, :] * dzc + w[1:2, :] * _shift_up(dzc, dzn, 1, tt) + w[0:1, :] * _shift_up(dzc, dzn, 2, tt)
        dp_ref[:, :d] = (dy_blk * zc).astype(dp_ref.dtype)
        dp_ref[:, d:2 * d] = (dz * u).astype(dp_ref.dtype)
        dp_ref[:, 2 * d:] = (dz * cg).astype(dp_ref.dtype)
        part = jnp.concatenate([jnp.sum(dzc * z2, axis=0, keepdims=True),
                                jnp.sum(dzc * z1, axis=0, keepdims=True),
                                jnp.sum(dzc * z, axis=0, keepdims=True),
                                jnp.zeros((5, d), F32)], axis=0)

        @pl.when(i == 0)
        def _():
            dw_ref[...] = part

        @pl.when(i > 0)
        def _():
            dw_ref[...] += part

    last8 = t_dim // 8 - 1
    return pl.pallas_call(
        body, name=name, grid=(n,),
        in_specs=[pl.BlockSpec((tt, d3), lambda i: (i, 0)),
                  pl.BlockSpec((8, d3), lambda i: (jnp.maximum(i * (tt // 8) - 1, 0), 0)),
                  pl.BlockSpec((8, d3), lambda i: (jnp.minimum((i + 1) * (tt // 8), last8), 0)),
                  pl.BlockSpec((tt, d), lambda i: (i, 0)),
                  pl.BlockSpec((8, d), lambda i: (jnp.minimum((i + 1) * (tt // 8), last8), 0)),
                  pl.BlockSpec(conv_w.shape, lambda i: (0, 0))],
        out_specs=[pl.BlockSpec((tt, d3), lambda i: (i, 0)), pl.BlockSpec((8, d), lambda i: (0, 0))],
        out_shape=[jax.ShapeDtypeStruct((t_dim, d3), BF16), jax.ShapeDtypeStruct((8, d), F32)],
        compiler_params=_params(dimension_semantics=("arbitrary",)),
    )(proj, proj, proj, dy, dy, conv_w)


def _window(ref, axis, n, idx):
    if axis is None:
        return ref
    sel = [slice(None)] * len(ref.shape)
    sel[axis] = pl.ds(pl.multiple_of(idx * n, n), n)
    return ref.at[tuple(sel)]


def _scatter_shapes(items):
    widths, shapes = [], []
    for arr, axis in items:
        shp = list(arr.shape)
        if axis is not None:
            shp[axis] //= N_DEV
        widths.append(None if axis is None else shp[axis])
        shapes.append(jax.ShapeDtypeStruct((N_DEV, *shp), arr.dtype))
    return widths, shapes


def _scatter_copies(items, widths, src, dst, send_sems, recv_sems, local_sems):
    x, y, c = lax.axis_index("x"), lax.axis_index("y"), lax.axis_index("c")
    me = 4 * x + 2 * y + c
    copies = [pltpu.make_async_copy(_window(src[w], items[w][1], widths[w], me), dst[w].at[me], local_sems.at[w])
              for w in range(len(items))]
    for k in range(1, N_DEV):
        px = 1 - x if k & 4 else x
        py = 1 - y if k & 2 else y
        pc = 1 - c if k & 1 else c
        for w in range(len(items)):
            copies.append(pltpu.make_async_remote_copy(
                src_ref=_window(src[w], items[w][1], widths[w], 4 * px + 2 * py + pc), dst_ref=dst[w].at[me],
                send_sem=send_sems.at[w, k - 1], recv_sem=recv_sems.at[w, k - 1],
                device_id=(px, py, pc), device_id_type=pl.DeviceIdType.MESH))
    return copies


def _scatter_sems(n_w):
    return [pltpu.SemaphoreType.DMA((n_w, N_DEV - 1)), pltpu.SemaphoreType.DMA((n_w, N_DEV - 1)),
            pltpu.SemaphoreType.DMA((n_w,))]


def _gather_shapes(items):
    widths = [arr.shape[axis] for arr, axis in items]
    shapes = [jax.ShapeDtypeStruct(tuple(s * N_DEV if a == axis else s for a, s in enumerate(arr.shape)), arr.dtype)
              for arr, axis in items]
    return widths, shapes


def _gather_phases(items, widths, src, dst, send_sems, recv_sems, local_sems):
    n_w = len(items)

    def run(phase):
        x, y, c = lax.axis_index("x"), lax.axis_index("y"), lax.axis_index("c")
        chips = [(1 - x, y), (x, 1 - y), (1 - x, 1 - y)]

        def place(w, origin):
            return _window(dst[w], items[w][1], widths[w], 4 * origin[0] + 2 * origin[1] + origin[2])

        def block_copy(w, n, origin, to, from_shard):
            return pltpu.make_async_remote_copy(
                src_ref=src[w] if from_shard else place(w, origin), dst_ref=place(w, origin),
                send_sem=send_sems.at[w, n], recv_sem=recv_sems.at[w, n],
                device_id=to, device_id_type=pl.DeviceIdType.MESH)

        def own(w):
            return pltpu.make_async_copy(src[w], place(w, (x, y, c)), local_sems.at[w])

        def first(w):
            return ([block_copy(w, 0, (x, y, c), (x, y, 1 - c), True)]
                    + [block_copy(w, 1 + n, (x, y, c), (*chip, c), True) for n, chip in enumerate(chips)])

        def passed(w, n):
            return block_copy(w, 4 + n, (*chips[n], c), (x, y, 1 - c), False)

        if phase == "start":
            for w in range(n_w):
                own(w).start()
                for cp in first(w):
                    cp.start()
        elif phase == "relay":
            for n, chip in enumerate(chips):
                for w in range(n_w):
                    block_copy(w, 1 + n, (*chip, c), (x, y, c), True).wait_recv()
                    passed(w, n).start()
        else:
            for w in range(n_w):
                block_copy(w, 0, (x, y, 1 - c), (x, y, c), True).wait_recv()
                for n, chip in enumerate(chips):
                    block_copy(w, 4 + n, (*chip, 1 - c), (x, y, c), False).wait_recv()
                for cp in first(w) + [passed(w, n) for n in range(3)]:
                    cp.wait_send()
                own(w).wait()

    return [functools.partial(run, phase) for phase in ("start", "relay", "finish")]


def _exchange(items, *, gather, name):
    n_w = len(items)
    widths, out_shape = _gather_shapes(items) if gather else _scatter_shapes(items)

    def body(*refs):
        src, dst = refs[:n_w], refs[n_w:2 * n_w]
        send_sems, recv_sems, local_sems = refs[2 * n_w:]
        if not gather:
            copies = _scatter_copies(items, widths, src, dst, send_sems, recv_sems, local_sems)
            for cp in copies:
                cp.start()
            for cp in copies:
                cp.wait()
            return
        for phase in _gather_phases(items, widths, src, dst, send_sems, recv_sems, local_sems):
            phase()

    return pl.pallas_call(
        body, name=name,
        in_specs=[pl.BlockSpec(memory_space=pl.ANY)] * n_w,
        out_specs=[pl.BlockSpec(memory_space=pl.ANY)] * n_w,
        out_shape=out_shape,
        scratch_shapes=_scatter_sems(n_w),
    )(*[arr for arr, _ in items])


def _row_tile(rows, cols):
    tr = rows
    while tr % 16 == 0 and tr * cols > 256 * 1024:
        tr //= 2
    return tr


def _sum_parts(parts, *, name):
    n_parts, rows, cols = parts.shape
    tr = _row_tile(rows, cols)

    def body(p_ref, o_ref):
        g = p_ref[0].astype(F32)
        for s in range(1, n_parts):
            g = g + p_ref[s].astype(F32)
        o_ref[...] = g

    return pl.pallas_call(
        body, name=name, grid=(rows // tr,),
        in_specs=[pl.BlockSpec((n_parts, tr, cols), lambda i: (0, i, 0))],
        out_specs=pl.BlockSpec((tr, cols), lambda i: (i, 0)),
        out_shape=jax.ShapeDtypeStruct((rows, cols), F32),
        compiler_params=_params(dimension_semantics=("parallel",)),
    )(parts)


def _adamw(parts, w, m, v, *, name):
    n_parts, rows, cols = parts.shape
    tr = _row_tile(rows, cols)

    def body(p_ref, w_ref, m_ref, v_ref, g_ref, d_ref, nm_ref, nv_ref):
        g = p_ref[0].astype(F32)
        for s in range(1, n_parts):
            g = g + p_ref[s].astype(F32)
        m_new = ADAM_B1 * m_ref[...] + (1.0 - ADAM_B1) * g
        v_new = ADAM_B2 * v_ref[...] + (1.0 - ADAM_B2) * (g * g)
        m_hat = m_new / (1.0 - ADAM_B1 ** ADAM_STEP)
        v_hat = v_new / (1.0 - ADAM_B2 ** ADAM_STEP)
        g_ref[...] = g
        d_ref[...] = -ADAM_LR * (m_hat / (jnp.sqrt(v_hat) + ADAM_EPS) + ADAM_WD * w_ref[...])
        nm_ref[...] = m_new
        nv_ref[...] = v_new

    spec = pl.BlockSpec((tr, cols), lambda i: (i, 0))
    return pl.pallas_call(
        body, name=name, grid=(rows // tr,),
        in_specs=[pl.BlockSpec((n_parts, tr, cols), lambda i: (0, i, 0)), spec, spec, spec],
        out_specs=[spec] * 4,
        out_shape=[jax.ShapeDtypeStruct((rows, cols), F32)] * 4,
        compiler_params=_params(dimension_semantics=("parallel",)),
    )(parts, w, m, v)


def _pad_rows(a, axis, to):
    pad = [(0, 0)] * a.ndim
    pad[axis] = (0, to - a.shape[axis])
    return jnp.pad(a, pad)


def kernel(x, p, norm_g, w_attn_in, b_forget, w_attn_out, w_conv_in, conv_w, w_conv_out, w_mlp_up, w_mlp_down, w_ple_proj, w_ple_gate, loss_target, m_norm_g, m_w_attn_in, m_b_forget, m_w_attn_out, m_w_conv_in, m_conv_w, m_w_conv_out, m_w_mlp_up, m_w_mlp_down, m_w_ple_proj, m_w_ple_gate, v_norm_g, v_w_attn_in, v_b_forget, v_w_attn_out, v_w_conv_in, v_conv_w, v_w_conv_out, v_w_mlp_up, v_w_mlp_down, v_w_ple_proj, v_w_ple_gate):
    shards = dict(norm_g=norm_g, w_attn_in=w_attn_in, b_forget=b_forget, w_attn_out=w_attn_out,
                  w_conv_in=w_conv_in, conv_w=conv_w, w_conv_out=w_conv_out, w_mlp_up=w_mlp_up,
                  w_mlp_down=w_mlp_down, w_ple_proj=w_ple_proj, w_ple_gate=w_ple_gate)
    m_shards = dict(norm_g=m_norm_g, w_attn_in=m_w_attn_in, b_forget=m_b_forget, w_attn_out=m_w_attn_out,
                    w_conv_in=m_w_conv_in, conv_w=m_conv_w, w_conv_out=m_w_conv_out, w_mlp_up=m_w_mlp_up,
                    w_mlp_down=m_w_mlp_down, w_ple_proj=m_w_ple_proj, w_ple_gate=m_w_ple_gate)
    v_shards = dict(norm_g=v_norm_g, w_attn_in=v_w_attn_in, b_forget=v_b_forget, w_attn_out=v_w_attn_out,
                    w_conv_in=v_w_conv_in, conv_w=v_conv_w, w_conv_out=v_w_conv_out, w_mlp_up=v_w_mlp_up,
                    w_mlp_down=v_w_mlp_down, w_ple_proj=v_w_ple_proj, w_ple_gate=v_w_ple_gate)
    t_dim, d = x.shape[-2:]
    depth = p.shape[0]
    n_attn, heads = b_forget.shape
    assert d == heads * HEAD_DIM and x.shape[0] == 1
    tb = min(512, t_dim // 2)
    x0 = x.reshape(t_dim, d)
    target = loss_target.reshape(t_dim, d)
    p_rows = p.reshape(depth * t_dim, p.shape[-1])

    in_cols = w_attn_in.shape[2]
    in_cols_pad = -(-in_cols // 16) * 16
    first_names = ['norm_g', 'w_attn_in', 'conv_w']
    rest_names = [n for n in WEIGHT_NAMES if n not in first_names + ['b_forget']]

    def gather_item(n):
        if n == 'w_attn_in':
            return _pad_rows(jnp.swapaxes(w_attn_in, 1, 2), 1, in_cols_pad).astype(BF16), 1
        return (shards[n] if n in ('norm_g', 'conv_w') else shards[n].astype(BF16)), SHARD_AXIS[n]

    full = dict(zip(first_names, _exchange([gather_item(n) for n in first_names], gather=True, name="gather_first")))
    gains = full['norm_g']
    taps = full['conv_w']
    w_in_t = full['w_attn_in'].reshape(n_attn, N_DEV, in_cols_pad, d)[:, :, :in_cols]
    w_in_t = _pad_rows(w_in_t.reshape(n_attn, N_DEV * in_cols, d), 1, 3 * d + LANES)
    bias_pad = jnp.pad(b_forget, ((0, 0), (0, LANES - heads)))

    def gain(i, k):
        return gains[i, k].reshape(1, d)

    def add_norm(x_prev, branch, g_branch, g_next, name):
        def fn(rows, vecs):
            x_new = rows[0] + _norm(rows[1], vecs[0])
            return [x_new, _norm(x_new, vecs[1])], []
        return _rows_call(fn, [x_prev, branch], [g_branch, g_next], [(d, F32), (d, BF16)], [], name=name)

    saved = []
    x_cur = x0
    hn = _rows_call(lambda rows, vecs: ([_norm(rows[0], vecs[0])], []), [x0], [gain(0, 0)], [(d, BF16)], [],
                    name="norm_in")[0]
    loss_rows = dy = None
    for i in range(depth):
        j = i // 2
        s = dict(x0=x_cur, hn=hn)
        if i % 2 == 0:
            s['qkv'] = _mm(hn, w_in_t[j, :3 * d], tb=True, out_dtypes=(BF16,), name=f"attn_in_{i}")
            s['fl'] = _mm(hn, w_in_t[j, 3 * d:], tb=True, name=f"attn_gate_{i}")
            c = _cumsum_fwd(s['fl'], bias_pad[j:j + 1], name=f"gate_cumsum_{i}")
            c_t = c[:, :heads].T
            s['prune'] = _prune_table(c_t, _head_norms(s['qkv'], name=f"head_norms_{i}"), tb)
            s['c_rows'] = c_t.reshape(heads // 2, 2, 1, t_dim)
            s['o'], s['lse_hb'], s['lse_rows'], rest = _flash_fwd(
                s['qkv'], s['c_rows'], s['prune'], tb=tb, name=f"attn_fwd_{i}",
                gather=[gather_item(n) for n in rest_names] if i == 0 else ())
            if i == 0:
                full.update(zip(rest_names, rest))
            s['m'] = _mm(s['o'], full['w_attn_out'][j], name=f"attn_out_{i}")
        else:
            s['proj'] = _mm(hn, full['w_conv_in'][j], name=f"conv_in_{i}")
            s['y'] = _conv_fwd(s['proj'], taps[j], name=f"conv_fwd_{i}")
            s['m'] = _mm(s['y'], full['w_conv_out'][j], name=f"conv_out_{i}")
        s['x1'], s['h2'] = add_norm(x_cur, s['m'], gain(i, 1), gain(i, 2), f"mix_norm_{i}")
        s['u'], s['a'] = _mm(s['h2'], full['w_mlp_up'][i], out_dtypes=(BF16, BF16), name=f"mlp_up_{i}",
                             epi=lambda acc: (acc, jnp.square(jnp.maximum(acc, 0.0))))
        s['f'] = _mm(s['a'], full['w_mlp_down'][i], name=f"mlp_down_{i}")
        s['x2'], s['h4'] = add_norm(s['x1'], s['f'], gain(i, 3), gain(i, 4), f"mlp_norm_{i}")
        s['pp'] = _mm(p_rows, full['w_ple_proj'][i], a_rows=t_dim, a_off=i * t_dim, name=f"ple_proj_{i}")
        s['gl'], s['e'] = _mm(s['h4'], full['w_ple_gate'][i], extras=(s['pp'],), out_dtypes=(F32, F32),
                              name=f"ple_gate_{i}", epi=lambda acc, pp: (acc, pp * _sigmoid(acc)))
        if i + 1 < depth:
            x_cur, hn = add_norm(s['x2'], s['e'], gain(i, 5), gain(i + 1, 0), f"ple_norm_{i}")
        else:
            def loss_fn(rows, vecs):
                err = rows[0] + _norm(rows[1], vecs[0]) - rows[2]
                part = 0.5 * jnp.sum(jnp.sum(err * err, axis=1, keepdims=True), axis=0, keepdims=True) / d
                return [err / d], [jnp.broadcast_to(part, (1, LANES))]
            dy, loss_rows = _rows_call(loss_fn, [s['x2'], s['e'], target], [gain(i, 5)], [(d, F32)],
                                       [(1, LANES)], name="loss")
        saved.append(s)
    loss = lax.psum(loss_rows[0, 0], ("x", "y", "c"))

    grads = {n: [None] * shards[n].shape[0] for n in WEIGHT_NAMES}
    d_gains = [[None] * 6 for _ in range(depth)]
    wgrad = functools.partial(_mm, ta=True, out_dtypes=(BF16,))
    dgrad = functools.partial(_mm, out_dtypes=(BF16,))
    axis_of = dict(SHARD_AXIS, w_attn_in=1)

    def in_t_blocks(layers):
        g = jnp.stack(layers)[:, :N_DEV * in_cols].reshape(len(layers), N_DEV, in_cols, d)
        return _pad_rows(g, 2, in_cols_pad).reshape(len(layers), N_DEV * in_cols_pad, d)

    early_names = early_items = early = None
    dx = dy
    for i in reversed(range(depth)):
        j = i // 2
        s = saved[i]

        def ple_fn(rows, vecs):
            de, dg = _norm_bwd(rows[0], vecs[0], rows[1])
            sg = _sigmoid(rows[2])
            return [de * sg, de * rows[3] * sg * (1.0 - sg)], [dg]
        dpp, dgl, d_gains[i][5] = _rows_call(ple_fn, [s['e'], dx, s['gl'], s['pp']], [gain(i, 5)],
                                             [(d, BF16), (d, BF16)], [(1, d)], name=f"ple_bwd_{i}")
        grads['w_ple_proj'][i] = wgrad(p_rows, dpp, a_rows=t_dim, a_off=i * t_dim, name=f"ple_proj_dw_{i}")
        grads['w_ple_gate'][i] = wgrad(s['h4'], dgl, name=f"ple_gate_dw_{i}")
        dh4 = dgrad(dgl, full['w_ple_gate'][i], tb=True, name=f"ple_gate_dx_{i}")

        def two_norm_bwd(x_res, dh, dx_in, branch, g_res, g_branch, name):
            def fn(rows, vecs):
                d_res, dg_res = _norm_bwd(rows[0], vecs[0], rows[1])
                dx_out = rows[2] + d_res
                d_branch, dg_branch = _norm_bwd(rows[3], vecs[1], dx_out)
                return [dx_out, d_branch], [dg_res, dg_branch]
            return _rows_call(fn, [x_res, dh, dx_in, branch], [g_res, g_branch], [(d, F32), (d, BF16)],
                              [(1, d), (1, d)], name=name)

        dx2, df, d_gains[i][4], d_gains[i][3] = two_norm_bwd(s['x2'], dh4, dx, s['f'], gain(i, 4), gain(i, 3),
                                                            f"mlp_norm_bwd_{i}")
        grads['w_mlp_down'][i] = wgrad(s['a'], df, name=f"mlp_down_dw_{i}")
        du = _mm(df, full['w_mlp_down'][i], tb=True, extras=(s['u'],), out_dtypes=(BF16,), name=f"mlp_down_dx_{i}",
                 epi=lambda acc, u: (acc * (2.0 * jnp.maximum(u.astype(F32), 0.0)),))
        grads['w_mlp_up'][i] = wgrad(s['h2'], du, name=f"mlp_up_dw_{i}")
        dh2 = dgrad(du, full['w_mlp_up'][i], tb=True, name=f"mlp_up_dx_{i}")
        dx1, dm, d_gains[i][2], d_gains[i][1] = two_norm_bwd(s['x1'], dh2, dx2, s['m'], gain(i, 2), gain(i, 1),
                                                            f"mix_norm_bwd_{i}")
        if i % 2 == 0:
            grads['w_attn_out'][j] = wgrad(s['o'], dm, name=f"attn_out_dw_{i}")
            do = _mm(dm, full['w_attn_out'][j], tb=True, out_dtypes=(BF16,), name=f"attn_out_dx_{i}")
            dq, delta_rows, rsum_rows = _flash_dq(s['qkv'], s['o'], do, s['c_rows'], s['lse_hb'],
                                                  s['prune'], tb=tb, name=f"attn_dq_{i}")
            if i == 0:
                early_names = [n for n in WEIGHT_NAMES if n not in ('norm_g', 'b_forget')]
                early_items = [(in_t_blocks(grads[n][1:]) if n == 'w_attn_in' else jnp.stack(grads[n]), axis_of[n])
                               for n in early_names]
            dk, dv, csum_rows, early = _flash_dkv(s['qkv'], do, s['c_rows'], s['lse_rows'], delta_rows, s['prune'],
                                                  tb=tb, name=f"attn_dkv_{i}", scatter=early_items if i == 0 else ())
            dc = (rsum_rows - csum_rows).reshape(heads, t_dim).T
            dfl, db = _cumsum_bwd(jnp.pad(dc, ((0, 0), (0, LANES - heads))), s['fl'], bias_pad[j:j + 1],
                                  name=f"gate_cumsum_bwd_{i}")
            grads['b_forget'][j] = db[0, :heads]
            dproj = jnp.concatenate([dq, dk, dv, dfl], axis=1)
            grads['w_attn_in'][j] = wgrad(dproj, s['hn'], name=f"attn_in_dw_{i}")
            dhn = dgrad(dproj, w_in_t[j], name=f"attn_in_dx_{i}")
        else:
            grads['w_conv_out'][j] = wgrad(s['y'], dm, name=f"conv_out_dw_{i}")
            dyc = _mm(dm, full['w_conv_out'][j], tb=True, name=f"conv_out_dx_{i}")
            dproj, dtaps = _conv_bwd(s['proj'], dyc, taps[j], name=f"conv_bwd_{i}")
            grads['conv_w'][j] = dtaps[:3].astype(BF16)
            grads['w_conv_in'][j] = wgrad(s['hn'], dproj, name=f"conv_in_dw_{i}")
            dhn = dgrad(dproj, full['w_conv_in'][j], tb=True, name=f"conv_in_dx_{i}")

        def in_fn(rows, vecs):
            d_res, dg = _norm_bwd(rows[0], vecs[0], rows[1])
            return [rows[2] + d_res], [dg]
        dx, d_gains[i][0] = _rows_call(in_fn, [s['x0'], dhn, dx1], [gain(i, 0)], [(d, F32)], [(1, d)],
                                       name=f"in_norm_bwd_{i}")
    grad_x = dx.reshape(x.shape)

    recv = dict(zip(early_names, early))
    late = _exchange([(in_t_blocks(grads['w_attn_in'][:1]), 1),
                      (jnp.stack([jnp.concatenate(row, axis=0) for row in d_gains]).astype(BF16), SHARD_AXIS['norm_g']),
                      (jnp.zeros((8, LANES), F32).at[:n_attn, :heads].set(jnp.stack(grads['b_forget'])), None)],
                     gather=False, name="exchange_late")
    recv['norm_g'] = late[1]
    recv['b_forget'] = late[2][:, :n_attn, :heads]
    g_in_t = jnp.concatenate([_sum_parts(part.reshape(N_DEV, -1, d), name=f"sum_attn_in_{k}")
                              for k, part in enumerate((late[0], recv['w_attn_in']))], axis=0)
    recv['w_attn_in'] = jnp.swapaxes(g_in_t.reshape(n_attn, in_cols_pad, d)[:, :in_cols], 1, 2)[None]
    results = {}
    for n in WEIGHT_NAMES:
        shp = shards[n].shape
        flat = lambda a: a.reshape(a.shape[:a.ndim - len(shp)] + (-1, shp[-1]))
        outs = _adamw(flat(recv[n]), flat(shards[n]), flat(m_shards[n]), flat(v_shards[n]), name=f"adamw_{n}")
        results[n] = [o.reshape(shp) for o in outs]
    return (loss, grad_x, *[results[n][k] for k in range(4) for n in WEIGHT_NAMES])
```

```python
import functools

import jax
import jax.numpy as jnp
from jax import lax
from jax.experimental import pallas as pl
from jax.experimental.pallas import tpu as pltpu

F32 = jnp.float32
BF16 = jnp.bfloat16

N_DEV = 8
LANES = 128
HEAD_DIM = 64
VMEM_LIMIT_BYTES = 56 * 1024 * 1024
RMS_EPS = 1e-6
NEG_INF = -1e30
ADAM_LR = 0.001
ADAM_B1 = 0.9
ADAM_B2 = 0.999
ADAM_EPS = 1e-08
ADAM_WD = 0.01
ADAM_STEP = 10
WEIGHT_NAMES = ('norm_g', 'w_attn_in', 'b_forget', 'w_attn_out', 'w_conv_in', 'conv_w', 'w_conv_out',
                'w_mlp_up', 'w_mlp_down', 'w_ple_proj', 'w_ple_gate')
SHARD_AXIS = {'norm_g': 2, 'w_attn_in': 2, 'b_forget': None, 'w_attn_out': 1, 'w_conv_in': 2, 'conv_w': 2,
              'w_conv_out': 1, 'w_mlp_up': 2, 'w_mlp_down': 1, 'w_ple_proj': 2, 'w_ple_gate': 1}


def _params(**kw):
    return pltpu.CompilerParams(vmem_limit_bytes=VMEM_LIMIT_BYTES, **kw)


def _tile(n, cap):
    if n <= cap:
        return n
    t = (cap // LANES) * LANES
    while n % t:
        t -= LANES
    return t


MM_VMEM_BUDGET = 36 * 1024 * 1024


def _mm(a, b, *, ta=False, tb=False, extras=(), epi=None, out_dtypes=(F32,), name, a_rows=None, a_off=0):
    rows_a = a_rows or a.shape[0]
    m_dim, k_dim = (a.shape[1], rows_a) if ta else (rows_a, a.shape[1])
    n_dim = b.shape[0] if tb else b.shape[1]
    assert k_dim == (b.shape[1] if tb else b.shape[0]) and a_off % rows_a == 0
    tk = _tile(k_dim, 1024 if k_dim <= 1024 else 2048)
    nk = k_dim // tk
    tn = _tile(n_dim, 1024)

    def vmem_bytes(tm):
        per_mn = sum(jnp.dtype(dt).itemsize for dt in out_dtypes) + sum(e.dtype.itemsize for e in extras)
        return (2 * (tm * tk * a.dtype.itemsize + tk * tn * b.dtype.itemsize) + 2 * tm * tn * per_mn
                + tm * tn * 4 * (2 + (nk > 1)))

    tm = next(t for t in (_tile(m_dim, 1024), _tile(m_dim, 512)) if t <= 512 or vmem_bytes(t) <= MM_VMEM_BUDGET)
    grid = (n_dim // tn, m_dim // tm, nk)
    off_m, off_k = (0, a_off // tk) if ta else (a_off // tm, 0)
    a_spec = (pl.BlockSpec((tk, tm), lambda j, i, k: (k + off_k, i)) if ta
              else pl.BlockSpec((tm, tk), lambda j, i, k: (i + off_m, k)))
    b_spec = (pl.BlockSpec((tn, tk), lambda j, i, k: (j, k)) if tb
              else pl.BlockSpec((tk, tn), lambda j, i, k: (k, j)))
    mn_spec = pl.BlockSpec((tm, tn), lambda j, i, k: (i, j))
    dims = (((0 if ta else 1,), (1 if tb else 0,)), ((), ()))
    n_extra, n_out = len(extras), len(out_dtypes)
    if epi is None:
        epi = lambda acc: (acc,)

    def body(a_ref, b_ref, *rest):
        e_refs, o_refs = rest[:n_extra], rest[n_extra:n_extra + n_out]
        part = lax.dot_general(a_ref[...].astype(BF16), b_ref[...].astype(BF16), dims,
                               preferred_element_type=F32)

        def finish(acc):
            for o_ref, val in zip(o_refs, epi(acc, *[e[...] for e in e_refs])):
                o_ref[...] = val.astype(o_ref.dtype)

        if nk == 1:
            finish(part)
        else:
            acc_ref = rest[-1]
            k = pl.program_id(2)

            @pl.when(k == 0)
            def _():
                acc_ref[...] = part

            @pl.when(k > 0)
            def _():
                acc_ref[...] += part

            @pl.when(k == nk - 1)
            def _():
                finish(acc_ref[...])

    outs = pl.pallas_call(
        body, name=name, grid=grid,
        in_specs=[a_spec, b_spec] + [mn_spec] * n_extra,
        out_specs=[mn_spec] * n_out,
        out_shape=[jax.ShapeDtypeStruct((m_dim, n_dim), dt) for dt in out_dtypes],
        scratch_shapes=[pltpu.VMEM((tm, tn), F32)] if nk > 1 else [],
        compiler_params=_params(dimension_semantics=("parallel", "parallel", "arbitrary")),
    )(a, b, *extras)
    return outs[0] if n_out == 1 else outs


def _rows(fn, row_ins, vec_ins, row_outs, vec_outs, *, name, tt=512, reverse=False):
    t_dim = row_ins[0].shape[0]
    tt = min(tt, t_dim)
    n = t_dim // tt
    n_ri, n_vi, n_ro, n_vo = len(row_ins), len(vec_ins), len(row_outs), len(vec_outs)
    pos = (lambda i: (n - 1 - i, 0)) if reverse else (lambda i: (i, 0))
    fixed = lambda i: (0, 0)

    def body(*refs):
        ri = refs[:n_ri]
        vi = refs[n_ri:n_ri + n_vi]
        ro = refs[n_ri + n_vi:n_ri + n_vi + n_ro]
        vo = refs[n_ri + n_vi + n_ro:n_ri + n_vi + n_ro + n_vo]
        scratch = refs[n_ri + n_vi + n_ro + n_vo:]
        r_out, v_out = fn([r[...] for r in ri], [v[...] for v in vi], *scratch)
        for o_ref, val in zip(ro, r_out):
            o_ref[...] = val.astype(o_ref.dtype)
        i = pl.program_id(0)
        for o_ref, val in zip(vo, v_out):
            @pl.when(i == 0)
            def _(o_ref=o_ref, val=val):
                o_ref[...] = val

            @pl.when(i > 0)
            def _(o_ref=o_ref, val=val):
                o_ref[...] += val

    return body, dict(
        grid=(n,),
        in_specs=[pl.BlockSpec((tt, r.shape[1]), pos) for r in row_ins]
        + [pl.BlockSpec(v.shape, fixed) for v in vec_ins],
        out_specs=[pl.BlockSpec((tt, w), pos) for w, _ in row_outs]
        + [pl.BlockSpec(s, fixed) for s in vec_outs],
        out_shape=[jax.ShapeDtypeStruct((t_dim, w), dt) for w, dt in row_outs]
        + [jax.ShapeDtypeStruct(s, F32) for s in vec_outs],
        name=name,
        compiler_params=_params(dimension_semantics=("arbitrary",)),
    )


def _rows_call(fn, row_ins, vec_ins, row_outs, vec_outs, *, name, tt=512, reverse=False, scratch=()):
    body, kw = _rows(fn, row_ins, vec_ins, row_outs, vec_outs, name=name, tt=tt, reverse=reverse)
    return pl.pallas_call(body, scratch_shapes=list(scratch), **kw)(*row_ins, *vec_ins)


def _rstd(x):
    return lax.rsqrt(jnp.mean(x * x, axis=-1, keepdims=True) + RMS_EPS)


def _norm(x, g):
    return x * _rstd(x) * g


def _norm_bwd(x, g, dy):
    xh = x * _rstd(x)
    gy = dy * g
    dx = _rstd(x) * (gy - xh * jnp.mean(gy * xh, axis=-1, keepdims=True))
    return dx, jnp.sum(dy * xh, axis=0, keepdims=True)


def _sigmoid(x):
    return 1.0 / (1.0 + jnp.exp(-x))


def _log_sigmoid(x):
    return jnp.minimum(x, 0.0) - jnp.log(1.0 + jnp.exp(-jnp.abs(x)))


def _split3(x):
    hi = x.astype(BF16)
    r1 = x - hi.astype(F32)
    mid = r1.astype(BF16)
    lo = (r1 - mid.astype(F32)).astype(BF16)
    return hi, mid, lo


def _cumsum_fwd(fl, bias, *, name):
    w = fl.shape[1]
    tt = min(512, fl.shape[0])

    def fn(rows, vecs, carry_ref):
        i = pl.program_id(0)

        @pl.when(i == 0)
        def _():
            carry_ref[...] = jnp.zeros_like(carry_ref)

        lf = _log_sigmoid(rows[0] + vecs[0])
        r = lax.broadcasted_iota(jnp.int32, (tt, tt), 0)
        c = lax.broadcasted_iota(jnp.int32, (tt, tt), 1)
        tri = (c <= r).astype(BF16)
        acc = carry_ref[0:1, :]
        for part in _split3(lf):
            acc = acc + jnp.dot(tri, part, preferred_element_type=F32)
        carry_ref[0:1, :] = acc[tt - 1:tt, :]
        return [acc], []

    return _rows_call(fn, [fl], [bias], [(w, F32)], [], name=name, tt=tt,
                      scratch=[pltpu.VMEM((8, w), F32)])[0]


def _cumsum_bwd(dc, fl, bias, *, name):
    w = fl.shape[1]
    tt = min(512, fl.shape[0])

    def fn(rows, vecs, carry_ref):
        i = pl.program_id(0)

        @pl.when(i == 0)
        def _():
            carry_ref[...] = jnp.zeros_like(carry_ref)

        r = lax.broadcasted_iota(jnp.int32, (tt, tt), 0)
        c = lax.broadcasted_iota(jnp.int32, (tt, tt), 1)
        tri = (c >= r).astype(BF16)
        acc = carry_ref[0:1, :]
        for part in _split3(rows[0]):
            acc = acc + jnp.dot(tri, part, preferred_element_type=F32)
        carry_ref[0:1, :] = acc[0:1, :]
        dfl = acc * _sigmoid(-(rows[1] + vecs[0]))
        return [dfl], [jnp.sum(dfl, axis=0, keepdims=True)]

    return _rows_call(fn, [dc, fl], [bias], [(w, BF16)], [(1, w)], name=name, tt=tt, reverse=True,
                      scratch=[pltpu.VMEM((8, w), F32)])


def _head_masks(tb):
    lane = lax.broadcasted_iota(jnp.int32, (tb, LANES), 1)
    return [lane < HEAD_DIM, lane >= HEAD_DIM]


PRUNE_MARGIN = 30.0


def _head_norms(qkv, *, name):
    t_dim = qkv.shape[0]
    d = qkv.shape[1] // 3
    heads = d // HEAD_DIM
    tt = min(512, t_dim)

    def body(q_ref, k_ref, o_ref):
        col = lax.broadcasted_iota(jnp.int32, (d, LANES), 0) // HEAD_DIM
        lane = lax.broadcasted_iota(jnp.int32, (d, LANES), 1)
        tile_max = None
        for ref, first in ((q_ref, 0), (k_ref, heads)):
            x = ref[...].astype(F32)
            sums = jnp.dot((x * x).astype(BF16), (col + first == lane).astype(BF16), preferred_element_type=F32)
            part = jnp.max(sums, axis=0, keepdims=True)
            tile_max = part if tile_max is None else jnp.maximum(tile_max, part)
        i = pl.program_id(0)

        @pl.when(i == 0)
        def _():
            o_ref[...] = tile_max

        @pl.when(i > 0)
        def _():
            o_ref[...] = jnp.maximum(o_ref[...], tile_max)

    return pl.pallas_call(
        body, name=name, grid=(t_dim // tt,),
        in_specs=[pl.BlockSpec((tt, d), lambda i: (i, 0)), pl.BlockSpec((tt, d), lambda i: (i, 1))],
        out_specs=pl.BlockSpec((1, LANES), lambda i: (0, 0)),
        out_shape=jax.ShapeDtypeStruct((1, LANES), F32),
        compiler_params=_params(dimension_semantics=("arbitrary",)),
    )(qkv, qkv)


def _prune_table(c_t, norms, tb):
    heads = c_t.shape[0]
    bound = 1.02 * HEAD_DIM ** -0.5 * jnp.sqrt(norms[0, :heads] * norms[0, heads:2 * heads])
    return jnp.concatenate([c_t[:, ::tb], c_t[:, tb - 1::tb], -(PRUNE_MARGIN + 2.0 * bound)[:, None]], axis=1)


def _kept_before(prune_ref, h, i, nq):
    first, thr = prune_ref[h, i], prune_ref[h, 2 * nq]
    return lax.while_loop(lambda n: (n < i) & (first - prune_ref[h, nq + jnp.maximum(i - 1 - n, 0)] >= thr),
                          lambda n: n + 1, jnp.int32(0))


def _kept_after(prune_ref, h, j, nq):
    last, thr = prune_ref[h, nq + j], prune_ref[h, 2 * nq]
    return lax.while_loop(lambda n: (j + 1 + n < nq) & (prune_ref[h, jnp.minimum(j + 1 + n, nq - 1)] - last >= thr),
                          lambda n: n + 1, jnp.int32(0))


def _as_row(col, tb):
    return jnp.transpose(jnp.broadcast_to(col, (tb, LANES)))[0:1, :]


def _flash_fwd(qkv, c_rows, prune, *, tb, name, gather=()):
    t_dim = qkv.shape[0]
    d = qkv.shape[1] // 3
    heads = d // HEAD_DIM
    cb = d // LANES
    nq = t_dim // tb
    n_w = len(gather)
    widths, gather_shapes = _gather_shapes(gather)

    def body(prune_ref, q_ref, k_ref, v_ref, cr_ref, *rest):
        src, (o_ref, lse_ref, lser_ref), dst = rest[:n_w], rest[n_w:n_w + 3], rest[n_w + 3:2 * n_w + 3]
        i = pl.program_id(1)
        h0 = 2 * pl.program_id(0)
        if gather:
            start, relay, finish = _gather_phases(gather, widths, src, dst, *rest[2 * n_w + 3:])
            pl.when((pl.program_id(0) == 0) & (i == 0))(start)
            pl.when((pl.program_id(0) == (3 * heads) // 8) & (i == 0))(relay)
        q = q_ref[...] * jnp.asarray(HEAD_DIM ** -0.5, BF16)
        masks = _head_masks(tb)
        row = lax.broadcasted_iota(jnp.int32, (tb, tb), 0)
        col = lax.broadcasted_iota(jnp.int32, (tb, tb), 1)
        qs = [jnp.where(masks[e], q, jnp.zeros_like(q)) for e in range(2)]

        def step(j, carry, diagonal):
            off = pl.multiple_of(j * tb, tb)
            kj = k_ref[pl.ds(off, tb), :]
            vj = v_ref[pl.ds(off, tb), :]
            out = []
            for e in range(2):
                m, l, acc = carry[e]
                crow = cr_ref[0, e, :, pl.ds(off, tb)]
                s = lax.dot_general(qs[e], kj, (((1,), (1,)), ((), ())), preferred_element_type=F32) - crow
                if diagonal:
                    s = jnp.where(col <= row, s, NEG_INF)
                m_new = jnp.maximum(m, jnp.max(s, axis=1, keepdims=True))
                p = jnp.exp(s - m_new)
                alpha = jnp.exp(m - m_new)
                l = alpha * l + jnp.sum(p, axis=1, keepdims=True)
                acc = alpha * acc + jnp.dot(p.astype(BF16), vj, preferred_element_type=F32)
                out.append((m_new, l, acc))
            return tuple(out)

        init = (jnp.full((tb, 1), NEG_INF, F32), jnp.zeros((tb, 1), F32), jnp.zeros((tb, LANES), F32))
        kept = jnp.maximum(_kept_before(prune_ref, h0, i, nq), _kept_before(prune_ref, h0 + 1, i, nq))
        carry = lax.fori_loop(i - kept, i, functools.partial(step, diagonal=False), (init, init))
        carry = step(i, carry, True)
        outs = []
        for e in range(2):
            m, l, acc = carry[e]
            outs.append(acc / l)
            lse = m + jnp.log(l)
            lse_ref[0, e] = jnp.broadcast_to(lse, (tb, LANES))
            lser_ref[0, e] = _as_row(lse, tb)
        o_ref[...] = jnp.where(masks[0], outs[0], outs[1]).astype(o_ref.dtype)
        if gather:
            pl.when((pl.program_id(0) == heads // 2 - 1) & (i == nq - 1))(finish)

    hb_spec = pl.BlockSpec((1, 2, tb, LANES), lambda h, i: (h, 0, i, 0))
    row_spec = pl.BlockSpec((1, 2, 1, t_dim), lambda h, i: (h, 0, 0, 0))
    row_blk = pl.BlockSpec((1, 2, 1, tb), lambda h, i: (h, 0, 0, i))
    hbm = pl.BlockSpec(memory_space=pl.ANY)
    outs = pl.pallas_call(
        body, name=name, grid=(heads // 2, nq),
        in_specs=[pl.BlockSpec(memory_space=pltpu.SMEM),
                  pl.BlockSpec((tb, LANES), lambda h, i: (i, h)),
                  pl.BlockSpec((t_dim, LANES), lambda h, i: (0, cb + h)),
                  pl.BlockSpec((t_dim, LANES), lambda h, i: (0, 2 * cb + h)),
                  row_spec] + [hbm] * n_w,
        out_specs=[pl.BlockSpec((tb, LANES), lambda h, i: (i, h)), hb_spec, row_blk] + [hbm] * n_w,
        out_shape=[jax.ShapeDtypeStruct((t_dim, d), BF16),
                   jax.ShapeDtypeStruct((heads // 2, 2, t_dim, LANES), F32),
                   jax.ShapeDtypeStruct((heads // 2, 2, 1, t_dim), F32)] + gather_shapes,
        scratch_shapes=_scatter_sems(n_w) if gather else [],
        compiler_params=_params(dimension_semantics=("arbitrary", "arbitrary")),
    )(prune, qkv, qkv, qkv, c_rows, *[arr for arr, _ in gather])
    return outs[0], outs[1], outs[2], outs[3:]


def _flash_dq(qkv, o, do, c_rows, lse_hb, prune, *, tb, name):
    t_dim = qkv.shape[0]
    d = qkv.shape[1] // 3
    heads = d // HEAD_DIM
    cb = d // LANES
    scale = HEAD_DIM ** -0.5
    nq = t_dim // tb

    def body(prune_ref, q_ref, k_ref, v_ref, o_ref, do_ref, cr_ref, lse_ref, dq_ref, dl_ref, rs_ref):
        i = pl.program_id(1)
        h0 = 2 * pl.program_id(0)
        q = q_ref[...] * jnp.asarray(scale, BF16)
        do_blk = do_ref[...]
        prod = do_blk.astype(F32) * o_ref[...].astype(F32)
        masks = _head_masks(tb)
        row = lax.broadcasted_iota(jnp.int32, (tb, tb), 0)
        col = lax.broadcasted_iota(jnp.int32, (tb, tb), 1)
        qs = [jnp.where(masks[e], q, jnp.zeros_like(q)) for e in range(2)]
        dos = [jnp.where(masks[e], do_blk, jnp.zeros_like(do_blk)) for e in range(2)]
        deltas = [jnp.sum(jnp.where(masks[e], prod, 0.0), axis=1, keepdims=True) for e in range(2)]
        lses = [lse_ref[0, e][:, 0:1] for e in range(2)]

        def step(j, carry, diagonal):
            off = pl.multiple_of(j * tb, tb)
            kj = k_ref[pl.ds(off, tb), :]
            vj = v_ref[pl.ds(off, tb), :]
            out = []
            for e in range(2):
                acc, rsum = carry[e]
                crow = cr_ref[0, e, :, pl.ds(off, tb)]
                s = lax.dot_general(qs[e], kj, (((1,), (1,)), ((), ())), preferred_element_type=F32) - crow
                if diagonal:
                    s = jnp.where(col <= row, s, NEG_INF)
                p = jnp.exp(s - lses[e])
                dp = lax.dot_general(dos[e], vj, (((1,), (1,)), ((), ())), preferred_element_type=F32)
                ds = p * (dp - deltas[e])
                out.append((acc + jnp.dot(ds.astype(BF16), kj, preferred_element_type=F32),
                            rsum + jnp.sum(ds, axis=1, keepdims=True)))
            return tuple(out)

        init = (jnp.zeros((tb, LANES), F32), jnp.zeros((tb, 1), F32))
        kept = jnp.maximum(_kept_before(prune_ref, h0, i, nq), _kept_before(prune_ref, h0 + 1, i, nq))
        carry = lax.fori_loop(i - kept, i, functools.partial(step, diagonal=False), (init, init))
        carry = step(i, carry, True)
        for e in range(2):
            dl_ref[0, e] = _as_row(deltas[e], tb)
            rs_ref[0, e] = _as_row(carry[e][1], tb)
        dq_ref[...] = (jnp.where(masks[0], carry[0][0], carry[1][0]) * scale).astype(dq_ref.dtype)

    blk = pl.BlockSpec((tb, LANES), lambda h, i: (i, h))
    hb_spec = pl.BlockSpec((1, 2, tb, LANES), lambda h, i: (h, 0, i, 0))
    row_spec = pl.BlockSpec((1, 2, 1, t_dim), lambda h, i: (h, 0, 0, 0))
    row_blk = pl.BlockSpec((1, 2, 1, tb), lambda h, i: (h, 0, 0, i))
    row_shape = jax.ShapeDtypeStruct((heads // 2, 2, 1, t_dim), F32)
    return pl.pallas_call(
        body, name=name, grid=(heads // 2, nq),
        in_specs=[pl.BlockSpec(memory_space=pltpu.SMEM), blk,
                  pl.BlockSpec((t_dim, LANES), lambda h, i: (0, cb + h)),
                  pl.BlockSpec((t_dim, LANES), lambda h, i: (0, 2 * cb + h)),
                  blk, blk, row_spec, hb_spec],
        out_specs=[blk, row_blk, row_blk],
        out_shape=[jax.ShapeDtypeStruct((t_dim, d), BF16), row_shape, row_shape],
        compiler_params=_params(dimension_semantics=("parallel", "arbitrary")),
    )(prune, qkv, qkv, qkv, o, do, c_rows, lse_hb)


def _flash_dkv(qkv, do, c_rows, lse_rows, delta_rows, prune, *, tb, name, scatter=()):
    t_dim = qkv.shape[0]
    d = qkv.shape[1] // 3
    heads = d // HEAD_DIM
    cb = d // LANES
    scale = HEAD_DIM ** -0.5
    nq = t_dim // tb
    n_w = len(scatter)
    widths, scatter_shapes = _scatter_shapes(scatter)

    def body(prune_ref, q_ref, k_ref, v_ref, do_ref, cc_ref, lr_ref, dr_ref, *rest):
        src, (dk_ref, dv_ref, dsum_ref), dst = rest[:n_w], rest[n_w:n_w + 3], rest[n_w + 3:2 * n_w + 3]
        j = pl.program_id(1)
        h0 = 2 * pl.program_id(0)
        if scatter:
            travel = lambda: _scatter_copies(scatter, widths, src, dst, *rest[2 * n_w + 3:])

            @pl.when((pl.program_id(0) == 0) & (j == 0))
            def _():
                for cp in travel():
                    cp.start()
        k_blk = k_ref[...] * jnp.asarray(scale, BF16)
        v_blk = v_ref[...]
        masks = _head_masks(tb)
        row = lax.broadcasted_iota(jnp.int32, (tb, tb), 0)
        col = lax.broadcasted_iota(jnp.int32, (tb, tb), 1)
        ks = [jnp.where(masks[e], k_blk, jnp.zeros_like(k_blk)) for e in range(2)]
        vs = [jnp.where(masks[e], v_blk, jnp.zeros_like(v_blk)) for e in range(2)]
        ccols = [jnp.transpose(jnp.broadcast_to(cc_ref[0, e], (LANES, tb)))[:, 0:1] for e in range(2)]

        def step(i, carry, diagonal):
            off = pl.multiple_of(i * tb, tb)
            qi = q_ref[pl.ds(off, tb), :]
            doi = do_ref[pl.ds(off, tb), :]
            out = []
            for e in range(2):
                dk, dv, dsum = carry[e]
                lse = lr_ref[0, e, :, pl.ds(off, tb)]
                delta = dr_ref[0, e, :, pl.ds(off, tb)]
                st = lax.dot_general(ks[e], qi, (((1,), (1,)), ((), ())), preferred_element_type=F32) - ccols[e]
                if diagonal:
                    st = jnp.where(col >= row, st, NEG_INF)
                pt = jnp.exp(st - lse)
                dpt = lax.dot_general(vs[e], doi, (((1,), (1,)), ((), ())), preferred_element_type=F32)
                dst = pt * (dpt - delta)
                out.append((dk + jnp.dot(dst.astype(BF16), qi, preferred_element_type=F32),
                            dv + jnp.dot(pt.astype(BF16), doi, preferred_element_type=F32),
                            dsum + jnp.sum(dst, axis=1, keepdims=True)))
            return tuple(out)

        zero = jnp.zeros((tb, LANES), F32)
        init = (zero, zero, jnp.zeros((tb, 1), F32))
        carry = step(j, (init, init), True)
        kept = jnp.maximum(_kept_after(prune_ref, h0, j, nq), _kept_after(prune_ref, h0 + 1, j, nq))
        carry = lax.fori_loop(j + 1, j + 1 + kept, functools.partial(step, diagonal=False), carry)
        for e in range(2):
            dsum_ref[0, e] = _as_row(carry[e][2], tb)
        dk_ref[...] = (jnp.where(masks[0], carry[0][0], carry[1][0]) * scale).astype(dk_ref.dtype)
        dv_ref[...] = jnp.where(masks[0], carry[0][1], carry[1][1]).astype(dv_ref.dtype)
        if scatter:
            @pl.when((pl.program_id(0) == heads // 2 - 1) & (j == nq - 1))
            def _():
                for cp in travel():
                    cp.wait()

    blk = pl.BlockSpec((tb, LANES), lambda h, j: (j, h))
    row_spec = pl.BlockSpec((1, 2, 1, t_dim), lambda h, j: (h, 0, 0, 0))
    row_blk = pl.BlockSpec((1, 2, 1, tb), lambda h, j: (h, 0, 0, j))
    hbm = pl.BlockSpec(memory_space=pl.ANY)
    outs = pl.pallas_call(
        body, name=name, grid=(heads // 2, nq),
        in_specs=[pl.BlockSpec(memory_space=pltpu.SMEM),
                  pl.BlockSpec((t_dim, LANES), lambda h, j: (0, h)),
                  pl.BlockSpec((tb, LANES), lambda h, j: (j, cb + h)),
                  pl.BlockSpec((tb, LANES), lambda h, j: (j, 2 * cb + h)),
                  pl.BlockSpec((t_dim, LANES), lambda h, j: (0, h)),
                  row_blk, row_spec, row_spec] + [hbm] * n_w,
        out_specs=[blk, blk, row_blk] + [hbm] * n_w,
        out_shape=[jax.ShapeDtypeStruct((t_dim, d), BF16), jax.ShapeDtypeStruct((t_dim, d), BF16),
                   jax.ShapeDtypeStruct((heads // 2, 2, 1, t_dim), F32)] + scatter_shapes,
        scratch_shapes=_scatter_sems(n_w) if scatter else [],
        compiler_params=_params(dimension_semantics=("arbitrary", "arbitrary")),
    )(prune, qkv, qkv, qkv, do, c_rows, lse_rows, delta_rows, *[arr for arr, _ in scatter])
    return outs[0], outs[1], outs[2], outs[3:]


def _shift_down(z, prev, n, tt):
    out = pltpu.roll(z, n, axis=0)
    row = lax.broadcasted_iota(jnp.int32, z.shape, 0)
    for r in range(n):
        out = jnp.where(row == r, prev[8 - n + r:8 - n + r + 1, :], out)
    return out


def _shift_up(z, nxt, n, tt):
    out = pltpu.roll(z, tt - n, axis=0)
    row = lax.broadcasted_iota(jnp.int32, z.shape, 0)
    for r in range(n):
        out = jnp.where(row == tt - n + r, nxt[r:r + 1, :], out)
    return out


def _conv_fwd(proj, conv_w, *, name, tt=256):
    t_dim, d3 = proj.shape
    d = d3 // 3
    tt = min(tt, t_dim)

    def body(p_ref, prev_ref, w_ref, y_ref):
        i = pl.program_id(0)
        p = p_ref[...]
        pp = prev_ref[...]
        z = p[:, d:2 * d] * p[:, 2 * d:]
        zp = jnp.where(i > 0, pp[:, d:2 * d] * pp[:, 2 * d:], 0.0)
        w = w_ref[...]
        zc = w[2:3, :] * z + w[1:2, :] * _shift_down(z, zp, 1, tt) + w[0:1, :] * _shift_down(z, zp, 2, tt)
        y_ref[...] = (p[:, :d] * zc).astype(y_ref.dtype)

    return pl.pallas_call(
        body, name=name, grid=(t_dim // tt,),
        in_specs=[pl.BlockSpec((tt, d3), lambda i: (i, 0)),
                  pl.BlockSpec((8, d3), lambda i: (jnp.maximum(i * (tt // 8) - 1, 0), 0)),
                  pl.BlockSpec(conv_w.shape, lambda i: (0, 0))],
        out_specs=pl.BlockSpec((tt, d), lambda i: (i, 0)),
        out_shape=jax.ShapeDtypeStruct((t_dim, d), BF16),
        compiler_params=_params(dimension_semantics=("arbitrary",)),
    )(proj, proj, conv_w)


def _conv_bwd(proj, dy, conv_w, *, name, tt=256):
    t_dim, d3 = proj.shape
    d = d3 // 3
    tt = min(tt, t_dim)
    n = t_dim // tt

    def body(p_ref, prev_ref, next_ref, dy_ref, dyn_ref, w_ref, dp_ref, dw_ref):
        i = pl.program_id(0)
        p = p_ref[...]
        pp = prev_ref[...]
        pn = next_ref[...]
        bg, cg, u = p[:, :d], p[:, d:2 * d], p[:, 2 * d:]
        z = cg * u
        zp = jnp.where(i > 0, pp[:, d:2 * d] * pp[:, 2 * d:], 0.0)
        w = w_ref[...]
        z1 = _shift_down(z, zp, 1, tt)
        z2 = _shift_down(z, zp, 2, tt)
        zc = w[2:3, :] * z + w[1:2, :] * z1 + w[0:1, :] * z2
        dy_blk = dy_ref[...]
        dzc = dy_blk * bg
        dzn = jnp.where(i < n - 1, dyn_ref[...] * pn[:, :d], 0.0)
        dz = w[2:3, :] * dzc + w[1:2, :] * _shift_up(dzc, dzn, 1, tt) + w[0:1, :] * _shift_up(dzc, dzn, 2, tt)
        dp_ref[:, :d] = (dy_blk * zc).astype(dp_ref.dtype)
        dp_ref[:, d:2 * d] = (dz * u).astype(dp_ref.dtype)
        dp_ref[:, 2 * d:] = (dz * cg).astype(dp_ref.dtype)
        part = jnp.concatenate([jnp.sum(dzc * z2, axis=0, keepdims=True),
                                jnp.sum(dzc * z1, axis=0, keepdims=True),
                                jnp.sum(dzc * z, axis=0, keepdims=True),
                                jnp.zeros((5, d), F32)], axis=0)

        @pl.when(i == 0)
        def _():
            dw_ref[...] = part

        @pl.when(i > 0)
        def _():
            dw_ref[...] += part

    last8 = t_dim // 8 - 1
    return pl.pallas_call(
        body, name=name, grid=(n,),
        in_specs=[pl.BlockSpec((tt, d3), lambda i: (i, 0)),
                  pl.BlockSpec((8, d3), lambda i: (jnp.maximum(i * (tt // 8) - 1, 0), 0)),
                  pl.BlockSpec((8, d3), lambda i: (jnp.minimum((i + 1) * (tt // 8), last8), 0)),
                  pl.BlockSpec((tt, d), lambda i: (i, 0)),
                  pl.BlockSpec((8, d), lambda i: (jnp.minimum((i + 1) * (tt // 8), last8), 0)),
                  pl.BlockSpec(conv_w.shape, lambda i: (0, 0))],
        out_specs=[pl.BlockSpec((tt, d3), lambda i: (i, 0)), pl.BlockSpec((8, d), lambda i: (0, 0))],
        out_shape=[jax.ShapeDtypeStruct((t_dim, d3), BF16), jax.ShapeDtypeStruct((8, d), F32)],
        compiler_params=_params(dimension_semantics=("arbitrary",)),
    )(proj, proj, proj, dy, dy, conv_w)


def _window(ref, axis, n, idx):
    if axis is None:
        return ref
    sel = [slice(None)] * len(ref.shape)
    sel[axis] = pl.ds(pl.multiple_of(idx * n, n), n)
    return ref.at[tuple(sel)]


def _scatter_shapes(items):
    widths, shapes = [], []
    for arr, axis in items:
        shp = list(arr.shape)
        if axis is not None:
            shp[axis] //= N_DEV
        widths.append(None if axis is None else shp[axis])
        shapes.append(jax.ShapeDtypeStruct((N_DEV, *shp), arr.dtype))
    return widths, shapes


def _scatter_copies(items, widths, src, dst, send_sems, recv_sems, local_sems):
    x, y, c = lax.axis_index("x"), lax.axis_index("y"), lax.axis_index("c")
    me = 4 * x + 2 * y + c
    copies = [pltpu.make_async_copy(_window(src[w], items[w][1], widths[w], me), dst[w].at[me], local_sems.at[w])
              for w in range(len(items))]
    for k in range(1, N_DEV):
        px = 1 - x if k & 4 else x
        py = 1 - y if k & 2 else y
        pc = 1 - c if k & 1 else c
        for w in range(len(items)):
            copies.append(pltpu.make_async_remote_copy(
                src_ref=_window(src[w], items[w][1], widths[w], 4 * px + 2 * py + pc), dst_ref=dst[w].at[me],
                send_sem=send_sems.at[w, k - 1], recv_sem=recv_sems.at[w, k - 1],
                device_id=(px, py, pc), device_id_type=pl.DeviceIdType.MESH))
    return copies


def _scatter_sems(n_w):
    return [pltpu.SemaphoreType.DMA((n_w, N_DEV - 1)), pltpu.SemaphoreType.DMA((n_w, N_DEV - 1)),
            pltpu.SemaphoreType.DMA((n_w,))]


def _gather_shapes(items):
    widths = [arr.shape[axis] for arr, axis in items]
    shapes = [jax.ShapeDtypeStruct(tuple(s * N_DEV if a == axis else s for a, s in enumerate(arr.shape)), arr.dtype)
              for arr, axis in items]
    return widths, shapes


def _gather_phases(items, widths, src, dst, send_sems, recv_sems, local_sems):
    n_w = len(items)

    def run(phase):
        x, y, c = lax.axis_index("x"), lax.axis_index("y"), lax.axis_index("c")
        chips = [(1 - x, y), (x, 1 - y), (1 - x, 1 - y)]

        def place(w, origin):
            return _window(dst[w], items[w][1], widths[w], 4 * origin[0] + 2 * origin[1] + origin[2])

        def block_copy(w, n, origin, to, from_shard):
            return pltpu.make_async_remote_copy(
                src_ref=src[w] if from_shard else place(w, origin), dst_ref=place(w, origin),
                send_sem=send_sems.at[w, n], recv_sem=recv_sems.at[w, n],
                device_id=to, device_id_type=pl.DeviceIdType.MESH)

        def own(w):
            return pltpu.make_async_copy(src[w], place(w, (x, y, c)), local_sems.at[w])

        def first(w):
            return ([block_copy(w, 0, (x, y, c), (x, y, 1 - c), True)]
                    + [block_copy(w, 1 + n, (x, y, c), (*chip, c), True) for n, chip in enumerate(chips)])

        def passed(w, n):
            return block_copy(w, 4 + n, (*chips[n], c), (x, y, 1 - c), False)

        if phase == "start":
            for w in range(n_w):
                own(w).start()
                for cp in first(w):
                    cp.start()
        elif phase == "relay":
            for n, chip in enumerate(chips):
                for w in range(n_w):
                    block_copy(w, 1 + n, (*chip, c), (x, y, c), True).wait_recv()
                    passed(w, n).start()
        else:
            for w in range(n_w):
                block_copy(w, 0, (x, y, 1 - c), (x, y, c), True).wait_recv()
                for n, chip in enumerate(chips):
                    block_copy(w, 4 + n, (*chip, 1 - c), (x, y, c), False).wait_recv()
                for cp in first(w) + [passed(w, n) for n in range(3)]:
                    cp.wait_send()
                own(w).wait()

    return [functools.partial(run, phase) for phase in ("start", "relay", "finish")]


def _exchange(items, *, gather, name):
    n_w = len(items)
    widths, out_shape = _gather_shapes(items) if gather else _scatter_shapes(items)

    def body(*refs):
        src, dst = refs[:n_w], refs[n_w:2 * n_w]
        send_sems, recv_sems, local_sems = refs[2 * n_w:]
        if not gather:
            copies = _scatter_copies(items, widths, src, dst, send_sems, recv_sems, local_sems)
            for cp in copies:
                cp.start()
            for cp in copies:
                cp.wait()
            return
        for phase in _gather_phases(items, widths, src, dst, send_sems, recv_sems, local_sems):
            phase()

    return pl.pallas_call(
        body, name=name,
        in_specs=[pl.BlockSpec(memory_space=pl.ANY)] * n_w,
        out_specs=[pl.BlockSpec(memory_space=pl.ANY)] * n_w,
        out_shape=out_shape,
        scratch_shapes=_scatter_sems(n_w),
    )(*[arr for arr, _ in items])


def _row_tile(rows, cols):
    tr = rows
    while tr % 16 == 0 and tr * cols > 256 * 1024:
        tr //= 2
    return tr


def _sum_parts(parts, *, name):
    n_parts, rows, cols = parts.shape
    tr = _row_tile(rows, cols)

    def body(p_ref, o_ref):
        g = p_ref[0].astype(F32)
        for s in range(1, n_parts):
            g = g + p_ref[s].astype(F32)
        o_ref[...] = g

    return pl.pallas_call(
        body, name=name, grid=(rows // tr,),
        in_specs=[pl.BlockSpec((n_parts, tr, cols), lambda i: (0, i, 0))],
        out_specs=pl.BlockSpec((tr, cols), lambda i: (i, 0)),
        out_shape=jax.ShapeDtypeStruct((rows, cols), F32),
        compiler_params=_params(dimension_semantics=("parallel",)),
    )(parts)


def _adamw(parts, w, m, v, *, name):
    n_parts, rows, cols = parts.shape
    tr = _row_tile(rows, cols)

    def body(p_ref, w_ref, m_ref, v_ref, g_ref, d_ref, nm_ref, nv_ref):
        g = p_ref[0].astype(F32)
        for s in range(1, n_parts):
            g = g + p_ref[s].astype(F32)
        m_new = ADAM_B1 * m_ref[...] + (1.0 - ADAM_B1) * g
        v_new = ADAM_B2 * v_ref[...] + (1.0 - ADAM_B2) * (g * g)
        m_hat = m_new / (1.0 - ADAM_B1 ** ADAM_STEP)
        v_hat = v_new / (1.0 - ADAM_B2 ** ADAM_STEP)
        g_ref[...] = g
        d_ref[...] = -ADAM_LR * (m_hat / (jnp.sqrt(v_hat) + ADAM_EPS) + ADAM_WD * w_ref[...])
        nm_ref[...] = m_new
        nv_ref[...] = v_new

    spec = pl.BlockSpec((tr, cols), lambda i: (i, 0))
    return pl.pallas_call(
        body, name=name, grid=(rows // tr,),
        in_specs=[pl.BlockSpec((n_parts, tr, cols), lambda i: (0, i, 0)), spec, spec, spec],
        out_specs=[spec] * 4,
        out_shape=[jax.ShapeDtypeStruct((rows, cols), F32)] * 4,
        compiler_params=_params(dimension_semantics=("parallel",)),
    )(parts, w, m, v)


def _pad_rows(a, axis, to):
    pad = [(0, 0)] * a.ndim
    pad[axis] = (0, to - a.shape[axis])
    return jnp.pad(a, pad)


def kernel(x, p, norm_g, w_attn_in, b_forget, w_attn_out, w_conv_in, conv_w, w_conv_out, w_mlp_up, w_mlp_down, w_ple_proj, w_ple_gate, loss_target, m_norm_g, m_w_attn_in, m_b_forget, m_w_attn_out, m_w_conv_in, m_conv_w, m_w_conv_out, m_w_mlp_up, m_w_mlp_down, m_w_ple_proj, m_w_ple_gate, v_norm_g, v_w_attn_in, v_b_forget, v_w_attn_out, v_w_conv_in, v_conv_w, v_w_conv_out, v_w_mlp_up, v_w_mlp_down, v_w_ple_proj, v_w_ple_gate):
    shards = dict(norm_g=norm_g, w_attn_in=w_attn_in, b_forget=b_forget, w_attn_out=w_attn_out,
                  w_conv_in=w_conv_in, conv_w=conv_w, w_conv_out=w_conv_out, w_mlp_up=w_mlp_up,
                  w_mlp_down=w_mlp_down, w_ple_proj=w_ple_proj, w_ple_gate=w_ple_gate)
    m_shards = dict(norm_g=m_norm_g, w_attn_in=m_w_attn_in, b_forget=m_b_forget, w_attn_out=m_w_attn_out,
                    w_conv_in=m_w_conv_in, conv_w=m_conv_w, w_conv_out=m_w_conv_out, w_mlp_up=m_w_mlp_up,
                    w_mlp_down=m_w_mlp_down, w_ple_proj=m_w_ple_proj, w_ple_gate=m_w_ple_gate)
    v_shards = dict(norm_g=v_norm_g, w_attn_in=v_w_attn_in, b_forget=v_b_forget, w_attn_out=v_w_attn_out,
                    w_conv_in=v_w_conv_in, conv_w=v_conv_w, w_conv_out=v_w_conv_out, w_mlp_up=v_w_mlp_up,
                    w_mlp_down=v_w_mlp_down, w_ple_proj=v_w_ple_proj, w_ple_gate=v_w_ple_gate)
    t_dim, d = x.shape[-2:]
    depth = p.shape[0]
    n_attn, heads = b_forget.shape
    assert d == heads * HEAD_DIM and x.shape[0] == 1
    tb = min(512, t_dim // 2)
    x0 = x.reshape(t_dim, d)
    target = loss_target.reshape(t_dim, d)
    p_rows = p.reshape(depth * t_dim, p.shape[-1])

    in_cols = w_attn_in.shape[2]
    in_cols_pad = -(-in_cols // 16) * 16
    first_names = ['norm_g', 'w_attn_in', 'conv_w']
    rest_names = [n for n in WEIGHT_NAMES if n not in first_names + ['b_forget']]

    def gather_item(n):
        if n == 'w_attn_in':
            return _pad_rows(jnp.swapaxes(w_attn_in, 1, 2), 1, in_cols_pad).astype(BF16), 1
        return (shards[n] if n in ('norm_g', 'conv_w') else shards[n].astype(BF16)), SHARD_AXIS[n]

    full = dict(zip(first_names, _exchange([gather_item(n) for n in first_names], gather=True, name="gather_first")))
    gains = full['norm_g']
    taps = full['conv_w']
    w_in_t = full['w_attn_in'].reshape(n_attn, N_DEV, in_cols_pad, d)[:, :, :in_cols]
    w_in_t = _pad_rows(w_in_t.reshape(n_attn, N_DEV * in_cols, d), 1, 3 * d + LANES)
    bias_pad = jnp.pad(b_forget, ((0, 0), (0, LANES - heads)))

    def gain(i, k):
        return gains[i, k].reshape(1, d)

    def add_norm(x_prev, branch, g_branch, g_next, name):
        def fn(rows, vecs):
            x_new = rows[0] + _norm(rows[1], vecs[0])
            return [x_new, _norm(x_new, vecs[1])], []
        return _rows_call(fn, [x_prev, branch], [g_branch, g_next], [(d, F32), (d, BF16)], [], name=name)

    saved = []
    x_cur = x0
    hn = _rows_call(lambda rows, vecs: ([_norm(rows[0], vecs[0])], []), [x0], [gain(0, 0)], [(d, BF16)], [],
                    name="norm_in")[0]
    loss_rows = dy = None
    for i in range(depth):
        j = i // 2
        s = dict(x0=x_cur, hn=hn)
        if i % 2 == 0:
            s['qkv'] = _mm(hn, w_in_t[j, :3 * d], tb=True, out_dtypes=(BF16,), name=f"attn_in_{i}")
            s['fl'] = _mm(hn, w_in_t[j, 3 * d:], tb=True, name=f"attn_gate_{i}")
            c = _cumsum_fwd(s['fl'], bias_pad[j:j + 1], name=f"gate_cumsum_{i}")
            c_t = c[:, :heads].T
            s['prune'] = _prune_table(c_t, _head_norms(s['qkv'], name=f"head_norms_{i}"), tb)
            s['c_rows'] = c_t.reshape(heads // 2, 2, 1, t_dim)
            s['o'], s['lse_hb'], s['lse_rows'], rest = _flash_fwd(
                s['qkv'], s['c_rows'], s['prune'], tb=tb, name=f"attn_fwd_{i}",
                gather=[gather_item(n) for n in rest_names] if i == 0 else ())
            if i == 0:
                full.update(zip(rest_names, rest))
            s['m'] = _mm(s['o'], full['w_attn_out'][j], name=f"attn_out_{i}")
        else:
            s['proj'] = _mm(hn, full['w_conv_in'][j], name=f"conv_in_{i}")
            s['y'] = _conv_fwd(s['proj'], taps[j], name=f"conv_fwd_{i}")
            s['m'] = _mm(s['y'], full['w_conv_out'][j], name=f"conv_out_{i}")
        s['x1'], s['h2'] = add_norm(x_cur, s['m'], gain(i, 1), gain(i, 2), f"mix_norm_{i}")
        s['u'], s['a'] = _mm(s['h2'], full['w_mlp_up'][i], out_dtypes=(BF16, BF16), name=f"mlp_up_{i}",
                             epi=lambda acc: (acc, jnp.square(jnp.maximum(acc, 0.0))))
        s['f'] = _mm(s['a'], full['w_mlp_down'][i], name=f"mlp_down_{i}")
        s['x2'], s['h4'] = add_norm(s['x1'], s['f'], gain(i, 3), gain(i, 4), f"mlp_norm_{i}")
        s['pp'] = _mm(p_rows, full['w_ple_proj'][i], a_rows=t_dim, a_off=i * t_dim, name=f"ple_proj_{i}")
        s['gl'], s['e'] = _mm(s['h4'], full['w_ple_gate'][i], extras=(s['pp'],), out_dtypes=(F32, F32),
                              name=f"ple_gate_{i}", epi=lambda acc, pp: (acc, pp * _sigmoid(acc)))
        if i + 1 < depth:
            x_cur, hn = add_norm(s['x2'], s['e'], gain(i, 5), gain(i + 1, 0), f"ple_norm_{i}")
        else:
            def loss_fn(rows, vecs):
                err = rows[0] + _norm(rows[1], vecs[0]) - rows[2]
                part = 0.5 * jnp.sum(jnp.sum(err * err, axis=1, keepdims=True), axis=0, keepdims=True) / d
                return [err / d], [jnp.broadcast_to(part, (1, LANES))]
            dy, loss_rows = _rows_call(loss_fn, [s['x2'], s['e'], target], [gain(i, 5)], [(d, F32)],
                                       [(1, LANES)], name="loss")
        saved.append(s)
    loss = lax.psum(loss_rows[0, 0], ("x", "y", "c"))

    grads = {n: [None] * shards[n].shape[0] for n in WEIGHT_NAMES}
    d_gains = [[None] * 6 for _ in range(depth)]
    wgrad = functools.partial(_mm, ta=True, out_dtypes=(BF16,))
    dgrad = functools.partial(_mm, out_dtypes=(BF16,))
    axis_of = dict(SHARD_AXIS, w_attn_in=1)

    def in_t_blocks(layers):
        g = jnp.stack(layers)[:, :N_DEV * in_cols].reshape(len(layers), N_DEV, in_cols, d)
        return _pad_rows(g, 2, in_cols_pad).reshape(len(layers), N_DEV * in_cols_pad, d)

    early_names = early_items = early = None
    dx = dy
    for i in reversed(range(depth)):
        j = i // 2
        s = saved[i]

        def ple_fn(rows, vecs):
            de, dg = _norm_bwd(rows[0], vecs[0], rows[1])
            sg = _sigmoid(rows[2])
            return [de * sg, de * rows[3] * sg * (1.0 - sg)], [dg]
        dpp, dgl, d_gains[i][5] = _rows_call(ple_fn, [s['e'], dx, s['gl'], s['pp']], [gain(i, 5)],
                                             [(d, BF16), (d, BF16)], [(1, d)], name=f"ple_bwd_{i}")
        grads['w_ple_proj'][i] = wgrad(p_rows, dpp, a_rows=t_dim, a_off=i * t_dim, name=f"ple_proj_dw_{i}")
        grads['w_ple_gate'][i] = wgrad(s['h4'], dgl, name=f"ple_gate_dw_{i}")
        dh4 = dgrad(dgl, full['w_ple_gate'][i], tb=True, name=f"ple_gate_dx_{i}")

        def two_norm_bwd(x_res, dh, dx_in, branch, g_res, g_branch, name):
            def fn(rows, vecs):
                d_res, dg_res = _norm_bwd(rows[0], vecs[0], rows[1])
                dx_out = rows[2] + d_res
                d_branch, dg_branch = _norm_bwd(rows[3], vecs[1], dx_out)
                return [dx_out, d_branch], [dg_res, dg_branch]
            return _rows_call(fn, [x_res, dh, dx_in, branch], [g_res, g_branch], [(d, F32), (d, BF16)],
                              [(1, d), (1, d)], name=name)

        dx2, df, d_gains[i][4], d_gains[i][3] = two_norm_bwd(s['x2'], dh4, dx, s['f'], gain(i, 4), gain(i, 3),
                                                            f"mlp_norm_bwd_{i}")
        grads['w_mlp_down'][i] = wgrad(s['a'], df, name=f"mlp_down_dw_{i}")
        du = _mm(df, full['w_mlp_down'][i], tb=True, extras=(s['u'],), out_dtypes=(BF16,), name=f"mlp_down_dx_{i}",
                 epi=lambda acc, u: (acc * (2.0 * jnp.maximum(u.astype(F32), 0.0)),))
        grads['w_mlp_up'][i] = wgrad(s['h2'], du, name=f"mlp_up_dw_{i}")
        dh2 = dgrad(du, full['w_mlp_up'][i], tb=True, name=f"mlp_up_dx_{i}")
        dx1, dm, d_gains[i][2], d_gains[i][1] = two_norm_bwd(s['x1'], dh2, dx2, s['m'], gain(i, 2), gain(i, 1),
                                                            f"mix_norm_bwd_{i}")
        if i % 2 == 0:
            grads['w_attn_out'][j] = wgrad(s['o'], dm, name=f"attn_out_dw_{i}")
            do = _mm(dm, full['w_attn_out'][j], tb=True, out_dtypes=(BF16,), name=f"attn_out_dx_{i}")
            dq, delta_rows, rsum_rows = _flash_dq(s['qkv'], s['o'], do, s['c_rows'], s['lse_hb'],
                                                  s['prune'], tb=tb, name=f"attn_dq_{i}")
            if i == 0:
                early_names = [n for n in WEIGHT_NAMES if n not in ('norm_g', 'b_forget')]
                early_items = [(in_t_blocks(grads[n][1:]) if n == 'w_attn_in' else jnp.stack(grads[n]), axis_of[n])
                               for n in early_names]
            dk, dv, csum_rows, early = _flash_dkv(s['qkv'], do, s['c_rows'], s['lse_rows'], delta_rows, s['prune'],
                                                  tb=tb, name=f"attn_dkv_{i}", scatter=early_items if i == 0 else ())
            dc = (rsum_rows - csum_rows).reshape(heads, t_dim).T
            dfl, db = _cumsum_bwd(jnp.pad(dc, ((0, 0), (0, LANES - heads))), s['fl'], bias_pad[j:j + 1],
                                  name=f"gate_cumsum_bwd_{i}")
            grads['b_forget'][j] = db[0, :heads]
            dproj = jnp.concatenate([dq, dk, dv, dfl], axis=1)
            grads['w_attn_in'][j] = wgrad(dproj, s['hn'], name=f"attn_in_dw_{i}")
            dhn = dgrad(dproj, w_in_t[j], name=f"attn_in_dx_{i}")
        else:
            grads['w_conv_out'][j] = wgrad(s['y'], dm, name=f"conv_out_dw_{i}")
            dyc = _mm(dm, full['w_conv_out'][j], tb=True, name=f"conv_out_dx_{i}")
            dproj, dtaps = _conv_bwd(s['proj'], dyc, taps[j], name=f"conv_bwd_{i}")
            grads['conv_w'][j] = dtaps[:3].astype(BF16)
            grads['w_conv_in'][j] = wgrad(s['hn'], dproj, name=f"conv_in_dw_{i}")
            dhn = dgrad(dproj, full['w_conv_in'][j], tb=True, name=f"conv_in_dx_{i}")

        def in_fn(rows, vecs):
            d_res, dg = _norm_bwd(rows[0], vecs[0], rows[1])
            return [rows[2] + d_res], [dg]
        dx, d_gains[i][0] = _rows_call(in_fn, [s['x0'], dhn, dx1], [gain(i, 0)], [(d, F32)], [(1, d)],
                                       name=f"in_norm_bwd_{i}")
    grad_x = dx.reshape(x.shape)

    recv = dict(zip(early_names, early))
    late = _exchange([(in_t_blocks(grads['w_attn_in'][:1]), 1),
                      (jnp.stack([jnp.concatenate(row, axis=0) for row in d_gains]).astype(BF16), SHARD_AXIS['norm_g']),
                      (jnp.zeros((8, LANES), F32).at[:n_attn, :heads].set(jnp.stack(grads['b_forget'])), None)],
                     gather=False, name="exchange_late")
    recv['norm_g'] = late[1]
    recv['b_forget'] = late[2][:, :n_attn, :heads]
    g_in_t = jnp.concatenate([_sum_parts(part.reshape(N_DEV, -1, d), name=f"sum_attn_in_{k}")
                              for k, part in enumerate((late[0], recv['w_attn_in']))], axis=0)
    recv['w_attn_in'] = jnp.swapaxes(g_in_t.reshape(n_attn, in_cols_pad, d)[:, :in_cols], 1, 2)[None]
    results = {}
    for n in WEIGHT_NAMES:
        shp = shards[n].shape
        flat = lambda a: a.reshape(a.shape[:a.ndim - len(shp)] + (-1, shp[-1]))
        outs = _adamw(flat(recv[n]), flat(shards[n]), flat(m_shards[n]), flat(v_shards[n]), name=f"adamw_{n}")
        results[n] = [o.reshape(shp) for o in outs]
    return (loss, grad_x, *[results[n][k] for k in range(4) for n in WEIGHT_NAMES])
```

```python
import functools

import jax
import jax.numpy as jnp
from jax import lax
from jax.experimental import pallas as pl
from jax.experimental.pallas import tpu as pltpu

F32 = jnp.float32
BF16 = jnp.bfloat16

N_DEV = 8
LANES = 128
HEAD_DIM = 64
VMEM_LIMIT_BYTES = 56 * 1024 * 1024
RMS_EPS = 1e-6
NEG_INF = -1e30
ADAM_LR = 0.001
ADAM_B1 = 0.9
ADAM_B2 = 0.999
ADAM_EPS = 1e-08
ADAM_WD = 0.01
ADAM_STEP = 10
WEIGHT_NAMES = ('norm_g', 'w_attn_in', 'b_forget', 'w_attn_out', 'w_conv_in', 'conv_w', 'w_conv_out',
                'w_mlp_up', 'w_mlp_down', 'w_ple_proj', 'w_ple_gate')
SHARD_AXIS = {'norm_g': 2, 'w_attn_in': 2, 'b_forget': None, 'w_attn_out': 1, 'w_conv_in': 2, 'conv_w': 2,
              'w_conv_out': 1, 'w_mlp_up': 2, 'w_mlp_down': 1, 'w_ple_proj': 2, 'w_ple_gate': 1}


def _params(**kw):
    return pltpu.CompilerParams(vmem_limit_bytes=VMEM_LIMIT_BYTES, **kw)


def _tile(n, cap):
    if n <= cap:
        return n
    t = (cap // LANES) * LANES
    while n % t:
        t -= LANES
    return t


MM_VMEM_BUDGET = 36 * 1024 * 1024


def _mm(a, b, *, ta=False, tb=False, extras=(), epi=None, out_dtypes=(F32,), name, a_rows=None, a_off=0):
    rows_a = a_rows or a.shape[0]
    m_dim, k_dim = (a.shape[1], rows_a) if ta else (rows_a, a.shape[1])
    n_dim = b.shape[0] if tb else b.shape[1]
    assert k_dim == (b.shape[1] if tb else b.shape[0]) and a_off % rows_a == 0
    tk = _tile(k_dim, 1024 if k_dim <= 1024 else 2048)
    nk = k_dim // tk
    tn = _tile(n_dim, 1024)

    def vmem_bytes(tm):
        per_mn = sum(jnp.dtype(dt).itemsize for dt in out_dtypes) + sum(e.dtype.itemsize for e in extras)
        return (2 * (tm * tk * a.dtype.itemsize + tk * tn * b.dtype.itemsize) + 2 * tm * tn * per_mn
                + tm * tn * 4 * (2 + (nk > 1)))

    tm = next(t for t in (_tile(m_dim, 1024), _tile(m_dim, 512)) if t <= 512 or vmem_bytes(t) <= MM_VMEM_BUDGET)
    grid = (n_dim // tn, m_dim // tm, nk)
    off_m, off_k = (0, a_off // tk) if ta else (a_off // tm, 0)
    a_spec = (pl.BlockSpec((tk, tm), lambda j, i, k: (k + off_k, i)) if ta
              else pl.BlockSpec((tm, tk), lambda j, i, k: (i + off_m, k)))
    b_spec = (pl.BlockSpec((tn, tk), lambda j, i, k: (j, k)) if tb
              else pl.BlockSpec((tk, tn), lambda j, i, k: (k, j)))
    mn_spec = pl.BlockSpec((tm, tn), lambda j, i, k: (i, j))
    dims = (((0 if ta else 1,), (1 if tb else 0,)), ((), ()))
    n_extra, n_out = len(extras), len(out_dtypes)
    if epi is None:
        epi = lambda acc: (acc,)

    def body(a_ref, b_ref, *rest):
        e_refs, o_refs = rest[:n_extra], rest[n_extra:n_extra + n_out]
        part = lax.dot_general(a_ref[...].astype(BF16), b_ref[...].astype(BF16), dims,
                               preferred_element_type=F32)

        def finish(acc):
            for o_ref, val in zip(o_refs, epi(acc, *[e[...] for e in e_refs])):
                o_ref[...] = val.astype(o_ref.dtype)

        if nk == 1:
            finish(part)
        else:
            acc_ref = rest[-1]
            k = pl.program_id(2)

            @pl.when(k == 0)
            def _():
                acc_ref[...] = part

            @pl.when(k > 0)
            def _():
                acc_ref[...] += part

            @pl.when(k == nk - 1)
            def _():
                finish(acc_ref[...])

    outs = pl.pallas_call(
        body, name=name, grid=grid,
        in_specs=[a_spec, b_spec] + [mn_spec] * n_extra,
        out_specs=[mn_spec] * n_out,
        out_shape=[jax.ShapeDtypeStruct((m_dim, n_dim), dt) for dt in out_dtypes],
        scratch_shapes=[pltpu.VMEM((tm, tn), F32)] if nk > 1 else [],
        compiler_params=_params(dimension_semantics=("parallel", "parallel", "arbitrary")),
    )(a, b, *extras)
    return outs[0] if n_out == 1 else outs


def _rows(fn, row_ins, vec_ins, row_outs, vec_outs, *, name, tt=512, reverse=False):
    t_dim = row_ins[0].shape[0]
    tt = min(tt, t_dim)
    n = t_dim // tt
    n_ri, n_vi, n_ro, n_vo = len(row_ins), len(vec_ins), len(row_outs), len(vec_outs)
    pos = (lambda i: (n - 1 - i, 0)) if reverse else (lambda i: (i, 0))
    fixed = lambda i: (0, 0)

    def body(*refs):
        ri = refs[:n_ri]
        vi = refs[n_ri:n_ri + n_vi]
        ro = refs[n_ri + n_vi:n_ri + n_vi + n_ro]
        vo = refs[n_ri + n_vi + n_ro:n_ri + n_vi + n_ro + n_vo]
        scratch = refs[n_ri + n_vi + n_ro + n_vo:]
        r_out, v_out = fn([r[...] for r in ri], [v[...] for v in vi], *scratch)
        for o_ref, val in zip(ro, r_out):
            o_ref[...] = val.astype(o_ref.dtype)
        i = pl.program_id(0)
        for o_ref, val in zip(vo, v_out):
            @pl.when(i == 0)
            def _(o_ref=o_ref, val=val):
                o_ref[...] = val

            @pl.when(i > 0)
            def _(o_ref=o_ref, val=val):
                o_ref[...] += val

    return body, dict(
        grid=(n,),
        in_specs=[pl.BlockSpec((tt, r.shape[1]), pos) for r in row_ins]
        + [pl.BlockSpec(v.shape, fixed) for v in vec_ins],
        out_specs=[pl.BlockSpec((tt, w), pos) for w, _ in row_outs]
        + [pl.BlockSpec(s, fixed) for s in vec_outs],
        out_shape=[jax.ShapeDtypeStruct((t_dim, w), dt) for w, dt in row_outs]
        + [jax.ShapeDtypeStruct(s, F32) for s in vec_outs],
        name=name,
        compiler_params=_params(dimension_semantics=("arbitrary",)),
    )


def _rows_call(fn, row_ins, vec_ins, row_outs, vec_outs, *, name, tt=512, reverse=False, scratch=()):
    body, kw = _rows(fn, row_ins, vec_ins, row_outs, vec_outs, name=name, tt=tt, reverse=reverse)
    return pl.pallas_call(body, scratch_shapes=list(scratch), **kw)(*row_ins, *vec_ins)


def _rstd(x):
    return lax.rsqrt(jnp.mean(x * x, axis=-1, keepdims=True) + RMS_EPS)


def _norm(x, g):
    return x * _rstd(x) * g


def _norm_bwd(x, g, dy):
    xh = x * _rstd(x)
    gy = dy * g
    dx = _rstd(x) * (gy - xh * jnp.mean(gy * xh, axis=-1, keepdims=True))
    return dx, jnp.sum(dy * xh, axis=0, keepdims=True)


def _sigmoid(x):
    return 1.0 / (1.0 + jnp.exp(-x))


def _log_sigmoid(x):
    return jnp.minimum(x, 0.0) - jnp.log(1.0 + jnp.exp(-jnp.abs(x)))


def _split3(x):
    hi = x.astype(BF16)
    r1 = x - hi.astype(F32)
    mid = r1.astype(BF16)
    lo = (r1 - mid.astype(F32)).astype(BF16)
    return hi, mid, lo


def _cumsum_fwd(fl, bias, *, name):
    w = fl.shape[1]
    tt = min(512, fl.shape[0])

    def fn(rows, vecs, carry_ref):
        i = pl.program_id(0)

        @pl.when(i == 0)
        def _():
            carry_ref[...] = jnp.zeros_like(carry_ref)

        lf = _log_sigmoid(rows[0] + vecs[0])
        r = lax.broadcasted_iota(jnp.int32, (tt, tt), 0)
        c = lax.broadcasted_iota(jnp.int32, (tt, tt), 1)
        tri = (c <= r).astype(BF16)
        acc = carry_ref[0:1, :]
        for part in _split3(lf):
            acc = acc + jnp.dot(tri, part, preferred_element_type=F32)
        carry_ref[0:1, :] = acc[tt - 1:tt, :]
        return [acc], []

    return _rows_call(fn, [fl], [bias], [(w, F32)], [], name=name, tt=tt,
                      scratch=[pltpu.VMEM((8, w), F32)])[0]


def _cumsum_bwd(dc, fl, bias, *, name):
    w = fl.shape[1]
    tt = min(512, fl.shape[0])

    def fn(rows, vecs, carry_ref):
        i = pl.program_id(0)

        @pl.when(i == 0)
        def _():
            carry_ref[...] = jnp.zeros_like(carry_ref)

        r = lax.broadcasted_iota(jnp.int32, (tt, tt), 0)
        c = lax.broadcasted_iota(jnp.int32, (tt, tt), 1)
        tri = (c >= r).astype(BF16)
        acc = carry_ref[0:1, :]
        for part in _split3(rows[0]):
            acc = acc + jnp.dot(tri, part, preferred_element_type=F32)
        carry_ref[0:1, :] = acc[0:1, :]
        dfl = acc * _sigmoid(-(rows[1] + vecs[0]))
        return [dfl], [jnp.sum(dfl, axis=0, keepdims=True)]

    return _rows_call(fn, [dc, fl], [bias], [(w, BF16)], [(1, w)], name=name, tt=tt, reverse=True,
                      scratch=[pltpu.VMEM((8, w), F32)])


def _head_masks(tb):
    lane = lax.broadcasted_iota(jnp.int32, (tb, LANES), 1)
    return [lane < HEAD_DIM, lane >= HEAD_DIM]


PRUNE_MARGIN = 30.0


def _head_norms(qkv, *, name):
    t_dim = qkv.shape[0]
    d = qkv.shape[1] // 3
    heads = d // HEAD_DIM
    tt = min(512, t_dim)

    def body(q_ref, k_ref, o_ref):
        col = lax.broadcasted_iota(jnp.int32, (d, LANES), 0) // HEAD_DIM
        lane = lax.broadcasted_iota(jnp.int32, (d, LANES), 1)
        tile_max = None
        for ref, first in ((q_ref, 0), (k_ref, heads)):
            x = ref[...].astype(F32)
            sums = jnp.dot((x * x).astype(BF16), (col + first == lane).astype(BF16), preferred_element_type=F32)
            part = jnp.max(sums, axis=0, keepdims=True)
            tile_max = part if tile_max is None else jnp.maximum(tile_max, part)
        i = pl.program_id(0)

        @pl.when(i == 0)
        def _():
            o_ref[...] = tile_max

        @pl.when(i > 0)
        def _():
            o_ref[...] = jnp.maximum(o_ref[...], tile_max)

    return pl.pallas_call(
        body, name=name, grid=(t_dim // tt,),
        in_specs=[pl.BlockSpec((tt, d), lambda i: (i, 0)), pl.BlockSpec((tt, d), lambda i: (i, 1))],
        out_specs=pl.BlockSpec((1, LANES), lambda i: (0, 0)),
        out_shape=jax.ShapeDtypeStruct((1, LANES), F32),
        compiler_params=_params(dimension_semantics=("arbitrary",)),
    )(qkv, qkv)


def _prune_table(c_t, norms, tb):
    heads = c_t.shape[0]
    bound = 1.02 * HEAD_DIM ** -0.5 * jnp.sqrt(norms[0, :heads] * norms[0, heads:2 * heads])
    return jnp.concatenate([c_t[:, ::tb], c_t[:, tb - 1::tb], -(PRUNE_MARGIN + 2.0 * bound)[:, None]], axis=1)


def _kept_before(prune_ref, h, i, nq):
    first, thr = prune_ref[h, i], prune_ref[h, 2 * nq]
    return lax.while_loop(lambda n: (n < i) & (first - prune_ref[h, nq + jnp.maximum(i - 1 - n, 0)] >= thr),
                          lambda n: n + 1, jnp.int32(0))


def _kept_after(prune_ref, h, j, nq):
    last, thr = prune_ref[h, nq + j], prune_ref[h, 2 * nq]
    return lax.while_loop(lambda n: (j + 1 + n < nq) & (prune_ref[h, jnp.minimum(j + 1 + n, nq - 1)] - last >= thr),
                          lambda n: n + 1, jnp.int32(0))


def _as_row(col, tb):
    return jnp.transpose(jnp.broadcast_to(col, (tb, LANES)))[0:1, :]


def _flash_fwd(qkv, c_rows, prune, *, tb, name, gather=()):
    t_dim = qkv.shape[0]
    d = qkv.shape[1] // 3
    heads = d // HEAD_DIM
    cb = d // LANES
    nq = t_dim // tb
    n_w = len(gather)
    widths, gather_shapes = _gather_shapes(gather)

    def body(prune_ref, q_ref, k_ref, v_ref, cr_ref, *rest):
        src, (o_ref, lse_ref, lser_ref), dst = rest[:n_w], rest[n_w:n_w + 3], rest[n_w + 3:2 * n_w + 3]
        i = pl.program_id(1)
        h0 = 2 * pl.program_id(0)
        if gather:
            start, relay, finish = _gather_phases(gather, widths, src, dst, *rest[2 * n_w + 3:])
            pl.when((pl.program_id(0) == 0) & (i == 0))(start)
            pl.when((pl.program_id(0) == (3 * heads) // 8) & (i == 0))(relay)
        q = q_ref[...] * jnp.asarray(HEAD_DIM ** -0.5, BF16)
        masks = _head_masks(tb)
        row = lax.broadcasted_iota(jnp.int32, (tb, tb), 0)
        col = lax.broadcasted_iota(jnp.int32, (tb, tb), 1)
        qs = [jnp.where(masks[e], q, jnp.zeros_like(q)) for e in range(2)]

        def step(j, carry, diagonal):
            off = pl.multiple_of(j * tb, tb)
            kj = k_ref[pl.ds(off, tb), :]
            vj = v_ref[pl.ds(off, tb), :]
            out = []
            for e in range(2):
                m, l, acc = carry[e]
                crow = cr_ref[0, e, :, pl.ds(off, tb)]
                s = lax.dot_general(qs[e], kj, (((1,), (1,)), ((), ())), preferred_element_type=F32) - crow
                if diagonal:
                    s = jnp.where(col <= row, s, NEG_INF)
                m_new = jnp.maximum(m, jnp.max(s, axis=1, keepdims=True))
                p = jnp.exp(s - m_new)
                alpha = jnp.exp(m - m_new)
                l = alpha * l + jnp.sum(p, axis=1, keepdims=True)
                acc = alpha * acc + jnp.dot(p.astype(BF16), vj, preferred_element_type=F32)
                out.append((m_new, l, acc))
            return tuple(out)

        init = (jnp.full((tb, 1), NEG_INF, F32), jnp.zeros((tb, 1), F32), jnp.zeros((tb, LANES), F32))
        kept = jnp.maximum(_kept_before(prune_ref, h0, i, nq), _kept_before(prune_ref, h0 + 1, i, nq))
        carry = lax.fori_loop(i - kept, i, functools.partial(step, diagonal=False), (init, init))
        carry = step(i, carry, True)
        outs = []
        for e in range(2):
            m, l, acc = carry[e]
            outs.append(acc / l)
            lse = m + jnp.log(l)
            lse_ref[0, e] = jnp.broadcast_to(lse, (tb, LANES))
            lser_ref[0, e] = _as_row(lse, tb)
        o_ref[...] = jnp.where(masks[0], outs[0], outs[1]).astype(o_ref.dtype)
        if gather:
            pl.when((pl.program_id(0) == heads // 2 - 1) & (i == nq - 1))(finish)

    hb_spec = pl.BlockSpec((1, 2, tb, LANES), lambda h, i: (h, 0, i, 0))
    row_spec = pl.BlockSpec((1, 2, 1, t_dim), lambda h, i: (h, 0, 0, 0))
    row_blk = pl.BlockSpec((1, 2, 1, tb), lambda h, i: (h, 0, 0, i))
    hbm = pl.BlockSpec(memory_space=pl.ANY)
    outs = pl.pallas_call(
        body, name=name, grid=(heads // 2, nq),
        in_specs=[pl.BlockSpec(memory_space=pltpu.SMEM),
                  pl.BlockSpec((tb, LANES), lambda h, i: (i, h)),
                  pl.BlockSpec((t_dim, LANES), lambda h, i: (0, cb + h)),
                  pl.BlockSpec((t_dim, LANES), lambda h, i: (0, 2 * cb + h)),
                  row_spec] + [hbm] * n_w,
        out_specs=[pl.BlockSpec((tb, LANES), lambda h, i: (i, h)), hb_spec, row_blk] + [hbm] * n_w,
        out_shape=[jax.ShapeDtypeStruct((t_dim, d), BF16),
                   jax.ShapeDtypeStruct((heads // 2, 2, t_dim, LANES), F32),
                   jax.ShapeDtypeStruct((heads // 2, 2, 1, t_dim), F32)] + gather_shapes,
        scratch_shapes=_scatter_sems(n_w) if gather else [],
        compiler_params=_params(dimension_semantics=("arbitrary", "arbitrary")),
    )(prune, qkv, qkv, qkv, c_rows, *[arr for arr, _ in gather])
    return outs[0], outs[1], outs[2], outs[3:]


def _flash_dq(qkv, o, do, c_rows, lse_hb, prune, *, tb, name):
    t_dim = qkv.shape[0]
    d = qkv.shape[1] // 3
    heads = d // HEAD_DIM
    cb = d // LANES
    scale = HEAD_DIM ** -0.5
    nq = t_dim // tb

    def body(prune_ref, q_ref, k_ref, v_ref, o_ref, do_ref, cr_ref, lse_ref, dq_ref, dl_ref, rs_ref):
        i = pl.program_id(1)
        h0 = 2 * pl.program_id(0)
        q = q_ref[...] * jnp.asarray(scale, BF16)
        do_blk = do_ref[...]
        prod = do_blk.astype(F32) * o_ref[...].astype(F32)
        masks = _head_masks(tb)
        row = lax.broadcasted_iota(jnp.int32, (tb, tb), 0)
        col = lax.broadcasted_iota(jnp.int32, (tb, tb), 1)
        qs = [jnp.where(masks[e], q, jnp.zeros_like(q)) for e in range(2)]
        dos = [jnp.where(masks[e], do_blk, jnp.zeros_like(do_blk)) for e in range(2)]
        deltas = [jnp.sum(jnp.where(masks[e], prod, 0.0), axis=1, keepdims=True) for e in range(2)]
        lses = [lse_ref[0, e][:, 0:1] for e in range(2)]

        def step(j, carry, diagonal):
            off = pl.multiple_of(j * tb, tb)
            kj = k_ref[pl.ds(off, tb), :]
            vj = v_ref[pl.ds(off, tb), :]
            out = []
            for e in range(2):
                acc, rsum = carry[e]
                crow = cr_ref[0, e, :, pl.ds(off, tb)]
                s = lax.dot_general(qs[e], kj, (((1,), (1,)), ((), ())), preferred_element_type=F32) - crow
                if diagonal:
                    s = jnp.where(col <= row, s, NEG_INF)
                p = jnp.exp(s - lses[e])
                dp = lax.dot_general(dos[e], vj, (((1,), (1,)), ((), ())), preferred_element_type=F32)
                ds = p * (dp - deltas[e])
                out.append((acc + jnp.dot(ds.astype(BF16), kj, preferred_element_type=F32),
                            rsum + jnp.sum(ds, axis=1, keepdims=True)))
            return tuple(out)

        init = (jnp.zeros((tb, LANES), F32), jnp.zeros((tb, 1), F32))
        kept = jnp.maximum(_kept_before(prune_ref, h0, i, nq), _kept_before(prune_ref, h0 + 1, i, nq))
        carry = lax.fori_loop(i - kept, i, functools.partial(step, diagonal=False), (init, init))
        carry = step(i, carry, True)
        for e in range(2):
            dl_ref[0, e] = _as_row(deltas[e], tb)
            rs_ref[0, e] = _as_row(carry[e][1], tb)
        dq_ref[...] = (jnp.where(masks[0], carry[0][0], carry[1][0]) * scale).astype(dq_ref.dtype)

    blk = pl.BlockSpec((tb, LANES), lambda h, i: (i, h))
    hb_spec = pl.BlockSpec((1, 2, tb, LANES), lambda h, i: (h, 0, i, 0))
    row_spec = pl.BlockSpec((1, 2, 1, t_dim), lambda h, i: (h, 0, 0, 0))
    row_blk = pl.BlockSpec((1, 2, 1, tb), lambda h, i: (h, 0, 0, i))
    row_shape = jax.ShapeDtypeStruct((heads // 2, 2, 1, t_dim), F32)
    return pl.pallas_call(
        body, name=name, grid=(heads // 2, nq),
        in_specs=[pl.BlockSpec(memory_space=pltpu.SMEM), blk,
                  pl.BlockSpec((t_dim, LANES), lambda h, i: (0, cb + h)),
                  pl.BlockSpec((t_dim, LANES), lambda h, i: (0, 2 * cb + h)),
                  blk, blk, row_spec, hb_spec],
        out_specs=[blk, row_blk, row_blk],
        out_shape=[jax.ShapeDtypeStruct((t_dim, d), BF16), row_shape, row_shape],
        compiler_params=_params(dimension_semantics=("parallel", "arbitrary")),
    )(prune, qkv, qkv, qkv, o, do, c_rows, lse_hb)


def _flash_dkv(qkv, do, c_rows, lse_rows, delta_rows, prune, *, tb, name, scatter=()):
    t_dim = qkv.shape[0]
    d = qkv.shape[1] // 3
    heads = d // HEAD_DIM
    cb = d // LANES
    scale = HEAD_DIM ** -0.5
    nq = t_dim // tb
    n_w = len(scatter)
    widths, scatter_shapes = _scatter_shapes(scatter)

    def body(prune_ref, q_ref, k_ref, v_ref, do_ref, cc_ref, lr_ref, dr_ref, *rest):
        src, (dk_ref, dv_ref, dsum_ref), dst = rest[:n_w], rest[n_w:n_w + 3], rest[n_w + 3:2 * n_w + 3]
        j = pl.program_id(1)
        h0 = 2 * pl.program_id(0)
        if scatter:
            travel = lambda: _scatter_copies(scatter, widths, src, dst, *rest[2 * n_w + 3:])

            @pl.when((pl.program_id(0) == 0) & (j == 0))
            def _():
                for cp in travel():
                    cp.start()
        k_blk = k_ref[...] * jnp.asarray(scale, BF16)
        v_blk = v_ref[...]
        masks = _head_masks(tb)
        row = lax.broadcasted_iota(jnp.int32, (tb, tb), 0)
        col = lax.broadcasted_iota(jnp.int32, (tb, tb), 1)
        ks = [jnp.where(masks[e], k_blk, jnp.zeros_like(k_blk)) for e in range(2)]
        vs = [jnp.where(masks[e], v_blk, jnp.zeros_like(v_blk)) for e in range(2)]
        ccols = [jnp.transpose(jnp.broadcast_to(cc_ref[0, e], (LANES, tb)))[:, 0:1] for e in range(2)]

        def step(i, carry, diagonal):
            off = pl.multiple_of(i * tb, tb)
            qi = q_ref[pl.ds(off, tb), :]
            doi = do_ref[pl.ds(off, tb), :]
            out = []
            for e in range(2):
                dk, dv, dsum = carry[e]
                lse = lr_ref[0, e, :, pl.ds(off, tb)]
                delta = dr_ref[0, e, :, pl.ds(off, tb)]
                st = lax.dot_general(ks[e], qi, (((1,), (1,)), ((), ())), preferred_element_type=F32) - ccols[e]
                if diagonal:
                    st = jnp.where(col >= row, st, NEG_INF)
                pt = jnp.exp(st - lse)
                dpt = lax.dot_general(vs[e], doi, (((1,), (1,)), ((), ())), preferred_element_type=F32)
                dst = pt * (dpt - delta)
                out.append((dk + jnp.dot(dst.astype(BF16), qi, preferred_element_type=F32),
                            dv + jnp.dot(pt.astype(BF16), doi, preferred_element_type=F32),
                            dsum + jnp.sum(dst, axis=1, keepdims=True)))
            return tuple(out)

        zero = jnp.zeros((tb, LANES), F32)
        init = (zero, zero, jnp.zeros((tb, 1), F32))
        carry = step(j, (init, init), True)
        kept = jnp.maximum(_kept_after(prune_ref, h0, j, nq), _kept_after(prune_ref, h0 + 1, j, nq))
        carry = lax.fori_loop(j + 1, j + 1 + kept, functools.partial(step, diagonal=False), carry)
        for e in range(2):
            dsum_ref[0, e] = _as_row(carry[e][2], tb)
        dk_ref[...] = (jnp.where(masks[0], carry[0][0], carry[1][0]) * scale).astype(dk_ref.dtype)
        dv_ref[...] = jnp.where(masks[0], carry[0][1], carry[1][1]).astype(dv_ref.dtype)
        if scatter:
            @pl.when((pl.program_id(0) == heads // 2 - 1) & (j == nq - 1))
            def _():
                for cp in travel():
                    cp.wait()

    blk = pl.BlockSpec((tb, LANES), lambda h, j: (j, h))
    row_spec = pl.BlockSpec((1, 2, 1, t_dim), lambda h, j: (h, 0, 0, 0))
    row_blk = pl.BlockSpec((1, 2, 1, tb), lambda h, j: (h, 0, 0, j))
    hbm = pl.BlockSpec(memory_space=pl.ANY)
    outs = pl.pallas_call(
        body, name=name, grid=(heads // 2, nq),
        in_specs=[pl.BlockSpec(memory_space=pltpu.SMEM),
                  pl.BlockSpec((t_dim, LANES), lambda h, j: (0, h)),
                  pl.BlockSpec((tb, LANES), lambda h, j: (j, cb + h)),
                  pl.BlockSpec((tb, LANES), lambda h, j: (j, 2 * cb + h)),
                  pl.BlockSpec((t_dim, LANES), lambda h, j: (0, h)),
                  row_blk, row_spec, row_spec] + [hbm] * n_w,
        out_specs=[blk, blk, row_blk] + [hbm] * n_w,
        out_shape=[jax.ShapeDtypeStruct((t_dim, d), BF16), jax.ShapeDtypeStruct((t_dim, d), BF16),
                   jax.ShapeDtypeStruct((heads // 2, 2, 1, t_dim), F32)] + scatter_shapes,
        scratch_shapes=_scatter_sems(n_w) if scatter else [],
        compiler_params=_params(dimension_semantics=("arbitrary", "arbitrary")),
    )(prune, qkv, qkv, qkv, do, c_rows, lse_rows, delta_rows, *[arr for arr, _ in scatter])
    return outs[0], outs[1], outs[2], outs[3:]


def _shift_down(z, prev, n, tt):
    out = pltpu.roll(z, n, axis=0)
    row = lax.broadcasted_iota(jnp.int32, z.shape, 0)
    for r in range(n):
        out = jnp.where(row == r, prev[8 - n + r:8 - n + r + 1, :], out)
    return out


def _shift_up(z, nxt, n, tt):
    out = pltpu.roll(z, tt - n, axis=0)
    row = lax.broadcasted_iota(jnp.int32, z.shape, 0)
    for r in range(n):
        out = jnp.where(row == tt - n + r, nxt[r:r + 1, :], out)
    return out


def _conv_fwd(proj, conv_w, *, name, tt=256):
    t_dim, d3 = proj.shape
    d = d3 // 3
    tt = min(tt, t_dim)

    def body(p_ref, prev_ref, w_ref, y_ref):
        i = pl.program_id(0)
        p = p_ref[...]
        pp = prev_ref[...]
        z = p[:, d:2 * d] * p[:, 2 * d:]
        zp = jnp.where(i > 0, pp[:, d:2 * d] * pp[:, 2 * d:], 0.0)
        w = w_ref[...]
        zc = w[2:3, :] * z + w[1:2, :] * _shift_down(z, zp, 1, tt) + w[0:1, :] * _shift_down(z, zp, 2, tt)
        y_ref[...] = (p[:, :d] * zc).astype(y_ref.dtype)

    return pl.pallas_call(
        body, name=name, grid=(t_dim // tt,),
        in_specs=[pl.BlockSpec((tt, d3), lambda i: (i, 0)),
                  pl.BlockSpec((8, d3), lambda i: (jnp.maximum(i * (tt // 8) - 1, 0), 0)),
                  pl.BlockSpec(conv_w.shape, lambda i: (0, 0))],
        out_specs=pl.BlockSpec((tt, d), lambda i: (i, 0)),
        out_shape=jax.ShapeDtypeStruct((t_dim, d), BF16),
        compiler_params=_params(dimension_semantics=("arbitrary",)),
    )(proj, proj, conv_w)


def _conv_bwd(proj, dy, conv_w, *, name, tt=256):
    t_dim, d3 = proj.shape
    d = d3 // 3
    tt = min(tt, t_dim)
    n = t_dim // tt

    def body(p_ref, prev_ref, next_ref, dy_ref, dyn_ref, w_ref, dp_ref, dw_ref):
        i = pl.program_id(0)
        p = p_ref[...]
        pp = prev_ref[...]
        pn = next_ref[...]
        bg, cg, u = p[:, :d], p[:, d:2 * d], p[:, 2 * d:]
        z = cg * u
        zp = jnp.where(i > 0, pp[:, d:2 * d] * pp[:, 2 * d:], 0.0)
        w = w_ref[...]
        z1 = _shift_down(z, zp, 1, tt)
        z2 = _shift_down(z, zp, 2, tt)
        zc = w[2:3, :] * z + w[1:2, :] * z1 + w[0:1, :] * z2
        dy_blk = dy_ref[...]
        dzc = dy_blk * bg
        dzn = jnp.where(i < n - 1, dyn_ref[...] * pn[:, :d], 0.0)
        dz = w[2:3, :] * dzc + w[1:2, :] * _shift_up(dzc, dzn, 1, tt) + w[0:1, :] * _shift_up(dzc, dzn, 2, tt)
        dp_ref[:, :d] = (dy_blk * zc).astype(dp_ref.dtype)
        dp_ref[:, d:2 * d] = (dz * u).astype(dp_ref.dtype)
        dp_ref[:, 2 * d:] = (dz * cg).astype(dp_ref.dtype)
        part = jnp.concatenate([jnp.sum(dzc * z2, axis=0, keepdims=True),
                                jnp.sum(dzc * z1, axis=0, keepdims=True),
                                jnp.sum(dzc * z, axis=0, keepdims=True),
                                jnp.zeros((5, d), F32)], axis=0)

        @pl.when(i == 0)
        def _():
            dw_ref[...] = part

        @pl.when(i > 0)
        def _():
            dw_ref[...] += part

    last8 = t_dim // 8 - 1
    return pl.pallas_call(
        body, name=name, grid=(n,),
        in_specs=[pl.BlockSpec((tt, d3), lambda i: (i, 0)),
                  pl.BlockSpec((8, d3), lambda i: (jnp.maximum(i * (tt // 8) - 1, 0), 0)),
                  pl.BlockSpec((8, d3), lambda i: (jnp.minimum((i + 1) * (tt // 8), last8), 0)),
                  pl.BlockSpec((tt, d), lambda i: (i, 0)),
                  pl.BlockSpec((8, d), lambda i: (jnp.minimum((i + 1) * (tt // 8), last8), 0)),
                  pl.BlockSpec(conv_w.shape, lambda i: (0, 0))],
        out_specs=[pl.BlockSpec((tt, d3), lambda i: (i, 0)), pl.BlockSpec((8, d), lambda i: (0, 0))],
        out_shape=[jax.ShapeDtypeStruct((t_dim, d3), BF16), jax.ShapeDtypeStruct((8, d), F32)],
        compiler_params=_params(dimension_semantics=("arbitrary",)),
    )(proj, proj, proj, dy, dy, conv_w)


def _window(ref, axis, n, idx):
    if axis is None:
        return ref
    sel = [slice(None)] * len(ref.shape)
    sel[axis] = pl.ds(pl.multiple_of(idx * n, n), n)
    return ref.at[tuple(sel)]


def _scatter_shapes(items):
    widths, shapes = [], []
    for arr, axis in items:
        shp = list(arr.shape)
        if axis is not None:
            shp[axis] //= N_DEV
        widths.append(None if axis is None else shp[axis])
        shapes.append(jax.ShapeDtypeStruct((N_DEV, *shp), arr.dtype))
    return widths, shapes


def _scatter_copies(items, widths, src, dst, send_sems, recv_sems, local_sems):
    x, y, c = lax.axis_index("x"), lax.axis_index("y"), lax.axis_index("c")
    me = 4 * x + 2 * y + c
    copies = [pltpu.make_async_copy(_window(src[w], items[w][1], widths[w], me), dst[w].at[me], local_sems.at[w])
              for w in range(len(items))]
    for k in range(1, N_DEV):
        px = 1 - x if k & 4 else x
        py = 1 - y if k & 2 else y
        pc = 1 - c if k & 1 else c
        for w in range(len(items)):
            copies.append(pltpu.make_async_remote_copy(
                src_ref=_window(src[w], items[w][1], widths[w], 4 * px + 2 * py + pc), dst_ref=dst[w].at[me],
                send_sem=send_sems.at[w, k - 1], recv_sem=recv_sems.at[w, k - 1],
                device_id=(px, py, pc), device_id_type=pl.DeviceIdType.MESH))
    return copies


def _scatter_sems(n_w):
    return [pltpu.SemaphoreType.DMA((n_w, N_DEV - 1)), pltpu.SemaphoreType.DMA((n_w, N_DEV - 1)),
            pltpu.SemaphoreType.DMA((n_w,))]


def _gather_shapes(items):
    widths = [arr.shape[axis] for arr, axis in items]
    shapes = [jax.ShapeDtypeStruct(tuple(s * N_DEV if a == axis else s for a, s in enumerate(arr.shape)), arr.dtype)
              for arr, axis in items]
    return widths, shapes


def _gather_phases(items, widths, src, dst, send_sems, recv_sems, local_sems):
    n_w = len(items)

    def run(phase):
        x, y, c = lax.axis_index("x"), lax.axis_index("y"), lax.axis_index("c")
        chips = [(1 - x, y), (x, 1 - y), (1 - x, 1 - y)]

        def place(w, origin):
            return _window(dst[w], items[w][1], widths[w], 4 * origin[0] + 2 * origin[1] + origin[2])

        def block_copy(w, n, origin, to, from_shard):
            return pltpu.make_async_remote_copy(
                src_ref=src[w] if from_shard else place(w, origin), dst_ref=place(w, origin),
                send_sem=send_sems.at[w, n], recv_sem=recv_sems.at[w, n],
                device_id=to, device_id_type=pl.DeviceIdType.MESH)

        def own(w):
            return pltpu.make_async_copy(src[w], place(w, (x, y, c)), local_sems.at[w])

        def first(w):
            return ([block_copy(w, 0, (x, y, c), (x, y, 1 - c), True)]
                    + [block_copy(w, 1 + n, (x, y, c), (*chip, c), True) for n, chip in enumerate(chips)])

        def passed(w, n):
            return block_copy(w, 4 + n, (*chips[n], c), (x, y, 1 - c), False)

        if phase == "start":
            for w in range(n_w):
                own(w).start()
                for cp in first(w):
                    cp.start()
        elif phase == "relay":
            for n, chip in enumerate(chips):
                for w in range(n_w):
                    block_copy(w, 1 + n, (*chip, c), (x, y, c), True).wait_recv()
                    passed(w, n).start()
        else:
            for w in range(n_w):
                block_copy(w, 0, (x, y, 1 - c), (x, y, c), True).wait_recv()
                for n, chip in enumerate(chips):
                    block_copy(w, 4 + n, (*chip, 1 - c), (x, y, c), False).wait_recv()
                for cp in first(w) + [passed(w, n) for n in range(3)]:
                    cp.wait_send()
                own(w).wait()

    return [functools.partial(run, phase) for phase in ("start", "relay", "finish")]


def _exchange(items, *, gather, name):
    n_w = len(items)
    widths, out_shape = _gather_shapes(items) if gather else _scatter_shapes(items)

    def body(*refs):
        src, dst = refs[:n_w], refs[n_w:2 * n_w]
        send_sems, recv_sems, local_sems = refs[2 * n_w:]
        if not gather:
            copies = _scatter_copies(items, widths, src, dst, send_sems, recv_sems, local_sems)
            for cp in copies:
                cp.start()
            for cp in copies:
                cp.wait()
            return
        for phase in _gather_phases(items, widths, src, dst, send_sems, recv_sems, local_sems):
            phase()

    return pl.pallas_call(
        body, name=name,
        in_specs=[pl.BlockSpec(memory_space=pl.ANY)] * n_w,
        out_specs=[pl.BlockSpec(memory_space=pl.ANY)] * n_w,
        out_shape=out_shape,
        scratch_shapes=_scatter_sems(n_w),
    )(*[arr for arr, _ in items])


def _row_tile(rows, cols):
    tr = rows
    while tr % 16 == 0 and tr * cols > 256 * 1024:
        tr //= 2
    return tr


def _sum_parts(parts, *, name):
    n_parts, rows, cols = parts.shape
    tr = _row_tile(rows, cols)

    def body(p_ref, o_ref):
        g = p_ref[0].astype(F32)
        for s in range(1, n_parts):
            g = g + p_ref[s].astype(F32)
        o_ref[...] = g

    return pl.pallas_call(
        body, name=name, grid=(rows // tr,),
        in_specs=[pl.BlockSpec((n_parts, tr, cols), lambda i: (0, i, 0))],
        out_specs=pl.BlockSpec((tr, cols), lambda i: (i, 0)),
        out_shape=jax.ShapeDtypeStruct((rows, cols), F32),
        compiler_params=_params(dimension_semantics=("parallel",)),
    )(parts)


def _adamw(parts, w, m, v, *, name):
    n_parts, rows, cols = parts.shape
    tr = _row_tile(rows, cols)

    def body(p_ref, w_ref, m_ref, v_ref, g_ref, d_ref, nm_ref, nv_ref):
        g = p_ref[0].astype(F32)
        for s in range(1, n_parts):
            g = g + p_ref[s].astype(F32)
        m_new = ADAM_B1 * m_ref[...] + (1.0 - ADAM_B1) * g
        v_new = ADAM_B2 * v_ref[...] + (1.0 - ADAM_B2) * (g * g)
        m_hat = m_new / (1.0 - ADAM_B1 ** ADAM_STEP)
        v_hat = v_new / (1.0 - ADAM_B2 ** ADAM_STEP)
        g_ref[...] = g
        d_ref[...] = -ADAM_LR * (m_hat / (jnp.sqrt(v_hat) + ADAM_EPS) + ADAM_WD * w_ref[...])
        nm_ref[...] = m_new
        nv_ref[...] = v_new

    spec = pl.BlockSpec((tr, cols), lambda i: (i, 0))
    return pl.pallas_call(
        body, name=name, grid=(rows // tr,),
        in_specs=[pl.BlockSpec((n_parts, tr, cols), lambda i: (0, i, 0)), spec, spec, spec],
        out_specs=[spec] * 4,
        out_shape=[jax.ShapeDtypeStruct((rows, cols), F32)] * 4,
        compiler_params=_params(dimension_semantics=("parallel",)),
    )(parts, w, m, v)


def _pad_rows(a, axis, to):
    pad = [(0, 0)] * a.ndim
    pad[axis] = (0, to - a.shape[axis])
    return jnp.pad(a, pad)


def kernel(x, p, norm_g, w_attn_in, b_forget, w_attn_out, w_conv_in, conv_w, w_conv_out, w_mlp_up, w_mlp_down, w_ple_proj, w_ple_gate, loss_target, m_norm_g, m_w_attn_in, m_b_forget, m_w_attn_out, m_w_conv_in, m_conv_w, m_w_conv_out, m_w_mlp_up, m_w_mlp_down, m_w_ple_proj, m_w_ple_gate, v_norm_g, v_w_attn_in, v_b_forget, v_w_attn_out, v_w_conv_in, v_conv_w, v_w_conv_out, v_w_mlp_up, v_w_mlp_down, v_w_ple_proj, v_w_ple_gate):
    shards = dict(norm_g=norm_g, w_attn_in=w_attn_in, b_forget=b_forget, w_attn_out=w_attn_out,
                  w_conv_in=w_conv_in, conv_w=conv_w, w_conv_out=w_conv_out, w_mlp_up=w_mlp_up,
                  w_mlp_down=w_mlp_down, w_ple_proj=w_ple_proj, w_ple_gate=w_ple_gate)
    m_shards = dict(norm_g=m_norm_g, w_attn_in=m_w_attn_in, b_forget=m_b_forget, w_attn_out=m_w_attn_out,
                    w_conv_in=m_w_conv_in, conv_w=m_conv_w, w_conv_out=m_w_conv_out, w_mlp_up=m_w_mlp_up,
                    w_mlp_down=m_w_mlp_down, w_ple_proj=m_w_ple_proj, w_ple_gate=m_w_ple_gate)
    v_shards = dict(norm_g=v_norm_g, w_attn_in=v_w_attn_in, b_forget=v_b_forget, w_attn_out=v_w_attn_out,
                    w_conv_in=v_w_conv_in, conv_w=v_conv_w, w_conv_out=v_w_conv_out, w_mlp_up=v_w_mlp_up,
                    w_mlp_down=v_w_mlp_down, w_ple_proj=v_w_ple_proj, w_ple_gate=v_w_ple_gate)
    t_dim, d = x.shape[-2:]
    depth = p.shape[0]
    n_attn, heads = b_forget.shape
    assert d == heads * HEAD_DIM and x.shape[0] == 1
    tb = min(512, t_dim // 2)
    x0 = x.reshape(t_dim, d)
    target = loss_target.reshape(t_dim, d)
    p_rows = p.reshape(depth * t_dim, p.shape[-1])

    in_cols = w_attn_in.shape[2]
    in_cols_pad = -(-in_cols // 16) * 16
    first_names = ['norm_g', 'w_attn_in', 'conv_w']
    rest_names = [n for n in WEIGHT_NAMES if n not in first_names + ['b_forget']]

    def gather_item(n):
        if n == 'w_attn_in':
            return _pad_rows(jnp.swapaxes(w_attn_in, 1, 2), 1, in_cols_pad).astype(BF16), 1
        return (shards[n] if n in ('norm_g', 'conv_w') else shards[n].astype(BF16)), SHARD_AXIS[n]

    full = dict(zip(first_names, _exchange([gather_item(n) for n in first_names], gather=True, name="gather_first")))
    gains = full['norm_g']
    taps = full['conv_w']
    w_in_t = full['w_attn_in'].reshape(n_attn, N_DEV, in_cols_pad, d)[:, :, :in_cols]
    w_in_t = _pad_rows(w_in_t.reshape(n_attn, N_DEV * in_cols, d), 1, 3 * d + LANES)
    bias_pad = jnp.pad(b_forget, ((0, 0), (0, LANES - heads)))

    def gain(i, k):
        return gains[i, k].reshape(1, d)

    def add_norm(x_prev, branch, g_branch, g_next, name):
        def fn(rows, vecs):
            x_new = rows[0] + _norm(rows[1], vecs[0])
            return [x_new, _norm(x_new, vecs[1])], []
        return _rows_call(fn, [x_prev, branch], [g_branch, g_next], [(d, F32), (d, BF16)], [], name=name)

    saved = []
    x_cur = x0
    hn = _rows_call(lambda rows, vecs: ([_norm(rows[0], vecs[0])], []), [x0], [gain(0, 0)], [(d, BF16)], [],
                    name="norm_in")[0]
    loss_rows = dy = None
    for i in range(depth):
        j = i // 2
        s = dict(x0=x_cur, hn=hn)
        if i % 2 == 0:
            s['qkv'] = _mm(hn, w_in_t[j, :3 * d], tb=True, out_dtypes=(BF16,), name=f"attn_in_{i}")
            s['fl'] = _mm(hn, w_in_t[j, 3 * d:], tb=True, name=f"attn_gate_{i}")
            c = _cumsum_fwd(s['fl'], bias_pad[j:j + 1], name=f"gate_cumsum_{i}")
            c_t = c[:, :heads].T
            s['prune'] = _prune_table(c_t, _head_norms(s['qkv'], name=f"head_norms_{i}"), tb)
            s['c_rows'] = c_t.reshape(heads // 2, 2, 1, t_dim)
            s['o'], s['lse_hb'], s['lse_rows'], rest = _flash_fwd(
                s['qkv'], s['c_rows'], s['prune'], tb=tb, name=f"attn_fwd_{i}",
                gather=[gather_item(n) for n in rest_names] if i == 0 else ())
            if i == 0:
                full.update(zip(rest_names, rest))
            s['m'] = _mm(s['o'], full['w_attn_out'][j], name=f"attn_out_{i}")
        else:
            s['proj'] = _mm(hn, full['w_conv_in'][j], name=f"conv_in_{i}")
            s['y'] = _conv_fwd(s['proj'], taps[j], name=f"conv_fwd_{i}")
            s['m'] = _mm(s['y'], full['w_conv_out'][j], name=f"conv_out_{i}")
        s['x1'], s['h2'] = add_norm(x_cur, s['m'], gain(i, 1), gain(i, 2), f"mix_norm_{i}")
        s['a'] = _mm(s['h2'], full['w_mlp_up'][i], out_dtypes=(BF16,), name=f"mlp_up_{i}",
                     epi=lambda acc: (jnp.square(jnp.maximum(acc, 0.0)),))
        s['f'] = _mm(s['a'], full['w_mlp_down'][i], name=f"mlp_down_{i}")
        s['x2'], s['h4'] = add_norm(s['x1'], s['f'], gain(i, 3), gain(i, 4), f"mlp_norm_{i}")
        s['pp'] = _mm(p_rows, full['w_ple_proj'][i], a_rows=t_dim, a_off=i * t_dim, name=f"ple_proj_{i}")
        s['gl'], s['e'] = _mm(s['h4'], full['w_ple_gate'][i], extras=(s['pp'],), out_dtypes=(F32, F32),
                              name=f"ple_gate_{i}", epi=lambda acc, pp: (acc, pp * _sigmoid(acc)))
        if i + 1 < depth:
            x_cur, hn = add_norm(s['x2'], s['e'], gain(i, 5), gain(i + 1, 0), f"ple_norm_{i}")
        else:
            def loss_fn(rows, vecs):
                err = rows[0] + _norm(rows[1], vecs[0]) - rows[2]
                part = 0.5 * jnp.sum(jnp.sum(err * err, axis=1, keepdims=True), axis=0, keepdims=True) / d
                return [err / d], [jnp.broadcast_to(part, (1, LANES))]
            dy, loss_rows = _rows_call(loss_fn, [s['x2'], s['e'], target], [gain(i, 5)], [(d, F32)],
                                       [(1, LANES)], name="loss")
        saved.append(s)
    loss = lax.psum(loss_rows[0, 0], ("x", "y", "c"))

    grads = {n: [None] * shards[n].shape[0] for n in WEIGHT_NAMES}
    d_gains = [[None] * 6 for _ in range(depth)]
    wgrad = functools.partial(_mm, ta=True, out_dtypes=(BF16,))
    dgrad = functools.partial(_mm, out_dtypes=(BF16,))
    axis_of = dict(SHARD_AXIS, w_attn_in=1)

    def in_t_blocks(layers):
        g = jnp.stack(layers)[:, :N_DEV * in_cols].reshape(len(layers), N_DEV, in_cols, d)
        return _pad_rows(g, 2, in_cols_pad).reshape(len(layers), N_DEV * in_cols_pad, d)

    early_names = early_items = early = None
    dx = dy
    for i in reversed(range(depth)):
        j = i // 2
        s = saved[i]

        def ple_fn(rows, vecs):
            de, dg = _norm_bwd(rows[0], vecs[0], rows[1])
            sg = _sigmoid(rows[2])
            return [de * sg, de * rows[3] * sg * (1.0 - sg)], [dg]
        dpp, dgl, d_gains[i][5] = _rows_call(ple_fn, [s['e'], dx, s['gl'], s['pp']], [gain(i, 5)],
                                             [(d, BF16), (d, BF16)], [(1, d)], name=f"ple_bwd_{i}")
        grads['w_ple_proj'][i] = wgrad(p_rows, dpp, a_rows=t_dim, a_off=i * t_dim, name=f"ple_proj_dw_{i}")
        grads['w_ple_gate'][i] = wgrad(s['h4'], dgl, name=f"ple_gate_dw_{i}")
        dh4 = dgrad(dgl, full['w_ple_gate'][i], tb=True, name=f"ple_gate_dx_{i}")

        def two_norm_bwd(x_res, dh, dx_in, branch, g_res, g_branch, name):
            def fn(rows, vecs):
                d_res, dg_res = _norm_bwd(rows[0], vecs[0], rows[1])
                dx_out = rows[2] + d_res
                d_branch, dg_branch = _norm_bwd(rows[3], vecs[1], dx_out)
                return [dx_out, d_branch], [dg_res, dg_branch]
            return _rows_call(fn, [x_res, dh, dx_in, branch], [g_res, g_branch], [(d, F32), (d, BF16)],
                              [(1, d), (1, d)], name=name)

        dx2, df, d_gains[i][4], d_gains[i][3] = two_norm_bwd(s['x2'], dh4, dx, s['f'], gain(i, 4), gain(i, 3),
                                                            f"mlp_norm_bwd_{i}")
        grads['w_mlp_down'][i] = wgrad(s['a'], df, name=f"mlp_down_dw_{i}")
        du = _mm(df, full['w_mlp_down'][i], tb=True, extras=(s['a'],), out_dtypes=(BF16,), name=f"mlp_down_dx_{i}",
                 epi=lambda acc, a: (acc * (2.0 * jnp.sqrt(a.astype(F32))),))
        grads['w_mlp_up'][i] = wgrad(s['h2'], du, name=f"mlp_up_dw_{i}")
        dh2 = dgrad(du, full['w_mlp_up'][i], tb=True, name=f"mlp_up_dx_{i}")
        dx1, dm, d_gains[i][2], d_gains[i][1] = two_norm_bwd(s['x1'], dh2, dx2, s['m'], gain(i, 2), gain(i, 1),
                                                            f"mix_norm_bwd_{i}")
        if i % 2 == 0:
            grads['w_attn_out'][j] = wgrad(s['o'], dm, name=f"attn_out_dw_{i}")
            do = _mm(dm, full['w_attn_out'][j], tb=True, out_dtypes=(BF16,), name=f"attn_out_dx_{i}")
            dq, delta_rows, rsum_rows = _flash_dq(s['qkv'], s['o'], do, s['c_rows'], s['lse_hb'],
                                                  s['prune'], tb=tb, name=f"attn_dq_{i}")
            if i == 0:
                early_names = [n for n in WEIGHT_NAMES if n not in ('norm_g', 'b_forget')]
                early_items = [(in_t_blocks(grads[n][1:]) if n == 'w_attn_in' else jnp.stack(grads[n]), axis_of[n])
                               for n in early_names]
            dk, dv, csum_rows, early = _flash_dkv(s['qkv'], do, s['c_rows'], s['lse_rows'], delta_rows, s['prune'],
                                                  tb=tb, name=f"attn_dkv_{i}", scatter=early_items if i == 0 else ())
            dc = (rsum_rows - csum_rows).reshape(heads, t_dim).T
            dfl, db = _cumsum_bwd(jnp.pad(dc, ((0, 0), (0, LANES - heads))), s['fl'], bias_pad[j:j + 1],
                                  name=f"gate_cumsum_bwd_{i}")
            grads['b_forget'][j] = db[0, :heads]
            dproj = jnp.concatenate([dq, dk, dv, dfl], axis=1)
            grads['w_attn_in'][j] = wgrad(dproj, s['hn'], name=f"attn_in_dw_{i}")
            dhn = dgrad(dproj, w_in_t[j], name=f"attn_in_dx_{i}")
        else:
            grads['w_conv_out'][j] = wgrad(s['y'], dm, name=f"conv_out_dw_{i}")
            dyc = _mm(dm, full['w_conv_out'][j], tb=True, name=f"conv_out_dx_{i}")
            dproj, dtaps = _conv_bwd(s['proj'], dyc, taps[j], name=f"conv_bwd_{i}")
            grads['conv_w'][j] = dtaps[:3].astype(BF16)
            grads['w_conv_in'][j] = wgrad(s['hn'], dproj, name=f"conv_in_dw_{i}")
            dhn = dgrad(dproj, full['w_conv_in'][j], tb=True, name=f"conv_in_dx_{i}")

        def in_fn(rows, vecs):
            d_res, dg = _norm_bwd(rows[0], vecs[0], rows[1])
            return [rows[2] + d_res], [dg]
        dx, d_gains[i][0] = _rows_call(in_fn, [s['x0'], dhn, dx1], [gain(i, 0)], [(d, F32)], [(1, d)],
                                       name=f"in_norm_bwd_{i}")
    grad_x = dx.reshape(x.shape)

    recv = dict(zip(early_names, early))
    late = _exchange([(in_t_blocks(grads['w_attn_in'][:1]), 1),
                      (jnp.stack([jnp.concatenate(row, axis=0) for row in d_gains]).astype(BF16), SHARD_AXIS['norm_g']),
                      (jnp.zeros((8, LANES), F32).at[:n_attn, :heads].set(jnp.stack(grads['b_forget'])), None)],
                     gather=False, name="exchange_late")
    recv['norm_g'] = late[1]
    recv['b_forget'] = late[2][:, :n_attn, :heads]
    g_in_t = jnp.concatenate([_sum_parts(part.reshape(N_DEV, -1, d), name=f"sum_attn_in_{k}")
                              for k, part in enumerate((late[0], recv['w_attn_in']))], axis=0)
    recv['w_attn_in'] = jnp.swapaxes(g_in_t.reshape(n_attn, in_cols_pad, d)[:, :in_cols], 1, 2)[None]
    results = {}
    for n in WEIGHT_NAMES:
        shp = shards[n].shape
        flat = lambda a: a.reshape(a.shape[:a.ndim - len(shp)] + (-1, shp[-1]))
        outs = _adamw(flat(recv[n]), flat(shards[n]), flat(m_shards[n]), flat(v_shards[n]), name=f"adamw_{n}")
        results[n] = [o.reshape(shp) for o in outs]
    return (loss, grad_x, *[results[n][k] for k in range(4) for n in WEIGHT_NAMES])
```

```python
import functools

import jax
import jax.numpy as jnp
from jax import lax
from jax.experimental import pallas as pl
from jax.experimental.pallas import tpu as pltpu

F32 = jnp.float32
BF16 = jnp.bfloat16

N_DEV = 8
LANES = 128
HEAD_DIM = 64
VMEM_LIMIT_BYTES = 56 * 1024 * 1024
RMS_EPS = 1e-6
NEG_INF = -1e30
ADAM_LR = 0.001
ADAM_B1 = 0.9
ADAM_B2 = 0.999
ADAM_EPS = 1e-08
ADAM_WD = 0.01
ADAM_STEP = 10
WEIGHT_NAMES = ('norm_g', 'w_attn_in', 'b_forget', 'w_attn_out', 'w_conv_in', 'conv_w', 'w_conv_out',
                'w_mlp_up', 'w_mlp_down', 'w_ple_proj', 'w_ple_gate')
SHARD_AXIS = {'norm_g': 2, 'w_attn_in': 2, 'b_forget': None, 'w_attn_out': 1, 'w_conv_in': 2, 'conv_w': 2,
              'w_conv_out': 1, 'w_mlp_up': 2, 'w_mlp_down': 1, 'w_ple_proj': 2, 'w_ple_gate': 1}


def _params(**kw):
    return pltpu.CompilerParams(vmem_limit_bytes=VMEM_LIMIT_BYTES, **kw)


def _tile(n, cap):
    if n <= cap:
        return n
    t = (cap // LANES) * LANES
    while n % t:
        t -= LANES
    return t


MM_VMEM_BUDGET = 36 * 1024 * 1024


def _mm(a, b, *, ta=False, tb=False, extras=(), epi=None, out_dtypes=(F32,), name, a_rows=None, a_off=0):
    rows_a = a_rows or a.shape[0]
    m_dim, k_dim = (a.shape[1], rows_a) if ta else (rows_a, a.shape[1])
    n_dim = b.shape[0] if tb else b.shape[1]
    assert k_dim == (b.shape[1] if tb else b.shape[0]) and a_off % rows_a == 0
    tk = _tile(k_dim, 1024 if k_dim <= 1024 else 2048)
    nk = k_dim // tk
    tn = _tile(n_dim, 1024)

    def vmem_bytes(tm):
        per_mn = sum(jnp.dtype(dt).itemsize for dt in out_dtypes) + sum(e.dtype.itemsize for e in extras)
        return (2 * (tm * tk * a.dtype.itemsize + tk * tn * b.dtype.itemsize) + 2 * tm * tn * per_mn
                + tm * tn * 4 * (2 + (nk > 1)))

    tm = next(t for t in (_tile(m_dim, 1024), _tile(m_dim, 512)) if t <= 512 or vmem_bytes(t) <= MM_VMEM_BUDGET)
    grid = (n_dim // tn, m_dim // tm, nk)
    off_m, off_k = (0, a_off // tk) if ta else (a_off // tm, 0)
    a_spec = (pl.BlockSpec((tk, tm), lambda j, i, k: (k + off_k, i)) if ta
              else pl.BlockSpec((tm, tk), lambda j, i, k: (i + off_m, k)))
    b_spec = (pl.BlockSpec((tn, tk), lambda j, i, k: (j, k)) if tb
              else pl.BlockSpec((tk, tn), lambda j, i, k: (k, j)))
    mn_spec = pl.BlockSpec((tm, tn), lambda j, i, k: (i, j))
    dims = (((0 if ta else 1,), (1 if tb else 0,)), ((), ()))
    n_extra, n_out = len(extras), len(out_dtypes)
    if epi is None:
        epi = lambda acc: (acc,)

    def body(a_ref, b_ref, *rest):
        e_refs, o_refs = rest[:n_extra], rest[n_extra:n_extra + n_out]
        part = lax.dot_general(a_ref[...].astype(BF16), b_ref[...].astype(BF16), dims,
                               preferred_element_type=F32)

        def finish(acc):
            for o_ref, val in zip(o_refs, epi(acc, *[e[...] for e in e_refs])):
                o_ref[...] = val.astype(o_ref.dtype)

        if nk == 1:
            finish(part)
        else:
            acc_ref = rest[-1]
            k = pl.program_id(2)

            @pl.when(k == 0)
            def _():
                acc_ref[...] = part

            @pl.when(k > 0)
            def _():
                acc_ref[...] += part

            @pl.when(k == nk - 1)
            def _():
                finish(acc_ref[...])

    outs = pl.pallas_call(
        body, name=name, grid=grid,
        in_specs=[a_spec, b_spec] + [mn_spec] * n_extra,
        out_specs=[mn_spec] * n_out,
        out_shape=[jax.ShapeDtypeStruct((m_dim, n_dim), dt) for dt in out_dtypes],
        scratch_shapes=[pltpu.VMEM((tm, tn), F32)] if nk > 1 else [],
        compiler_params=_params(dimension_semantics=("parallel", "parallel", "arbitrary")),
    )(a, b, *extras)
    return outs[0] if n_out == 1 else outs


def _rows(fn, row_ins, vec_ins, row_outs, vec_outs, *, name, tt=512, reverse=False):
    t_dim = row_ins[0].shape[0]
    tt = min(tt, t_dim)
    n = t_dim // tt
    n_ri, n_vi, n_ro, n_vo = len(row_ins), len(vec_ins), len(row_outs), len(vec_outs)
    pos = (lambda i: (n - 1 - i, 0)) if reverse else (lambda i: (i, 0))
    fixed = lambda i: (0, 0)

    def body(*refs):
        ri = refs[:n_ri]
        vi = refs[n_ri:n_ri + n_vi]
        ro = refs[n_ri + n_vi:n_ri + n_vi + n_ro]
        vo = refs[n_ri + n_vi + n_ro:n_ri + n_vi + n_ro + n_vo]
        scratch = refs[n_ri + n_vi + n_ro + n_vo:]
        r_out, v_out = fn([r[...] for r in ri], [v[...] for v in vi], *scratch)
        for o_ref, val in zip(ro, r_out):
            o_ref[...] = val.astype(o_ref.dtype)
        i = pl.program_id(0)
        for o_ref, val in zip(vo, v_out):
            @pl.when(i == 0)
            def _(o_ref=o_ref, val=val):
                o_ref[...] = val

            @pl.when(i > 0)
            def _(o_ref=o_ref, val=val):
                o_ref[...] += val

    return body, dict(
        grid=(n,),
        in_specs=[pl.BlockSpec((tt, r.shape[1]), pos) for r in row_ins]
        + [pl.BlockSpec(v.shape, fixed) for v in vec_ins],
        out_specs=[pl.BlockSpec((tt, w), pos) for w, _ in row_outs]
        + [pl.BlockSpec(s, fixed) for s in vec_outs],
        out_shape=[jax.ShapeDtypeStruct((t_dim, w), dt) for w, dt in row_outs]
        + [jax.ShapeDtypeStruct(s, F32) for s in vec_outs],
        name=name,
        compiler_params=_params(dimension_semantics=("arbitrary",)),
    )


def _rows_call(fn, row_ins, vec_ins, row_outs, vec_outs, *, name, tt=512, reverse=False, scratch=()):
    body, kw = _rows(fn, row_ins, vec_ins, row_outs, vec_outs, name=name, tt=tt, reverse=reverse)
    return pl.pallas_call(body, scratch_shapes=list(scratch), **kw)(*row_ins, *vec_ins)


def _rstd(x):
    return lax.rsqrt(jnp.mean(x * x, axis=-1, keepdims=True) + RMS_EPS)


def _norm(x, g):
    return x * _rstd(x) * g


def _norm_bwd(x, g, dy):
    xh = x * _rstd(x)
    gy = dy * g
    dx = _rstd(x) * (gy - xh * jnp.mean(gy * xh, axis=-1, keepdims=True))
    return dx, jnp.sum(dy * xh, axis=0, keepdims=True)


def _sigmoid(x):
    return 1.0 / (1.0 + jnp.exp(-x))


def _log_sigmoid(x):
    return jnp.minimum(x, 0.0) - jnp.log(1.0 + jnp.exp(-jnp.abs(x)))


def _split3(x):
    hi = x.astype(BF16)
    r1 = x - hi.astype(F32)
    mid = r1.astype(BF16)
    lo = (r1 - mid.astype(F32)).astype(BF16)
    return hi, mid, lo


def _cumsum_fwd(fl, bias, *, name):
    w = fl.shape[1]
    tt = min(512, fl.shape[0])

    def fn(rows, vecs, carry_ref):
        i = pl.program_id(0)

        @pl.when(i == 0)
        def _():
            carry_ref[...] = jnp.zeros_like(carry_ref)

        lf = _log_sigmoid(rows[0] + vecs[0])
        r = lax.broadcasted_iota(jnp.int32, (tt, tt), 0)
        c = lax.broadcasted_iota(jnp.int32, (tt, tt), 1)
        tri = (c <= r).astype(BF16)
        acc = carry_ref[0:1, :]
        for part in _split3(lf):
            acc = acc + jnp.dot(tri, part, preferred_element_type=F32)
        carry_ref[0:1, :] = acc[tt - 1:tt, :]
        return [acc], []

    return _rows_call(fn, [fl], [bias], [(w, F32)], [], name=name, tt=tt,
                      scratch=[pltpu.VMEM((8, w), F32)])[0]


def _cumsum_bwd(dc, fl, bias, *, name):
    w = fl.shape[1]
    tt = min(512, fl.shape[0])

    def fn(rows, vecs, carry_ref):
        i = pl.program_id(0)

        @pl.when(i == 0)
        def _():
            carry_ref[...] = jnp.zeros_like(carry_ref)

        r = lax.broadcasted_iota(jnp.int32, (tt, tt), 0)
        c = lax.broadcasted_iota(jnp.int32, (tt, tt), 1)
        tri = (c >= r).astype(BF16)
        acc = carry_ref[0:1, :]
        for part in _split3(rows[0]):
            acc = acc + jnp.dot(tri, part, preferred_element_type=F32)
        carry_ref[0:1, :] = acc[0:1, :]
        dfl = acc * _sigmoid(-(rows[1] + vecs[0]))
        return [dfl], [jnp.sum(dfl, axis=0, keepdims=True)]

    return _rows_call(fn, [dc, fl], [bias], [(w, BF16)], [(1, w)], name=name, tt=tt, reverse=True,
                      scratch=[pltpu.VMEM((8, w), F32)])


def _head_masks(tb):
    lane = lax.broadcasted_iota(jnp.int32, (tb, LANES), 1)
    return [lane < HEAD_DIM, lane >= HEAD_DIM]


PRUNE_MARGIN = 30.0


def _head_norms(qkv, *, name):
    t_dim = qkv.shape[0]
    d = qkv.shape[1] // 3
    heads = d // HEAD_DIM
    tt = min(512, t_dim)

    def body(q_ref, k_ref, o_ref):
        col = lax.broadcasted_iota(jnp.int32, (d, LANES), 0) // HEAD_DIM
        lane = lax.broadcasted_iota(jnp.int32, (d, LANES), 1)
        tile_max = None
        for ref, first in ((q_ref, 0), (k_ref, heads)):
            x = ref[...].astype(F32)
            sums = jnp.dot((x * x).astype(BF16), (col + first == lane).astype(BF16), preferred_element_type=F32)
            part = jnp.max(sums, axis=0, keepdims=True)
            tile_max = part if tile_max is None else jnp.maximum(tile_max, part)
        i = pl.program_id(0)

        @pl.when(i == 0)
        def _():
            o_ref[...] = tile_max

        @pl.when(i > 0)
        def _():
            o_ref[...] = jnp.maximum(o_ref[...], tile_max)

    return pl.pallas_call(
        body, name=name, grid=(t_dim // tt,),
        in_specs=[pl.BlockSpec((tt, d), lambda i: (i, 0)), pl.BlockSpec((tt, d), lambda i: (i, 1))],
        out_specs=pl.BlockSpec((1, LANES), lambda i: (0, 0)),
        out_shape=jax.ShapeDtypeStruct((1, LANES), F32),
        compiler_params=_params(dimension_semantics=("arbitrary",)),
    )(qkv, qkv)


def _prune_table(c_t, norms, tb):
    heads = c_t.shape[0]
    bound = 1.02 * HEAD_DIM ** -0.5 * jnp.sqrt(norms[0, :heads] * norms[0, heads:2 * heads])
    return jnp.concatenate([c_t[:, ::tb], c_t[:, tb - 1::tb], -(PRUNE_MARGIN + 2.0 * bound)[:, None]], axis=1)


def _kept_before(prune_ref, h, i, nq):
    first, thr = prune_ref[h, i], prune_ref[h, 2 * nq]
    return lax.while_loop(lambda n: (n < i) & (first - prune_ref[h, nq + jnp.maximum(i - 1 - n, 0)] >= thr),
                          lambda n: n + 1, jnp.int32(0))


def _kept_after(prune_ref, h, j, nq):
    last, thr = prune_ref[h, nq + j], prune_ref[h, 2 * nq]
    return lax.while_loop(lambda n: (j + 1 + n < nq) & (prune_ref[h, jnp.minimum(j + 1 + n, nq - 1)] - last >= thr),
                          lambda n: n + 1, jnp.int32(0))


def _as_row(col, tb):
    return jnp.transpose(jnp.broadcast_to(col, (tb, LANES)))[0:1, :]


def _flash_fwd(qkv, c_rows, prune, *, tb, name, gather=()):
    t_dim = qkv.shape[0]
    d = qkv.shape[1] // 3
    heads = d // HEAD_DIM
    cb = d // LANES
    nq = t_dim // tb
    n_w = len(gather)
    widths, gather_shapes = _gather_shapes(gather)

    def body(prune_ref, q_ref, k_ref, v_ref, cr_ref, *rest):
        src, (o_ref, lser_ref), dst = rest[:n_w], rest[n_w:n_w + 2], rest[n_w + 2:2 * n_w + 2]
        i = pl.program_id(1)
        h0 = 2 * pl.program_id(0)
        if gather:
            start, relay, finish = _gather_phases(gather, widths, src, dst, *rest[2 * n_w + 2:])
            pl.when((pl.program_id(0) == 0) & (i == 0))(start)
            pl.when((pl.program_id(0) == (3 * heads) // 8) & (i == 0))(relay)
        q = q_ref[...] * jnp.asarray(HEAD_DIM ** -0.5, BF16)
        masks = _head_masks(tb)
        row = lax.broadcasted_iota(jnp.int32, (tb, tb), 0)
        col = lax.broadcasted_iota(jnp.int32, (tb, tb), 1)
        qs = [jnp.where(masks[e], q, jnp.zeros_like(q)) for e in range(2)]

        def step(j, carry, diagonal):
            off = pl.multiple_of(j * tb, tb)
            kj = k_ref[pl.ds(off, tb), :]
            vj = v_ref[pl.ds(off, tb), :]
            out = []
            for e in range(2):
                m, l, acc = carry[e]
                crow = cr_ref[0, e, :, pl.ds(off, tb)]
                s = lax.dot_general(qs[e], kj, (((1,), (1,)), ((), ())), preferred_element_type=F32) - crow
                if diagonal:
                    s = jnp.where(col <= row, s, NEG_INF)
                m_new = jnp.maximum(m, jnp.max(s, axis=1, keepdims=True))
                p = jnp.exp(s - m_new)
                alpha = jnp.exp(m - m_new)
                l = alpha * l + jnp.sum(p, axis=1, keepdims=True)
                acc = alpha * acc + jnp.dot(p.astype(BF16), vj, preferred_element_type=F32)
                out.append((m_new, l, acc))
            return tuple(out)

        init = (jnp.full((tb, 1), NEG_INF, F32), jnp.zeros((tb, 1), F32), jnp.zeros((tb, LANES), F32))
        kept = jnp.maximum(_kept_before(prune_ref, h0, i, nq), _kept_before(prune_ref, h0 + 1, i, nq))
        carry = lax.fori_loop(i - kept, i, functools.partial(step, diagonal=False), (init, init))
        carry = step(i, carry, True)
        outs = []
        for e in range(2):
            m, l, acc = carry[e]
            outs.append(acc / l)
            lser_ref[0, e] = _as_row(m + jnp.log(l), tb)
        o_ref[...] = jnp.where(masks[0], outs[0], outs[1]).astype(o_ref.dtype)
        if gather:
            pl.when((pl.program_id(0) == heads // 2 - 1) & (i == nq - 1))(finish)

    row_spec = pl.BlockSpec((1, 2, 1, t_dim), lambda h, i: (h, 0, 0, 0))
    row_blk = pl.BlockSpec((1, 2, 1, tb), lambda h, i: (h, 0, 0, i))
    hbm = pl.BlockSpec(memory_space=pl.ANY)
    outs = pl.pallas_call(
        body, name=name, grid=(heads // 2, nq),
        in_specs=[pl.BlockSpec(memory_space=pltpu.SMEM),
                  pl.BlockSpec((tb, LANES), lambda h, i: (i, h)),
                  pl.BlockSpec((t_dim, LANES), lambda h, i: (0, cb + h)),
                  pl.BlockSpec((t_dim, LANES), lambda h, i: (0, 2 * cb + h)),
                  row_spec] + [hbm] * n_w,
        out_specs=[pl.BlockSpec((tb, LANES), lambda h, i: (i, h)), row_blk] + [hbm] * n_w,
        out_shape=[jax.ShapeDtypeStruct((t_dim, d), BF16),
                   jax.ShapeDtypeStruct((heads // 2, 2, 1, t_dim), F32)] + gather_shapes,
        scratch_shapes=_scatter_sems(n_w) if gather else [],
        compiler_params=_params(dimension_semantics=("arbitrary", "arbitrary")),
    )(prune, qkv, qkv, qkv, c_rows, *[arr for arr, _ in gather])
    return outs[0], outs[1], outs[2:]


def _flash_dq(qkv, o, do, c_rows, lse_rows, prune, *, tb, name, scatter=()):
    t_dim = qkv.shape[0]
    d = qkv.shape[1] // 3
    heads = d // HEAD_DIM
    cb = d // LANES
    scale = HEAD_DIM ** -0.5
    nq = t_dim // tb
    n_w = len(scatter)
    widths, scatter_shapes = _scatter_shapes(scatter)

    def body(prune_ref, q_ref, k_ref, v_ref, o_ref, do_ref, cr_ref, lse_ref, *rest):
        src, (dq_ref, dl_ref, rs_ref), dst = rest[:n_w], rest[n_w:n_w + 3], rest[n_w + 3:2 * n_w + 3]
        i = pl.program_id(1)
        h0 = 2 * pl.program_id(0)
        if scatter:
            travel = lambda: _scatter_copies(scatter, widths, src, dst, *rest[2 * n_w + 3:])

            @pl.when((pl.program_id(0) == 0) & (i == 0))
            def _():
                for cp in travel():
                    cp.start()
        q = q_ref[...] * jnp.asarray(scale, BF16)
        do_blk = do_ref[...]
        prod = do_blk.astype(F32) * o_ref[...].astype(F32)
        masks = _head_masks(tb)
        row = lax.broadcasted_iota(jnp.int32, (tb, tb), 0)
        col = lax.broadcasted_iota(jnp.int32, (tb, tb), 1)
        qs = [jnp.where(masks[e], q, jnp.zeros_like(q)) for e in range(2)]
        dos = [jnp.where(masks[e], do_blk, jnp.zeros_like(do_blk)) for e in range(2)]
        deltas = [jnp.sum(jnp.where(masks[e], prod, 0.0), axis=1, keepdims=True) for e in range(2)]
        lses = [jnp.transpose(jnp.broadcast_to(lse_ref[0, e], (LANES, tb)))[:, 0:1] for e in range(2)]

        def step(j, carry, diagonal):
            off = pl.multiple_of(j * tb, tb)
            kj = k_ref[pl.ds(off, tb), :]
            vj = v_ref[pl.ds(off, tb), :]
            out = []
            for e in range(2):
                acc, rsum = carry[e]
                crow = cr_ref[0, e, :, pl.ds(off, tb)]
                s = lax.dot_general(qs[e], kj, (((1,), (1,)), ((), ())), preferred_element_type=F32) - crow
                if diagonal:
                    s = jnp.where(col <= row, s, NEG_INF)
                p = jnp.exp(s - lses[e])
                dp = lax.dot_general(dos[e], vj, (((1,), (1,)), ((), ())), preferred_element_type=F32)
                ds = p * (dp - deltas[e])
                out.append((acc + jnp.dot(ds.astype(BF16), kj, preferred_element_type=F32),
                            rsum + jnp.sum(ds, axis=1, keepdims=True)))
            return tuple(out)

        init = (jnp.zeros((tb, LANES), F32), jnp.zeros((tb, 1), F32))
        kept = jnp.maximum(_kept_before(prune_ref, h0, i, nq), _kept_before(prune_ref, h0 + 1, i, nq))
        carry = lax.fori_loop(i - kept, i, functools.partial(step, diagonal=False), (init, init))
        carry = step(i, carry, True)
        for e in range(2):
            dl_ref[0, e] = _as_row(deltas[e], tb)
            rs_ref[0, e] = _as_row(carry[e][1], tb)
        dq_ref[...] = (jnp.where(masks[0], carry[0][0], carry[1][0]) * scale).astype(dq_ref.dtype)
        if scatter:
            @pl.when((pl.program_id(0) == heads // 2 - 1) & (i == nq - 1))
            def _():
                for cp in travel():
                    cp.wait()

    blk = pl.BlockSpec((tb, LANES), lambda h, i: (i, h))
    row_spec = pl.BlockSpec((1, 2, 1, t_dim), lambda h, i: (h, 0, 0, 0))
    row_blk = pl.BlockSpec((1, 2, 1, tb), lambda h, i: (h, 0, 0, i))
    row_shape = jax.ShapeDtypeStruct((heads // 2, 2, 1, t_dim), F32)
    hbm = pl.BlockSpec(memory_space=pl.ANY)
    outs = pl.pallas_call(
        body, name=name, grid=(heads // 2, nq),
        in_specs=[pl.BlockSpec(memory_space=pltpu.SMEM), blk,
                  pl.BlockSpec((t_dim, LANES), lambda h, i: (0, cb + h)),
                  pl.BlockSpec((t_dim, LANES), lambda h, i: (0, 2 * cb + h)),
                  blk, blk, row_spec, row_blk] + [hbm] * n_w,
        out_specs=[blk, row_blk, row_blk] + [hbm] * n_w,
        out_shape=[jax.ShapeDtypeStruct((t_dim, d), BF16), row_shape, row_shape] + scatter_shapes,
        scratch_shapes=_scatter_sems(n_w) if scatter else [],
        compiler_params=_params(dimension_semantics=("arbitrary", "arbitrary")),
    )(prune, qkv, qkv, qkv, o, do, c_rows, lse_rows, *[arr for arr, _ in scatter])
    return outs[0], outs[1], outs[2], outs[3:]


def _flash_dkv(qkv, do, c_rows, lse_rows, delta_rows, prune, *, tb, name, scatter=()):
    t_dim = qkv.shape[0]
    d = qkv.shape[1] // 3
    heads = d // HEAD_DIM
    cb = d // LANES
    scale = HEAD_DIM ** -0.5
    nq = t_dim // tb
    n_w = len(scatter)
    widths, scatter_shapes = _scatter_shapes(scatter)

    def body(prune_ref, q_ref, k_ref, v_ref, do_ref, cc_ref, lr_ref, dr_ref, *rest):
        src, (dk_ref, dv_ref, dsum_ref), dst = rest[:n_w], rest[n_w:n_w + 3], rest[n_w + 3:2 * n_w + 3]
        j = pl.program_id(1)
        h0 = 2 * pl.program_id(0)
        if scatter:
            travel = lambda: _scatter_copies(scatter, widths, src, dst, *rest[2 * n_w + 3:])

            @pl.when((pl.program_id(0) == 0) & (j == 0))
            def _():
                for cp in travel():
                    cp.start()
        k_blk = k_ref[...] * jnp.asarray(scale, BF16)
        v_blk = v_ref[...]
        masks = _head_masks(tb)
        row = lax.broadcasted_iota(jnp.int32, (tb, tb), 0)
        col = lax.broadcasted_iota(jnp.int32, (tb, tb), 1)
        ks = [jnp.where(masks[e], k_blk, jnp.zeros_like(k_blk)) for e in range(2)]
        vs = [jnp.where(masks[e], v_blk, jnp.zeros_like(v_blk)) for e in range(2)]
        ccols = [jnp.transpose(jnp.broadcast_to(cc_ref[0, e], (LANES, tb)))[:, 0:1] for e in range(2)]

        def step(i, carry, diagonal):
            off = pl.multiple_of(i * tb, tb)
            qi = q_ref[pl.ds(off, tb), :]
            doi = do_ref[pl.ds(off, tb), :]
            out = []
            for e in range(2):
                dk, dv, dsum = carry[e]
                lse = lr_ref[0, e, :, pl.ds(off, tb)]
                delta = dr_ref[0, e, :, pl.ds(off, tb)]
                st = lax.dot_general(ks[e], qi, (((1,), (1,)), ((), ())), preferred_element_type=F32) - ccols[e]
                if diagonal:
                    st = jnp.where(col >= row, st, NEG_INF)
                pt = jnp.exp(st - lse)
                dpt = lax.dot_general(vs[e], doi, (((1,), (1,)), ((), ())), preferred_element_type=F32)
                dst = pt * (dpt - delta)
                out.append((dk + jnp.dot(dst.astype(BF16), qi, preferred_element_type=F32),
                            dv + jnp.dot(pt.astype(BF16), doi, preferred_element_type=F32),
                            dsum + jnp.sum(dst, axis=1, keepdims=True)))
            return tuple(out)

        zero = jnp.zeros((tb, LANES), F32)
        init = (zero, zero, jnp.zeros((tb, 1), F32))
        carry = step(j, (init, init), True)
        kept = jnp.maximum(_kept_after(prune_ref, h0, j, nq), _kept_after(prune_ref, h0 + 1, j, nq))
        carry = lax.fori_loop(j + 1, j + 1 + kept, functools.partial(step, diagonal=False), carry)
        for e in range(2):
            dsum_ref[0, e] = _as_row(carry[e][2], tb)
        dk_ref[...] = (jnp.where(masks[0], carry[0][0], carry[1][0]) * scale).astype(dk_ref.dtype)
        dv_ref[...] = jnp.where(masks[0], carry[0][1], carry[1][1]).astype(dv_ref.dtype)
        if scatter:
            @pl.when((pl.program_id(0) == heads // 2 - 1) & (j == nq - 1))
            def _():
                for cp in travel():
                    cp.wait()

    blk = pl.BlockSpec((tb, LANES), lambda h, j: (j, h))
    row_spec = pl.BlockSpec((1, 2, 1, t_dim), lambda h, j: (h, 0, 0, 0))
    row_blk = pl.BlockSpec((1, 2, 1, tb), lambda h, j: (h, 0, 0, j))
    hbm = pl.BlockSpec(memory_space=pl.ANY)
    outs = pl.pallas_call(
        body, name=name, grid=(heads // 2, nq),
        in_specs=[pl.BlockSpec(memory_space=pltpu.SMEM),
                  pl.BlockSpec((t_dim, LANES), lambda h, j: (0, h)),
                  pl.BlockSpec((tb, LANES), lambda h, j: (j, cb + h)),
                  pl.BlockSpec((tb, LANES), lambda h, j: (j, 2 * cb + h)),
                  pl.BlockSpec((t_dim, LANES), lambda h, j: (0, h)),
                  row_blk, row_spec, row_spec] + [hbm] * n_w,
        out_specs=[blk, blk, row_blk] + [hbm] * n_w,
        out_shape=[jax.ShapeDtypeStruct((t_dim, d), BF16), jax.ShapeDtypeStruct((t_dim, d), BF16),
                   jax.ShapeDtypeStruct((heads // 2, 2, 1, t_dim), F32)] + scatter_shapes,
        scratch_shapes=_scatter_sems(n_w) if scatter else [],
        compiler_params=_params(dimension_semantics=("arbitrary", "arbitrary")),
    )(prune, qkv, qkv, qkv, do, c_rows, lse_rows, delta_rows, *[arr for arr, _ in scatter])
    return outs[0], outs[1], outs[2], outs[3:]


def _shift_down(z, prev, n, tt):
    out = pltpu.roll(z, n, axis=0)
    row = lax.broadcasted_iota(jnp.int32, z.shape, 0)
    for r in range(n):
        out = jnp.where(row == r, prev[8 - n + r:8 - n + r + 1, :], out)
    return out


def _shift_up(z, nxt, n, tt):
    out = pltpu.roll(z, tt - n, axis=0)
    row = lax.broadcasted_iota(jnp.int32, z.shape, 0)
    for r in range(n):
        out = jnp.where(row == tt - n + r, nxt[r:r + 1, :], out)
    return out


def _conv_fwd(proj, conv_w, *, name, tt=256):
    t_dim, d3 = proj.shape
    d = d3 // 3
    tt = min(tt, t_dim)

    def body(p_ref, prev_ref, w_ref, y_ref):
        i = pl.program_id(0)
        p = p_ref[...]
        pp = prev_ref[...]
        z = p[:, d:2 * d] * p[:, 2 * d:]
        zp = jnp.where(i > 0, pp[:, d:2 * d] * pp[:, 2 * d:], 0.0)
        w = w_ref[...]
        zc = w[2:3, :] * z + w[1:2, :] * _shift_down(z, zp, 1, tt) + w[0:1, :] * _shift_down(z, zp, 2, tt)
        y_ref[...] = (p[:, :d] * zc).astype(y_ref.dtype)

    return pl.pallas_call(
        body, name=name, grid=(t_dim // tt,),
        in_specs=[pl.BlockSpec((tt, d3), lambda i: (i, 0)),
                  pl.BlockSpec((8, d3), lambda i: (jnp.maximum(i * (tt // 8) - 1, 0), 0)),
                  pl.BlockSpec(conv_w.shape, lambda i: (0, 0))],
        out_specs=pl.BlockSpec((tt, d), lambda i: (i, 0)),
        out_shape=jax.ShapeDtypeStruct((t_dim, d), BF16),
        compiler_params=_params(dimension_semantics=("arbitrary",)),
    )(proj, proj, conv_w)


def _conv_bwd(proj, dy, conv_w, *, name, tt=256):
    t_dim, d3 = proj.shape
    d = d3 // 3
    tt = min(tt, t_dim)
    n = t_dim // tt

    def body(p_ref, prev_ref, next_ref, dy_ref, dyn_ref, w_ref, dp_ref, dw_ref):
        i = pl.program_id(0)
        p = p_ref[...]
        pp = prev_ref[...]
        pn = next_ref[...]
        bg, cg, u = p[:, :d], p[:, d:2 * d], p[:, 2 * d:]
        z = cg * u
        zp = jnp.where(i > 0, pp[:, d:2 * d] * pp[:, 2 * d:], 0.0)
        w = w_ref[...]
        z1 = _shift_down(z, zp, 1, tt)
        z2 = _shift_down(z, zp, 2, tt)
        zc = w[2:3, :] * z + w[1:2, :] * z1 + w[0:1, :] * z2
        dy_blk = dy_ref[...]
        dzc = dy_blk * bg
        dzn = jnp.where(i < n - 1, dyn_ref[...] * pn[:, :d], 0.0)
        dz = w[2:3, :] * dzc + w[1:2, :] * _shift_up(dzc, dzn, 1, tt) + w[0:1, :] * _shift_up(dzc, dzn, 2, tt)
        dp_ref[:, :d] = (dy_blk * zc).astype(dp_ref.dtype)
        dp_ref[:, d:2 * d] = (dz * u).astype(dp_ref.dtype)
        dp_ref[:, 2 * d:] = (dz * cg).astype(dp_ref.dtype)
        part = jnp.concatenate([jnp.sum(dzc * z2, axis=0, keepdims=True),
                                jnp.sum(dzc * z1, axis=0, keepdims=True),
                                jnp.sum(dzc * z, axis=0, keepdims=True),
                                jnp.zeros((5, d), F32)], axis=0)

        @pl.when(i == 0)
        def _():
            dw_ref[...] = part

        @pl.when(i > 0)
        def _():
            dw_ref[...] += part

    last8 = t_dim // 8 - 1
    return pl.pallas_call(
        body, name=name, grid=(n,),
        in_specs=[pl.BlockSpec((tt, d3), lambda i: (i, 0)),
                  pl.BlockSpec((8, d3), lambda i: (jnp.maximum(i * (tt // 8) - 1, 0), 0)),
                  pl.BlockSpec((8, d3), lambda i: (jnp.minimum((i + 1) * (tt // 8), last8), 0)),
                  pl.BlockSpec((tt, d), lambda i: (i, 0)),
                  pl.BlockSpec((8, d), lambda i: (jnp.minimum((i + 1) * (tt // 8), last8), 0)),
                  pl.BlockSpec(conv_w.shape, lambda i: (0, 0))],
        out_specs=[pl.BlockSpec((tt, d3), lambda i: (i, 0)), pl.BlockSpec((8, d), lambda i: (0, 0))],
        out_shape=[jax.ShapeDtypeStruct((t_dim, d3), BF16), jax.ShapeDtypeStruct((8, d), F32)],
        compiler_params=_params(dimension_semantics=("arbitrary",)),
    )(proj, proj, proj, dy, dy, conv_w)


def _window(ref, axis, n, idx):
    if axis is None:
        return ref
    sel = [slice(None)] * len(ref.shape)
    sel[axis] = pl.ds(pl.multiple_of(idx * n, n), n)
    return ref.at[tuple(sel)]


def _scatter_shapes(items):
    widths, shapes = [], []
    for arr, axis in items:
        shp = list(arr.shape)
        if axis is not None:
            shp[axis] //= N_DEV
        widths.append(None if axis is None else shp[axis])
        shapes.append(jax.ShapeDtypeStruct((N_DEV, *shp), arr.dtype))
    return widths, shapes


def _scatter_copies(items, widths, src, dst, send_sems, recv_sems, local_sems):
    x, y, c = lax.axis_index("x"), lax.axis_index("y"), lax.axis_index("c")
    me = 4 * x + 2 * y + c
    copies = [pltpu.make_async_copy(_window(src[w], items[w][1], widths[w], me), dst[w].at[me], local_sems.at[w])
              for w in range(len(items))]
    for k in range(1, N_DEV):
        px = 1 - x if k & 4 else x
        py = 1 - y if k & 2 else y
        pc = 1 - c if k & 1 else c
        for w in range(len(items)):
            copies.append(pltpu.make_async_remote_copy(
                src_ref=_window(src[w], items[w][1], widths[w], 4 * px + 2 * py + pc), dst_ref=dst[w].at[me],
                send_sem=send_sems.at[w, k - 1], recv_sem=recv_sems.at[w, k - 1],
                device_id=(px, py, pc), device_id_type=pl.DeviceIdType.MESH))
    return copies


def _scatter_sems(n_w):
    return [pltpu.SemaphoreType.DMA((n_w, N_DEV - 1)), pltpu.SemaphoreType.DMA((n_w, N_DEV - 1)),
            pltpu.SemaphoreType.DMA((n_w,))]


def _gather_shapes(items):
    widths = [arr.shape[axis] for arr, axis in items]
    shapes = [jax.ShapeDtypeStruct(tuple(s * N_DEV if a == axis else s for a, s in enumerate(arr.shape)), arr.dtype)
              for arr, axis in items]
    return widths, shapes


def _gather_phases(items, widths, src, dst, send_sems, recv_sems, local_sems):
    n_w = len(items)

    def run(phase):
        x, y, c = lax.axis_index("x"), lax.axis_index("y"), lax.axis_index("c")
        chips = [(1 - x, y), (x, 1 - y), (1 - x, 1 - y)]

        def place(w, origin):
            return _window(dst[w], items[w][1], widths[w], 4 * origin[0] + 2 * origin[1] + origin[2])

        def block_copy(w, n, origin, to, from_shard):
            return pltpu.make_async_remote_copy(
                src_ref=src[w] if from_shard else place(w, origin), dst_ref=place(w, origin),
                send_sem=send_sems.at[w, n], recv_sem=recv_sems.at[w, n],
                device_id=to, device_id_type=pl.DeviceIdType.MESH)

        def own(w):
            return pltpu.make_async_copy(src[w], place(w, (x, y, c)), local_sems.at[w])

        def first(w):
            return ([block_copy(w, 0, (x, y, c), (x, y, 1 - c), True)]
                    + [block_copy(w, 1 + n, (x, y, c), (*chip, c), True) for n, chip in enumerate(chips)])

        def passed(w, n):
            return block_copy(w, 4 + n, (*chips[n], c), (x, y, 1 - c), False)

        if phase == "start":
            for w in range(n_w):
                own(w).start()
                for cp in first(w):
                    cp.start()
        elif phase == "relay":
            for n, chip in enumerate(chips):
                for w in range(n_w):
                    block_copy(w, 1 + n, (*chip, c), (x, y, c), True).wait_recv()
                    passed(w, n).start()
        else:
            for w in range(n_w):
                block_copy(w, 0, (x, y, 1 - c), (x, y, c), True).wait_recv()
                for n, chip in enumerate(chips):
                    block_copy(w, 4 + n, (*chip, 1 - c), (x, y, c), False).wait_recv()
                for cp in first(w) + [passed(w, n) for n in range(3)]:
                    cp.wait_send()
                own(w).wait()

    return [functools.partial(run, phase) for phase in ("start", "relay", "finish")]


def _exchange(items, *, gather, name):
    n_w = len(items)
    widths, out_shape = _gather_shapes(items) if gather else _scatter_shapes(items)

    def body(*refs):
        src, dst = refs[:n_w], refs[n_w:2 * n_w]
        send_sems, recv_sems, local_sems = refs[2 * n_w:]
        if not gather:
            copies = _scatter_copies(items, widths, src, dst, send_sems, recv_sems, local_sems)
            for cp in copies:
                cp.start()
            for cp in copies:
                cp.wait()
            return
        for phase in _gather_phases(items, widths, src, dst, send_sems, recv_sems, local_sems):
            phase()

    return pl.pallas_call(
        body, name=name,
        in_specs=[pl.BlockSpec(memory_space=pl.ANY)] * n_w,
        out_specs=[pl.BlockSpec(memory_space=pl.ANY)] * n_w,
        out_shape=out_shape,
        scratch_shapes=_scatter_sems(n_w),
    )(*[arr for arr, _ in items])


def _row_tile(rows, cols):
    tr = rows
    while tr % 16 == 0 and tr * cols > 256 * 1024:
        tr //= 2
    return tr


def _sum_parts(parts, *, name):
    n_parts, rows, cols = parts.shape
    tr = _row_tile(rows, cols)

    def body(p_ref, o_ref):
        g = p_ref[0].astype(F32)
        for s in range(1, n_parts):
            g = g + p_ref[s].astype(F32)
        o_ref[...] = g

    return pl.pallas_call(
        body, name=name, grid=(rows // tr,),
        in_specs=[pl.BlockSpec((n_parts, tr, cols), lambda i: (0, i, 0))],
        out_specs=pl.BlockSpec((tr, cols), lambda i: (i, 0)),
        out_shape=jax.ShapeDtypeStruct((rows, cols), F32),
        compiler_params=_params(dimension_semantics=("parallel",)),
    )(parts)


def _adamw(parts, w, m, v, *, name):
    n_parts, rows, cols = parts.shape
    tr = _row_tile(rows, cols)

    def body(p_ref, w_ref, m_ref, v_ref, g_ref, d_ref, nm_ref, nv_ref):
        g = p_ref[0].astype(F32)
        for s in range(1, n_parts):
            g = g + p_ref[s].astype(F32)
        m_new = ADAM_B1 * m_ref[...] + (1.0 - ADAM_B1) * g
        v_new = ADAM_B2 * v_ref[...] + (1.0 - ADAM_B2) * (g * g)
        m_hat = m_new / (1.0 - ADAM_B1 ** ADAM_STEP)
        v_hat = v_new / (1.0 - ADAM_B2 ** ADAM_STEP)
        g_ref[...] = g
        d_ref[...] = -ADAM_LR * (m_hat / (jnp.sqrt(v_hat) + ADAM_EPS) + ADAM_WD * w_ref[...])
        nm_ref[...] = m_new
        nv_ref[...] = v_new

    spec = pl.BlockSpec((tr, cols), lambda i: (i, 0))
    return pl.pallas_call(
        body, name=name, grid=(rows // tr,),
        in_specs=[pl.BlockSpec((n_parts, tr, cols), lambda i: (0, i, 0)), spec, spec, spec],
        out_specs=[spec] * 4,
        out_shape=[jax.ShapeDtypeStruct((rows, cols), F32)] * 4,
        compiler_params=_params(dimension_semantics=("parallel",)),
    )(parts, w, m, v)


def _pad_rows(a, axis, to):
    pad = [(0, 0)] * a.ndim
    pad[axis] = (0, to - a.shape[axis])
    return jnp.pad(a, pad)


def kernel(x, p, norm_g, w_attn_in, b_forget, w_attn_out, w_conv_in, conv_w, w_conv_out, w_mlp_up, w_mlp_down, w_ple_proj, w_ple_gate, loss_target, m_norm_g, m_w_attn_in, m_b_forget, m_w_attn_out, m_w_conv_in, m_conv_w, m_w_conv_out, m_w_mlp_up, m_w_mlp_down, m_w_ple_proj, m_w_ple_gate, v_norm_g, v_w_attn_in, v_b_forget, v_w_attn_out, v_w_conv_in, v_conv_w, v_w_conv_out, v_w_mlp_up, v_w_mlp_down, v_w_ple_proj, v_w_ple_gate):
    shards = dict(norm_g=norm_g, w_attn_in=w_attn_in, b_forget=b_forget, w_attn_out=w_attn_out,
                  w_conv_in=w_conv_in, conv_w=conv_w, w_conv_out=w_conv_out, w_mlp_up=w_mlp_up,
                  w_mlp_down=w_mlp_down, w_ple_proj=w_ple_proj, w_ple_gate=w_ple_gate)
    m_shards = dict(norm_g=m_norm_g, w_attn_in=m_w_attn_in, b_forget=m_b_forget, w_attn_out=m_w_attn_out,
                    w_conv_in=m_w_conv_in, conv_w=m_conv_w, w_conv_out=m_w_conv_out, w_mlp_up=m_w_mlp_up,
                    w_mlp_down=m_w_mlp_down, w_ple_proj=m_w_ple_proj, w_ple_gate=m_w_ple_gate)
    v_shards = dict(norm_g=v_norm_g, w_attn_in=v_w_attn_in, b_forget=v_b_forget, w_attn_out=v_w_attn_out,
                    w_conv_in=v_w_conv_in, conv_w=v_conv_w, w_conv_out=v_w_conv_out, w_mlp_up=v_w_mlp_up,
                    w_mlp_down=v_w_mlp_down, w_ple_proj=v_w_ple_proj, w_ple_gate=v_w_ple_gate)
    t_dim, d = x.shape[-2:]
    depth = p.shape[0]
    n_attn, heads = b_forget.shape
    assert d == heads * HEAD_DIM and x.shape[0] == 1
    tb = min(512, t_dim // 2)
    x0 = x.reshape(t_dim, d)
    target = loss_target.reshape(t_dim, d)
    p_rows = p.reshape(depth * t_dim, p.shape[-1])

    in_cols = w_attn_in.shape[2]
    in_cols_pad = -(-in_cols // 16) * 16
    first_names = ['norm_g', 'w_attn_in', 'conv_w']
    rest_names = [n for n in WEIGHT_NAMES if n not in first_names + ['b_forget']]

    def gather_item(n):
        if n == 'w_attn_in':
            return _pad_rows(jnp.swapaxes(w_attn_in, 1, 2), 1, in_cols_pad).astype(BF16), 1
        return (shards[n] if n in ('norm_g', 'conv_w') else shards[n].astype(BF16)), SHARD_AXIS[n]

    full = dict(zip(first_names, _exchange([gather_item(n) for n in first_names], gather=True, name="gather_first")))
    gains = full['norm_g']
    taps = full['conv_w']
    w_in_t = full['w_attn_in'].reshape(n_attn, N_DEV, in_cols_pad, d)[:, :, :in_cols]
    w_in_t = _pad_rows(w_in_t.reshape(n_attn, N_DEV * in_cols, d), 1, 3 * d + LANES)
    bias_pad = jnp.pad(b_forget, ((0, 0), (0, LANES - heads)))

    def gain(i, k):
        return gains[i, k].reshape(1, d)

    def add_norm(x_prev, branch, g_branch, g_next, name):
        def fn(rows, vecs):
            x_new = rows[0] + _norm(rows[1], vecs[0])
            return [x_new, _norm(x_new, vecs[1])], []
        return _rows_call(fn, [x_prev, branch], [g_branch, g_next], [(d, F32), (d, BF16)], [], name=name)

    saved = []
    x_cur = x0
    hn = _rows_call(lambda rows, vecs: ([_norm(rows[0], vecs[0])], []), [x0], [gain(0, 0)], [(d, BF16)], [],
                    name="norm_in")[0]
    loss_rows = dy = None
    for i in range(depth):
        j = i // 2
        s = dict(x0=x_cur, hn=hn)
        if i % 2 == 0:
            s['qkv'] = _mm(hn, w_in_t[j, :3 * d], tb=True, out_dtypes=(BF16,), name=f"attn_in_{i}")
            s['fl'] = _mm(hn, w_in_t[j, 3 * d:], tb=True, name=f"attn_gate_{i}")
            c = _cumsum_fwd(s['fl'], bias_pad[j:j + 1], name=f"gate_cumsum_{i}")
            c_t = c[:, :heads].T
            s['prune'] = _prune_table(c_t, _head_norms(s['qkv'], name=f"head_norms_{i}"), tb)
            s['c_rows'] = c_t.reshape(heads // 2, 2, 1, t_dim)
            s['o'], s['lse_rows'], rest = _flash_fwd(
                s['qkv'], s['c_rows'], s['prune'], tb=tb, name=f"attn_fwd_{i}",
                gather=[gather_item(n) for n in rest_names] if i == 0 else ())
            if i == 0:
                full.update(zip(rest_names, rest))
            s['m'] = _mm(s['o'], full['w_attn_out'][j], name=f"attn_out_{i}")
        else:
            s['proj'] = _mm(hn, full['w_conv_in'][j], name=f"conv_in_{i}")
            s['y'] = _conv_fwd(s['proj'], taps[j], name=f"conv_fwd_{i}")
            s['m'] = _mm(s['y'], full['w_conv_out'][j], name=f"conv_out_{i}")
        s['x1'], s['h2'] = add_norm(x_cur, s['m'], gain(i, 1), gain(i, 2), f"mix_norm_{i}")
        s['u'], s['a'] = _mm(s['h2'], full['w_mlp_up'][i], out_dtypes=(BF16, BF16), name=f"mlp_up_{i}",
                             epi=lambda acc: (acc, jnp.square(jnp.maximum(acc, 0.0))))
        s['f'] = _mm(s['a'], full['w_mlp_down'][i], name=f"mlp_down_{i}")
        s['x2'], s['h4'] = add_norm(s['x1'], s['f'], gain(i, 3), gain(i, 4), f"mlp_norm_{i}")
        s['pp'] = _mm(p_rows, full['w_ple_proj'][i], a_rows=t_dim, a_off=i * t_dim, name=f"ple_proj_{i}")
        s['gl'], s['e'] = _mm(s['h4'], full['w_ple_gate'][i], extras=(s['pp'],), out_dtypes=(F32, F32),
                              name=f"ple_gate_{i}", epi=lambda acc, pp: (acc, pp * _sigmoid(acc)))
        if i + 1 < depth:
            x_cur, hn = add_norm(s['x2'], s['e'], gain(i, 5), gain(i + 1, 0), f"ple_norm_{i}")
        else:
            def loss_fn(rows, vecs):
                err = rows[0] + _norm(rows[1], vecs[0]) - rows[2]
                part = 0.5 * jnp.sum(jnp.sum(err * err, axis=1, keepdims=True), axis=0, keepdims=True) / d
                return [err / d], [jnp.broadcast_to(part, (1, LANES))]
            dy, loss_rows = _rows_call(loss_fn, [s['x2'], s['e'], target], [gain(i, 5)], [(d, F32)],
                                       [(1, LANES)], name="loss")
        saved.append(s)
    loss = lax.psum(loss_rows[0, 0], ("x", "y", "c"))

    grads = {n: [None] * shards[n].shape[0] for n in WEIGHT_NAMES}
    d_gains = [[None] * 6 for _ in range(depth)]
    wgrad = functools.partial(_mm, ta=True, out_dtypes=(BF16,))
    dgrad = functools.partial(_mm, out_dtypes=(BF16,))
    axis_of = dict(SHARD_AXIS, w_attn_in=1)

    def in_t_blocks(layers):
        g = jnp.stack(layers)[:, :N_DEV * in_cols].reshape(len(layers), N_DEV, in_cols, d)
        return _pad_rows(g, 2, in_cols_pad).reshape(len(layers), N_DEV * in_cols_pad, d)

    early_names = early_items = early = None
    dx = dy
    for i in reversed(range(depth)):
        j = i // 2
        s = saved[i]

        def ple_fn(rows, vecs):
            de, dg = _norm_bwd(rows[0], vecs[0], rows[1])
            sg = _sigmoid(rows[2])
            return [de * sg, de * rows[3] * sg * (1.0 - sg)], [dg]
        dpp, dgl, d_gains[i][5] = _rows_call(ple_fn, [s['e'], dx, s['gl'], s['pp']], [gain(i, 5)],
                                             [(d, BF16), (d, BF16)], [(1, d)], name=f"ple_bwd_{i}")
        grads['w_ple_proj'][i] = wgrad(p_rows, dpp, a_rows=t_dim, a_off=i * t_dim, name=f"ple_proj_dw_{i}")
        grads['w_ple_gate'][i] = wgrad(s['h4'], dgl, name=f"ple_gate_dw_{i}")
        dh4 = dgrad(dgl, full['w_ple_gate'][i], tb=True, name=f"ple_gate_dx_{i}")

        def two_norm_bwd(x_res, dh, dx_in, branch, g_res, g_branch, name):
            def fn(rows, vecs):
                d_res, dg_res = _norm_bwd(rows[0], vecs[0], rows[1])
                dx_out = rows[2] + d_res
                d_branch, dg_branch = _norm_bwd(rows[3], vecs[1], dx_out)
                return [dx_out, d_branch], [dg_res, dg_branch]
            return _rows_call(fn, [x_res, dh, dx_in, branch], [g_res, g_branch], [(d, F32), (d, BF16)],
                              [(1, d), (1, d)], name=name)

        dx2, df, d_gains[i][4], d_gains[i][3] = two_norm_bwd(s['x2'], dh4, dx, s['f'], gain(i, 4), gain(i, 3),
                                                            f"mlp_norm_bwd_{i}")
        grads['w_mlp_down'][i] = wgrad(s['a'], df, name=f"mlp_down_dw_{i}")
        du = _mm(df, full['w_mlp_down'][i], tb=True, extras=(s['u'],), out_dtypes=(BF16,), name=f"mlp_down_dx_{i}",
                 epi=lambda acc, u: (acc * (2.0 * jnp.maximum(u.astype(F32), 0.0)),))
        grads['w_mlp_up'][i] = wgrad(s['h2'], du, name=f"mlp_up_dw_{i}")
        dh2 = dgrad(du, full['w_mlp_up'][i], tb=True, name=f"mlp_up_dx_{i}")
        dx1, dm, d_gains[i][2], d_gains[i][1] = two_norm_bwd(s['x1'], dh2, dx2, s['m'], gain(i, 2), gain(i, 1),
                                                            f"mix_norm_bwd_{i}")
        if i % 2 == 0:
            grads['w_attn_out'][j] = wgrad(s['o'], dm, name=f"attn_out_dw_{i}")
            do = _mm(dm, full['w_attn_out'][j], tb=True, out_dtypes=(BF16,), name=f"attn_out_dx_{i}")
            with_dq = ['w_mlp_up', 'w_mlp_down', 'w_ple_proj', 'w_ple_gate'] if i == 0 else []
            with_dkv = [n for n in WEIGHT_NAMES if n not in with_dq + ['norm_g', 'b_forget']] if i == 0 else []
            item = lambda n: (in_t_blocks(grads[n][1:]) if n == 'w_attn_in' else jnp.stack(grads[n]), axis_of[n])
            dq, delta_rows, rsum_rows, got_dq = _flash_dq(s['qkv'], s['o'], do, s['c_rows'], s['lse_rows'], s['prune'],
                                                          tb=tb, name=f"attn_dq_{i}", scatter=[item(n) for n in with_dq])
            dk, dv, csum_rows, got_dkv = _flash_dkv(s['qkv'], do, s['c_rows'], s['lse_rows'], delta_rows, s['prune'],
                                                    tb=tb, name=f"attn_dkv_{i}", scatter=[item(n) for n in with_dkv])
            if i == 0:
                early_names, early = with_dq + with_dkv, [*got_dq, *got_dkv]
            dc = (rsum_rows - csum_rows).reshape(heads, t_dim).T
            dfl, db = _cumsum_bwd(jnp.pad(dc, ((0, 0), (0, LANES - heads))), s['fl'], bias_pad[j:j + 1],
                                  name=f"gate_cumsum_bwd_{i}")
            grads['b_forget'][j] = db[0, :heads]
            dproj = jnp.concatenate([dq, dk, dv, dfl], axis=1)
            grads['w_attn_in'][j] = wgrad(dproj, s['hn'], name=f"attn_in_dw_{i}")
            dhn = dgrad(dproj, w_in_t[j], name=f"attn_in_dx_{i}")
        else:
            grads['w_conv_out'][j] = wgrad(s['y'], dm, name=f"conv_out_dw_{i}")
            dyc = _mm(dm, full['w_conv_out'][j], tb=True, name=f"conv_out_dx_{i}")
            dproj, dtaps = _conv_bwd(s['proj'], dyc, taps[j], name=f"conv_bwd_{i}")
            grads['conv_w'][j] = dtaps[:3].astype(BF16)
            grads['w_conv_in'][j] = wgrad(s['hn'], dproj, name=f"conv_in_dw_{i}")
            dhn = dgrad(dproj, full['w_conv_in'][j], tb=True, name=f"conv_in_dx_{i}")

        def in_fn(rows, vecs):
            d_res, dg = _norm_bwd(rows[0], vecs[0], rows[1])
            return [rows[2] + d_res], [dg]
        dx, d_gains[i][0] = _rows_call(in_fn, [s['x0'], dhn, dx1], [gain(i, 0)], [(d, F32)], [(1, d)],
                                       name=f"in_norm_bwd_{i}")
    grad_x = dx.reshape(x.shape)

    recv = dict(zip(early_names, early))
    late = _exchange([(in_t_blocks(grads['w_attn_in'][:1]), 1),
                      (jnp.stack([jnp.concatenate(row, axis=0) for row in d_gains]).astype(BF16), SHARD_AXIS['norm_g']),
                      (jnp.zeros((8, LANES), F32).at[:n_attn, :heads].set(jnp.stack(grads['b_forget'])), None)],
                     gather=False, name="exchange_late")
    recv['norm_g'] = late[1]
    recv['b_forget'] = late[2][:, :n_attn, :heads]
    g_in_t = jnp.concatenate([_sum_parts(part.reshape(N_DEV, -1, d), name=f"sum_attn_in_{k}")
                              for k, part in enumerate((late[0], recv['w_attn_in']))], axis=0)
    recv['w_attn_in'] = jnp.swapaxes(g_in_t.reshape(n_attn, in_cols_pad, d)[:, :in_cols], 1, 2)[None]
    results = {}
    for n in WEIGHT_NAMES:
        shp = shards[n].shape
        flat = lambda a: a.reshape(a.shape[:a.ndim - len(shp)] + (-1, shp[-1]))
        outs = _adamw(flat(recv[n]), flat(shards[n]), flat(m_shards[n]), flat(v_shards[n]), name=f"adamw_{n}")
        results[n] = [o.reshape(shp) for o in outs]
    return (loss, grad_x, *[results[n][k] for k in range(4) for n in WEIGHT_NAMES])
```

```python
import functools

import jax
import jax.numpy as jnp
from jax import lax
from jax.experimental import pallas as pl
from jax.experimental.pallas import tpu as pltpu

F32 = jnp.float32
BF16 = jnp.bfloat16

N_DEV = 8
LANES = 128
HEAD_DIM = 64
VMEM_LIMIT_BYTES = 56 * 1024 * 1024
RMS_EPS = 1e-6
NEG_INF = -1e30
ADAM_LR = 0.001
ADAM_B1 = 0.9
ADAM_B2 = 0.999
ADAM_EPS = 1e-08
ADAM_WD = 0.01
ADAM_STEP = 10
WEIGHT_NAMES = ('norm_g', 'w_attn_in', 'b_forget', 'w_attn_out', 'w_conv_in', 'conv_w', 'w_conv_out',
                'w_mlp_up', 'w_mlp_down', 'w_ple_proj', 'w_ple_gate')
SHARD_AXIS = {'norm_g': 2, 'w_attn_in': 2, 'b_forget': None, 'w_attn_out': 1, 'w_conv_in': 2, 'conv_w': 2,
              'w_conv_out': 1, 'w_mlp_up': 2, 'w_mlp_down': 1, 'w_ple_proj': 2, 'w_ple_gate': 1}


def _params(**kw):
    return pltpu.CompilerParams(vmem_limit_bytes=VMEM_LIMIT_BYTES, **kw)


def _tile(n, cap):
    if n <= cap:
        return n
    t = (cap // LANES) * LANES
    while n % t:
        t -= LANES
    return t


MM_VMEM_BUDGET = 36 * 1024 * 1024


def _mm(a, b, *, ta=False, tb=False, extras=(), epi=None, out_dtypes=(F32,), name, a_rows=None, a_off=0):
    rows_a = a_rows or a.shape[0]
    m_dim, k_dim = (a.shape[1], rows_a) if ta else (rows_a, a.shape[1])
    n_dim = b.shape[0] if tb else b.shape[1]
    assert k_dim == (b.shape[1] if tb else b.shape[0]) and a_off % rows_a == 0
    tk = _tile(k_dim, 1024 if k_dim <= 1024 else 2048)
    nk = k_dim // tk
    tn = _tile(n_dim, 1024)

    def vmem_bytes(tm):
        per_mn = sum(jnp.dtype(dt).itemsize for dt in out_dtypes) + sum(e.dtype.itemsize for e in extras)
        return (2 * (tm * tk * a.dtype.itemsize + tk * tn * b.dtype.itemsize) + 2 * tm * tn * per_mn
                + tm * tn * 4 * (2 + (nk > 1)))

    tm = next(t for t in (_tile(m_dim, 1024), _tile(m_dim, 512)) if t <= 512 or vmem_bytes(t) <= MM_VMEM_BUDGET)
    grid = (n_dim // tn, m_dim // tm, nk)
    off_m, off_k = (0, a_off // tk) if ta else (a_off // tm, 0)
    a_spec = (pl.BlockSpec((tk, tm), lambda j, i, k: (k + off_k, i)) if ta
              else pl.BlockSpec((tm, tk), lambda j, i, k: (i + off_m, k)))
    b_spec = (pl.BlockSpec((tn, tk), lambda j, i, k: (j, k)) if tb
              else pl.BlockSpec((tk, tn), lambda j, i, k: (k, j)))
    mn_spec = pl.BlockSpec((tm, tn), lambda j, i, k: (i, j))
    dims = (((0 if ta else 1,), (1 if tb else 0,)), ((), ()))
    n_extra, n_out = len(extras), len(out_dtypes)
    if epi is None:
        epi = lambda acc: (acc,)

    def body(a_ref, b_ref, *rest):
        e_refs, o_refs = rest[:n_extra], rest[n_extra:n_extra + n_out]
        part = lax.dot_general(a_ref[...].astype(BF16), b_ref[...].astype(BF16), dims,
                               preferred_element_type=F32)

        def finish(acc):
            for o_ref, val in zip(o_refs, epi(acc, *[e[...] for e in e_refs])):
                o_ref[...] = val.astype(o_ref.dtype)

        if nk == 1:
            finish(part)
        else:
            acc_ref = rest[-1]
            k = pl.program_id(2)

            @pl.when(k == 0)
            def _():
                acc_ref[...] = part

            @pl.when(k > 0)
            def _():
                acc_ref[...] += part

            @pl.when(k == nk - 1)
            def _():
                finish(acc_ref[...])

    outs = pl.pallas_call(
        body, name=name, grid=grid,
        in_specs=[a_spec, b_spec] + [mn_spec] * n_extra,
        out_specs=[mn_spec] * n_out,
        out_shape=[jax.ShapeDtypeStruct((m_dim, n_dim), dt) for dt in out_dtypes],
        scratch_shapes=[pltpu.VMEM((tm, tn), F32)] if nk > 1 else [],
        compiler_params=_params(dimension_semantics=("parallel", "parallel", "arbitrary")),
    )(a, b, *extras)
    return outs[0] if n_out == 1 else outs


def _rows(fn, row_ins, vec_ins, row_outs, vec_outs, *, name, tt=512, reverse=False):
    t_dim = row_ins[0].shape[0]
    tt = min(tt, t_dim)
    n = t_dim // tt
    n_ri, n_vi, n_ro, n_vo = len(row_ins), len(vec_ins), len(row_outs), len(vec_outs)
    pos = (lambda i: (n - 1 - i, 0)) if reverse else (lambda i: (i, 0))
    fixed = lambda i: (0, 0)

    def body(*refs):
        ri = refs[:n_ri]
        vi = refs[n_ri:n_ri + n_vi]
        ro = refs[n_ri + n_vi:n_ri + n_vi + n_ro]
        vo = refs[n_ri + n_vi + n_ro:n_ri + n_vi + n_ro + n_vo]
        scratch = refs[n_ri + n_vi + n_ro + n_vo:]
        r_out, v_out = fn([r[...] for r in ri], [v[...] for v in vi], *scratch)
        for o_ref, val in zip(ro, r_out):
            o_ref[...] = val.astype(o_ref.dtype)
        i = pl.program_id(0)
        for o_ref, val in zip(vo, v_out):
            @pl.when(i == 0)
            def _(o_ref=o_ref, val=val):
                o_ref[...] = val

            @pl.when(i > 0)
            def _(o_ref=o_ref, val=val):
                o_ref[...] += val

    return body, dict(
        grid=(n,),
        in_specs=[pl.BlockSpec((tt, r.shape[1]), pos) for r in row_ins]
        + [pl.BlockSpec(v.shape, fixed) for v in vec_ins],
        out_specs=[pl.BlockSpec((tt, w), pos) for w, _ in row_outs]
        + [pl.BlockSpec(s, fixed) for s in vec_outs],
        out_shape=[jax.ShapeDtypeStruct((t_dim, w), dt) for w, dt in row_outs]
        + [jax.ShapeDtypeStruct(s, F32) for s in vec_outs],
        name=name,
        compiler_params=_params(dimension_semantics=("arbitrary",)),
    )


def _rows_call(fn, row_ins, vec_ins, row_outs, vec_outs, *, name, tt=512, reverse=False, scratch=()):
    body, kw = _rows(fn, row_ins, vec_ins, row_outs, vec_outs, name=name, tt=tt, reverse=reverse)
    return pl.pallas_call(body, scratch_shapes=list(scratch), **kw)(*row_ins, *vec_ins)


def _rstd(x):
    return lax.rsqrt(jnp.mean(x * x, axis=-1, keepdims=True) + RMS_EPS)


def _norm(x, g):
    return x * _rstd(x) * g


def _norm_bwd(x, g, dy):
    xh = x * _rstd(x)
    gy = dy * g
    dx = _rstd(x) * (gy - xh * jnp.mean(gy * xh, axis=-1, keepdims=True))
    return dx, jnp.sum(dy * xh, axis=0, keepdims=True)


def _sigmoid(x):
    return 1.0 / (1.0 + jnp.exp(-x))


def _log_sigmoid(x):
    return jnp.minimum(x, 0.0) - jnp.log(1.0 + jnp.exp(-jnp.abs(x)))


def _split3(x):
    hi = x.astype(BF16)
    r1 = x - hi.astype(F32)
    mid = r1.astype(BF16)
    lo = (r1 - mid.astype(F32)).astype(BF16)
    return hi, mid, lo


def _cumsum_fwd(fl, bias, *, name):
    w = fl.shape[1]
    tt = min(512, fl.shape[0])

    def fn(rows, vecs, carry_ref):
        i = pl.program_id(0)

        @pl.when(i == 0)
        def _():
            carry_ref[...] = jnp.zeros_like(carry_ref)

        lf = _log_sigmoid(rows[0] + vecs[0])
        r = lax.broadcasted_iota(jnp.int32, (tt, tt), 0)
        c = lax.broadcasted_iota(jnp.int32, (tt, tt), 1)
        tri = (c <= r).astype(BF16)
        acc = carry_ref[0:1, :]
        for part in _split3(lf):
            acc = acc + jnp.dot(tri, part, preferred_element_type=F32)
        carry_ref[0:1, :] = acc[tt - 1:tt, :]
        return [acc], []

    return _rows_call(fn, [fl], [bias], [(w, F32)], [], name=name, tt=tt,
                      scratch=[pltpu.VMEM((8, w), F32)])[0]


def _cumsum_bwd(dc, fl, bias, *, name):
    w = fl.shape[1]
    tt = min(512, fl.shape[0])

    def fn(rows, vecs, carry_ref):
        i = pl.program_id(0)

        @pl.when(i == 0)
        def _():
            carry_ref[...] = jnp.zeros_like(carry_ref)

        r = lax.broadcasted_iota(jnp.int32, (tt, tt), 0)
        c = lax.broadcasted_iota(jnp.int32, (tt, tt), 1)
        tri = (c >= r).astype(BF16)
        acc = carry_ref[0:1, :]
        for part in _split3(rows[0]):
            acc = acc + jnp.dot(tri, part, preferred_element_type=F32)
        carry_ref[0:1, :] = acc[0:1, :]
        dfl = acc * _sigmoid(-(rows[1] + vecs[0]))
        return [dfl], [jnp.sum(dfl, axis=0, keepdims=True)]

    return _rows_call(fn, [dc, fl], [bias], [(w, BF16)], [(1, w)], name=name, tt=tt, reverse=True,
                      scratch=[pltpu.VMEM((8, w), F32)])


def _head_masks(tb):
    lane = lax.broadcasted_iota(jnp.int32, (tb, LANES), 1)
    return [lane < HEAD_DIM, lane >= HEAD_DIM]


PRUNE_MARGIN = 30.0


def _head_norms(qkv, *, name):
    t_dim = qkv.shape[0]
    d = qkv.shape[1] // 3
    heads = d // HEAD_DIM
    tt = min(512, t_dim)

    def body(q_ref, k_ref, o_ref):
        col = lax.broadcasted_iota(jnp.int32, (d, LANES), 0) // HEAD_DIM
        lane = lax.broadcasted_iota(jnp.int32, (d, LANES), 1)
        tile_max = None
        for ref, first in ((q_ref, 0), (k_ref, heads)):
            x = ref[...].astype(F32)
            sums = jnp.dot((x * x).astype(BF16), (col + first == lane).astype(BF16), preferred_element_type=F32)
            part = jnp.max(sums, axis=0, keepdims=True)
            tile_max = part if tile_max is None else jnp.maximum(tile_max, part)
        i = pl.program_id(0)

        @pl.when(i == 0)
        def _():
            o_ref[...] = tile_max

        @pl.when(i > 0)
        def _():
            o_ref[...] = jnp.maximum(o_ref[...], tile_max)

    return pl.pallas_call(
        body, name=name, grid=(t_dim // tt,),
        in_specs=[pl.BlockSpec((tt, d), lambda i: (i, 0)), pl.BlockSpec((tt, d), lambda i: (i, 1))],
        out_specs=pl.BlockSpec((1, LANES), lambda i: (0, 0)),
        out_shape=jax.ShapeDtypeStruct((1, LANES), F32),
        compiler_params=_params(dimension_semantics=("arbitrary",)),
    )(qkv, qkv)


def _prune_table(c_t, norms, tb):
    heads = c_t.shape[0]
    bound = 1.02 * HEAD_DIM ** -0.5 * jnp.sqrt(norms[0, :heads] * norms[0, heads:2 * heads])
    return jnp.concatenate([c_t[:, ::tb], c_t[:, tb - 1::tb], -(PRUNE_MARGIN + 2.0 * bound)[:, None]], axis=1)


def _kept_before(prune_ref, h, i, nq):
    first, thr = prune_ref[h, i], prune_ref[h, 2 * nq]
    return lax.while_loop(lambda n: (n < i) & (first - prune_ref[h, nq + jnp.maximum(i - 1 - n, 0)] >= thr),
                          lambda n: n + 1, jnp.int32(0))


def _kept_after(prune_ref, h, j, nq):
    last, thr = prune_ref[h, nq + j], prune_ref[h, 2 * nq]
    return lax.while_loop(lambda n: (j + 1 + n < nq) & (prune_ref[h, jnp.minimum(j + 1 + n, nq - 1)] - last >= thr),
                          lambda n: n + 1, jnp.int32(0))


def _as_row(col, tb):
    return jnp.transpose(jnp.broadcast_to(col, (tb, LANES)))[0:1, :]


def _flash_fwd(qkv, c_rows, prune, *, tb, name, gather=()):
    t_dim = qkv.shape[0]
    d = qkv.shape[1] // 3
    heads = d // HEAD_DIM
    cb = d // LANES
    nq = t_dim // tb
    n_w = len(gather)
    widths, gather_shapes = _gather_shapes(gather)

    def body(prune_ref, q_ref, k_ref, v_ref, cr_ref, *rest):
        src, (o_ref, lser_ref), dst = rest[:n_w], rest[n_w:n_w + 2], rest[n_w + 2:2 * n_w + 2]
        i = pl.program_id(1)
        h0 = 2 * pl.program_id(0)
        if gather:
            start, relay, finish = _gather_phases(gather, widths, src, dst, *rest[2 * n_w + 2:])
            pl.when((pl.program_id(0) == 0) & (i == 0))(start)
            pl.when((pl.program_id(0) == (3 * heads) // 8) & (i == 0))(relay)
        q = q_ref[...] * jnp.asarray(HEAD_DIM ** -0.5, BF16)
        masks = _head_masks(tb)
        row = lax.broadcasted_iota(jnp.int32, (tb, tb), 0)
        col = lax.broadcasted_iota(jnp.int32, (tb, tb), 1)
        qs = [jnp.where(masks[e], q, jnp.zeros_like(q)) for e in range(2)]

        def step(j, carry, diagonal):
            off = pl.multiple_of(j * tb, tb)
            kj = k_ref[pl.ds(off, tb), :]
            vj = v_ref[pl.ds(off, tb), :]
            out = []
            for e in range(2):
                m, l, acc = carry[e]
                crow = cr_ref[0, e, :, pl.ds(off, tb)]
                s = lax.dot_general(qs[e], kj, (((1,), (1,)), ((), ())), preferred_element_type=F32) - crow
                if diagonal:
                    s = jnp.where(col <= row, s, NEG_INF)
                m_new = jnp.maximum(m, jnp.max(s, axis=1, keepdims=True))
                p = jnp.exp(s - m_new)
                alpha = jnp.exp(m - m_new)
                l = alpha * l + jnp.sum(p, axis=1, keepdims=True)
                acc = alpha * acc + jnp.dot(p.astype(BF16), vj, preferred_element_type=F32)
                out.append((m_new, l, acc))
            return tuple(out)

        init = (jnp.full((tb, 1), NEG_INF, F32), jnp.zeros((tb, 1), F32), jnp.zeros((tb, LANES), F32))
        kept = jnp.maximum(_kept_before(prune_ref, h0, i, nq), _kept_before(prune_ref, h0 + 1, i, nq))
        carry = lax.fori_loop(i - kept, i, functools.partial(step, diagonal=False), (init, init))
        carry = step(i, carry, True)
        outs = []
        for e in range(2):
            m, l, acc = carry[e]
            outs.append(acc / l)
            lser_ref[0, e] = _as_row(m + jnp.log(l), tb)
        o_ref[...] = jnp.where(masks[0], outs[0], outs[1]).astype(o_ref.dtype)
        if gather:
            pl.when((pl.program_id(0) == heads // 2 - 1) & (i == nq - 1))(finish)

    row_spec = pl.BlockSpec((1, 2, 1, t_dim), lambda h, i: (h, 0, 0, 0))
    row_blk = pl.BlockSpec((1, 2, 1, tb), lambda h, i: (h, 0, 0, i))
    hbm = pl.BlockSpec(memory_space=pl.ANY)
    outs = pl.pallas_call(
        body, name=name, grid=(heads // 2, nq),
        in_specs=[pl.BlockSpec(memory_space=pltpu.SMEM),
                  pl.BlockSpec((tb, LANES), lambda h, i: (i, h)),
                  pl.BlockSpec((t_dim, LANES), lambda h, i: (0, cb + h)),
                  pl.BlockSpec((t_dim, LANES), lambda h, i: (0, 2 * cb + h)),
                  row_spec] + [hbm] * n_w,
        out_specs=[pl.BlockSpec((tb, LANES), lambda h, i: (i, h)), row_blk] + [hbm] * n_w,
        out_shape=[jax.ShapeDtypeStruct((t_dim, d), BF16),
                   jax.ShapeDtypeStruct((heads // 2, 2, 1, t_dim), F32)] + gather_shapes,
        scratch_shapes=_scatter_sems(n_w) if gather else [],
        compiler_params=_params(dimension_semantics=("arbitrary", "arbitrary")),
    )(prune, qkv, qkv, qkv, c_rows, *[arr for arr, _ in gather])
    return outs[0], outs[1], outs[2:]


def _flash_dq(qkv, o, do, c_rows, lse_rows, prune, *, tb, name, scatter=()):
    t_dim = qkv.shape[0]
    d = qkv.shape[1] // 3
    heads = d // HEAD_DIM
    cb = d // LANES
    scale = HEAD_DIM ** -0.5
    nq = t_dim // tb
    n_w = len(scatter)
    widths, scatter_shapes = _scatter_shapes(scatter)

    def body(prune_ref, q_ref, k_ref, v_ref, o_ref, do_ref, cr_ref, lse_ref, *rest):
        src, (dq_ref, dl_ref, rs_ref), dst = rest[:n_w], rest[n_w:n_w + 3], rest[n_w + 3:2 * n_w + 3]
        i = pl.program_id(1)
        h0 = 2 * pl.program_id(0)
        if scatter:
            travel = lambda: _scatter_copies(scatter, widths, src, dst, *rest[2 * n_w + 3:])

            @pl.when((pl.program_id(0) == 0) & (i == 0))
            def _():
                for cp in travel():
                    cp.start()
        q = q_ref[...] * jnp.asarray(scale, BF16)
        do_blk = do_ref[...]
        prod = do_blk.astype(F32) * o_ref[...].astype(F32)
        masks = _head_masks(tb)
        row = lax.broadcasted_iota(jnp.int32, (tb, tb), 0)
        col = lax.broadcasted_iota(jnp.int32, (tb, tb), 1)
        qs = [jnp.where(masks[e], q, jnp.zeros_like(q)) for e in range(2)]
        dos = [jnp.where(masks[e], do_blk, jnp.zeros_like(do_blk)) for e in range(2)]
        deltas = [jnp.sum(jnp.where(masks[e], prod, 0.0), axis=1, keepdims=True) for e in range(2)]
        lses = [jnp.transpose(jnp.broadcast_to(lse_ref[0, e], (LANES, tb)))[:, 0:1] for e in range(2)]

        def step(j, carry, diagonal):
            off = pl.multiple_of(j * tb, tb)
            kj = k_ref[pl.ds(off, tb), :]
            vj = v_ref[pl.ds(off, tb), :]
            out = []
            for e in range(2):
                acc, rsum = carry[e]
                crow = cr_ref[0, e, :, pl.ds(off, tb)]
                s = lax.dot_general(qs[e], kj, (((1,), (1,)), ((), ())), preferred_element_type=F32) - crow
                if diagonal:
                    s = jnp.where(col <= row, s, NEG_INF)
                p = jnp.exp(s - lses[e])
                dp = lax.dot_general(dos[e], vj, (((1,), (1,)), ((), ())), preferred_element_type=F32)
                ds = p * (dp - deltas[e])
                out.append((acc + jnp.dot(ds.astype(BF16), kj, preferred_element_type=F32),
                            rsum + jnp.sum(ds, axis=1, keepdims=True)))
            return tuple(out)

        init = (jnp.zeros((tb, LANES), F32), jnp.zeros((tb, 1), F32))
        kept = jnp.maximum(_kept_before(prune_ref, h0, i, nq), _kept_before(prune_ref, h0 + 1, i, nq))
        carry = lax.fori_loop(i - kept, i, functools.partial(step, diagonal=False), (init, init))
        carry = step(i, carry, True)
        for e in range(2):
            dl_ref[0, e] = _as_row(deltas[e], tb)
            rs_ref[0, e] = _as_row(carry[e][1], tb)
        dq_ref[...] = (jnp.where(masks[0], carry[0][0], carry[1][0]) * scale).astype(dq_ref.dtype)
        if scatter:
            @pl.when((pl.program_id(0) == heads // 2 - 1) & (i == nq - 1))
            def _():
                for cp in travel():
                    cp.wait()

    blk = pl.BlockSpec((tb, LANES), lambda h, i: (i, h))
    row_spec = pl.BlockSpec((1, 2, 1, t_dim), lambda h, i: (h, 0, 0, 0))
    row_blk = pl.BlockSpec((1, 2, 1, tb), lambda h, i: (h, 0, 0, i))
    row_shape = jax.ShapeDtypeStruct((heads // 2, 2, 1, t_dim), F32)
    hbm = pl.BlockSpec(memory_space=pl.ANY)
    outs = pl.pallas_call(
        body, name=name, grid=(heads // 2, nq),
        in_specs=[pl.BlockSpec(memory_space=pltpu.SMEM), blk,
                  pl.BlockSpec((t_dim, LANES), lambda h, i: (0, cb + h)),
                  pl.BlockSpec((t_dim, LANES), lambda h, i: (0, 2 * cb + h)),
                  blk, blk, row_spec, row_blk] + [hbm] * n_w,
        out_specs=[blk, row_blk, row_blk] + [hbm] * n_w,
        out_shape=[jax.ShapeDtypeStruct((t_dim, d), BF16), row_shape, row_shape] + scatter_shapes,
        scratch_shapes=_scatter_sems(n_w) if scatter else [],
        compiler_params=_params(dimension_semantics=("arbitrary", "arbitrary")),
    )(prune, qkv, qkv, qkv, o, do, c_rows, lse_rows, *[arr for arr, _ in scatter])
    return outs[0], outs[1], outs[2], outs[3:]


def _flash_dkv(qkv, do, c_rows, lse_rows, delta_rows, prune, *, tb, name, scatter=()):
    t_dim = qkv.shape[0]
    d = qkv.shape[1] // 3
    heads = d // HEAD_DIM
    cb = d // LANES
    scale = HEAD_DIM ** -0.5
    nq = t_dim // tb
    n_w = len(scatter)
    widths, scatter_shapes = _scatter_shapes(scatter)

    def body(prune_ref, q_ref, k_ref, v_ref, do_ref, cc_ref, lr_ref, dr_ref, *rest):
        src, (dk_ref, dv_ref, dsum_ref), dst = rest[:n_w], rest[n_w:n_w + 3], rest[n_w + 3:2 * n_w + 3]
        j = pl.program_id(1)
        h0 = 2 * pl.program_id(0)
        if scatter:
            travel = lambda: _scatter_copies(scatter, widths, src, dst, *rest[2 * n_w + 3:])

            @pl.when((pl.program_id(0) == 0) & (j == 0))
            def _():
                for cp in travel():
                    cp.start()
        k_blk = k_ref[...] * jnp.asarray(scale, BF16)
        v_blk = v_ref[...]
        masks = _head_masks(tb)
        row = lax.broadcasted_iota(jnp.int32, (tb, tb), 0)
        col = lax.broadcasted_iota(jnp.int32, (tb, tb), 1)
        ks = [jnp.where(masks[e], k_blk, jnp.zeros_like(k_blk)) for e in range(2)]
        vs = [jnp.where(masks[e], v_blk, jnp.zeros_like(v_blk)) for e in range(2)]
        ccols = [jnp.transpose(jnp.broadcast_to(cc_ref[0, e], (LANES, tb)))[:, 0:1] for e in range(2)]

        def step(i, carry, diagonal):
            off = pl.multiple_of(i * tb, tb)
            qi = q_ref[pl.ds(off, tb), :]
            doi = do_ref[pl.ds(off, tb), :]
            out = []
            for e in range(2):
                dk, dv, dsum = carry[e]
                lse = lr_ref[0, e, :, pl.ds(off, tb)]
                delta = dr_ref[0, e, :, pl.ds(off, tb)]
                st = lax.dot_general(ks[e], qi, (((1,), (1,)), ((), ())), preferred_element_type=F32) - ccols[e]
                if diagonal:
                    st = jnp.where(col >= row, st, NEG_INF)
                pt = jnp.exp(st - lse)
                dpt = lax.dot_general(vs[e], doi, (((1,), (1,)), ((), ())), preferred_element_type=F32)
                dst = pt * (dpt - delta)
                out.append((dk + jnp.dot(dst.astype(BF16), qi, preferred_element_type=F32),
                            dv + jnp.dot(pt.astype(BF16), doi, preferred_element_type=F32),
                            dsum + jnp.sum(dst, axis=1, keepdims=True)))
            return tuple(out)

        zero = jnp.zeros((tb, LANES), F32)
        init = (zero, zero, jnp.zeros((tb, 1), F32))
        carry = step(j, (init, init), True)
        kept = jnp.maximum(_kept_after(prune_ref, h0, j, nq), _kept_after(prune_ref, h0 + 1, j, nq))
        carry = lax.fori_loop(j + 1, j + 1 + kept, functools.partial(step, diagonal=False), carry)
        for e in range(2):
            dsum_ref[0, e] = _as_row(carry[e][2], tb)
        dk_ref[...] = (jnp.where(masks[0], carry[0][0], carry[1][0]) * scale).astype(dk_ref.dtype)
        dv_ref[...] = jnp.where(masks[0], carry[0][1], carry[1][1]).astype(dv_ref.dtype)
        if scatter:
            @pl.when((pl.program_id(0) == heads // 2 - 1) & (j == nq - 1))
            def _():
                for cp in travel():
                    cp.wait()

    blk = pl.BlockSpec((tb, LANES), lambda h, j: (j, h))
    row_spec = pl.BlockSpec((1, 2, 1, t_dim), lambda h, j: (h, 0, 0, 0))
    row_blk = pl.BlockSpec((1, 2, 1, tb), lambda h, j: (h, 0, 0, j))
    hbm = pl.BlockSpec(memory_space=pl.ANY)
    outs = pl.pallas_call(
        body, name=name, grid=(heads // 2, nq),
        in_specs=[pl.BlockSpec(memory_space=pltpu.SMEM),
                  pl.BlockSpec((t_dim, LANES), lambda h, j: (0, h)),
                  pl.BlockSpec((tb, LANES), lambda h, j: (j, cb + h)),
                  pl.BlockSpec((tb, LANES), lambda h, j: (j, 2 * cb + h)),
                  pl.BlockSpec((t_dim, LANES), lambda h, j: (0, h)),
                  row_blk, row_spec, row_spec] + [hbm] * n_w,
        out_specs=[blk, blk, row_blk] + [hbm] * n_w,
        out_shape=[jax.ShapeDtypeStruct((t_dim, d), BF16), jax.ShapeDtypeStruct((t_dim, d), BF16),
                   jax.ShapeDtypeStruct((heads // 2, 2, 1, t_dim), F32)] + scatter_shapes,
        scratch_shapes=_scatter_sems(n_w) if scatter else [],
        compiler_params=_params(dimension_semantics=("arbitrary", "arbitrary")),
    )(prune, qkv, qkv, qkv, do, c_rows, lse_rows, delta_rows, *[arr for arr, _ in scatter])
    return outs[0], outs[1], outs[2], outs[3:]


def _shift_down(z, prev, n, tt):
    out = pltpu.roll(z, n, axis=0)
    row = lax.broadcasted_iota(jnp.int32, z.shape, 0)
    for r in range(n):
        out = jnp.where(row == r, prev[8 - n + r:8 - n + r + 1, :], out)
    return out


def _shift_up(z, nxt, n, tt):
    out = pltpu.roll(z, tt - n, axis=0)
    row = lax.broadcasted_iota(jnp.int32, z.shape, 0)
    for r in range(n):
        out = jnp.where(row == tt - n + r, nxt[r:r + 1, :], out)
    return out


def _conv_fwd(proj, conv_w, *, name, tt=256):
    t_dim, d3 = proj.shape
    d = d3 // 3
    tt = min(tt, t_dim)

    def body(p_ref, prev_ref, w_ref, y_ref):
        i = pl.program_id(0)
        p = p_ref[...]
        pp = prev_ref[...]
        z = p[:, d:2 * d] * p[:, 2 * d:]
        zp = jnp.where(i > 0, pp[:, d:2 * d] * pp[:, 2 * d:], 0.0)
        w = w_ref[...]
        zc = w[2:3, :] * z + w[1:2, :] * _shift_down(z, zp, 1, tt) + w[0:1, :] * _shift_down(z, zp, 2, tt)
        y_ref[...] = (p[:, :d] * zc).astype(y_ref.dtype)

    return pl.pallas_call(
        body, name=name, grid=(t_dim // tt,),
        in_specs=[pl.BlockSpec((tt, d3), lambda i: (i, 0)),
                  pl.BlockSpec((8, d3), lambda i: (jnp.maximum(i * (tt // 8) - 1, 0), 0)),
                  pl.BlockSpec(conv_w.shape, lambda i: (0, 0))],
        out_specs=pl.BlockSpec((tt, d), lambda i: (i, 0)),
        out_shape=jax.ShapeDtypeStruct((t_dim, d), BF16),
        compiler_params=_params(dimension_semantics=("arbitrary",)),
    )(proj, proj, conv_w)


def _conv_bwd(proj, dy, conv_w, *, name, tt=256):
    t_dim, d3 = proj.shape
    d = d3 // 3
    tt = min(tt, t_dim)
    n = t_dim // tt

    def body(p_ref, prev_ref, next_ref, dy_ref, dyn_ref, w_ref, dp_ref, dw_ref):
        i = pl.program_id(0)
        p = p_ref[...]
        pp = prev_ref[...]
        pn = next_ref[...]
        bg, cg, u = p[:, :d], p[:, d:2 * d], p[:, 2 * d:]
        z = cg * u
        zp = jnp.where(i > 0, pp[:, d:2 * d] * pp[:, 2 * d:], 0.0)
        w = w_ref[...]
        z1 = _shift_down(z, zp, 1, tt)
        z2 = _shift_down(z, zp, 2, tt)
        zc = w[2:3, :] * z + w[1:2, :] * z1 + w[0:1, :] * z2
        dy_blk = dy_ref[...]
        dzc = dy_blk * bg
        dzn = jnp.where(i < n - 1, dyn_ref[...] * pn[:, :d], 0.0)
        dz = w[2:3, :] * dzc + w[1:2, :] * _shift_up(dzc, dzn, 1, tt) + w[0:1, :] * _shift_up(dzc, dzn, 2, tt)
        dp_ref[:, :d] = (dy_blk * zc).astype(dp_ref.dtype)
        dp_ref[:, d:2 * d] = (dz * u).astype(dp_ref.dtype)
        dp_ref[:, 2 * d:] = (dz * cg).astype(dp_ref.dtype)
        part = jnp.concatenate([jnp.sum(dzc * z2, axis=0, keepdims=True),
                                jnp.sum(dzc * z1, axis=0, keepdims=True),
                                jnp.sum(dzc * z, axis=0, keepdims=True),
                                jnp.zeros((5, d), F32)], axis=0)

        @pl.when(i == 0)
        def _():
            dw_ref[...] = part

        @pl.when(i > 0)
        def _():
            dw_ref[...] += part

    last8 = t_dim // 8 - 1
    return pl.pallas_call(
        body, name=name, grid=(n,),
        in_specs=[pl.BlockSpec((tt, d3), lambda i: (i, 0)),
                  pl.BlockSpec((8, d3), lambda i: (jnp.maximum(i * (tt // 8) - 1, 0), 0)),
                  pl.BlockSpec((8, d3), lambda i: (jnp.minimum((i + 1) * (tt // 8), last8), 0)),
                  pl.BlockSpec((tt, d), lambda i: (i, 0)),
                  pl.BlockSpec((8, d), lambda i: (jnp.minimum((i + 1) * (tt // 8), last8), 0)),
                  pl.BlockSpec(conv_w.shape, lambda i: (0, 0))],
        out_specs=[pl.BlockSpec((tt, d3), lambda i: (i, 0)), pl.BlockSpec((8, d), lambda i: (0, 0))],
        out_shape=[jax.ShapeDtypeStruct((t_dim, d3), BF16), jax.ShapeDtypeStruct((8, d), F32)],
        compiler_params=_params(dimension_semantics=("arbitrary",)),
    )(proj, proj, proj, dy, dy, conv_w)


def _window(ref, axis, n, idx):
    if axis is None:
        return ref
    sel = [slice(None)] * len(ref.shape)
    sel[axis] = pl.ds(pl.multiple_of(idx * n, n), n)
    return ref.at[tuple(sel)]


def _scatter_shapes(items):
    widths, shapes = [], []
    for arr, axis in items:
        shp = list(arr.shape)
        if axis is not None:
            shp[axis] //= N_DEV
        widths.append(None if axis is None else shp[axis])
        shapes.append(jax.ShapeDtypeStruct((N_DEV, *shp), arr.dtype))
    return widths, shapes


def _scatter_copies(items, widths, src, dst, send_sems, recv_sems, local_sems):
    x, y, c = lax.axis_index("x"), lax.axis_index("y"), lax.axis_index("c")
    me = 4 * x + 2 * y + c
    copies = [pltpu.make_async_copy(_window(src[w], items[w][1], widths[w], me), dst[w].at[me], local_sems.at[w])
              for w in range(len(items))]
    for k in range(1, N_DEV):
        px = 1 - x if k & 4 else x
        py = 1 - y if k & 2 else y
        pc = 1 - c if k & 1 else c
        for w in range(len(items)):
            copies.append(pltpu.make_async_remote_copy(
                src_ref=_window(src[w], items[w][1], widths[w], 4 * px + 2 * py + pc), dst_ref=dst[w].at[me],
                send_sem=send_sems.at[w, k - 1], recv_sem=recv_sems.at[w, k - 1],
                device_id=(px, py, pc), device_id_type=pl.DeviceIdType.MESH))
    return copies


def _scatter_sems(n_w):
    return [pltpu.SemaphoreType.DMA((n_w, N_DEV - 1)), pltpu.SemaphoreType.DMA((n_w, N_DEV - 1)),
            pltpu.SemaphoreType.DMA((n_w,))]


def _gather_shapes(items):
    widths = [arr.shape[axis] for arr, axis in items]
    shapes = [jax.ShapeDtypeStruct(tuple(s * N_DEV if a == axis else s for a, s in enumerate(arr.shape)), arr.dtype)
              for arr, axis in items]
    return widths, shapes


def _gather_phases(items, widths, src, dst, send_sems, recv_sems, local_sems):
    n_w = len(items)

    def run(phase):
        x, y, c = lax.axis_index("x"), lax.axis_index("y"), lax.axis_index("c")
        chips = [(1 - x, y), (x, 1 - y), (1 - x, 1 - y)]

        def place(w, origin):
            return _window(dst[w], items[w][1], widths[w], 4 * origin[0] + 2 * origin[1] + origin[2])

        def block_copy(w, n, origin, to, from_shard):
            return pltpu.make_async_remote_copy(
                src_ref=src[w] if from_shard else place(w, origin), dst_ref=place(w, origin),
                send_sem=send_sems.at[w, n], recv_sem=recv_sems.at[w, n],
                device_id=to, device_id_type=pl.DeviceIdType.MESH)

        def own(w):
            return pltpu.make_async_copy(src[w], place(w, (x, y, c)), local_sems.at[w])

        def first(w):
            return ([block_copy(w, 0, (x, y, c), (x, y, 1 - c), True)]
                    + [block_copy(w, 1 + n, (x, y, c), (*chip, c), True) for n, chip in enumerate(chips)])

        def passed(w, n):
            return block_copy(w, 4 + n, (*chips[n], c), (x, y, 1 - c), False)

        if phase == "start":
            for w in range(n_w):
                own(w).start()
                for cp in first(w):
                    cp.start()
        elif phase == "relay":
            for n, chip in enumerate(chips):
                for w in range(n_w):
                    block_copy(w, 1 + n, (*chip, c), (x, y, c), True).wait_recv()
                    passed(w, n).start()
        else:
            for w in range(n_w):
                block_copy(w, 0, (x, y, 1 - c), (x, y, c), True).wait_recv()
                for n, chip in enumerate(chips):
                    block_copy(w, 4 + n, (*chip, 1 - c), (x, y, c), False).wait_recv()
                for cp in first(w) + [passed(w, n) for n in range(3)]:
                    cp.wait_send()
                own(w).wait()

    return [functools.partial(run, phase) for phase in ("start", "relay", "finish")]


def _exchange(items, *, gather, name):
    n_w = len(items)
    widths, out_shape = _gather_shapes(items) if gather else _scatter_shapes(items)

    def body(*refs):
        src, dst = refs[:n_w], refs[n_w:2 * n_w]
        send_sems, recv_sems, local_sems = refs[2 * n_w:]
        if not gather:
            copies = _scatter_copies(items, widths, src, dst, send_sems, recv_sems, local_sems)
            for cp in copies:
                cp.start()
            for cp in copies:
                cp.wait()
            return
        for phase in _gather_phases(items, widths, src, dst, send_sems, recv_sems, local_sems):
            phase()

    return pl.pallas_call(
        body, name=name,
        in_specs=[pl.BlockSpec(memory_space=pl.ANY)] * n_w,
        out_specs=[pl.BlockSpec(memory_space=pl.ANY)] * n_w,
        out_shape=out_shape,
        scratch_shapes=_scatter_sems(n_w),
    )(*[arr for arr, _ in items])


def _row_tile(rows, cols):
    tr = rows
    while tr % 16 == 0 and tr * cols > 256 * 1024:
        tr //= 2
    return tr


def _sum_parts(parts, *, name):
    n_parts, rows, cols = parts.shape
    tr = _row_tile(rows, cols)

    def body(p_ref, o_ref):
        g = p_ref[0].astype(F32)
        for s in range(1, n_parts):
            g = g + p_ref[s].astype(F32)
        o_ref[...] = g

    return pl.pallas_call(
        body, name=name, grid=(rows // tr,),
        in_specs=[pl.BlockSpec((n_parts, tr, cols), lambda i: (0, i, 0))],
        out_specs=pl.BlockSpec((tr, cols), lambda i: (i, 0)),
        out_shape=jax.ShapeDtypeStruct((rows, cols), F32),
        compiler_params=_params(dimension_semantics=("parallel",)),
    )(parts)


def _adamw(parts, w, m, v, *, name):
    n_parts, rows, cols = parts.shape
    tr = _row_tile(rows, cols)

    def body(p_ref, w_ref, m_ref, v_ref, g_ref, d_ref, nm_ref, nv_ref):
        g = p_ref[0].astype(F32)
        for s in range(1, n_parts):
            g = g + p_ref[s].astype(F32)
        m_new = ADAM_B1 * m_ref[...] + (1.0 - ADAM_B1) * g
        v_new = ADAM_B2 * v_ref[...] + (1.0 - ADAM_B2) * (g * g)
        m_hat = m_new / (1.0 - ADAM_B1 ** ADAM_STEP)
        v_hat = v_new / (1.0 - ADAM_B2 ** ADAM_STEP)
        g_ref[...] = g
        d_ref[...] = -ADAM_LR * (m_hat / (jnp.sqrt(v_hat) + ADAM_EPS) + ADAM_WD * w_ref[...])
        nm_ref[...] = m_new
        nv_ref[...] = v_new

    spec = pl.BlockSpec((tr, cols), lambda i: (i, 0))
    return pl.pallas_call(
        body, name=name, grid=(rows // tr,),
        in_specs=[pl.BlockSpec((n_parts, tr, cols), lambda i: (0, i, 0)), spec, spec, spec],
        out_specs=[spec] * 4,
        out_shape=[jax.ShapeDtypeStruct((rows, cols), F32)] * 4,
        compiler_params=_params(dimension_semantics=("parallel",)),
    )(parts, w, m, v)


def _pad_rows(a, axis, to):
    pad = [(0, 0)] * a.ndim
    pad[axis] = (0, to - a.shape[axis])
    return jnp.pad(a, pad)


def kernel(x, p, norm_g, w_attn_in, b_forget, w_attn_out, w_conv_in, conv_w, w_conv_out, w_mlp_up, w_mlp_down, w_ple_proj, w_ple_gate, loss_target, m_norm_g, m_w_attn_in, m_b_forget, m_w_attn_out, m_w_conv_in, m_conv_w, m_w_conv_out, m_w_mlp_up, m_w_mlp_down, m_w_ple_proj, m_w_ple_gate, v_norm_g, v_w_attn_in, v_b_forget, v_w_attn_out, v_w_conv_in, v_conv_w, v_w_conv_out, v_w_mlp_up, v_w_mlp_down, v_w_ple_proj, v_w_ple_gate):
    shards = dict(norm_g=norm_g, w_attn_in=w_attn_in, b_forget=b_forget, w_attn_out=w_attn_out,
                  w_conv_in=w_conv_in, conv_w=conv_w, w_conv_out=w_conv_out, w_mlp_up=w_mlp_up,
                  w_mlp_down=w_mlp_down, w_ple_proj=w_ple_proj, w_ple_gate=w_ple_gate)
    m_shards = dict(norm_g=m_norm_g, w_attn_in=m_w_attn_in, b_forget=m_b_forget, w_attn_out=m_w_attn_out,
                    w_conv_in=m_w_conv_in, conv_w=m_conv_w, w_conv_out=m_w_conv_out, w_mlp_up=m_w_mlp_up,
                    w_mlp_down=m_w_mlp_down, w_ple_proj=m_w_ple_proj, w_ple_gate=m_w_ple_gate)
    v_shards = dict(norm_g=v_norm_g, w_attn_in=v_w_attn_in, b_forget=v_b_forget, w_attn_out=v_w_attn_out,
                    w_conv_in=v_w_conv_in, conv_w=v_conv_w, w_conv_out=v_w_conv_out, w_mlp_up=v_w_mlp_up,
                    w_mlp_down=v_w_mlp_down, w_ple_proj=v_w_ple_proj, w_ple_gate=v_w_ple_gate)
    t_dim, d = x.shape[-2:]
    depth = p.shape[0]
    n_attn, heads = b_forget.shape
    assert d == heads * HEAD_DIM and x.shape[0] == 1
    tb = min(512, t_dim // 2)
    x0 = x.reshape(t_dim, d)
    target = loss_target.reshape(t_dim, d)
    p_rows = p.reshape(depth * t_dim, p.shape[-1])

    in_cols = w_attn_in.shape[2]
    in_cols_pad = -(-in_cols // 16) * 16
    first_names = ['norm_g', 'w_attn_in', 'conv_w']
    rest_names = [n for n in WEIGHT_NAMES if n not in first_names + ['b_forget']]

    def gather_item(n):
        if n == 'w_attn_in':
            return _pad_rows(jnp.swapaxes(w_attn_in, 1, 2), 1, in_cols_pad).astype(BF16), 1
        return (shards[n] if n in ('norm_g', 'conv_w') else shards[n].astype(BF16)), SHARD_AXIS[n]

    full = dict(zip(first_names, _exchange([gather_item(n) for n in first_names], gather=True, name="gather_first")))
    gains = full['norm_g']
    taps = full['conv_w']
    w_in_t = full['w_attn_in'].reshape(n_attn, N_DEV, in_cols_pad, d)[:, :, :in_cols]
    w_in_t = _pad_rows(w_in_t.reshape(n_attn, N_DEV * in_cols, d), 1, 3 * d + LANES)
    bias_pad = jnp.pad(b_forget, ((0, 0), (0, LANES - heads)))

    def gain(i, k):
        return gains[i, k].reshape(1, d)

    def add_norm(x_prev, branch, g_branch, g_next, name):
        def fn(rows, vecs):
            x_new = rows[0] + _norm(rows[1], vecs[0])
            return [x_new, _norm(x_new, vecs[1])], []
        return _rows_call(fn, [x_prev, branch], [g_branch, g_next], [(d, F32), (d, BF16)], [], name=name)

    saved = []
    x_cur = x0
    hn = _rows_call(lambda rows, vecs: ([_norm(rows[0], vecs[0])], []), [x0], [gain(0, 0)], [(d, BF16)], [],
                    name="norm_in")[0]
    loss_rows = dy = None
    for i in range(depth):
        j = i // 2
        s = dict(x0=x_cur, hn=hn)
        if i % 2 == 0:
            s['qkv'] = _mm(hn, w_in_t[j, :3 * d], tb=True, out_dtypes=(BF16,), name=f"attn_in_{i}")
            s['fl'] = _mm(hn, w_in_t[j, 3 * d:], tb=True, name=f"attn_gate_{i}")
            c = _cumsum_fwd(s['fl'], bias_pad[j:j + 1], name=f"gate_cumsum_{i}")
            c_t = c[:, :heads].T
            s['prune'] = _prune_table(c_t, _head_norms(s['qkv'], name=f"head_norms_{i}"), tb)
            s['c_rows'] = c_t.reshape(heads // 2, 2, 1, t_dim)
            s['o'], s['lse_rows'], rest = _flash_fwd(
                s['qkv'], s['c_rows'], s['prune'], tb=tb, name=f"attn_fwd_{i}",
                gather=[gather_item(n) for n in rest_names] if i == 0 else ())
            if i == 0:
                full.update(zip(rest_names, rest))
            s['m'] = _mm(s['o'], full['w_attn_out'][j], out_dtypes=(BF16,), name=f"attn_out_{i}")
        else:
            s['proj'] = _mm(hn, full['w_conv_in'][j], name=f"conv_in_{i}")
            s['y'] = _conv_fwd(s['proj'], taps[j], name=f"conv_fwd_{i}")
            s['m'] = _mm(s['y'], full['w_conv_out'][j], out_dtypes=(BF16,), name=f"conv_out_{i}")
        s['x1'], s['h2'] = add_norm(x_cur, s['m'], gain(i, 1), gain(i, 2), f"mix_norm_{i}")
        s['u'], s['a'] = _mm(s['h2'], full['w_mlp_up'][i], out_dtypes=(BF16, BF16), name=f"mlp_up_{i}",
                             epi=lambda acc: (acc, jnp.square(jnp.maximum(acc, 0.0))))
        s['f'] = _mm(s['a'], full['w_mlp_down'][i], out_dtypes=(BF16,), name=f"mlp_down_{i}")
        s['x2'], s['h4'] = add_norm(s['x1'], s['f'], gain(i, 3), gain(i, 4), f"mlp_norm_{i}")
        s['pp'] = _mm(p_rows, full['w_ple_proj'][i], a_rows=t_dim, a_off=i * t_dim, name=f"ple_proj_{i}")
        s['gl'], s['e'] = _mm(s['h4'], full['w_ple_gate'][i], extras=(s['pp'],), out_dtypes=(F32, BF16),
                              name=f"ple_gate_{i}", epi=lambda acc, pp: (acc, pp * _sigmoid(acc)))
        if i + 1 < depth:
            x_cur, hn = add_norm(s['x2'], s['e'], gain(i, 5), gain(i + 1, 0), f"ple_norm_{i}")
        else:
            def loss_fn(rows, vecs):
                err = rows[0] + _norm(rows[1], vecs[0]) - rows[2]
                part = 0.5 * jnp.sum(jnp.sum(err * err, axis=1, keepdims=True), axis=0, keepdims=True) / d
                return [err / d], [jnp.broadcast_to(part, (1, LANES))]
            dy, loss_rows = _rows_call(loss_fn, [s['x2'], s['e'], target], [gain(i, 5)], [(d, F32)],
                                       [(1, LANES)], name="loss")
        saved.append(s)
    loss = lax.psum(loss_rows[0, 0], ("x", "y", "c"))

    grads = {n: [None] * shards[n].shape[0] for n in WEIGHT_NAMES}
    d_gains = [[None] * 6 for _ in range(depth)]
    wgrad = functools.partial(_mm, ta=True, out_dtypes=(BF16,))
    dgrad = functools.partial(_mm, out_dtypes=(BF16,))
    axis_of = dict(SHARD_AXIS, w_attn_in=1)

    def in_t_blocks(layers):
        g = jnp.stack(layers)[:, :N_DEV * in_cols].reshape(len(layers), N_DEV, in_cols, d)
        return _pad_rows(g, 2, in_cols_pad).reshape(len(layers), N_DEV * in_cols_pad, d)

    early_names = early_items = early = None
    dx = dy
    for i in reversed(range(depth)):
        j = i // 2
        s = saved[i]

        def ple_fn(rows, vecs):
            de, dg = _norm_bwd(rows[0], vecs[0], rows[1])
            sg = _sigmoid(rows[2])
            return [de * sg, de * rows[3] * sg * (1.0 - sg)], [dg]
        dpp, dgl, d_gains[i][5] = _rows_call(ple_fn, [s['e'], dx, s['gl'], s['pp']], [gain(i, 5)],
                                             [(d, BF16), (d, BF16)], [(1, d)], name=f"ple_bwd_{i}")
        grads['w_ple_proj'][i] = wgrad(p_rows, dpp, a_rows=t_dim, a_off=i * t_dim, name=f"ple_proj_dw_{i}")
        grads['w_ple_gate'][i] = wgrad(s['h4'], dgl, name=f"ple_gate_dw_{i}")
        dh4 = dgrad(dgl, full['w_ple_gate'][i], tb=True, name=f"ple_gate_dx_{i}")

        def two_norm_bwd(x_res, dh, dx_in, branch, g_res, g_branch, name):
            def fn(rows, vecs):
                d_res, dg_res = _norm_bwd(rows[0], vecs[0], rows[1])
                dx_out = rows[2] + d_res
                d_branch, dg_branch = _norm_bwd(rows[3], vecs[1], dx_out)
                return [dx_out, d_branch], [dg_res, dg_branch]
            return _rows_call(fn, [x_res, dh, dx_in, branch], [g_res, g_branch], [(d, F32), (d, BF16)],
                              [(1, d), (1, d)], name=name)

        dx2, df, d_gains[i][4], d_gains[i][3] = two_norm_bwd(s['x2'], dh4, dx, s['f'], gain(i, 4), gain(i, 3),
                                                            f"mlp_norm_bwd_{i}")
        grads['w_mlp_down'][i] = wgrad(s['a'], df, name=f"mlp_down_dw_{i}")
        du = _mm(df, full['w_mlp_down'][i], tb=True, extras=(s['u'],), out_dtypes=(BF16,), name=f"mlp_down_dx_{i}",
                 epi=lambda acc, u: (acc * (2.0 * jnp.maximum(u.astype(F32), 0.0)),))
        grads['w_mlp_up'][i] = wgrad(s['h2'], du, name=f"mlp_up_dw_{i}")
        dh2 = dgrad(du, full['w_mlp_up'][i], tb=True, name=f"mlp_up_dx_{i}")
        dx1, dm, d_gains[i][2], d_gains[i][1] = two_norm_bwd(s['x1'], dh2, dx2, s['m'], gain(i, 2), gain(i, 1),
                                                            f"mix_norm_bwd_{i}")
        if i % 2 == 0:
            grads['w_attn_out'][j] = wgrad(s['o'], dm, name=f"attn_out_dw_{i}")
            do = _mm(dm, full['w_attn_out'][j], tb=True, out_dtypes=(BF16,), name=f"attn_out_dx_{i}")
            with_dq = ['w_mlp_up', 'w_mlp_down', 'w_ple_proj', 'w_ple_gate'] if i == 0 else []
            with_dkv = [n for n in WEIGHT_NAMES if n not in with_dq + ['norm_g', 'b_forget']] if i == 0 else []
            item = lambda n: (in_t_blocks(grads[n][1:]) if n == 'w_attn_in' else jnp.stack(grads[n]), axis_of[n])
            dq, delta_rows, rsum_rows, got_dq = _flash_dq(s['qkv'], s['o'], do, s['c_rows'], s['lse_rows'], s['prune'],
                                                          tb=tb, name=f"attn_dq_{i}", scatter=[item(n) for n in with_dq])
            dk, dv, csum_rows, got_dkv = _flash_dkv(s['qkv'], do, s['c_rows'], s['lse_rows'], delta_rows, s['prune'],
                                                    tb=tb, name=f"attn_dkv_{i}", scatter=[item(n) for n in with_dkv])
            if i == 0:
                early_names, early = with_dq + with_dkv, [*got_dq, *got_dkv]
            dc = (rsum_rows - csum_rows).reshape(heads, t_dim).T
            dfl, db = _cumsum_bwd(jnp.pad(dc, ((0, 0), (0, LANES - heads))), s['fl'], bias_pad[j:j + 1],
                                  name=f"gate_cumsum_bwd_{i}")
            grads['b_forget'][j] = db[0, :heads]
            dproj = jnp.concatenate([dq, dk, dv, dfl], axis=1)
            grads['w_attn_in'][j] = wgrad(dproj, s['hn'], name=f"attn_in_dw_{i}")
            dhn = dgrad(dproj, w_in_t[j], name=f"attn_in_dx_{i}")
        else:
            grads['w_conv_out'][j] = wgrad(s['y'], dm, name=f"conv_out_dw_{i}")
            dyc = _mm(dm, full['w_conv_out'][j], tb=True, name=f"conv_out_dx_{i}")
            dproj, dtaps = _conv_bwd(s['proj'], dyc, taps[j], name=f"conv_bwd_{i}")
            grads['conv_w'][j] = dtaps[:3].astype(BF16)
            grads['w_conv_in'][j] = wgrad(s['hn'], dproj, name=f"conv_in_dw_{i}")
            dhn = dgrad(dproj, full['w_conv_in'][j], tb=True, name=f"conv_in_dx_{i}")

        def in_fn(rows, vecs):
            d_res, dg = _norm_bwd(rows[0], vecs[0], rows[1])
            return [rows[2] + d_res], [dg]
        dx, d_gains[i][0] = _rows_call(in_fn, [s['x0'], dhn, dx1], [gain(i, 0)], [(d, F32)], [(1, d)],
                                       name=f"in_norm_bwd_{i}")
    grad_x = dx.reshape(x.shape)

    recv = dict(zip(early_names, early))
    late = _exchange([(in_t_blocks(grads['w_attn_in'][:1]), 1),
                      (jnp.stack([jnp.concatenate(row, axis=0) for row in d_gains]).astype(BF16), SHARD_AXIS['norm_g']),
                      (jnp.zeros((8, LANES), F32).at[:n_attn, :heads].set(jnp.stack(grads['b_forget'])), None)],
                     gather=False, name="exchange_late")
    recv['norm_g'] = late[1]
    recv['b_forget'] = late[2][:, :n_attn, :heads]
    g_in_t = jnp.concatenate([_sum_parts(part.reshape(N_DEV, -1, d), name=f"sum_attn_in_{k}")
                              for k, part in enumerate((late[0], recv['w_attn_in']))], axis=0)
    recv['w_attn_in'] = jnp.swapaxes(g_in_t.reshape(n_attn, in_cols_pad, d)[:, :in_cols], 1, 2)[None]
    results = {}
    for n in WEIGHT_NAMES:
        shp = shards[n].shape
        flat = lambda a: a.reshape(a.shape[:a.ndim - len(shp)] + (-1, shp[-1]))
        outs = _adamw(flat(recv[n]), flat(shards[n]), flat(m_shards[n]), flat(v_shards[n]), name=f"adamw_{n}")
        results[n] = [o.reshape(shp) for o in outs]
    return (loss, grad_x, *[results[n][k] for k in range(4) for n in WEIGHT_NAMES])
```

```python
import functools

import jax
import jax.numpy as jnp
from jax import lax
from jax.experimental import pallas as pl
from jax.experimental.pallas import tpu as pltpu

F32 = jnp.float32
BF16 = jnp.bfloat16

N_DEV = 8
LANES = 128
HEAD_DIM = 64
VMEM_LIMIT_BYTES = 56 * 1024 * 1024
RMS_EPS = 1e-6
NEG_INF = -1e30
ADAM_LR = 0.001
ADAM_B1 = 0.9
ADAM_B2 = 0.999
ADAM_EPS = 1e-08
ADAM_WD = 0.01
ADAM_STEP = 10
WEIGHT_NAMES = ('norm_g', 'w_attn_in', 'b_forget', 'w_attn_out', 'w_conv_in', 'conv_w', 'w_conv_out',
                'w_mlp_up', 'w_mlp_down', 'w_ple_proj', 'w_ple_gate')
SHARD_AXIS = {'norm_g': 2, 'w_attn_in': 2, 'b_forget': None, 'w_attn_out': 1, 'w_conv_in': 2, 'conv_w': 2,
              'w_conv_out': 1, 'w_mlp_up': 2, 'w_mlp_down': 1, 'w_ple_proj': 2, 'w_ple_gate': 1}


def _params(**kw):
    return pltpu.CompilerParams(vmem_limit_bytes=VMEM_LIMIT_BYTES, **kw)


def _tile(n, cap):
    if n <= cap:
        return n
    t = (cap // LANES) * LANES
    while n % t:
        t -= LANES
    return t


MM_VMEM_BUDGET = 36 * 1024 * 1024


def _mm(a, b, *, ta=False, tb=False, extras=(), epi=None, out_dtypes=(F32,), name, a_rows=None, a_off=0):
    rows_a = a_rows or a.shape[0]
    m_dim, k_dim = (a.shape[1], rows_a) if ta else (rows_a, a.shape[1])
    n_dim = b.shape[0] if tb else b.shape[1]
    assert k_dim == (b.shape[1] if tb else b.shape[0]) and a_off % rows_a == 0
    tk = _tile(k_dim, 1024 if k_dim <= 1024 else 2048)
    nk = k_dim // tk
    tn = _tile(n_dim, 1024)

    def vmem_bytes(tm):
        per_mn = sum(jnp.dtype(dt).itemsize for dt in out_dtypes) + sum(e.dtype.itemsize for e in extras)
        return (2 * (tm * tk * a.dtype.itemsize + tk * tn * b.dtype.itemsize) + 2 * tm * tn * per_mn
                + tm * tn * 4 * (2 + (nk > 1)))

    tm = next(t for t in (_tile(m_dim, 1024), _tile(m_dim, 512)) if t <= 512 or vmem_bytes(t) <= MM_VMEM_BUDGET)
    grid = (n_dim // tn, m_dim // tm, nk)
    off_m, off_k = (0, a_off // tk) if ta else (a_off // tm, 0)
    a_spec = (pl.BlockSpec((tk, tm), lambda j, i, k: (k + off_k, i)) if ta
              else pl.BlockSpec((tm, tk), lambda j, i, k: (i + off_m, k)))
    b_spec = (pl.BlockSpec((tn, tk), lambda j, i, k: (j, k)) if tb
              else pl.BlockSpec((tk, tn), lambda j, i, k: (k, j)))
    mn_spec = pl.BlockSpec((tm, tn), lambda j, i, k: (i, j))
    dims = (((0 if ta else 1,), (1 if tb else 0,)), ((), ()))
    n_extra, n_out = len(extras), len(out_dtypes)
    if epi is None:
        epi = lambda acc: (acc,)

    def body(a_ref, b_ref, *rest):
        e_refs, o_refs = rest[:n_extra], rest[n_extra:n_extra + n_out]
        part = lax.dot_general(a_ref[...].astype(BF16), b_ref[...].astype(BF16), dims,
                               preferred_element_type=F32)

        def finish(acc):
            for o_ref, val in zip(o_refs, epi(acc, *[e[...] for e in e_refs])):
                o_ref[...] = val.astype(o_ref.dtype)

        if nk == 1:
            finish(part)
        else:
            acc_ref = rest[-1]
            k = pl.program_id(2)

            @pl.when(k == 0)
            def _():
                acc_ref[...] = part

            @pl.when(k > 0)
            def _():
                acc_ref[...] += part

            @pl.when(k == nk - 1)
            def _():
                finish(acc_ref[...])

    outs = pl.pallas_call(
        body, name=name, grid=grid,
        in_specs=[a_spec, b_spec] + [mn_spec] * n_extra,
        out_specs=[mn_spec] * n_out,
        out_shape=[jax.ShapeDtypeStruct((m_dim, n_dim), dt) for dt in out_dtypes],
        scratch_shapes=[pltpu.VMEM((tm, tn), F32)] if nk > 1 else [],
        compiler_params=_params(dimension_semantics=("parallel", "parallel", "arbitrary")),
    )(a, b, *extras)
    return outs[0] if n_out == 1 else outs


def _rows(fn, row_ins, vec_ins, row_outs, vec_outs, *, name, tt=512, reverse=False):
    t_dim = row_ins[0].shape[0]
    tt = min(tt, t_dim)
    n = t_dim // tt
    n_ri, n_vi, n_ro, n_vo = len(row_ins), len(vec_ins), len(row_outs), len(vec_outs)
    pos = (lambda i: (n - 1 - i, 0)) if reverse else (lambda i: (i, 0))
    fixed = lambda i: (0, 0)

    def body(*refs):
        ri = refs[:n_ri]
        vi = refs[n_ri:n_ri + n_vi]
        ro = refs[n_ri + n_vi:n_ri + n_vi + n_ro]
        vo = refs[n_ri + n_vi + n_ro:n_ri + n_vi + n_ro + n_vo]
        scratch = refs[n_ri + n_vi + n_ro + n_vo:]
        r_out, v_out = fn([r[...] for r in ri], [v[...] for v in vi], *scratch)
        for o_ref, val in zip(ro, r_out):
            o_ref[...] = val.astype(o_ref.dtype)
        i = pl.program_id(0)
        for o_ref, val in zip(vo, v_out):
            @pl.when(i == 0)
            def _(o_ref=o_ref, val=val):
                o_ref[...] = val

            @pl.when(i > 0)
            def _(o_ref=o_ref, val=val):
                o_ref[...] += val

    return body, dict(
        grid=(n,),
        in_specs=[pl.BlockSpec((tt, r.shape[1]), pos) for r in row_ins]
        + [pl.BlockSpec(v.shape, fixed) for v in vec_ins],
        out_specs=[pl.BlockSpec((tt, w), pos) for w, _ in row_outs]
        + [pl.BlockSpec(s, fixed) for s in vec_outs],
        out_shape=[jax.ShapeDtypeStruct((t_dim, w), dt) for w, dt in row_outs]
        + [jax.ShapeDtypeStruct(s, F32) for s in vec_outs],
        name=name,
        compiler_params=_params(dimension_semantics=("arbitrary",)),
    )


def _rows_call(fn, row_ins, vec_ins, row_outs, vec_outs, *, name, tt=512, reverse=False, scratch=()):
    body, kw = _rows(fn, row_ins, vec_ins, row_outs, vec_outs, name=name, tt=tt, reverse=reverse)
    return pl.pallas_call(body, scratch_shapes=list(scratch), **kw)(*row_ins, *vec_ins)


def _rstd(x):
    return lax.rsqrt(jnp.mean(x * x, axis=-1, keepdims=True) + RMS_EPS)


def _norm(x, g):
    return x * _rstd(x) * g


def _norm_bwd(x, g, dy):
    xh = x * _rstd(x)
    gy = dy * g
    dx = _rstd(x) * (gy - xh * jnp.mean(gy * xh, axis=-1, keepdims=True))
    return dx, jnp.sum(dy * xh, axis=0, keepdims=True)


def _sigmoid(x):
    return 1.0 / (1.0 + jnp.exp(-x))


def _log_sigmoid(x):
    return jnp.minimum(x, 0.0) - jnp.log(1.0 + jnp.exp(-jnp.abs(x)))


def _split3(x):
    hi = x.astype(BF16)
    r1 = x - hi.astype(F32)
    mid = r1.astype(BF16)
    lo = (r1 - mid.astype(F32)).astype(BF16)
    return hi, mid, lo


def _cumsum_fwd(fl, bias, *, name):
    w = fl.shape[1]
    tt = min(512, fl.shape[0])

    def fn(rows, vecs, carry_ref):
        i = pl.program_id(0)

        @pl.when(i == 0)
        def _():
            carry_ref[...] = jnp.zeros_like(carry_ref)

        lf = _log_sigmoid(rows[0] + vecs[0])
        r = lax.broadcasted_iota(jnp.int32, (tt, tt), 0)
        c = lax.broadcasted_iota(jnp.int32, (tt, tt), 1)
        tri = (c <= r).astype(BF16)
        acc = carry_ref[0:1, :]
        for part in _split3(lf):
            acc = acc + jnp.dot(tri, part, preferred_element_type=F32)
        carry_ref[0:1, :] = acc[tt - 1:tt, :]
        return [acc], []

    return _rows_call(fn, [fl], [bias], [(w, F32)], [], name=name, tt=tt,
                      scratch=[pltpu.VMEM((8, w), F32)])[0]


def _cumsum_bwd(dc, fl, bias, *, name):
    w = fl.shape[1]
    tt = min(512, fl.shape[0])

    def fn(rows, vecs, carry_ref):
        i = pl.program_id(0)

        @pl.when(i == 0)
        def _():
            carry_ref[...] = jnp.zeros_like(carry_ref)

        r = lax.broadcasted_iota(jnp.int32, (tt, tt), 0)
        c = lax.broadcasted_iota(jnp.int32, (tt, tt), 1)
        tri = (c >= r).astype(BF16)
        acc = carry_ref[0:1, :]
        for part in _split3(rows[0]):
            acc = acc + jnp.dot(tri, part, preferred_element_type=F32)
        carry_ref[0:1, :] = acc[0:1, :]
        dfl = acc * _sigmoid(-(rows[1] + vecs[0]))
        return [dfl], [jnp.sum(dfl, axis=0, keepdims=True)]

    return _rows_call(fn, [dc, fl], [bias], [(w, BF16)], [(1, w)], name=name, tt=tt, reverse=True,
                      scratch=[pltpu.VMEM((8, w), F32)])


def _head_masks(tb):
    lane = lax.broadcasted_iota(jnp.int32, (tb, LANES), 1)
    return [lane < HEAD_DIM, lane >= HEAD_DIM]


PRUNE_MARGIN = 30.0
NORM_SLACK = 1.02


def _head_norms(qkv, *, name):
    t_dim = qkv.shape[0]
    d = qkv.shape[1] // 3
    heads = d // HEAD_DIM
    tt = min(512, t_dim)

    def body(q_ref, k_ref, o_ref):
        col = lax.broadcasted_iota(jnp.int32, (d, LANES), 0) // HEAD_DIM
        lane = lax.broadcasted_iota(jnp.int32, (d, LANES), 1)
        tile_max = None
        for ref, first in ((q_ref, 0), (k_ref, heads)):
            x = ref[...].astype(F32)
            sums = jnp.dot((x * x).astype(BF16), (col + first == lane).astype(BF16), preferred_element_type=F32)
            part = jnp.max(sums, axis=0, keepdims=True)
            tile_max = part if tile_max is None else jnp.maximum(tile_max, part)
        i = pl.program_id(0)

        @pl.when(i == 0)
        def _():
            o_ref[...] = tile_max

        @pl.when(i > 0)
        def _():
            o_ref[...] = jnp.maximum(o_ref[...], tile_max)

    return pl.pallas_call(
        body, name=name, grid=(t_dim // tt,),
        in_specs=[pl.BlockSpec((tt, d), lambda i: (i, 0)), pl.BlockSpec((tt, d), lambda i: (i, 1))],
        out_specs=pl.BlockSpec((1, LANES), lambda i: (0, 0)),
        out_shape=jax.ShapeDtypeStruct((1, LANES), F32),
        compiler_params=_params(dimension_semantics=("arbitrary",)),
    )(qkv, qkv)


def _prune_table(c_t, norms, tb):
    heads = c_t.shape[0]
    bound = NORM_SLACK * HEAD_DIM ** -0.5 * jnp.sqrt(norms[0, :heads] * norms[0, heads:2 * heads])
    return jnp.concatenate([c_t[:, ::tb], c_t[:, tb - 1::tb], -(PRUNE_MARGIN + 2.0 * bound)[:, None]], axis=1)


def _kept_before(prune_ref, h, i, nq):
    first, thr = prune_ref[h, i], prune_ref[h, 2 * nq]
    return lax.while_loop(lambda n: (n < i) & (first - prune_ref[h, nq + jnp.maximum(i - 1 - n, 0)] >= thr),
                          lambda n: n + 1, jnp.int32(0))


def _kept_after(prune_ref, h, j, nq):
    last, thr = prune_ref[h, nq + j], prune_ref[h, 2 * nq]
    return lax.while_loop(lambda n: (j + 1 + n < nq) & (prune_ref[h, jnp.minimum(j + 1 + n, nq - 1)] - last >= thr),
                          lambda n: n + 1, jnp.int32(0))


def _as_row(col, tb):
    return jnp.transpose(jnp.broadcast_to(col, (tb, LANES)))[0:1, :]


def _flash_fwd(qkv, c_rows, prune, *, tb, name, gather=()):
    t_dim = qkv.shape[0]
    d = qkv.shape[1] // 3
    heads = d // HEAD_DIM
    cb = d // LANES
    nq = t_dim // tb
    n_w = len(gather)
    widths, gather_shapes = _gather_shapes(gather)

    def body(prune_ref, q_ref, k_ref, v_ref, cr_ref, *rest):
        src, (o_ref, lser_ref), dst = rest[:n_w], rest[n_w:n_w + 2], rest[n_w + 2:2 * n_w + 2]
        i = pl.program_id(1)
        h0 = 2 * pl.program_id(0)
        if gather:
            start, relay, finish = _gather_phases(gather, widths, src, dst, *rest[2 * n_w + 2:])
            pl.when((pl.program_id(0) == 0) & (i == 0))(start)
            pl.when((pl.program_id(0) == (3 * heads) // 8) & (i == 0))(relay)
        q = q_ref[...] * jnp.asarray(HEAD_DIM ** -0.5, BF16)
        masks = _head_masks(tb)
        row = lax.broadcasted_iota(jnp.int32, (tb, tb), 0)
        col = lax.broadcasted_iota(jnp.int32, (tb, tb), 1)
        qs = [jnp.where(masks[e], q, jnp.zeros_like(q)) for e in range(2)]

        def step(j, carry, diagonal):
            off = pl.multiple_of(j * tb, tb)
            kj = k_ref[pl.ds(off, tb), :]
            vj = v_ref[pl.ds(off, tb), :]
            out = []
            for e in range(2):
                m, l, acc = carry[e]
                crow = cr_ref[0, e, :, pl.ds(off, tb)]
                s = lax.dot_general(qs[e], kj, (((1,), (1,)), ((), ())), preferred_element_type=F32) - crow
                if diagonal:
                    s = jnp.where(col <= row, s, NEG_INF)
                m_new = jnp.maximum(m, jnp.max(s, axis=1, keepdims=True))
                p = jnp.exp(s - m_new)
                alpha = jnp.exp(m - m_new)
                l = alpha * l + jnp.sum(p, axis=1, keepdims=True)
                acc = alpha * acc + jnp.dot(p.astype(BF16), vj, preferred_element_type=F32)
                out.append((m_new, l, acc))
            return tuple(out)

        init = (jnp.full((tb, 1), NEG_INF, F32), jnp.zeros((tb, 1), F32), jnp.zeros((tb, LANES), F32))
        kept = jnp.maximum(_kept_before(prune_ref, h0, i, nq), _kept_before(prune_ref, h0 + 1, i, nq))
        carry = lax.fori_loop(i - kept, i, functools.partial(step, diagonal=False), (init, init))
        carry = step(i, carry, True)
        outs = []
        for e in range(2):
            m, l, acc = carry[e]
            outs.append(acc / l)
            lser_ref[0, e] = _as_row(m + jnp.log(l), tb)
        o_ref[...] = jnp.where(masks[0], outs[0], outs[1]).astype(o_ref.dtype)
        if gather:
            pl.when((pl.program_id(0) == heads // 2 - 1) & (i == nq - 1))(finish)

    row_spec = pl.BlockSpec((1, 2, 1, t_dim), lambda h, i: (h, 0, 0, 0))
    row_blk = pl.BlockSpec((1, 2, 1, tb), lambda h, i: (h, 0, 0, i))
    hbm = pl.BlockSpec(memory_space=pl.ANY)
    outs = pl.pallas_call(
        body, name=name, grid=(heads // 2, nq),
        in_specs=[pl.BlockSpec(memory_space=pltpu.SMEM),
                  pl.BlockSpec((tb, LANES), lambda h, i: (i, h)),
                  pl.BlockSpec((t_dim, LANES), lambda h, i: (0, cb + h)),
                  pl.BlockSpec((t_dim, LANES), lambda h, i: (0, 2 * cb + h)),
                  row_spec] + [hbm] * n_w,
        out_specs=[pl.BlockSpec((tb, LANES), lambda h, i: (i, h)), row_blk] + [hbm] * n_w,
        out_shape=[jax.ShapeDtypeStruct((t_dim, d), BF16),
                   jax.ShapeDtypeStruct((heads // 2, 2, 1, t_dim), F32)] + gather_shapes,
        scratch_shapes=_scatter_sems(n_w) if gather else [],
        compiler_params=_params(dimension_semantics=("arbitrary", "arbitrary")),
    )(prune, qkv, qkv, qkv, c_rows, *[arr for arr, _ in gather])
    return outs[0], outs[1], outs[2:]


def _flash_dq(qkv, o, do, c_rows, lse_rows, prune, *, tb, name, scatter=()):
    t_dim = qkv.shape[0]
    d = qkv.shape[1] // 3
    heads = d // HEAD_DIM
    cb = d // LANES
    scale = HEAD_DIM ** -0.5
    nq = t_dim // tb
    n_w = len(scatter)
    widths, scatter_shapes = _scatter_shapes(scatter)

    def body(prune_ref, q_ref, k_ref, v_ref, o_ref, do_ref, cr_ref, lse_ref, *rest):
        src, (dq_ref, dl_ref, rs_ref), dst = rest[:n_w], rest[n_w:n_w + 3], rest[n_w + 3:2 * n_w + 3]
        i = pl.program_id(1)
        h0 = 2 * pl.program_id(0)
        if scatter:
            travel = lambda: _scatter_copies(scatter, widths, src, dst, *rest[2 * n_w + 3:])

            @pl.when((pl.program_id(0) == 0) & (i == 0))
            def _():
                for cp in travel():
                    cp.start()
        q = q_ref[...] * jnp.asarray(scale, BF16)
        do_blk = do_ref[...]
        prod = do_blk.astype(F32) * o_ref[...].astype(F32)
        masks = _head_masks(tb)
        row = lax.broadcasted_iota(jnp.int32, (tb, tb), 0)
        col = lax.broadcasted_iota(jnp.int32, (tb, tb), 1)
        qs = [jnp.where(masks[e], q, jnp.zeros_like(q)) for e in range(2)]
        dos = [jnp.where(masks[e], do_blk, jnp.zeros_like(do_blk)) for e in range(2)]
        deltas = [jnp.sum(jnp.where(masks[e], prod, 0.0), axis=1, keepdims=True) for e in range(2)]
        lses = [jnp.transpose(jnp.broadcast_to(lse_ref[0, e], (LANES, tb)))[:, 0:1] for e in range(2)]

        def step(j, carry, diagonal):
            off = pl.multiple_of(j * tb, tb)
            kj = k_ref[pl.ds(off, tb), :]
            vj = v_ref[pl.ds(off, tb), :]
            out = []
            for e in range(2):
                acc, rsum = carry[e]
                crow = cr_ref[0, e, :, pl.ds(off, tb)]
                s = lax.dot_general(qs[e], kj, (((1,), (1,)), ((), ())), preferred_element_type=F32) - crow
                if diagonal:
                    s = jnp.where(col <= row, s, NEG_INF)
                p = jnp.exp(s - lses[e])
                dp = lax.dot_general(dos[e], vj, (((1,), (1,)), ((), ())), preferred_element_type=F32)
                ds = p * (dp - deltas[e])
                out.append((acc + jnp.dot(ds.astype(BF16), kj, preferred_element_type=F32),
                            rsum + jnp.sum(ds, axis=1, keepdims=True)))
            return tuple(out)

        init = (jnp.zeros((tb, LANES), F32), jnp.zeros((tb, 1), F32))
        kept = jnp.maximum(_kept_before(prune_ref, h0, i, nq), _kept_before(prune_ref, h0 + 1, i, nq))
        carry = lax.fori_loop(i - kept, i, functools.partial(step, diagonal=False), (init, init))
        carry = step(i, carry, True)
        for e in range(2):
            dl_ref[0, e] = _as_row(deltas[e], tb)
            rs_ref[0, e] = _as_row(carry[e][1], tb)
        dq_ref[...] = (jnp.where(masks[0], carry[0][0], carry[1][0]) * scale).astype(dq_ref.dtype)
        if scatter:
            @pl.when((pl.program_id(0) == heads // 2 - 1) & (i == nq - 1))
            def _():
                for cp in travel():
                    cp.wait()

    blk = pl.BlockSpec((tb, LANES), lambda h, i: (i, h))
    row_spec = pl.BlockSpec((1, 2, 1, t_dim), lambda h, i: (h, 0, 0, 0))
    row_blk = pl.BlockSpec((1, 2, 1, tb), lambda h, i: (h, 0, 0, i))
    row_shape = jax.ShapeDtypeStruct((heads // 2, 2, 1, t_dim), F32)
    hbm = pl.BlockSpec(memory_space=pl.ANY)
    outs = pl.pallas_call(
        body, name=name, grid=(heads // 2, nq),
        in_specs=[pl.BlockSpec(memory_space=pltpu.SMEM), blk,
                  pl.BlockSpec((t_dim, LANES), lambda h, i: (0, cb + h)),
                  pl.BlockSpec((t_dim, LANES), lambda h, i: (0, 2 * cb + h)),
                  blk, blk, row_spec, row_blk] + [hbm] * n_w,
        out_specs=[blk, row_blk, row_blk] + [hbm] * n_w,
        out_shape=[jax.ShapeDtypeStruct((t_dim, d), BF16), row_shape, row_shape] + scatter_shapes,
        scratch_shapes=_scatter_sems(n_w) if scatter else [],
        compiler_params=_params(dimension_semantics=("arbitrary", "arbitrary")),
    )(prune, qkv, qkv, qkv, o, do, c_rows, lse_rows, *[arr for arr, _ in scatter])
    return outs[0], outs[1], outs[2], outs[3:]


def _flash_dkv(qkv, do, c_rows, lse_rows, delta_rows, prune, *, tb, name, scatter=()):
    t_dim = qkv.shape[0]
    d = qkv.shape[1] // 3
    heads = d // HEAD_DIM
    cb = d // LANES
    scale = HEAD_DIM ** -0.5
    nq = t_dim // tb
    n_w = len(scatter)
    widths, scatter_shapes = _scatter_shapes(scatter)

    def body(prune_ref, q_ref, k_ref, v_ref, do_ref, cc_ref, lr_ref, dr_ref, *rest):
        src, (dk_ref, dv_ref, dsum_ref), dst = rest[:n_w], rest[n_w:n_w + 3], rest[n_w + 3:2 * n_w + 3]
        j = pl.program_id(1)
        h0 = 2 * pl.program_id(0)
        if scatter:
            travel = lambda: _scatter_copies(scatter, widths, src, dst, *rest[2 * n_w + 3:])

            @pl.when((pl.program_id(0) == 0) & (j == 0))
            def _():
                for cp in travel():
                    cp.start()
        k_blk = k_ref[...] * jnp.asarray(scale, BF16)
        v_blk = v_ref[...]
        masks = _head_masks(tb)
        row = lax.broadcasted_iota(jnp.int32, (tb, tb), 0)
        col = lax.broadcasted_iota(jnp.int32, (tb, tb), 1)
        ks = [jnp.where(masks[e], k_blk, jnp.zeros_like(k_blk)) for e in range(2)]
        vs = [jnp.where(masks[e], v_blk, jnp.zeros_like(v_blk)) for e in range(2)]
        ccols = [jnp.transpose(jnp.broadcast_to(cc_ref[0, e], (LANES, tb)))[:, 0:1] for e in range(2)]

        def step(i, carry, diagonal):
            off = pl.multiple_of(i * tb, tb)
            qi = q_ref[pl.ds(off, tb), :]
            doi = do_ref[pl.ds(off, tb), :]
            out = []
            for e in range(2):
                dk, dv, dsum = carry[e]
                lse = lr_ref[0, e, :, pl.ds(off, tb)]
                delta = dr_ref[0, e, :, pl.ds(off, tb)]
                st = lax.dot_general(ks[e], qi, (((1,), (1,)), ((), ())), preferred_element_type=F32) - ccols[e]
                if diagonal:
                    st = jnp.where(col >= row, st, NEG_INF)
                pt = jnp.exp(st - lse)
                dpt = lax.dot_general(vs[e], doi, (((1,), (1,)), ((), ())), preferred_element_type=F32)
                dst = pt * (dpt - delta)
                out.append((dk + jnp.dot(dst.astype(BF16), qi, preferred_element_type=F32),
                            dv + jnp.dot(pt.astype(BF16), doi, preferred_element_type=F32),
                            dsum + jnp.sum(dst, axis=1, keepdims=True)))
            return tuple(out)

        zero = jnp.zeros((tb, LANES), F32)
        init = (zero, zero, jnp.zeros((tb, 1), F32))
        carry = step(j, (init, init), True)
        kept = jnp.maximum(_kept_after(prune_ref, h0, j, nq), _kept_after(prune_ref, h0 + 1, j, nq))
        carry = lax.fori_loop(j + 1, j + 1 + kept, functools.partial(step, diagonal=False), carry)
        for e in range(2):
            dsum_ref[0, e] = _as_row(carry[e][2], tb)
        dk_ref[...] = (jnp.where(masks[0], carry[0][0], carry[1][0]) * scale).astype(dk_ref.dtype)
        dv_ref[...] = jnp.where(masks[0], carry[0][1], carry[1][1]).astype(dv_ref.dtype)
        if scatter:
            @pl.when((pl.program_id(0) == heads // 2 - 1) & (j == nq - 1))
            def _():
                for cp in travel():
                    cp.wait()

    blk = pl.BlockSpec((tb, LANES), lambda h, j: (j, h))
    row_spec = pl.BlockSpec((1, 2, 1, t_dim), lambda h, j: (h, 0, 0, 0))
    row_blk = pl.BlockSpec((1, 2, 1, tb), lambda h, j: (h, 0, 0, j))
    hbm = pl.BlockSpec(memory_space=pl.ANY)
    outs = pl.pallas_call(
        body, name=name, grid=(heads // 2, nq),
        in_specs=[pl.BlockSpec(memory_space=pltpu.SMEM),
                  pl.BlockSpec((t_dim, LANES), lambda h, j: (0, h)),
                  pl.BlockSpec((tb, LANES), lambda h, j: (j, cb + h)),
                  pl.BlockSpec((tb, LANES), lambda h, j: (j, 2 * cb + h)),
                  pl.BlockSpec((t_dim, LANES), lambda h, j: (0, h)),
                  row_blk, row_spec, row_spec] + [hbm] * n_w,
        out_specs=[blk, blk, row_blk] + [hbm] * n_w,
        out_shape=[jax.ShapeDtypeStruct((t_dim, d), BF16), jax.ShapeDtypeStruct((t_dim, d), BF16),
                   jax.ShapeDtypeStruct((heads // 2, 2, 1, t_dim), F32)] + scatter_shapes,
        scratch_shapes=_scatter_sems(n_w) if scatter else [],
        compiler_params=_params(dimension_semantics=("arbitrary", "arbitrary")),
    )(prune, qkv, qkv, qkv, do, c_rows, lse_rows, delta_rows, *[arr for arr, _ in scatter])
    return outs[0], outs[1], outs[2], outs[3:]


def _shift_down(z, prev, n, tt):
    out = pltpu.roll(z, n, axis=0)
    row = lax.broadcasted_iota(jnp.int32, z.shape, 0)
    for r in range(n):
        out = jnp.where(row == r, prev[8 - n + r:8 - n + r + 1, :], out)
    return out


def _shift_up(z, nxt, n, tt):
    out = pltpu.roll(z, tt - n, axis=0)
    row = lax.broadcasted_iota(jnp.int32, z.shape, 0)
    for r in range(n):
        out = jnp.where(row == tt - n + r, nxt[r:r + 1, :], out)
    return out


def _conv_fwd(proj, conv_w, *, name, tt=256):
    t_dim, d3 = proj.shape
    d = d3 // 3
    tt = min(tt, t_dim)

    def body(p_ref, prev_ref, w_ref, y_ref):
        i = pl.program_id(0)
        p = p_ref[...]
        pp = prev_ref[...]
        z = p[:, d:2 * d] * p[:, 2 * d:]
        zp = jnp.where(i > 0, pp[:, d:2 * d] * pp[:, 2 * d:], 0.0)
        w = w_ref[...]
        zc = w[2:3, :] * z + w[1:2, :] * _shift_down(z, zp, 1, tt) + w[0:1, :] * _shift_down(z, zp, 2, tt)
        y_ref[...] = (p[:, :d] * zc).astype(y_ref.dtype)

    return pl.pallas_call(
        body, name=name, grid=(t_dim // tt,),
        in_specs=[pl.BlockSpec((tt, d3), lambda i: (i, 0)),
                  pl.BlockSpec((8, d3), lambda i: (jnp.maximum(i * (tt // 8) - 1, 0), 0)),
                  pl.BlockSpec(conv_w.shape, lambda i: (0, 0))],
        out_specs=pl.BlockSpec((tt, d), lambda i: (i, 0)),
        out_shape=jax.ShapeDtypeStruct((t_dim, d), BF16),
        compiler_params=_params(dimension_semantics=("arbitrary",)),
    )(proj, proj, conv_w)


def _conv_bwd(proj, dy, conv_w, *, name, tt=256):
    t_dim, d3 = proj.shape
    d = d3 // 3
    tt = min(tt, t_dim)
    n = t_dim // tt

    def body(p_ref, prev_ref, next_ref, dy_ref, dyn_ref, w_ref, dp_ref, dw_ref):
        i = pl.program_id(0)
        p = p_ref[...]
        pp = prev_ref[...]
        pn = next_ref[...]
        bg, cg, u = p[:, :d], p[:, d:2 * d], p[:, 2 * d:]
        z = cg * u
        zp = jnp.where(i > 0, pp[:, d:2 * d] * pp[:, 2 * d:], 0.0)
        w = w_ref[...]
        z1 = _shift_down(z, zp, 1, tt)
        z2 = _shift_down(z, zp, 2, tt)
        zc = w[2:3, :] * z + w[1:2, :] * z1 + w[0:1, :] * z2
        dy_blk = dy_ref[...]
        dzc = dy_blk * bg
        dzn = jnp.where(i < n - 1, dyn_ref[...] * pn[:, :d], 0.0)
        dz = w[2:3, :] * dzc + w[1:2, :] * _shift_up(dzc, dzn, 1, tt) + w[0:1, :] * _shift_up(dzc, dzn, 2, tt)
        dp_ref[:, :d] = (dy_blk * zc).astype(dp_ref.dtype)
        dp_ref[:, d:2 * d] = (dz * u).astype(dp_ref.dtype)
        dp_ref[:, 2 * d:] = (dz * cg).astype(dp_ref.dtype)
        part = jnp.concatenate([jnp.sum(dzc * z2, axis=0, keepdims=True),
                                jnp.sum(dzc * z1, axis=0, keepdims=True),
                                jnp.sum(dzc * z, axis=0, keepdims=True),
                                jnp.zeros((5, d), F32)], axis=0)

        @pl.when(i == 0)
        def _():
            dw_ref[...] = part

        @pl.when(i > 0)
        def _():
            dw_ref[...] += part

    last8 = t_dim // 8 - 1
    return pl.pallas_call(
        body, name=name, grid=(n,),
        in_specs=[pl.BlockSpec((tt, d3), lambda i: (i, 0)),
                  pl.BlockSpec((8, d3), lambda i: (jnp.maximum(i * (tt // 8) - 1, 0), 0)),
                  pl.BlockSpec((8, d3), lambda i: (jnp.minimum((i + 1) * (tt // 8), last8), 0)),
                  pl.BlockSpec((tt, d), lambda i: (i, 0)),
                  pl.BlockSpec((8, d), lambda i: (jnp.minimum((i + 1) * (tt // 8), last8), 0)),
                  pl.BlockSpec(conv_w.shape, lambda i: (0, 0))],
        out_specs=[pl.BlockSpec((tt, d3), lambda i: (i, 0)), pl.BlockSpec((8, d), lambda i: (0, 0))],
        out_shape=[jax.ShapeDtypeStruct((t_dim, d3), BF16), jax.ShapeDtypeStruct((8, d), F32)],
        compiler_params=_params(dimension_semantics=("arbitrary",)),
    )(proj, proj, proj, dy, dy, conv_w)


def _window(ref, axis, n, idx):
    if axis is None:
        return ref
    sel = [slice(None)] * len(ref.shape)
    sel[axis] = pl.ds(pl.multiple_of(idx * n, n), n)
    return ref.at[tuple(sel)]


def _scatter_shapes(items):
    widths, shapes = [], []
    for arr, axis in items:
        shp = list(arr.shape)
        if axis is not None:
            shp[axis] //= N_DEV
        widths.append(None if axis is None else shp[axis])
        shapes.append(jax.ShapeDtypeStruct((N_DEV, *shp), arr.dtype))
    return widths, shapes


def _scatter_copies(items, widths, src, dst, send_sems, recv_sems, local_sems):
    x, y, c = lax.axis_index("x"), lax.axis_index("y"), lax.axis_index("c")
    me = 4 * x + 2 * y + c
    copies = [pltpu.make_async_copy(_window(src[w], items[w][1], widths[w], me), dst[w].at[me], local_sems.at[w])
              for w in range(len(items))]
    for k in range(1, N_DEV):
        px = 1 - x if k & 4 else x
        py = 1 - y if k & 2 else y
        pc = 1 - c if k & 1 else c
        for w in range(len(items)):
            copies.append(pltpu.make_async_remote_copy(
                src_ref=_window(src[w], items[w][1], widths[w], 4 * px + 2 * py + pc), dst_ref=dst[w].at[me],
                send_sem=send_sems.at[w, k - 1], recv_sem=recv_sems.at[w, k - 1],
                device_id=(px, py, pc), device_id_type=pl.DeviceIdType.MESH))
    return copies


def _scatter_sems(n_w):
    return [pltpu.SemaphoreType.DMA((n_w, N_DEV - 1)), pltpu.SemaphoreType.DMA((n_w, N_DEV - 1)),
            pltpu.SemaphoreType.DMA((n_w,))]


def _gather_shapes(items):
    widths = [arr.shape[axis] for arr, axis in items]
    shapes = [jax.ShapeDtypeStruct(tuple(s * N_DEV if a == axis else s for a, s in enumerate(arr.shape)), arr.dtype)
              for arr, axis in items]
    return widths, shapes


def _gather_phases(items, widths, src, dst, send_sems, recv_sems, local_sems):
    n_w = len(items)

    def run(phase):
        x, y, c = lax.axis_index("x"), lax.axis_index("y"), lax.axis_index("c")
        chips = [(1 - x, y), (x, 1 - y), (1 - x, 1 - y)]

        def place(w, origin):
            return _window(dst[w], items[w][1], widths[w], 4 * origin[0] + 2 * origin[1] + origin[2])

        def block_copy(w, n, origin, to, from_shard):
            return pltpu.make_async_remote_copy(
                src_ref=src[w] if from_shard else place(w, origin), dst_ref=place(w, origin),
                send_sem=send_sems.at[w, n], recv_sem=recv_sems.at[w, n],
                device_id=to, device_id_type=pl.DeviceIdType.MESH)

        def own(w):
            return pltpu.make_async_copy(src[w], place(w, (x, y, c)), local_sems.at[w])

        def first(w):
            return ([block_copy(w, 0, (x, y, c), (x, y, 1 - c), True)]
                    + [block_copy(w, 1 + n, (x, y, c), (*chip, c), True) for n, chip in enumerate(chips)])

        def passed(w, n):
            return block_copy(w, 4 + n, (*chips[n], c), (x, y, 1 - c), False)

        if phase == "start":
            for w in range(n_w):
                own(w).start()
                for cp in first(w):
                    cp.start()
        elif phase == "relay":
            for n, chip in enumerate(chips):
                for w in range(n_w):
                    block_copy(w, 1 + n, (*chip, c), (x, y, c), True).wait_recv()
                    passed(w, n).start()
        else:
            for w in range(n_w):
                block_copy(w, 0, (x, y, 1 - c), (x, y, c), True).wait_recv()
                for n, chip in enumerate(chips):
                    block_copy(w, 4 + n, (*chip, 1 - c), (x, y, c), False).wait_recv()
                for cp in first(w) + [passed(w, n) for n in range(3)]:
                    cp.wait_send()
                own(w).wait()

    return [functools.partial(run, phase) for phase in ("start", "relay", "finish")]


def _exchange(items, *, gather, name):
    n_w = len(items)
    widths, out_shape = _gather_shapes(items) if gather else _scatter_shapes(items)

    def body(*refs):
        src, dst = refs[:n_w], refs[n_w:2 * n_w]
        send_sems, recv_sems, local_sems = refs[2 * n_w:]
        if not gather:
            copies = _scatter_copies(items, widths, src, dst, send_sems, recv_sems, local_sems)
            for cp in copies:
                cp.start()
            for cp in copies:
                cp.wait()
            return
        for phase in _gather_phases(items, widths, src, dst, send_sems, recv_sems, local_sems):
            phase()

    return pl.pallas_call(
        body, name=name,
        in_specs=[pl.BlockSpec(memory_space=pl.ANY)] * n_w,
        out_specs=[pl.BlockSpec(memory_space=pl.ANY)] * n_w,
        out_shape=out_shape,
        scratch_shapes=_scatter_sems(n_w),
    )(*[arr for arr, _ in items])


def _row_tile(rows, cols):
    tr = rows
    while tr % 16 == 0 and tr * cols > 256 * 1024:
        tr //= 2
    return tr


def _sum_parts(parts, *, name):
    n_parts, rows, cols = parts.shape
    tr = _row_tile(rows, cols)

    def body(p_ref, o_ref):
        g = p_ref[0].astype(F32)
        for s in range(1, n_parts):
            g = g + p_ref[s].astype(F32)
        o_ref[...] = g

    return pl.pallas_call(
        body, name=name, grid=(rows // tr,),
        in_specs=[pl.BlockSpec((n_parts, tr, cols), lambda i: (0, i, 0))],
        out_specs=pl.BlockSpec((tr, cols), lambda i: (i, 0)),
        out_shape=jax.ShapeDtypeStruct((rows, cols), F32),
        compiler_params=_params(dimension_semantics=("parallel",)),
    )(parts)


def _adamw(parts, w, m, v, *, name):
    n_parts, rows, cols = parts.shape
    tr = _row_tile(rows, cols)

    def body(p_ref, w_ref, m_ref, v_ref, g_ref, d_ref, nm_ref, nv_ref):
        g = p_ref[0].astype(F32)
        for s in range(1, n_parts):
            g = g + p_ref[s].astype(F32)
        m_new = ADAM_B1 * m_ref[...] + (1.0 - ADAM_B1) * g
        v_new = ADAM_B2 * v_ref[...] + (1.0 - ADAM_B2) * (g * g)
        m_hat = m_new / (1.0 - ADAM_B1 ** ADAM_STEP)
        v_hat = v_new / (1.0 - ADAM_B2 ** ADAM_STEP)
        g_ref[...] = g
        d_ref[...] = -ADAM_LR * (m_hat / (jnp.sqrt(v_hat) + ADAM_EPS) + ADAM_WD * w_ref[...])
        nm_ref[...] = m_new
        nv_ref[...] = v_new

    spec = pl.BlockSpec((tr, cols), lambda i: (i, 0))
    return pl.pallas_call(
        body, name=name, grid=(rows // tr,),
        in_specs=[pl.BlockSpec((n_parts, tr, cols), lambda i: (0, i, 0)), spec, spec, spec],
        out_specs=[spec] * 4,
        out_shape=[jax.ShapeDtypeStruct((rows, cols), F32)] * 4,
        compiler_params=_params(dimension_semantics=("parallel",)),
    )(parts, w, m, v)


def _pad_rows(a, axis, to):
    pad = [(0, 0)] * a.ndim
    pad[axis] = (0, to - a.shape[axis])
    return jnp.pad(a, pad)


def kernel(x, p, norm_g, w_attn_in, b_forget, w_attn_out, w_conv_in, conv_w, w_conv_out, w_mlp_up, w_mlp_down, w_ple_proj, w_ple_gate, loss_target, m_norm_g, m_w_attn_in, m_b_forget, m_w_attn_out, m_w_conv_in, m_conv_w, m_w_conv_out, m_w_mlp_up, m_w_mlp_down, m_w_ple_proj, m_w_ple_gate, v_norm_g, v_w_attn_in, v_b_forget, v_w_attn_out, v_w_conv_in, v_conv_w, v_w_conv_out, v_w_mlp_up, v_w_mlp_down, v_w_ple_proj, v_w_ple_gate):
    shards = dict(norm_g=norm_g, w_attn_in=w_attn_in, b_forget=b_forget, w_attn_out=w_attn_out,
                  w_conv_in=w_conv_in, conv_w=conv_w, w_conv_out=w_conv_out, w_mlp_up=w_mlp_up,
                  w_mlp_down=w_mlp_down, w_ple_proj=w_ple_proj, w_ple_gate=w_ple_gate)
    m_shards = dict(norm_g=m_norm_g, w_attn_in=m_w_attn_in, b_forget=m_b_forget, w_attn_out=m_w_attn_out,
                    w_conv_in=m_w_conv_in, conv_w=m_conv_w, w_conv_out=m_w_conv_out, w_mlp_up=m_w_mlp_up,
                    w_mlp_down=m_w_mlp_down, w_ple_proj=m_w_ple_proj, w_ple_gate=m_w_ple_gate)
    v_shards = dict(norm_g=v_norm_g, w_attn_in=v_w_attn_in, b_forget=v_b_forget, w_attn_out=v_w_attn_out,
                    w_conv_in=v_w_conv_in, conv_w=v_conv_w, w_conv_out=v_w_conv_out, w_mlp_up=v_w_mlp_up,
                    w_mlp_down=v_w_mlp_down, w_ple_proj=v_w_ple_proj, w_ple_gate=v_w_ple_gate)
    t_dim, d = x.shape[-2:]
    depth = p.shape[0]
    n_attn, heads = b_forget.shape
    assert d == heads * HEAD_DIM and x.shape[0] == 1
    tb = min(512, t_dim // 2)
    x0 = x.reshape(t_dim, d)
    target = loss_target.reshape(t_dim, d)
    p_rows = p.reshape(depth * t_dim, p.shape[-1])

    in_cols = w_attn_in.shape[2]
    in_cols_pad = -(-in_cols // 16) * 16
    first_names = ['norm_g', 'w_attn_in', 'conv_w']
    rest_names = [n for n in WEIGHT_NAMES if n not in first_names + ['b_forget']]

    def gather_item(n):
        if n == 'w_attn_in':
            return _pad_rows(jnp.swapaxes(w_attn_in, 1, 2), 1, in_cols_pad).astype(BF16), 1
        return (shards[n] if n in ('norm_g', 'conv_w') else shards[n].astype(BF16)), SHARD_AXIS[n]

    full = dict(zip(first_names, _exchange([gather_item(n) for n in first_names], gather=True, name="gather_first")))
    gains = full['norm_g']
    taps = full['conv_w']
    w_in_t = full['w_attn_in'].reshape(n_attn, N_DEV, in_cols_pad, d)[:, :, :in_cols]
    w_in_t = _pad_rows(w_in_t.reshape(n_attn, N_DEV * in_cols, d), 1, 3 * d + LANES)
    bias_pad = jnp.pad(b_forget, ((0, 0), (0, LANES - heads)))

    def gain(i, k):
        return gains[i, k].reshape(1, d)

    def add_norm(x_prev, branch, g_branch, g_next, name):
        def fn(rows, vecs):
            x_new = rows[0] + _norm(rows[1], vecs[0])
            return [x_new, _norm(x_new, vecs[1])], []
        return _rows_call(fn, [x_prev, branch], [g_branch, g_next], [(d, F32), (d, BF16)], [], name=name)

    saved = []
    x_cur = x0
    hn = _rows_call(lambda rows, vecs: ([_norm(rows[0], vecs[0])], []), [x0], [gain(0, 0)], [(d, BF16)], [],
                    name="norm_in")[0]
    loss_rows = dy = None
    for i in range(depth):
        j = i // 2
        s = dict(x0=x_cur, hn=hn)
        if i % 2 == 0:
            s['qkv'] = _mm(hn, w_in_t[j, :3 * d], tb=True, out_dtypes=(BF16,), name=f"attn_in_{i}")
            s['fl'] = _mm(hn, w_in_t[j, 3 * d:], tb=True, name=f"attn_gate_{i}")
            c = _cumsum_fwd(s['fl'], bias_pad[j:j + 1], name=f"gate_cumsum_{i}")
            c_t = c[:, :heads].T
            s['prune'] = _prune_table(c_t, _head_norms(s['qkv'], name=f"head_norms_{i}"), tb)
            s['c_rows'] = c_t.reshape(heads // 2, 2, 1, t_dim)
            s['o'], s['lse_rows'], rest = _flash_fwd(
                s['qkv'], s['c_rows'], s['prune'], tb=tb, name=f"attn_fwd_{i}",
                gather=[gather_item(n) for n in rest_names] if i == 0 else ())
            if i == 0:
                full.update(zip(rest_names, rest))
            s['m'] = _mm(s['o'], full['w_attn_out'][j], out_dtypes=(BF16,), name=f"attn_out_{i}")
        else:
            s['proj'] = _mm(hn, full['w_conv_in'][j], name=f"conv_in_{i}")
            s['y'] = _conv_fwd(s['proj'], taps[j], name=f"conv_fwd_{i}")
            s['m'] = _mm(s['y'], full['w_conv_out'][j], out_dtypes=(BF16,), name=f"conv_out_{i}")
        s['x1'], s['h2'] = add_norm(x_cur, s['m'], gain(i, 1), gain(i, 2), f"mix_norm_{i}")
        s['u'], s['a'] = _mm(s['h2'], full['w_mlp_up'][i], out_dtypes=(BF16, BF16), name=f"mlp_up_{i}",
                             epi=lambda acc: (acc, jnp.square(jnp.maximum(acc, 0.0))))
        s['f'] = _mm(s['a'], full['w_mlp_down'][i], out_dtypes=(BF16,), name=f"mlp_down_{i}")
        s['x2'], s['h4'] = add_norm(s['x1'], s['f'], gain(i, 3), gain(i, 4), f"mlp_norm_{i}")
        s['pp'] = _mm(p_rows, full['w_ple_proj'][i], a_rows=t_dim, a_off=i * t_dim, name=f"ple_proj_{i}")
        s['gl'], s['e'] = _mm(s['h4'], full['w_ple_gate'][i], extras=(s['pp'],), out_dtypes=(F32, BF16),
                              name=f"ple_gate_{i}", epi=lambda acc, pp: (acc, pp * _sigmoid(acc)))
        if i + 1 < depth:
            x_cur, hn = add_norm(s['x2'], s['e'], gain(i, 5), gain(i + 1, 0), f"ple_norm_{i}")
        else:
            def loss_fn(rows, vecs):
                err = rows[0] + _norm(rows[1], vecs[0]) - rows[2]
                part = 0.5 * jnp.sum(jnp.sum(err * err, axis=1, keepdims=True), axis=0, keepdims=True) / d
                return [err / d], [jnp.broadcast_to(part, (1, LANES))]
            dy, loss_rows = _rows_call(loss_fn, [s['x2'], s['e'], target], [gain(i, 5)], [(d, F32)],
                                       [(1, LANES)], name="loss")
        saved.append(s)
    loss = lax.psum(loss_rows[0, 0], ("x", "y", "c"))

    grads = {n: [None] * shards[n].shape[0] for n in WEIGHT_NAMES}
    d_gains = [[None] * 6 for _ in range(depth)]
    wgrad = functools.partial(_mm, ta=True, out_dtypes=(BF16,))
    dgrad = functools.partial(_mm, out_dtypes=(BF16,))
    axis_of = dict(SHARD_AXIS, w_attn_in=1)

    def in_t_blocks(layers):
        g = jnp.stack(layers)[:, :N_DEV * in_cols].reshape(len(layers), N_DEV, in_cols, d)
        return _pad_rows(g, 2, in_cols_pad).reshape(len(layers), N_DEV * in_cols_pad, d)

    early_names = early = None
    dx = dy
    for i in reversed(range(depth)):
        j = i // 2
        s = saved[i]

        def ple_fn(rows, vecs):
            de, dg = _norm_bwd(rows[0], vecs[0], rows[1])
            sg = _sigmoid(rows[2])
            return [de * sg, de * rows[3] * sg * (1.0 - sg)], [dg]
        if i == depth - 1:
            dpp, dgl, d_gains[i][5] = _rows_call(ple_fn, [s['e'], dx, s['gl'], s['pp']], [gain(i, 5)],
                                                 [(d, BF16), (d, BF16)], [(1, d)], name=f"ple_bwd_{i}")
        else:
            dpp, dgl = ple_ahead
        grads['w_ple_proj'][i] = wgrad(p_rows, dpp, a_rows=t_dim, a_off=i * t_dim, name=f"ple_proj_dw_{i}")
        grads['w_ple_gate'][i] = wgrad(s['h4'], dgl, name=f"ple_gate_dw_{i}")
        dh4 = dgrad(dgl, full['w_ple_gate'][i], tb=True, name=f"ple_gate_dx_{i}")

        def two_norm_bwd(x_res, dh, dx_in, branch, g_res, g_branch, name):
            def fn(rows, vecs):
                d_res, dg_res = _norm_bwd(rows[0], vecs[0], rows[1])
                dx_out = rows[2] + d_res
                d_branch, dg_branch = _norm_bwd(rows[3], vecs[1], dx_out)
                return [dx_out, d_branch], [dg_res, dg_branch]
            return _rows_call(fn, [x_res, dh, dx_in, branch], [g_res, g_branch], [(d, F32), (d, BF16)],
                              [(1, d), (1, d)], name=name)

        dx2, df, d_gains[i][4], d_gains[i][3] = two_norm_bwd(s['x2'], dh4, dx, s['f'], gain(i, 4), gain(i, 3),
                                                            f"mlp_norm_bwd_{i}")
        grads['w_mlp_down'][i] = wgrad(s['a'], df, name=f"mlp_down_dw_{i}")
        du = _mm(df, full['w_mlp_down'][i], tb=True, extras=(s['u'],), out_dtypes=(BF16,), name=f"mlp_down_dx_{i}",
                 epi=lambda acc, u: (acc * (2.0 * jnp.maximum(u.astype(F32), 0.0)),))
        grads['w_mlp_up'][i] = wgrad(s['h2'], du, name=f"mlp_up_dw_{i}")
        dh2 = dgrad(du, full['w_mlp_up'][i], tb=True, name=f"mlp_up_dx_{i}")
        dx1, dm, d_gains[i][2], d_gains[i][1] = two_norm_bwd(s['x1'], dh2, dx2, s['m'], gain(i, 2), gain(i, 1),
                                                            f"mix_norm_bwd_{i}")
        if i % 2 == 0:
            grads['w_attn_out'][j] = wgrad(s['o'], dm, name=f"attn_out_dw_{i}")
            do = _mm(dm, full['w_attn_out'][j], tb=True, out_dtypes=(BF16,), name=f"attn_out_dx_{i}")
            with_dq = ['w_mlp_up', 'w_mlp_down', 'w_ple_proj', 'w_ple_gate'] if i == 0 else []
            with_dkv = [n for n in WEIGHT_NAMES if n not in with_dq + ['norm_g', 'b_forget']] if i == 0 else []
            item = lambda n: (in_t_blocks(grads[n][1:]) if n == 'w_attn_in' else jnp.stack(grads[n]), axis_of[n])
            dq, delta_rows, rsum_rows, got_dq = _flash_dq(s['qkv'], s['o'], do, s['c_rows'], s['lse_rows'], s['prune'],
                                                          tb=tb, name=f"attn_dq_{i}", scatter=[item(n) for n in with_dq])
            dk, dv, csum_rows, got_dkv = _flash_dkv(s['qkv'], do, s['c_rows'], s['lse_rows'], delta_rows, s['prune'],
                                                    tb=tb, name=f"attn_dkv_{i}", scatter=[item(n) for n in with_dkv])
            if i == 0:
                early_names, early = with_dq + with_dkv, [*got_dq, *got_dkv]
            dc = (rsum_rows - csum_rows).reshape(heads, t_dim).T
            dfl, db = _cumsum_bwd(jnp.pad(dc, ((0, 0), (0, LANES - heads))), s['fl'], bias_pad[j:j + 1],
                                  name=f"gate_cumsum_bwd_{i}")
            grads['b_forget'][j] = db[0, :heads]
            dproj = jnp.concatenate([dq, dk, dv, dfl], axis=1)
            grads['w_attn_in'][j] = wgrad(dproj, s['hn'], name=f"attn_in_dw_{i}")
            dhn = dgrad(dproj, w_in_t[j], name=f"attn_in_dx_{i}")
        else:
            grads['w_conv_out'][j] = wgrad(s['y'], dm, name=f"conv_out_dw_{i}")
            dyc = _mm(dm, full['w_conv_out'][j], tb=True, name=f"conv_out_dx_{i}")
            dproj, dtaps = _conv_bwd(s['proj'], dyc, taps[j], name=f"conv_bwd_{i}")
            grads['conv_w'][j] = dtaps[:3].astype(BF16)
            grads['w_conv_in'][j] = wgrad(s['hn'], dproj, name=f"conv_in_dw_{i}")
            dhn = dgrad(dproj, full['w_conv_in'][j], tb=True, name=f"conv_in_dx_{i}")

        def in_fn(rows, vecs):
            d_res, dg = _norm_bwd(rows[0], vecs[0], rows[1])
            return [rows[2] + d_res], [dg]
        def in_ple_fn(rows, vecs):
            d_res, dg = _norm_bwd(rows[0], vecs[0], rows[1])
            dx_new = rows[2] + d_res
            de, dg_below = _norm_bwd(rows[3], vecs[1], dx_new)
            sg = _sigmoid(rows[4])
            return [dx_new, de * sg, de * rows[5] * sg * (1.0 - sg)], [dg, dg_below]

        if i == 0:
            dx, d_gains[i][0] = _rows_call(in_fn, [s['x0'], dhn, dx1], [gain(i, 0)], [(d, F32)], [(1, d)],
                                           name=f"in_norm_bwd_{i}")
        else:
            below = saved[i - 1]
            dx, *ple_ahead, d_gains[i][0], d_gains[i - 1][5] = _rows_call(
                in_ple_fn, [s['x0'], dhn, dx1, below['e'], below['gl'], below['pp']], [gain(i, 0), gain(i - 1, 5)],
                [(d, F32), (d, BF16), (d, BF16)], [(1, d), (1, d)], tt=256, name=f"in_norm_ple_bwd_{i}")
    grad_x = dx.reshape(x.shape)

    recv = dict(zip(early_names, early))
    late = _exchange([(in_t_blocks(grads['w_attn_in'][:1]), 1),
                      (jnp.stack([jnp.concatenate(row, axis=0) for row in d_gains]).astype(BF16), SHARD_AXIS['norm_g']),
                      (jnp.zeros((8, LANES), F32).at[:n_attn, :heads].set(jnp.stack(grads['b_forget'])), None)],
                     gather=False, name="exchange_late")
    recv['norm_g'] = late[1]
    recv['b_forget'] = late[2][:, :n_attn, :heads]
    g_in_t = jnp.concatenate([_sum_parts(part.reshape(N_DEV, -1, d), name=f"sum_attn_in_{k}")
                              for k, part in enumerate((late[0], recv['w_attn_in']))], axis=0)
    recv['w_attn_in'] = jnp.swapaxes(g_in_t.reshape(n_attn, in_cols_pad, d)[:, :in_cols], 1, 2)[None]
    results = {}
    for n in WEIGHT_NAMES:
        shp = shards[n].shape
        flat = lambda a: a.reshape(a.shape[:a.ndim - len(shp)] + (-1, shp[-1]))
        outs = _adamw(flat(recv[n]), flat(shards[n]), flat(m_shards[n]), flat(v_shards[n]), name=f"adamw_{n}")
        results[n] = [o.reshape(shp) for o in outs]
    return (loss, grad_x, *[results[n][k] for k in range(4) for n in WEIGHT_NAMES])
```
